```python
import math
import jax
import jax.numpy as jnp
from jax import lax
import numpy as np

D_MODEL = 1024
BATCH = 8
SEQ = 2048
DEPTH = 4

CTX_LEN = 256
GRID_W = 64
HEAD_DIM = 64
NA_HEADS = 4
NA_ROWS_MAX = 8
NA_COLS = 16
GA_HEADS = 4
GA_KV_HEADS = 2
SSM_GROUPS = 16
SSM_GROUP_CH = 16
SSM_STATE = 64
SW_HEADS = 4
SW_KV_HEADS = 2
SW_WINDOW = 128
Q_BLOCK = 128
NA_WIDTH = NA_HEADS * HEAD_DIM
GA_WIDTH = GA_HEADS * HEAD_DIM
GA_KV_WIDTH = GA_KV_HEADS * HEAD_DIM
SSM_WIDTH = SSM_GROUPS * SSM_GROUP_CH
SW_WIDTH = SW_HEADS * HEAD_DIM
SW_KV_WIDTH = SW_KV_HEADS * HEAD_DIM
MIX_WIDTH = NA_WIDTH + GA_WIDTH + SSM_WIDTH + SW_WIDTH
IN_SIZES = (NA_WIDTH, NA_WIDTH, NA_WIDTH, GA_WIDTH, GA_KV_WIDTH, GA_KV_WIDTH, SSM_WIDTH, SW_WIDTH, SW_KV_WIDTH, SW_KV_WIDTH)
IN_WIDTH = sum(IN_SIZES)
D_FF = 2816
N_EXPERTS = 8
TOP_K = 2
D_FF_EXPERT = 3584
ADA_CHUNKS = 6
ROPE_THETA = 10000.0
LN_EPS = 1e-6
RMS_EPS = 1e-6
NEG_INF = -1e30
DEEPNORM_ALPHA = (2 * DEPTH) ** 0.25
DEEPNORM_BETA = (8 * DEPTH) ** -0.25
F32 = jnp.float32

kernel_name = 'hybrid_dit_parallel_mixer_trunk'


def _layer_norm(x, g, b):
    xf = x.astype(F32)
    xc = xf - jnp.mean(xf, -1, keepdims=True)
    y = xc * lax.rsqrt(jnp.mean(xc * xc, -1, keepdims=True) + LN_EPS)
    return (y * g.astype(F32) + b.astype(F32)).astype(x.dtype)


def _rms_norm(x, g):
    xf = x.astype(F32)
    y = xf * lax.rsqrt(jnp.mean(xf * xf, -1, keepdims=True) + RMS_EPS)
    return (y * g.astype(F32)).astype(x.dtype)


def _heads(t):
    return t.reshape(t.shape[:-1] + (t.shape[-1] // HEAD_DIM, HEAD_DIM))


def _split_cols(p):
    return jnp.split(p, np.cumsum(IN_SIZES)[:-1].tolist(), axis=-1)


def _axial_rope_tables(n_tokens):
    pos = jnp.arange(n_tokens, dtype=jnp.int32)
    row = (pos // GRID_W).astype(F32)
    col = (pos % GRID_W).astype(F32)
    n_freq = HEAD_DIM // 4
    inv_freq = ROPE_THETA ** (-jnp.arange(n_freq, dtype=F32) / n_freq)
    ang = jnp.concatenate([row[:, None] * inv_freq, col[:, None] * inv_freq], axis=-1)
    return jnp.cos(ang), jnp.sin(ang)


def _apply_rope(x, cos, sin):
    xf = x.astype(F32)
    x1, x2 = jnp.split(xf, 2, axis=-1)
    c = cos[None, :, None, :]
    s = sin[None, :, None, :]
    return jnp.concatenate([x1 * c - x2 * s, x1 * s + x2 * c], axis=-1).astype(x.dtype)


def _sink_softmax(s, sink_b):
    m = jnp.maximum(jnp.max(s, -1, keepdims=True), sink_b)
    e = jnp.exp(s - m)
    return e / (jnp.sum(e, -1, keepdims=True) + jnp.exp(sink_b - m))


def _context_attention(q, k, v, sink=None):
    b, n, hq, dh = q.shape
    g = k.shape[2]
    r = hq // g
    qg = q.reshape(b, n, g, r, dh)
    s = jnp.einsum('bqgrd,bkgd->bgrqk', qg, k, preferred_element_type=F32) * (dh ** -0.5)
    if sink is None:
        p = jax.nn.softmax(s, -1)
    else:
        p = _sink_softmax(s, sink.astype(F32).reshape(1, g, r, 1, 1))
    o = jnp.einsum('bgrqk,bkgd->bqgrd', p.astype(v.dtype), v)
    return o.reshape(b, n, hq * dh)


def _neighbourhood_attention(q, k, v, k_ctx, v_ctx, rpb):
    b, n, h, dh = q.shape
    rows = n // GRID_W
    kh = min(NA_ROWS_MAX, rows)
    r_idx = jnp.arange(rows)
    row_start = jnp.clip(r_idx - kh // 2, 0, rows - kh)
    band_rows = row_start[:, None] + jnp.arange(kh)[None, :]
    qg = q.reshape(b, rows, GRID_W, h, dh)
    kg = k.reshape(b, rows, GRID_W, h, dh)[:, band_rows]
    vg = v.reshape(b, rows, GRID_W, h, dh)[:, band_rows]
    c_idx = jnp.arange(GRID_W)
    col_start = jnp.clip(c_idx - NA_COLS // 2, 0, GRID_W - NA_COLS)
    col_valid = (c_idx[None, :] >= col_start[:, None]) & (c_idx[None, :] < col_start[:, None] + NA_COLS)
    row_rel = band_rows - r_idx[:, None] + NA_ROWS_MAX - 1
    col_rel = jnp.clip(c_idx[None, :] - c_idx[:, None], 1 - NA_COLS, NA_COLS - 1) + NA_COLS - 1
    bias = rpb.astype(F32)[:, row_rel[:, None, :, None], col_rel[None, :, None, :]]
    scale = dh ** -0.5
    s_loc = jnp.einsum('brqhd,brikhd->bhrqik', qg, kg, preferred_element_type=F32) * scale + bias
    s_loc = jnp.where(col_valid[:, None, :], s_loc, NEG_INF).reshape(b, h, rows, GRID_W, kh * GRID_W)
    s_ctx = jnp.einsum('brqhd,bchd->bhrqc', qg, k_ctx, preferred_element_type=F32) * scale
    p = jax.nn.softmax(jnp.concatenate([s_loc, s_ctx], -1), -1).astype(v.dtype)
    n_loc = kh * GRID_W
    p_loc = p[..., :n_loc].reshape(b, h, rows, GRID_W, kh, GRID_W)
    o = (jnp.einsum('bhrqik,brikhd->brqhd', p_loc, vg)
         + jnp.einsum('bhrqc,bchd->brqhd', p[..., n_loc:], v_ctx))
    return o.reshape(b, n, h * dh)


def _global_gqa(q, k, v, k_ctx, v_ctx):
    b, n, hq, dh = q.shape
    g = k.shape[2]
    r = hq // g
    k_all = jnp.concatenate([k_ctx, k], axis=1)
    v_all = jnp.concatenate([v_ctx, v], axis=1)
    qb = jnp.moveaxis(q.reshape(b, n // Q_BLOCK, Q_BLOCK, g, r, dh), 1, 0)

    def attend(qi):
        s = jnp.einsum('bqgrd,bkgd->bgrqk', qi, k_all, preferred_element_type=F32) * (dh ** -0.5)
        p = jax.nn.softmax(s, -1).astype(v_all.dtype)
        return jnp.einsum('bgrqk,bkgd->bqgrd', p, v_all)

    o = lax.map(attend, qb)
    return jnp.moveaxis(o, 0, 1).reshape(b, n, hq * dh)


def _window_gqa(q, k, v, k_ctx, v_ctx, sink):
    b, n, hq, dh = q.shape
    g = k.shape[2]
    r = hq // g
    nb = n // Q_BLOCK
    side = SW_WINDOW // Q_BLOCK
    span = (2 * side + 1) * Q_BLOCK

    def band(t):
        tp = jnp.pad(t, ((0, 0), (SW_WINDOW, SW_WINDOW), (0, 0), (0, 0))).reshape(b, nb + 2 * side, Q_BLOCK, g, dh)
        return jnp.concatenate([tp[:, j:j + nb] for j in range(2 * side + 1)], axis=2)

    kw = band(k)
    vw = band(v)
    blk = jnp.arange(nb)[:, None, None] * Q_BLOCK
    qpos = blk + jnp.arange(Q_BLOCK)[None, :, None]
    kpos = blk - SW_WINDOW + jnp.arange(span)[None, None, :]
    valid = (kpos >= 0) & (kpos < n) & (jnp.abs(kpos - qpos) <= SW_WINDOW)
    qg = q.reshape(b, nb, Q_BLOCK, g, r, dh)
    scale = dh ** -0.5
    s_loc = jnp.einsum('bnqgrd,bnkgd->bgrnqk', qg, kw, preferred_element_type=F32) * scale
    s_loc = jnp.where(valid, s_loc, NEG_INF)
    s_ctx = jnp.einsum('bnqgrd,bcgd->bgrnqc', qg, k_ctx, preferred_element_type=F32) * scale
    p = _sink_softmax(jnp.concatenate([s_loc, s_ctx], -1), sink.astype(F32).reshape(1, g, r, 1, 1, 1)).astype(v.dtype)
    o = (jnp.einsum('bgrnqk,bnkgd->bnqgrd', p[..., :span], vw)
         + jnp.einsum('bgrnqc,bcgd->bnqgrd', p[..., span:], v_ctx))
    return o.reshape(b, n, hq * dh)


def _s5_discretise(lam_re, lam_im, log_step, b_re, b_im):
    lam_re = lam_re.astype(F32)
    lam_im = lam_im.astype(F32)
    step = jnp.exp(log_step.astype(F32))[:, None]
    mag = jnp.exp(lam_re * step)
    ab_re = mag * jnp.cos(lam_im * step)
    ab_im = mag * jnp.sin(lam_im * step)
    den = lam_re * lam_re + lam_im * lam_im
    num_re = ab_re - 1.0
    f_re = ((num_re * lam_re + ab_im * lam_im) / den)[..., None]
    f_im = ((ab_im * lam_re - num_re * lam_im) / den)[..., None]
    b_re = b_re.astype(F32)
    b_im = b_im.astype(F32)
    return ab_re, ab_im, f_re * b_re - f_im * b_im, f_re * b_im + f_im * b_re


def _complex_linear_scan(u, ab_re, ab_im, bb_re, bb_im, reverse):
    bu_re = jnp.einsum('blgh,gph->blgp', u, bb_re)
    bu_im = jnp.einsum('blgh,gph->blgp', u, bb_im)
    a_re = jnp.broadcast_to(ab_re, bu_re.shape)
    a_im = jnp.broadcast_to(ab_im, bu_re.shape)

    def combine(e1, e2):
        a1r, a1i, b1r, b1i = e1
        a2r, a2i, b2r, b2i = e2
        return (a2r * a1r - a2i * a1i, a2r * a1i + a2i * a1r,
                a2r * b1r - a2i * b1i + b2r, a2r * b1i + a2i * b1r + b2i)

    return lax.associative_scan(combine, (a_re, a_im, bu_re, bu_im), axis=1, reverse=reverse)


def _s5_readout(h_re, h_im, c_re, c_im):
    return (jnp.einsum('blgp,ghp->blgh', h_re, c_re.astype(F32))
            - jnp.einsum('blgp,ghp->blgh', h_im, c_im.astype(F32)))


def _s5_mixer(u_lat, u_ctx, lam_re, lam_im, log_step, b_re, b_im, c_re, c_im, d_skip, w_glu, need_ctx):
    b, n, _ = u_lat.shape
    nc = u_ctx.shape[1]
    ul = u_lat.astype(F32).reshape(b, n, SSM_GROUPS, SSM_GROUP_CH)
    uc = u_ctx.astype(F32).reshape(b, nc, SSM_GROUPS, SSM_GROUP_CH)
    d = d_skip.astype(F32).reshape(SSM_GROUPS, SSM_GROUP_CH)
    y_lat = d * ul
    y_ctx = d * uc if need_ctx else None
    for direction in range(2):
        reverse = direction == 1
        ab_re, ab_im, bb_re, bb_im = _s5_discretise(lam_re[direction], lam_im[direction], log_step[direction],
                                                    b_re[direction], b_im[direction])
        _, _, hc_re, hc_im = _complex_linear_scan(uc, ab_re, ab_im, bb_re, bb_im, reverse)
        end = 0 if reverse else nc - 1
        h0_re = hc_re[:, end][:, None]
        h0_im = hc_im[:, end][:, None]
        ac_re, ac_im, hl_re, hl_im = _complex_linear_scan(ul, ab_re, ab_im, bb_re, bb_im, reverse)
        hl_re, hl_im = (hl_re + ac_re * h0_re - ac_im * h0_im,
                        hl_im + ac_re * h0_im + ac_im * h0_re)
        y_lat = y_lat + _s5_readout(hl_re, hl_im, c_re[direction], c_im[direction])
        if need_ctx:
            y_ctx = y_ctx + _s5_readout(hc_re, hc_im, c_re[direction], c_im[direction])

    def glu(y, m):
        y = jax.nn.gelu(y.reshape(b, m, SSM_WIDTH))
        return (y * jax.nn.sigmoid(y @ w_glu.astype(F32))).astype(u_lat.dtype)

    return glu(y_lat, n), (glu(y_ctx, nc) if need_ctx else None)


def _parallel_mixer(a_lat, a_ctx, cos, sin, w_in, w_out, na_rpb, ga_q_norm, ga_k_norm,
                    lam_re, lam_im, log_step, b_re, b_im, c_re, c_im, d_skip, w_glu, sw_sink, need_ctx):
    qa, ka, va, qb, kb, vb, u, qd, kd, vd = _split_cols(a_lat @ w_in)
    qa_c, ka_c, va_c, qb_c, kb_c, vb_c, u_c, qd_c, kd_c, vd_c = _split_cols(a_ctx @ w_in)
    ka_c, va_c = _heads(ka_c), _heads(va_c)
    out_a = _neighbourhood_attention(_heads(qa), _heads(ka), _heads(va), ka_c, va_c, na_rpb)
    kb_c, vb_c = _rms_norm(_heads(kb_c), ga_k_norm), _heads(vb_c)
    qb = _apply_rope(_rms_norm(_heads(qb), ga_q_norm), cos, sin)
    kb = _apply_rope(_rms_norm(_heads(kb), ga_k_norm), cos, sin)
    out_b = _global_gqa(qb, kb, _heads(vb), kb_c, vb_c)
    out_c, ctx_c = _s5_mixer(u, u_c, lam_re, lam_im, log_step, b_re, b_im, c_re, c_im, d_skip, w_glu, need_ctx)
    kd_c, vd_c = _heads(kd_c), _heads(vd_c)
    qd = _apply_rope(_heads(qd), cos, sin)
    kd = _apply_rope(_heads(kd), cos, sin)
    out_d = _window_gqa(qd, kd, _heads(vd), kd_c, vd_c, sw_sink)
    y_lat = jnp.concatenate([out_a, out_b, out_c, out_d], axis=-1) @ w_out
    if not need_ctx:
        return y_lat, None
    ctx_a = _context_attention(_heads(qa_c), ka_c, va_c)
    ctx_b = _context_attention(_rms_norm(_heads(qb_c), ga_q_norm), kb_c, vb_c)
    ctx_d = _context_attention(_heads(qd_c), kd_c, vd_c, sw_sink)
    y_ctx = jnp.concatenate([ctx_a, ctx_b, ctx_c, ctx_d], axis=-1) @ w_out
    return y_lat, y_ctx


def _swiglu(h, w_gate, w_up, w_down):
    return (jax.nn.silu(h @ w_gate) * (h @ w_up)) @ w_down


def _moe_swiglu(h, w_router, b_router, w_gate, w_up, w_down):
    logits = jnp.einsum('bld,de->ble', h, w_router, preferred_element_type=F32) + b_router.astype(F32)
    top_val, top_idx = lax.top_k(logits, TOP_K)
    top_w = jax.nn.softmax(top_val, -1)
    gate = jnp.sum(jax.nn.one_hot(top_idx, N_EXPERTS, dtype=F32) * top_w[..., None], axis=-2)
    out = jnp.zeros(h.shape, F32)
    for e in range(N_EXPERTS):
        out = out + gate[..., e:e + 1] * _swiglu(h, w_gate[e], w_up[e], w_down[e]).astype(F32)
    return out.astype(h.dtype)


def setup_inputs(seed: int = 0) -> dict:
    key = jax.random.key(seed)
    k = jax.random.split(key, 33)
    n_dense = (DEPTH + 1) // 2
    n_moe = DEPTH // 2

    def nrm(kk, shape, std):
        return std * jax.random.normal(kk, shape, F32)

    lam_im_base = jnp.pi * jnp.arange(SSM_STATE, dtype=F32)
    return {
        'x': nrm(k[0], (BATCH, SEQ, D_MODEL), 1.0),
        'c': nrm(k[1], (BATCH, D_MODEL), 1.0),
        'ctx': nrm(k[2], (BATCH, CTX_LEN, D_MODEL), 1.0),
        'c_ctx': nrm(k[3], (D_MODEL,), 1.0),
        'ada_w': nrm(k[4], (DEPTH, D_MODEL, ADA_CHUNKS * D_MODEL), 0.5 * D_MODEL ** -0.5),
        'ada_b': nrm(k[5], (DEPTH, ADA_CHUNKS * D_MODEL), 0.02),
        'w_in': nrm(k[6], (DEPTH, D_MODEL, IN_WIDTH), D_MODEL ** -0.5),
        'w_out': nrm(k[7], (DEPTH, MIX_WIDTH, D_MODEL), DEEPNORM_BETA * MIX_WIDTH ** -0.5),
        'na_rpb': nrm(k[8], (DEPTH, NA_HEADS, 2 * NA_ROWS_MAX - 1, 2 * NA_COLS - 1), 0.02),
        'ga_q_norm': 1.0 + nrm(k[9], (DEPTH, HEAD_DIM), 0.02),
        'ga_k_norm': 1.0 + nrm(k[10], (DEPTH, HEAD_DIM), 0.02),
        'ssm_lambda_re': -0.5 + nrm(k[11], (DEPTH, 2, SSM_GROUPS, SSM_STATE), 0.01),
        'ssm_lambda_im': lam_im_base + nrm(k[12], (DEPTH, 2, SSM_GROUPS, SSM_STATE), 0.01),
        'ssm_log_step': jax.random.uniform(k[13], (DEPTH, 2, SSM_GROUPS), F32, math.log(1e-3), math.log(1e-1)),
        'ssm_b_re': nrm(k[14], (DEPTH, 2, SSM_GROUPS, SSM_STATE, SSM_GROUP_CH), (2 * SSM_GROUP_CH) ** -0.5),
        'ssm_b_im': nrm(k[15], (DEPTH, 2, SSM_GROUPS, SSM_STATE, SSM_GROUP_CH), (2 * SSM_GROUP_CH) ** -0.5),
        'ssm_c_re': nrm(k[16], (DEPTH, 2, SSM_GROUPS, SSM_GROUP_CH, SSM_STATE), SSM_STATE ** -0.5),
        'ssm_c_im': nrm(k[17], (DEPTH, 2, SSM_GROUPS, SSM_GROUP_CH, SSM_STATE), SSM_STATE ** -0.5),
        'ssm_d': nrm(k[18], (DEPTH, SSM_WIDTH), 1.0),
        'ssm_w_glu': nrm(k[19], (DEPTH, SSM_WIDTH, SSM_WIDTH), SSM_WIDTH ** -0.5),
        'sw_sink': nrm(k[20], (DEPTH, SW_HEADS), 0.5),
        'ln1_g': 1.0 + nrm(k[21], (DEPTH, D_MODEL), 0.02),
        'ln1_b': nrm(k[22], (DEPTH, D_MODEL), 0.02),
        'ln2_g': 1.0 + nrm(k[23], (DEPTH, D_MODEL), 0.02),
        'ln2_b': nrm(k[24], (DEPTH, D_MODEL), 0.02),
        'ffn_w_gate': nrm(k[25], (n_dense, D_MODEL, D_FF), D_MODEL ** -0.5),
        'ffn_w_up': nrm(k[26], (n_dense, D_MODEL, D_FF), D_MODEL ** -0.5),
        'ffn_w_down': nrm(k[27], (n_dense, D_FF, D_MODEL), DEEPNORM_BETA * D_FF ** -0.5),
        'moe_w_router': nrm(k[28], (n_moe, D_MODEL, N_EXPERTS), D_MODEL ** -0.5),
        'moe_b_router': nrm(k[29], (n_moe, N_EXPERTS), 0.01),
        'moe_w_gate': nrm(k[30], (n_moe, N_EXPERTS, D_MODEL, D_FF_EXPERT), D_MODEL ** -0.5),
        'moe_w_up': nrm(k[31], (n_moe, N_EXPERTS, D_MODEL, D_FF_EXPERT), D_MODEL ** -0.5),
        'moe_w_down': nrm(k[32], (n_moe, N_EXPERTS, D_FF_EXPERT, D_MODEL), DEEPNORM_BETA * D_FF_EXPERT ** -0.5),
    }


def reference(x, c, ctx, c_ctx, ada_w, ada_b, w_in, w_out, na_rpb, ga_q_norm, ga_k_norm,
              ssm_lambda_re, ssm_lambda_im, ssm_log_step, ssm_b_re, ssm_b_im, ssm_c_re, ssm_c_im,
              ssm_d, ssm_w_glu, sw_sink, ln1_g, ln1_b, ln2_g, ln2_b,
              ffn_w_gate, ffn_w_up, ffn_w_down,
              moe_w_router, moe_b_router, moe_w_gate, moe_w_up, moe_w_down):
    n = x.shape[1]
    cos, sin = _axial_rope_tables(n)
    act_c = jax.nn.silu(c)
    act_cc = jax.nn.silu(c_ctx)
    h, hc = x, ctx
    for layer in range(DEPTH):
        need_ctx = layer < DEPTH - 1
        shift1, scale1, gate1, shift2, scale2, gate2 = [
            m[:, None, :] for m in jnp.split(act_c @ ada_w[layer] + ada_b[layer], ADA_CHUNKS, axis=-1)]
        shift1c, scale1c, gate1c, shift2c, scale2c, gate2c = jnp.split(
            act_cc @ ada_w[layer] + ada_b[layer], ADA_CHUNKS, axis=-1)
        y, yc = _parallel_mixer(
            h * (1.0 + scale1) + shift1, hc * (1.0 + scale1c) + shift1c, cos, sin,
            w_in[layer], w_out[layer], na_rpb[layer], ga_q_norm[layer], ga_k_norm[layer],
            ssm_lambda_re[layer], ssm_lambda_im[layer], ssm_log_step[layer],
            ssm_b_re[layer], ssm_b_im[layer], ssm_c_re[layer], ssm_c_im[layer],
            ssm_d[layer], ssm_w_glu[layer], sw_sink[layer], need_ctx)
        h = _layer_norm(DEEPNORM_ALPHA * h + gate1 * y, ln1_g[layer], ln1_b[layer])
        if need_ctx:
            hc = _layer_norm(DEEPNORM_ALPHA * hc + gate1c * yc, ln1_g[layer], ln1_b[layer])
        f_in = h * (1.0 + scale2) + shift2
        if need_ctx:
            f_in = jnp.concatenate([f_in, hc * (1.0 + scale2c) + shift2c], axis=1)
        i = layer // 2
        if layer % 2 == 0:
            f = _swiglu(f_in, ffn_w_gate[i], ffn_w_up[i], ffn_w_down[i])
        else:
            f = _moe_swiglu(f_in, moe_w_router[i], moe_b_router[i], moe_w_gate[i], moe_w_up[i], moe_w_down[i])
        h = _layer_norm(DEEPNORM_ALPHA * h + gate2 * f[:, :n], ln2_g[layer], ln2_b[layer])
        if need_ctx:
            hc = _layer_norm(DEEPNORM_ALPHA * hc + gate2c * f[:, n:], ln2_g[layer], ln2_b[layer])
    return h
```

```python
import functools
import math

import jax
import jax.numpy as jnp
import numpy as np
from jax import lax
from jax.experimental import pallas as pl
from jax.experimental.pallas import tpu as pltpu

F32 = jnp.float32
BF16 = jnp.bfloat16

D_MODEL = 1024
DEPTH = 4
GRID_W = 64
HEAD_DIM = 64
NA_ROWS = 8
NA_COLS = 16
SSM_GROUPS = 16
SSM_GROUP_CH = 16
SSM_STATE = 64
SSM_WIDTH = SSM_GROUPS * SSM_GROUP_CH
SSM_FLAT = SSM_GROUPS * SSM_STATE
SW_WINDOW = 128
N_EXPERTS = 8
ADA_CHUNKS = 6
ROPE_THETA = 10000.0
LN_EPS = 1e-6
RMS_EPS = 1e-6
NEG_INF = -1e30
DEEPNORM_ALPHA = (2 * DEPTH) ** 0.25
ATT_SCALE = HEAD_DIM ** -0.5

C_QA, C_KA, C_VA, C_QB, C_KB, C_VB, C_U, C_QD, C_KD, C_VD, IN_WIDTH = (
    0, 256, 512, 768, 1024, 1152, 1280, 1536, 1792, 1920, 2048)

LANES = 128
SUBLANES = 8
MOD_ROWS = 16
VMEM_LIMIT = 56 * 1024 * 1024


def _cparams(*sem):
    return pltpu.CompilerParams(dimension_semantics=sem, vmem_limit_bytes=VMEM_LIMIT)


def _dot(a, b):
    return jnp.dot(a, b, preferred_element_type=F32)


def _dot_nt(a, b):
    return lax.dot_general(a, b, (((1,), (1,)), ((), ())), preferred_element_type=F32)


def _dot_tn(a, b):
    return lax.dot_general(a, b, (((0,), (0,)), ((), ())), preferred_element_type=F32)


def _split_bf16(x):
    hi = x.astype(BF16)
    lo = (x - hi.astype(F32)).astype(BF16)
    return hi, lo


def _layer_norm(z, g, b):
    zc = z - jnp.mean(z, axis=-1, keepdims=True)
    y = zc * lax.rsqrt(jnp.mean(zc * zc, axis=-1, keepdims=True) + LN_EPS)
    return y * g + b


def _mods_kernel(c_ref, w_ref, b_ref, o_ref):
    c = c_ref[...]
    act = (c * jax.nn.sigmoid(c)).astype(BF16)
    o_ref[...] = _dot(act, w_ref[...].astype(BF16)) + b_ref[...]


def _mods_call(c_all, ada_w, ada_b):
    tn = 1536
    n = ADA_CHUNKS * D_MODEL
    out = pl.pallas_call(
        _mods_kernel,
        grid=(DEPTH, n // tn),
        in_specs=[
            pl.BlockSpec((MOD_ROWS, D_MODEL), lambda l, j: (0, 0)),
            pl.BlockSpec((None, D_MODEL, tn), lambda l, j: (l, 0, j)),
            pl.BlockSpec((None, 1, tn), lambda l, j: (l, 0, j)),
        ],
        out_specs=pl.BlockSpec((None, MOD_ROWS, tn), lambda l, j: (l, 0, j)),
        out_shape=jax.ShapeDtypeStruct((DEPTH, MOD_ROWS, n), F32),
        compiler_params=_cparams("arbitrary", "arbitrary"),
    )(c_all, ada_w, ada_b.reshape(DEPTH, 1, n))
    return out.reshape(DEPTH * MOD_ROWS * ADA_CHUNKS, 1, D_MODEL)


def _mod_spec(layer, chunk, row_fn):
    def index(i, *_):
        return ((layer * MOD_ROWS + row_fn(i)) * ADA_CHUNKS + chunk, 0, 0)
    return pl.BlockSpec((None, 1, D_MODEL), index)


def _seg_rms(x, seg, g):
    hi, lo = _split_bf16(x * x)
    ss = _dot(hi, seg) + _dot(lo, seg)
    return x * lax.rsqrt(ss * (1.0 / HEAD_DIM) + RMS_EPS) * g


def _rope(x, cos, sin_signed, first_half):
    outs = []
    for j in range(x.shape[1] // LANES):
        xs = x[:, j * LANES:(j + 1) * LANES]
        partner = jnp.where(first_half,
                            pltpu.roll(xs, LANES - HEAD_DIM // 2, 1),
                            pltpu.roll(xs, HEAD_DIM // 2, 1))
        outs.append(xs * cos + partner * sin_signed)
    return outs[0] if len(outs) == 1 else jnp.concatenate(outs, axis=1)


def _in_proj_kernel(x_ref, sh_ref, sc_ref, w_ref, cos_ref, sin_ref, seg_ref, gq_ref, gk_ref,
                    qa_ref, ka_ref, va_ref, qb_ref, kb_ref, vb_ref, u_ref, qd_ref, kd_ref, vd_ref,
                    *, rope):
    a = (x_ref[...] * (1.0 + sc_ref[...]) + sh_ref[...]).astype(BF16)

    def proj(c0, c1):
        return _dot(a, w_ref[:, c0:c1])

    qa_ref[...] = (proj(C_QA, C_KA) * ATT_SCALE).astype(BF16)
    ka_ref[...] = proj(C_KA, C_VA).astype(BF16)
    va_ref[...] = proj(C_VA, C_QB).astype(BF16)
    vb_ref[...] = proj(C_VB, C_U).astype(BF16)
    u_ref[...] = proj(C_U, C_QD)
    vd_ref[...] = proj(C_VD, IN_WIDTH).astype(BF16)

    seg = seg_ref[...]
    qb = _seg_rms(proj(C_QB, C_KB), seg, gq_ref[...])
    kb = _seg_rms(proj(C_KB, C_VB), seg[:LANES, :LANES], gk_ref[...])
    qd = proj(C_QD, C_KD)
    kd = proj(C_KD, C_VD)
    if rope:
        cos = cos_ref[...]
        sin = sin_ref[...]
        lane = lax.broadcasted_iota(jnp.int32, cos.shape, 1)
        first_half = (lane % HEAD_DIM) < (HEAD_DIM // 2)
        qb = _rope(qb, cos, sin, first_half)
        kb = _rope(kb, cos, sin, first_half)
        qd = _rope(qd, cos, sin, first_half)
        kd = _rope(kd, cos, sin, first_half)
    qb_ref[...] = (qb * ATT_SCALE).astype(BF16)
    kb_ref[...] = kb.astype(BF16)
    qd_ref[...] = (qd * ATT_SCALE).astype(BF16)
    kd_ref[...] = kd.astype(BF16)


def _in_proj_call(x, modr, layer, w_in, cos_t, sin_t, seg, gq, gk, *, batch, is_ctx):
    rows = x.shape[0]
    per_batch = rows // batch
    tm = 256 if is_ctx else 512
    nb = per_batch // tm
    row_fn = (lambda i: batch) if is_ctx else (lambda i: i // nb)
    table_spec = pl.BlockSpec((tm, LANES), (lambda i: (0, 0)) if is_ctx else (lambda i: (i % nb, 0)))
    full = lambda shape: pl.BlockSpec(shape, lambda i: (0,) * len(shape))
    rowblk = lambda w: pl.BlockSpec((tm, w), lambda i: (i, 0))
    wide = lambda w, dt: jax.ShapeDtypeStruct((rows, w), dt)
    out_shapes = [wide(256, BF16), wide(256, BF16), wide(256, BF16),
                  wide(256, BF16), wide(128, BF16), wide(128, BF16),
                  jax.ShapeDtypeStruct((per_batch, batch * SSM_WIDTH), F32),
                  wide(256, BF16), wide(128, BF16), wide(128, BF16)]
    out_specs = [rowblk(256), rowblk(256), rowblk(256), rowblk(256), rowblk(128), rowblk(128),
                 pl.BlockSpec((tm, SSM_WIDTH), lambda i: (i % nb, i // nb)),
                 rowblk(256), rowblk(128), rowblk(128)]
    return pl.pallas_call(
        functools.partial(_in_proj_kernel, rope=not is_ctx),
        grid=(rows // tm,),
        in_specs=[rowblk(D_MODEL), _mod_spec(layer, 0, row_fn), _mod_spec(layer, 1, row_fn),
                  full((D_MODEL, IN_WIDTH)), table_spec, table_spec, full((256, 256)),
                  full((1, 256)), full((1, 128))],
        out_specs=out_specs,
        out_shape=out_shapes,
        compiler_params=_cparams("arbitrary"),
    )(x, modr, modr, w_in, cos_t, sin_t, seg, gq, gk)


def _softmax_av(scores, values, sink=None):
    m = functools.reduce(jnp.maximum, [jnp.max(s, axis=-1, keepdims=True) for s in scores])
    if sink is not None:
        m = jnp.maximum(m, sink)
    es = [jnp.exp(s - m) for s in scores]
    den = functools.reduce(jnp.add, [jnp.sum(e, axis=-1, keepdims=True) for e in es])
    if sink is not None:
        den = den + jnp.exp(sink - m)
    o = functools.reduce(jnp.add, [_dot(e.astype(BF16), v) for e, v in zip(es, values)])
    return o * (1.0 / den)


def _head(x, h):
    return x[:, h * HEAD_DIM:(h + 1) * HEAD_DIM]


def _gqa_group(q, g, kparts, vparts, sink_pair=None, masks=None):
    tq = q.shape[0]
    q2 = jnp.concatenate([_head(q, 2 * g), _head(q, 2 * g + 1)], axis=0)
    scores = [_dot_nt(q2, _head(k, g)) for k in kparts]
    if masks is not None:
        scores = [s if mk is None else jnp.where(mk, s, NEG_INF) for s, mk in zip(scores, masks)]
    sink = None
    if sink_pair is not None:
        row = lax.broadcasted_iota(jnp.int32, (2 * tq, 1), 0)
        sink = jnp.where(row < tq, sink_pair[0], sink_pair[1])
    o2 = _softmax_av(scores, [_head(v, g) for v in vparts], sink)
    return jnp.concatenate([o2[:tq], o2[tq:]], axis=1)


def _na_kernel(q_ref, k_ref, v_ref, kc_ref, vc_ref, bias_ref, o_ref):
    r = pl.program_id(1)
    rows = k_ref.shape[0] // GRID_W
    start = pl.multiple_of(jnp.clip(r - NA_ROWS // 2, 0, rows - NA_ROWS) * GRID_W, GRID_W)
    q = q_ref[...]
    kband = k_ref[pl.ds(start, NA_ROWS * GRID_W), :]
    vband = v_ref[pl.ds(start, NA_ROWS * GRID_W), :]
    kc = kc_ref[...]
    vc = vc_ref[...]
    outs = []
    for h in range(4):
        qh = _head(q, h)
        s_loc = _dot_nt(qh, _head(kband, h)) + bias_ref[h]
        s_ctx = _dot_nt(qh, _head(kc, h))
        outs.append(_softmax_av([s_loc, s_ctx], [_head(vband, h), _head(vc, h)]))
    o_ref[...] = jnp.concatenate(outs, axis=1).astype(BF16)


def _na_call(qa, ka, va, ka_c, va_c, bias_tab, *, batch):
    seq = qa.shape[0] // batch
    n_ctx = ka_c.shape[0] // batch
    rows = seq // GRID_W
    lo = NA_ROWS // 2

    def pattern(b, r):
        return (r - jnp.clip(r - lo, 0, rows - NA_ROWS), 0, 0, 0)

    return pl.pallas_call(
        _na_kernel,
        grid=(batch, rows),
        in_specs=[
            pl.BlockSpec((GRID_W, 256), lambda b, r: (b * rows + r, 0)),
            pl.BlockSpec((seq, 256), lambda b, r: (b, 0)),
            pl.BlockSpec((seq, 256), lambda b, r: (b, 0)),
            pl.BlockSpec((n_ctx, 256), lambda b, r: (b, 0)),
            pl.BlockSpec((n_ctx, 256), lambda b, r: (b, 0)),
            pl.BlockSpec((None, 4, GRID_W, NA_ROWS * GRID_W), pattern),
        ],
        out_specs=pl.BlockSpec((GRID_W, 256), lambda b, r: (b * rows + r, 0)),
        out_shape=jax.ShapeDtypeStruct(qa.shape, BF16),
        compiler_params=_cparams("arbitrary", "arbitrary"),
    )(qa, ka, va, ka_c, va_c, bias_tab)


def _na_bias_table(rpb):
    c_idx = jnp.arange(GRID_W)
    col_start = jnp.clip(c_idx - NA_COLS // 2, 0, GRID_W - NA_COLS)
    col_valid = (c_idx[None, :] >= col_start[:, None]) & (c_idx[None, :] < col_start[:, None] + NA_COLS)
    col_rel = jnp.clip(c_idx[None, :] - c_idx[:, None], 1 - NA_COLS, NA_COLS - 1) + NA_COLS - 1
    row_rel = jnp.arange(NA_ROWS)[None, :] - jnp.arange(NA_ROWS)[:, None] + NA_ROWS - 1
    bias = rpb.astype(F32)[:, row_rel[:, None, :, None], col_rel[None, :, None, :]]
    bias = jnp.where(col_valid[None, None, :, None, :], bias, NEG_INF)
    return jnp.transpose(bias, (1, 0, 2, 3, 4)).reshape(NA_ROWS, 4, GRID_W, NA_ROWS * GRID_W)


def _global_kernel(q_ref, k_ref, v_ref, kc_ref, vc_ref, o_ref):
    q = q_ref[...]
    kparts = [kc_ref[...], k_ref[...]]
    vparts = [vc_ref[...], v_ref[...]]
    for g in range(2):
        o_ref[:, g * LANES:(g + 1) * LANES] = _gqa_group(q, g, kparts, vparts).astype(BF16)


def _global_call(qb, kb, vb, kb_c, vb_c, *, batch):
    seq = qb.shape[0] // batch
    n_ctx = kb_c.shape[0] // batch
    tq = 256
    nq = seq // tq
    return pl.pallas_call(
        _global_kernel,
        grid=(batch, nq),
        in_specs=[
            pl.BlockSpec((tq, 256), lambda b, i: (b * nq + i, 0)),
            pl.BlockSpec((seq, 128), lambda b, i: (b, 0)),
            pl.BlockSpec((seq, 128), lambda b, i: (b, 0)),
            pl.BlockSpec((n_ctx, 128), lambda b, i: (b, 0)),
            pl.BlockSpec((n_ctx, 128), lambda b, i: (b, 0)),
        ],
        out_specs=pl.BlockSpec((tq, 256), lambda b, i: (b * nq + i, 0)),
        out_shape=jax.ShapeDtypeStruct(qb.shape, BF16),
        compiler_params=_cparams("arbitrary", "arbitrary"),
    )(qb, kb, vb, kb_c, vb_c)


def _window_kernel(sink_ref, q_ref, k_ref, v_ref, kc_ref, vc_ref, o_ref):
    i = pl.program_id(1)
    tq = q_ref.shape[0]
    seq = k_ref.shape[0]
    span = tq + 2 * SW_WINDOW
    start = pl.multiple_of(jnp.clip(i * tq - SW_WINDOW, 0, seq - span), LANES)
    q = q_ref[...]
    kw = k_ref[pl.ds(start, span), :]
    vw = v_ref[pl.ds(start, span), :]
    qpos = i * tq + lax.broadcasted_iota(jnp.int32, (2 * tq, span), 0) % tq
    kpos = start + lax.broadcasted_iota(jnp.int32, (2 * tq, span), 1)
    valid = jnp.abs(kpos - qpos) <= SW_WINDOW
    for g in range(2):
        o = _gqa_group(q, g, [kw, kc_ref[...]], [vw, vc_ref[...]],
                       sink_pair=(sink_ref[2 * g], sink_ref[2 * g + 1]), masks=[valid, None])
        o_ref[:, g * LANES:(g + 1) * LANES] = o.astype(BF16)


def _window_call(sink, qd, kd, vd, kd_c, vd_c, *, batch):
    seq = qd.shape[0] // batch
    n_ctx = kd_c.shape[0] // batch
    tq = 128
    nq = seq // tq
    return pl.pallas_call(
        _window_kernel,
        grid=(batch, nq),
        in_specs=[
            pl.BlockSpec(memory_space=pltpu.SMEM),
            pl.BlockSpec((tq, 256), lambda b, i: (b * nq + i, 0)),
            pl.BlockSpec((seq, 128), lambda b, i: (b, 0)),
            pl.BlockSpec((seq, 128), lambda b, i: (b, 0)),
            pl.BlockSpec((n_ctx, 128), lambda b, i: (b, 0)),
            pl.BlockSpec((n_ctx, 128), lambda b, i: (b, 0)),
        ],
        out_specs=pl.BlockSpec((tq, 256), lambda b, i: (b * nq + i, 0)),
        out_shape=jax.ShapeDtypeStruct(qd.shape, BF16),
        compiler_params=_cparams("arbitrary", "arbitrary"),
    )(sink, qd, kd, vd, kd_c, vd_c)


def _ctx_attn_kernel(sink_ref, qa_ref, ka_ref, va_ref, qb_ref, kb_ref, vb_ref, qd_ref, kd_ref, vd_ref,
                     oa_ref, ob_ref, od_ref):
    qa, ka, va = qa_ref[...], ka_ref[...], va_ref[...]
    outs = [_softmax_av([_dot_nt(_head(qa, h), _head(ka, h))], [_head(va, h)]) for h in range(4)]
    oa_ref[...] = jnp.concatenate(outs, axis=1).astype(BF16)
    for g in range(2):
        ob_ref[:, g * LANES:(g + 1) * LANES] = _gqa_group(
            qb_ref[...], g, [kb_ref[...]], [vb_ref[...]]).astype(BF16)
        od_ref[:, g * LANES:(g + 1) * LANES] = _gqa_group(
            qd_ref[...], g, [kd_ref[...]], [vd_ref[...]],
            sink_pair=(sink_ref[2 * g], sink_ref[2 * g + 1])).astype(BF16)


def _ctx_attn_call(sink, cx, *, batch):
    n_ctx = cx["qa"].shape[0] // batch
    blk = lambda w: pl.BlockSpec((n_ctx, w), lambda b: (b, 0))
    names = ("qa", "ka", "va", "qb", "kb", "vb", "qd", "kd", "vd")
    widths = (256, 256, 256, 256, 128, 128, 256, 128, 128)
    shape = jax.ShapeDtypeStruct(cx["qa"].shape, BF16)
    return pl.pallas_call(
        _ctx_attn_kernel,
        grid=(batch,),
        in_specs=[pl.BlockSpec(memory_space=pltpu.SMEM)] + [blk(w) for w in widths],
        out_specs=[blk(256)] * 3,
        out_shape=[shape] * 3,
        compiler_params=_cparams("arbitrary"),
    )(sink, *[cx[n] for n in names])


def _s5_discretise(lam_re, lam_im, log_step, b_re, b_im):
    step = jnp.exp(log_step.astype(F32))[:, None]
    lam_re = lam_re.astype(F32)
    lam_im = lam_im.astype(F32)
    mag = jnp.exp(lam_re * step)
    ab_re = mag * jnp.cos(lam_im * step)
    ab_im = mag * jnp.sin(lam_im * step)
    den = lam_re * lam_re + lam_im * lam_im
    num_re = ab_re - 1.0
    f_re = ((num_re * lam_re + ab_im * lam_im) / den)[..., None]
    f_im = ((ab_im * lam_re - num_re * lam_im) / den)[..., None]
    b_re = b_re.astype(F32)
    b_im = b_im.astype(F32)
    return ab_re, ab_im, f_re * b_re - f_im * b_im, f_re * b_im + f_im * b_re


def _s5_operands(lam_re, lam_im, log_step, b_re, b_im, c_re, c_im):
    ab_re, ab_im, bb_re, bb_im = _s5_discretise(lam_re, lam_im, log_step, b_re, b_im)
    eye = jnp.eye(SSM_GROUPS, dtype=F32)
    bd_in = lambda m: jnp.einsum("gph,gk->ghkp", m, eye).reshape(SSM_WIDTH, SSM_FLAT)
    bd_out = lambda m: jnp.einsum("ghp,gk->kpgh", m.astype(F32), eye).reshape(SSM_FLAT, SSM_WIDTH)
    a = jnp.stack([ab_re.reshape(SSM_FLAT), ab_im.reshape(SSM_FLAT)])
    bbd = jnp.concatenate([bd_in(bb_re), bd_in(bb_im)], axis=1).astype(BF16)
    cbd = jnp.concatenate([bd_out(c_re), -bd_out(c_im)], axis=0).astype(BF16)
    return a, bbd, cbd


def _s5_kernel(*refs, reverse, final, batch, n_ctx_chunks):
    if final:
        (uc_ref, ul_ref, pc_ref, pl_ref, a_ref, bbd_ref, cbd_ref, d_ref, wglu_ref,
         oc_ref, ol_ref, bu_ref, h_ref) = refs
    else:
        uc_ref, ul_ref, a_ref, bbd_ref, cbd_ref, oc_ref, ol_ref, bu_ref, h_ref = refs
        pc_ref = pl_ref = d_ref = wglu_ref = None
    c = pl.program_id(0)
    steps = bu_ref.shape[0] // batch

    @pl.when(c == 0)
    def _():
        h_ref[...] = jnp.zeros_like(h_ref)

    def run(u_ref, prev_ref, o_ref):
        u = u_ref[...]
        bu_ref[...] = _dot(u.astype(BF16), bbd_ref[...])
        ar = jnp.broadcast_to(a_ref[0:1, :], (batch, SSM_FLAT))
        ai = jnp.broadcast_to(a_ref[1:2, :], (batch, SSM_FLAT))

        def step(i, h):
            t = steps - 1 - i if reverse else i
            row = pl.multiple_of(t * batch, batch)
            b = bu_ref[pl.ds(row, batch), :]
            hr, hi = h[:, :SSM_FLAT], h[:, SSM_FLAT:]
            nr = ar * hr - ai * hi + b[:, :SSM_FLAT]
            ni = ar * hi + ai * hr + b[:, SSM_FLAT:]
            hn = jnp.concatenate([nr, ni], axis=1)
            bu_ref[pl.ds(row, batch), :] = hn
            return hn

        h_ref[...] = lax.fori_loop(0, steps, step, h_ref[...])
        y = _dot(bu_ref[...].astype(BF16), cbd_ref[...])
        if not final:
            o_ref[...] = y
        else:
            y = jax.nn.gelu(y + prev_ref[...] + d_ref[...] * u)
            z = _dot(y.astype(BF16), wglu_ref[...])
            o_ref[...] = (y * jax.nn.sigmoid(z)).astype(BF16)

    @pl.when(c < n_ctx_chunks)
    def _():
        run(uc_ref, pc_ref, oc_ref)

    @pl.when(c >= n_ctx_chunks)
    def _():
        run(ul_ref, pl_ref, ol_ref)


def _s5_pass(u_ctx, u_lat, prev, a, bbd, cbd, d_skip, w_glu, *, batch, reverse):
    final = prev is not None
    t_chunk = 128
    rows = t_chunk * batch
    ncc = u_ctx.shape[0] // t_chunk
    nlc = u_lat.shape[0] // t_chunk
    flat = lambda x: x.reshape(x.shape[0] * batch, SSM_WIDTH)
    if reverse:
        ctx_idx = lambda c: (jnp.maximum(ncc - 1 - c, 0), 0)
        lat_idx = lambda c: (nlc - 1 - jnp.maximum(c - ncc, 0), 0)
    else:
        ctx_idx = lambda c: (jnp.minimum(c, ncc - 1), 0)
        lat_idx = lambda c: (jnp.maximum(c - ncc, 0), 0)
    full = lambda shape: pl.BlockSpec(shape, lambda c: (0,) * len(shape))
    u_specs = [pl.BlockSpec((rows, SSM_WIDTH), ctx_idx), pl.BlockSpec((rows, SSM_WIDTH), lat_idx)]
    par_specs = [full((2, SSM_FLAT)), full((SSM_WIDTH, 2 * SSM_FLAT)), full((2 * SSM_FLAT, SSM_WIDTH))]
    args = [flat(u_ctx), flat(u_lat)]
    in_specs = list(u_specs)
    if final:
        args += [flat(prev[0]), flat(prev[1])]
        in_specs += u_specs
    args += [a, bbd, cbd]
    in_specs += par_specs
    if final:
        args += [d_skip, w_glu]
        in_specs += [full((1, SSM_WIDTH)), full((SSM_WIDTH, SSM_WIDTH))]
    dt = BF16 if final else F32
    oc, ol = pl.pallas_call(
        functools.partial(_s5_kernel, reverse=reverse, final=final, batch=batch, n_ctx_chunks=ncc),
        grid=(ncc + nlc,),
        in_specs=in_specs,
        out_specs=u_specs,
        out_shape=[jax.ShapeDtypeStruct((u_ctx.shape[0] * batch, SSM_WIDTH), dt),
                   jax.ShapeDtypeStruct((u_lat.shape[0] * batch, SSM_WIDTH), dt)],
        scratch_shapes=[pltpu.VMEM((rows, 2 * SSM_FLAT), F32), pltpu.VMEM((batch, 2 * SSM_FLAT), F32)],
        compiler_params=_cparams("arbitrary"),
    )(*args)
    return oc.reshape(u_ctx.shape), ol.reshape(u_lat.shape)


def _out_proj_kernel(oa_ref, ob_ref, oc_ref, od_ref, w_ref, x_ref, g1_ref, sh2_ref, sc2_ref,
                     lng_ref, lnb_ref, h_ref, f_ref):
    y = (_dot(oa_ref[...], w_ref[0:256, :]) + _dot(ob_ref[...], w_ref[256:512, :])
         + _dot(oc_ref[...], w_ref[512:768, :]) + _dot(od_ref[...], w_ref[768:1024, :]))
    h1 = _layer_norm(DEEPNORM_ALPHA * x_ref[...] + g1_ref[...] * y, lng_ref[...], lnb_ref[...])
    h_ref[...] = h1
    f_ref[...] = (h1 * (1.0 + sc2_ref[...]) + sh2_ref[...]).astype(BF16)


def _out_proj_call(oa, ob, oc_tm, od, w_out, x, modr, layer, lng, lnb, *, batch, is_ctx):
    rows = x.shape[0]
    per_batch = rows // batch
    tm = 256 if is_ctx else 512
    nb = per_batch // tm
    row_fn = (lambda i: batch) if is_ctx else (lambda i: i // nb)
    rowblk = lambda w: pl.BlockSpec((tm, w), lambda i: (i, 0))
    full = lambda shape: pl.BlockSpec(shape, lambda i: (0,) * len(shape))
    return pl.pallas_call(
        _out_proj_kernel,
        grid=(rows // tm,),
        in_specs=[rowblk(256), rowblk(256),
                  pl.BlockSpec((tm, SSM_WIDTH), lambda i: (i % nb, i // nb)),
                  rowblk(256), full((D_MODEL, D_MODEL)), rowblk(D_MODEL),
                  _mod_spec(layer, 2, row_fn), _mod_spec(layer, 3, row_fn), _mod_spec(layer, 4, row_fn),
                  full((1, D_MODEL)), full((1, D_MODEL))],
        out_specs=[rowblk(D_MODEL), rowblk(D_MODEL)],
        out_shape=[jax.ShapeDtypeStruct((rows, D_MODEL), F32), jax.ShapeDtypeStruct((rows, D_MODEL), BF16)],
        compiler_params=_cparams("arbitrary"),
    )(oa, ob, oc_tm, od, w_out, x, modr, modr, modr, lng, lnb)


def _ffn_kernel(f_ref, wg_ref, wu_ref, wd_ref, h_ref, g2_ref, lng_ref, lnb_ref, o_ref, acc_ref):
    j = pl.program_id(1)

    @pl.when(j == 0)
    def _():
        acc_ref[...] = jnp.zeros_like(acc_ref)

    f = f_ref[...]
    g = _dot(f, wg_ref[...])
    u = _dot(f, wu_ref[...])
    acc_ref[...] += _dot((g * jax.nn.sigmoid(g) * u).astype(BF16), wd_ref[...])

    @pl.when(j == pl.num_programs(1) - 1)
    def _():
        o_ref[...] = _layer_norm(DEEPNORM_ALPHA * h_ref[...] + g2_ref[...] * acc_ref[...],
                                 lng_ref[...], lnb_ref[...])


def _ffn_call(f_in, wg, wu, wd, h1, modr, layer, lng, lnb, *, batch, is_ctx):
    rows = h1.shape[0]
    per_batch = rows // batch
    tm = 256 if is_ctx else 512
    d_ff = wg.shape[1]
    tf = d_ff // 2
    nb = per_batch // tm
    row_fn = (lambda i: batch) if is_ctx else (lambda i: i // nb)
    rowblk = lambda w: pl.BlockSpec((tm, w), lambda i, j: (i, 0))
    full = lambda shape: pl.BlockSpec(shape, lambda i, j: (0,) * len(shape))
    return pl.pallas_call(
        _ffn_kernel,
        grid=(rows // tm, d_ff // tf),
        in_specs=[rowblk(D_MODEL),
                  pl.BlockSpec((D_MODEL, tf), lambda i, j: (0, j)),
                  pl.BlockSpec((D_MODEL, tf), lambda i, j: (0, j)),
                  pl.BlockSpec((tf, D_MODEL), lambda i, j: (j, 0)),
                  rowblk(D_MODEL), _mod_spec(layer, 5, row_fn), full((1, D_MODEL)), full((1, D_MODEL))],
        out_specs=rowblk(D_MODEL),
        out_shape=jax.ShapeDtypeStruct((rows, D_MODEL), F32),
        scratch_shapes=[pltpu.VMEM((tm, D_MODEL), F32)],
        compiler_params=_cparams("arbitrary", "arbitrary"),
    )(f_in, wg, wu, wd, h1, modr, lng, lnb)


MOE_TM = 1024
MOE_CHUNK = 256


def _router_kernel(h_ref, sh2_ref, sc2_ref, wr_ref, br_ref, tri_ref, gate_ref, rank_ref, cnt_ref):
    f = h_ref[...] * (1.0 + sc2_ref[...]) + sh2_ref[...]
    f_hi, f_lo = _split_bf16(f)
    w_hi, w_lo = _split_bf16(wr_ref[...])
    logits = _dot_nt(w_hi, f_hi) + _dot_nt(w_hi, f_lo) + _dot_nt(w_lo, f_hi) + br_ref[...]
    ie = lax.broadcasted_iota(jnp.int32, logits.shape, 0)
    m1 = jnp.max(logits, axis=0, keepdims=True)
    i1 = jnp.min(jnp.where(logits == m1, ie, N_EXPERTS), axis=0, keepdims=True)
    rest = jnp.where(ie == i1, -jnp.inf, logits)
    m2 = jnp.max(rest, axis=0, keepdims=True)
    i2 = jnp.min(jnp.where(rest == m2, ie, N_EXPERTS), axis=0, keepdims=True)
    e2 = jnp.exp(m2 - m1)
    den = 1.0 + e2
    gate_ref[...] = jnp.where(ie == i1, 1.0 / den, 0.0) + jnp.where(ie == i2, e2 / den, 0.0)
    sel = jnp.where((ie == i1) | (ie == i2), 1.0, 0.0)
    rank = _dot(sel.astype(BF16), tri_ref[...])
    rank_ref[...] = jnp.where(sel > 0.0, rank, -1.0)
    cnt = jnp.sum(sel, axis=1, keepdims=True)
    cnt_ref[...] = jnp.broadcast_to(cnt, cnt_ref.shape).astype(jnp.int32)


def _router_call(h1, modr, layer, wr_t, b_r, tri, *, batch, is_ctx):
    rows = h1.shape[0]
    tm = MOE_TM
    nblk = rows // tm
    nb = (rows // batch) // tm if not is_ctx else 1
    row_fn = (lambda i: batch) if is_ctx else (lambda i: i // nb)
    full = lambda shape: pl.BlockSpec(shape, lambda i: (0,) * len(shape))
    blk3 = lambda w: pl.BlockSpec((None, N_EXPERTS, w), lambda i: (i, 0, 0))
    return pl.pallas_call(
        _router_kernel,
        grid=(nblk,),
        in_specs=[pl.BlockSpec((tm, D_MODEL), lambda i: (i, 0)),
                  _mod_spec(layer, 3, row_fn), _mod_spec(layer, 4, row_fn),
                  full((N_EXPERTS, D_MODEL)), full((N_EXPERTS, 1)), full((tm, tm))],
        out_specs=[blk3(tm), blk3(tm), blk3(LANES)],
        out_shape=[jax.ShapeDtypeStruct((nblk, N_EXPERTS, tm), F32),
                   jax.ShapeDtypeStruct((nblk, N_EXPERTS, tm), F32),
                   jax.ShapeDtypeStruct((nblk, N_EXPERTS, LANES), jnp.int32)],
        compiler_params=_cparams("arbitrary"),
    )(h1, modr, modr, wr_t, b_r, tri)


def _moe_kernel(cnt_ref, f_ref, gate_ref, rank_ref, wg_ref, wu_ref, wd_ref, h_ref, g2_ref,
                lng_ref, lnb_ref, o_ref, acc_ref, xg_ref):
    i, e, j = pl.program_id(0), pl.program_id(1), pl.program_id(2)
    tm = f_ref.shape[0]
    n_rows = cnt_ref[i * N_EXPERTS + e]
    n_chunks = lax.shift_right_logical(n_rows + (MOE_CHUNK - 1), int(math.log2(MOE_CHUNK)))

    @pl.when((e == 0) & (j == 0))
    def _():
        acc_ref[...] = jnp.zeros_like(acc_ref)

    rank_row = rank_ref[pl.ds(e, 1), :]
    gate_row = gate_ref[pl.ds(e, 1), :]

    def chunk(c, carry):
        base = pl.multiple_of(c * MOE_CHUNK, MOE_CHUNK)
        slot = (lax.broadcasted_iota(jnp.int32, (MOE_CHUNK, tm), 0) + base).astype(F32)
        hit = rank_row == slot
        sel = jnp.where(hit, 1.0, 0.0).astype(BF16)

        @pl.when(j == 0)
        def _():
            xg_ref[pl.ds(base, MOE_CHUNK), :] = _dot(sel, f_ref[...]).astype(BF16)

        xc = xg_ref[pl.ds(base, MOE_CHUNK), :]
        g = _dot(xc, wg_ref[...])
        u = _dot(xc, wu_ref[...])
        y = _dot((g * jax.nn.sigmoid(g) * u).astype(BF16), wd_ref[...])
        y = y * jnp.sum(jnp.where(hit, gate_row, 0.0), axis=1, keepdims=True)
        y_hi, y_lo = _split_bf16(y)
        acc_ref[...] += _dot_tn(sel, y_hi) + _dot_tn(sel, y_lo)
        return carry

    lax.fori_loop(0, n_chunks, chunk, 0)

    @pl.when((e == pl.num_programs(1) - 1) & (j == pl.num_programs(2) - 1))
    def _():
        o_ref[...] = _layer_norm(DEEPNORM_ALPHA * h_ref[...] + g2_ref[...] * acc_ref[...],
                                 lng_ref[...], lnb_ref[...])


def _moe_call(cnt, f_in, gate_t, rank_t, wg, wu, wd, h1, modr, layer, lng, lnb, *, batch, is_ctx):
    rows = h1.shape[0]
    tm = MOE_TM
    d_ff = wg.shape[2]
    tf = d_ff // 4
    nb = (rows // batch) // tm if not is_ctx else 1
    row_fn = (lambda i: batch) if is_ctx else (lambda i: i // nb)
    rowblk = lambda w: pl.BlockSpec((tm, w), lambda i, e, j, c: (i, 0))
    full = lambda shape: pl.BlockSpec(shape, lambda i, e, j, c: (0,) * len(shape))
    blk3 = pl.BlockSpec((None, N_EXPERTS, tm), lambda i, e, j, c: (i, 0, 0))
    grid_spec = pltpu.PrefetchScalarGridSpec(
        num_scalar_prefetch=1,
        grid=(rows // tm, N_EXPERTS, d_ff // tf),
        in_specs=[rowblk(D_MODEL), blk3, blk3,
                  pl.BlockSpec((None, D_MODEL, tf), lambda i, e, j, c: (e, 0, j)),
                  pl.BlockSpec((None, D_MODEL, tf), lambda i, e, j, c: (e, 0, j)),
                  pl.BlockSpec((None, tf, D_MODEL), lambda i, e, j, c: (e, j, 0)),
                  rowblk(D_MODEL), _mod_spec(layer, 5, row_fn), full((1, D_MODEL)), full((1, D_MODEL))],
        out_specs=rowblk(D_MODEL),
        scratch_shapes=[pltpu.VMEM((tm, D_MODEL), F32), pltpu.VMEM((tm, D_MODEL), BF16)],
    )
    return pl.pallas_call(
        _moe_kernel,
        grid_spec=grid_spec,
        out_shape=jax.ShapeDtypeStruct((rows, D_MODEL), F32),
        compiler_params=_cparams("arbitrary", "arbitrary", "arbitrary"),
    )(cnt, f_in, gate_t, rank_t, wg, wu, wd, h1, modr, lng, lnb)


def _rope_tables(seq):
    pos = jnp.arange(seq, dtype=jnp.int32)
    row = (pos // GRID_W).astype(F32)
    col = (pos % GRID_W).astype(F32)
    n_freq = HEAD_DIM // 4
    inv_freq = ROPE_THETA ** (-jnp.arange(n_freq, dtype=F32) / n_freq)
    ang = jnp.concatenate([row[:, None] * inv_freq, col[:, None] * inv_freq], axis=-1)
    cos, sin = jnp.cos(ang), jnp.sin(ang)
    reps = LANES // (HEAD_DIM // 2)
    sign = jnp.tile(jnp.concatenate([-jnp.ones(HEAD_DIM // 2, F32), jnp.ones(HEAD_DIM // 2, F32)]),
                    LANES // HEAD_DIM)
    return jnp.tile(cos, (1, reps)), jnp.tile(sin, (1, reps)) * sign


def kernel(x, c, ctx, c_ctx, ada_w, ada_b, w_in, w_out, na_rpb, ga_q_norm, ga_k_norm,
           ssm_lambda_re, ssm_lambda_im, ssm_log_step, ssm_b_re, ssm_b_im, ssm_c_re, ssm_c_im,
           ssm_d, ssm_w_glu, sw_sink, ln1_g, ln1_b, ln2_g, ln2_b,
           ffn_w_gate, ffn_w_up, ffn_w_down,
           moe_w_router, moe_b_router, moe_w_gate, moe_w_up, moe_w_down):
    batch, seq, _ = x.shape
    n_ctx = ctx.shape[1]
    assert batch < MOD_ROWS and seq % 512 == 0 and n_ctx % 256 == 0

    c_all = jnp.zeros((MOD_ROWS, D_MODEL), F32).at[:batch].set(c).at[batch].set(c_ctx)
    modr = _mods_call(c_all, ada_w, ada_b)

    cos_t, sin_t = _rope_tables(seq)
    seg = jnp.kron(jnp.eye(256 // HEAD_DIM, dtype=F32), jnp.ones((HEAD_DIM, HEAD_DIM), F32)).astype(BF16)
    tri = (jnp.arange(MOE_TM)[:, None] < jnp.arange(MOE_TM)[None, :]).astype(BF16)

    h = x.reshape(batch * seq, D_MODEL)
    hc = ctx.reshape(batch * n_ctx, D_MODEL)
    names = ("qa", "ka", "va", "qb", "kb", "vb", "u", "qd", "kd", "vd")

    for layer in range(DEPTH):
        need_ctx = layer < DEPTH - 1
        w_in_l = w_in[layer].astype(BF16)
        w_out_l = w_out[layer].astype(BF16)
        gq = jnp.tile(ga_q_norm[layer].astype(F32), 256 // HEAD_DIM)[None, :]
        gk = jnp.tile(ga_k_norm[layer].astype(F32), 128 // HEAD_DIM)[None, :]
        sink = sw_sink[layer].astype(F32)
        ln1 = (ln1_g[layer][None, :], ln1_b[layer][None, :])
        ln2 = (ln2_g[layer][None, :], ln2_b[layer][None, :])

        lat = dict(zip(names, _in_proj_call(h, modr, layer, w_in_l, cos_t, sin_t, seg, gq, gk,
                                            batch=batch, is_ctx=False)))
        cx = dict(zip(names, _in_proj_call(hc, modr, layer, w_in_l, cos_t, sin_t, seg, gq, gk,
                                           batch=batch, is_ctx=True)))

        out_a = _na_call(lat["qa"], lat["ka"], lat["va"], cx["ka"], cx["va"],
                         _na_bias_table(na_rpb[layer]), batch=batch)
        out_b = _global_call(lat["qb"], lat["kb"], lat["vb"], cx["kb"], cx["vb"], batch=batch)
        out_d = _window_call(sink, lat["qd"], lat["kd"], lat["vd"], cx["kd"], cx["vd"], batch=batch)

        dirs = [_s5_operands(ssm_lambda_re[layer, d], ssm_lambda_im[layer, d], ssm_log_step[layer, d],
                             ssm_b_re[layer, d], ssm_b_im[layer, d], ssm_c_re[layer, d], ssm_c_im[layer, d])
                for d in range(2)]
        fwd = _s5_pass(cx["u"], lat["u"], None, *dirs[0], None, None, batch=batch, reverse=False)
        ctx_c, out_c = _s5_pass(cx["u"], lat["u"], fwd, *dirs[1], ssm_d[layer].astype(F32)[None, :],
                                ssm_w_glu[layer].astype(BF16), batch=batch, reverse=True)

        h1, f_in = _out_proj_call(out_a, out_b, out_c, out_d, w_out_l, h, modr, layer, *ln1,
                                  batch=batch, is_ctx=False)
        streams = [(h1, f_in, False)]
        if need_ctx:
            ctx_a, ctx_b, ctx_d = _ctx_attn_call(sink, cx, batch=batch)
            hc1, fc_in = _out_proj_call(ctx_a, ctx_b, ctx_c, ctx_d, w_out_l, hc, modr, layer, *ln1,
                                        batch=batch, is_ctx=True)
            streams.append((hc1, fc_in, True))

        i = layer // 2
        outs = []
        if layer % 2 == 0:
            wg, wu, wd = (ffn_w_gate[i].astype(BF16), ffn_w_up[i].astype(BF16), ffn_w_down[i].astype(BF16))
            for s1, sf, is_ctx in streams:
                outs.append(_ffn_call(sf, wg, wu, wd, s1, modr, layer, *ln2, batch=batch, is_ctx=is_ctx))
        else:
            wg, wu, wd = (moe_w_gate[i].astype(BF16), moe_w_up[i].astype(BF16), moe_w_down[i].astype(BF16))
            wr_t = moe_w_router[i].astype(F32).T
            b_r = moe_b_router[i].astype(F32)[:, None]
            for s1, sf, is_ctx in streams:
                gate_t, rank_t, cnt = _router_call(s1, modr, layer, wr_t, b_r, tri, batch=batch, is_ctx=is_ctx)
                outs.append(_moe_call(cnt[:, :, 0].reshape(-1), sf, gate_t, rank_t, wg, wu, wd, s1, modr,
                                      layer, *ln2, batch=batch, is_ctx=is_ctx))
        h = outs[0]
        if need_ctx:
            hc = outs[1]
    return h.reshape(batch, seq, D_MODEL)
```

```python
import functools

import jax
import jax.numpy as jnp
from jax import lax
from jax.experimental import pallas as pl
from jax.experimental.pallas import tpu as pltpu

F32 = jnp.float32
BF16 = jnp.bfloat16

D_MODEL = 1024
DEPTH = 4
GRID_W = 64
HEAD_DIM = 64
NA_ROWS = 8
NA_COLS = 16
SSM_GROUPS = 16
SSM_GROUP_CH = 16
SSM_STATE = 64
SSM_WIDTH = SSM_GROUPS * SSM_GROUP_CH
SSM_FLAT = SSM_GROUPS * SSM_STATE
SW_WINDOW = 128
N_EXPERTS = 8
ADA_CHUNKS = 6
ROPE_THETA = 10000.0
LN_EPS = 1e-6
RMS_EPS = 1e-6
NEG_INF = -1e30
DEEPNORM_ALPHA = (2 * DEPTH) ** 0.25
ATT_SCALE = HEAD_DIM ** -0.5

C_QA, C_KA, C_VA, C_QB, C_KB, C_VB, C_U, C_QD, C_KD, C_VD, IN_WIDTH = (
    0, 256, 512, 768, 1024, 1152, 1280, 1536, 1792, 1920, 2048)

LANES = 128
SUBLANES = 8
MOD_ROWS = 16
VMEM_LIMIT = 56 * 1024 * 1024


def _cparams(*sem):
    return pltpu.CompilerParams(dimension_semantics=sem, vmem_limit_bytes=VMEM_LIMIT)


def _dot(a, b):
    return jnp.dot(a, b, preferred_element_type=F32)


def _dot_nt(a, b):
    return lax.dot_general(a, b, (((1,), (1,)), ((), ())), preferred_element_type=F32)


def _dot_tn(a, b):
    return lax.dot_general(a, b, (((0,), (0,)), ((), ())), preferred_element_type=F32)


def _split_bf16(x):
    hi = x.astype(BF16)
    lo = (x - hi.astype(F32)).astype(BF16)
    return hi, lo


def _layer_norm(z, g, b):
    zc = z - jnp.mean(z, axis=-1, keepdims=True)
    y = zc * lax.rsqrt(jnp.mean(zc * zc, axis=-1, keepdims=True) + LN_EPS)
    return y * g + b


def _mods_kernel(c_ref, w_ref, b_ref, o_ref):
    c = c_ref[...]
    act = (c * jax.nn.sigmoid(c)).astype(BF16)
    o_ref[...] = _dot(act, w_ref[...].astype(BF16)) + b_ref[...]


def _mods_call(c_all, ada_w, ada_b):
    tn = 1536
    n = ADA_CHUNKS * D_MODEL
    out = pl.pallas_call(
        _mods_kernel,
        grid=(DEPTH, n // tn),
        in_specs=[
            pl.BlockSpec((MOD_ROWS, D_MODEL), lambda l, j: (0, 0)),
            pl.BlockSpec((None, D_MODEL, tn), lambda l, j: (l, 0, j)),
            pl.BlockSpec((None, 1, tn), lambda l, j: (l, 0, j)),
        ],
        out_specs=pl.BlockSpec((None, MOD_ROWS, tn), lambda l, j: (l, 0, j)),
        out_shape=jax.ShapeDtypeStruct((DEPTH, MOD_ROWS, n), F32),
        name="mods",
        compiler_params=_cparams("arbitrary", "arbitrary"),
    )(c_all, ada_w, ada_b.reshape(DEPTH, 1, n))
    return out.reshape(DEPTH * MOD_ROWS * ADA_CHUNKS, 1, D_MODEL)


def _mod_spec(layer, chunk, row_fn):
    def index(i, *_):
        return ((layer * MOD_ROWS + row_fn(i)) * ADA_CHUNKS + chunk, 0, 0)
    return pl.BlockSpec((None, 1, D_MODEL), index)


def _seg_rms(x, seg, g):
    hi, lo = _split_bf16(x * x)
    ss = _dot(hi, seg) + _dot(lo, seg)
    return x * lax.rsqrt(ss * (1.0 / HEAD_DIM) + RMS_EPS) * g


def _rope(x, cos, sin_signed, first_half):
    outs = []
    for j in range(x.shape[1] // LANES):
        xs = x[:, j * LANES:(j + 1) * LANES]
        partner = jnp.where(first_half,
                            pltpu.roll(xs, LANES - HEAD_DIM // 2, 1),
                            pltpu.roll(xs, HEAD_DIM // 2, 1))
        outs.append(xs * cos + partner * sin_signed)
    return outs[0] if len(outs) == 1 else jnp.concatenate(outs, axis=1)


def _in_proj_kernel(x_ref, sh_ref, sc_ref, w_ref, cos_ref, sin_ref, seg_ref, gq_ref, gk_ref,
                    qa_ref, ka_ref, va_ref, qb_ref, kb_ref, vb_ref, u_ref, qd_ref, kd_ref, vd_ref,
                    *, rope):
    a = (x_ref[...] * (1.0 + sc_ref[...]) + sh_ref[...]).astype(BF16)

    def proj(c0, c1):
        return _dot(a, w_ref[:, c0:c1])

    qa_ref[...] = (proj(C_QA, C_KA) * ATT_SCALE).astype(BF16)
    ka_ref[...] = proj(C_KA, C_VA).astype(BF16)
    va_ref[...] = proj(C_VA, C_QB).astype(BF16)
    vb_ref[...] = proj(C_VB, C_U).astype(BF16)
    u_ref[...] = proj(C_U, C_QD)
    vd_ref[...] = proj(C_VD, IN_WIDTH).astype(BF16)

    seg = seg_ref[...]
    qb = _seg_rms(proj(C_QB, C_KB), seg, gq_ref[...])
    kb = _seg_rms(proj(C_KB, C_VB), seg[:LANES, :LANES], gk_ref[...])
    qd = proj(C_QD, C_KD)
    kd = proj(C_KD, C_VD)
    if rope:
        cos = cos_ref[...]
        sin = sin_ref[...]
        lane = lax.broadcasted_iota(jnp.int32, cos.shape, 1)
        first_half = (lane % HEAD_DIM) < (HEAD_DIM // 2)
        qb = _rope(qb, cos, sin, first_half)
        kb = _rope(kb, cos, sin, first_half)
        qd = _rope(qd, cos, sin, first_half)
        kd = _rope(kd, cos, sin, first_half)
    qb_ref[...] = (qb * ATT_SCALE).astype(BF16)
    kb_ref[...] = kb.astype(BF16)
    qd_ref[...] = (qd * ATT_SCALE).astype(BF16)
    kd_ref[...] = kd.astype(BF16)


def _in_proj_call(x, modr, layer, w_in, cos_t, sin_t, seg, gq, gk, *, batch, is_ctx):
    rows = x.shape[0]
    per_batch = rows // batch
    tm = 256 if is_ctx else 512
    nb = per_batch // tm
    row_fn = (lambda i: batch) if is_ctx else (lambda i: i // nb)
    table_spec = pl.BlockSpec((tm, LANES), (lambda i: (0, 0)) if is_ctx else (lambda i: (i % nb, 0)))
    full = lambda shape: pl.BlockSpec(shape, lambda i: (0,) * len(shape))
    rowblk = lambda w: pl.BlockSpec((tm, w), lambda i: (i, 0))
    wide = lambda w, dt: jax.ShapeDtypeStruct((rows, w), dt)
    out_shapes = [wide(256, BF16), wide(256, BF16), wide(256, BF16),
                  wide(256, BF16), wide(128, BF16), wide(128, BF16),
                  jax.ShapeDtypeStruct((per_batch, batch * SSM_WIDTH), F32),
                  wide(256, BF16), wide(128, BF16), wide(128, BF16)]
    out_specs = [rowblk(256), rowblk(256), rowblk(256), rowblk(256), rowblk(128), rowblk(128),
                 pl.BlockSpec((tm, SSM_WIDTH), lambda i: (i % nb, i // nb)),
                 rowblk(256), rowblk(128), rowblk(128)]
    return pl.pallas_call(
        functools.partial(_in_proj_kernel, rope=not is_ctx),
        grid=(rows // tm,),
        in_specs=[rowblk(D_MODEL), _mod_spec(layer, 0, row_fn), _mod_spec(layer, 1, row_fn),
                  full((D_MODEL, IN_WIDTH)), table_spec, table_spec, full((256, 256)),
                  full((1, 256)), full((1, 128))],
        out_specs=out_specs,
        out_shape=out_shapes,
        name="in_proj_ctx" if is_ctx else "in_proj",
        compiler_params=_cparams("arbitrary"),
    )(x, modr, modr, w_in, cos_t, sin_t, seg, gq, gk)


def _softmax_av(scores, values, sink=None):
    m = functools.reduce(jnp.maximum, [jnp.max(s, axis=-1, keepdims=True) for s in scores])
    if sink is not None:
        m = jnp.maximum(m, sink)
    es = [jnp.exp(s - m) for s in scores]
    den = functools.reduce(jnp.add, [jnp.sum(e, axis=-1, keepdims=True) for e in es])
    if sink is not None:
        den = den + jnp.exp(sink - m)
    o = functools.reduce(jnp.add, [_dot(e.astype(BF16), v) for e, v in zip(es, values)])
    return o * (1.0 / den)


def _head(x, h):
    return x[:, h * HEAD_DIM:(h + 1) * HEAD_DIM]


def _gqa_group(q, g, kparts, vparts, sink_pair=None, masks=None):
    tq = q.shape[0]
    q2 = jnp.concatenate([_head(q, 2 * g), _head(q, 2 * g + 1)], axis=0)
    scores = [_dot_nt(q2, _head(k, g)) for k in kparts]
    if masks is not None:
        scores = [s if mk is None else jnp.where(mk, s, NEG_INF) for s, mk in zip(scores, masks)]
    sink = None
    if sink_pair is not None:
        row = lax.broadcasted_iota(jnp.int32, (2 * tq, 1), 0)
        sink = jnp.where(row < tq, sink_pair[0], sink_pair[1])
    o2 = _softmax_av(scores, [_head(v, g) for v in vparts], sink)
    return jnp.concatenate([o2[:tq], o2[tq:]], axis=1)


def _na_kernel(q_ref, k_ref, v_ref, kc_ref, vc_ref, bias_ref, o_ref):
    r = pl.program_id(1)
    rows = k_ref.shape[0] // GRID_W
    start = pl.multiple_of(jnp.clip(r - NA_ROWS // 2, 0, rows - NA_ROWS) * GRID_W, GRID_W)
    q = q_ref[...]
    kband = k_ref[pl.ds(start, NA_ROWS * GRID_W), :]
    vband = v_ref[pl.ds(start, NA_ROWS * GRID_W), :]
    kc = kc_ref[...]
    vc = vc_ref[...]
    outs = []
    for h in range(4):
        qh = _head(q, h)
        s_loc = _dot_nt(qh, _head(kband, h)) + bias_ref[h]
        s_ctx = _dot_nt(qh, _head(kc, h))
        outs.append(_softmax_av([s_loc, s_ctx], [_head(vband, h), _head(vc, h)]))
    o_ref[...] = jnp.concatenate(outs, axis=1).astype(BF16)


def _na_call(qa, ka, va, ka_c, va_c, bias_tab, *, batch):
    seq = qa.shape[0] // batch
    n_ctx = ka_c.shape[0] // batch
    rows = seq // GRID_W
    lo = NA_ROWS // 2

    def pattern(b, r):
        return (r - jnp.clip(r - lo, 0, rows - NA_ROWS), 0, 0, 0)

    return pl.pallas_call(
        _na_kernel,
        grid=(batch, rows),
        in_specs=[
            pl.BlockSpec((GRID_W, 256), lambda b, r: (b * rows + r, 0)),
            pl.BlockSpec((seq, 256), lambda b, r: (b, 0)),
            pl.BlockSpec((seq, 256), lambda b, r: (b, 0)),
            pl.BlockSpec((n_ctx, 256), lambda b, r: (b, 0)),
            pl.BlockSpec((n_ctx, 256), lambda b, r: (b, 0)),
            pl.BlockSpec((None, 4, GRID_W, NA_ROWS * GRID_W), pattern),
        ],
        out_specs=pl.BlockSpec((GRID_W, 256), lambda b, r: (b * rows + r, 0)),
        out_shape=jax.ShapeDtypeStruct(qa.shape, BF16),
        name="na_attn",
        compiler_params=_cparams("arbitrary", "arbitrary"),
    )(qa, ka, va, ka_c, va_c, bias_tab)


def _na_bias_table(rpb):
    c_idx = jnp.arange(GRID_W)
    col_start = jnp.clip(c_idx - NA_COLS // 2, 0, GRID_W - NA_COLS)
    col_valid = (c_idx[None, :] >= col_start[:, None]) & (c_idx[None, :] < col_start[:, None] + NA_COLS)
    col_rel = jnp.clip(c_idx[None, :] - c_idx[:, None], 1 - NA_COLS, NA_COLS - 1) + NA_COLS - 1
    row_rel = jnp.arange(NA_ROWS)[None, :] - jnp.arange(NA_ROWS)[:, None] + NA_ROWS - 1
    row_sel = jax.nn.one_hot(row_rel, 2 * NA_ROWS - 1, dtype=F32)
    col_sel = jax.nn.one_hot(col_rel, 2 * NA_COLS - 1, dtype=F32)
    exact = lax.Precision.HIGHEST
    by_col = jnp.einsum("hab,qkb->haqk", rpb.astype(F32), col_sel, precision=exact)
    bias = jnp.einsum("pia,haqk->phqik", row_sel, by_col, precision=exact)
    bias = jnp.where(col_valid[None, None, :, None, :], bias, NEG_INF)
    return bias.reshape(NA_ROWS, 4, GRID_W, NA_ROWS * GRID_W)


def _global_kernel(q_ref, k_ref, v_ref, kc_ref, vc_ref, o_ref):
    q = q_ref[...]
    kparts = [kc_ref[...], k_ref[...]]
    vparts = [vc_ref[...], v_ref[...]]
    for g in range(2):
        o_ref[:, g * LANES:(g + 1) * LANES] = _gqa_group(q, g, kparts, vparts).astype(BF16)


def _global_call(qb, kb, vb, kb_c, vb_c, *, batch):
    seq = qb.shape[0] // batch
    n_ctx = kb_c.shape[0] // batch
    tq = 256
    nq = seq // tq
    return pl.pallas_call(
        _global_kernel,
        grid=(batch, nq),
        in_specs=[
            pl.BlockSpec((tq, 256), lambda b, i: (b * nq + i, 0)),
            pl.BlockSpec((seq, 128), lambda b, i: (b, 0)),
            pl.BlockSpec((seq, 128), lambda b, i: (b, 0)),
            pl.BlockSpec((n_ctx, 128), lambda b, i: (b, 0)),
            pl.BlockSpec((n_ctx, 128), lambda b, i: (b, 0)),
        ],
        out_specs=pl.BlockSpec((tq, 256), lambda b, i: (b * nq + i, 0)),
        out_shape=jax.ShapeDtypeStruct(qb.shape, BF16),
        name="global_attn",
        compiler_params=_cparams("arbitrary", "arbitrary"),
    )(qb, kb, vb, kb_c, vb_c)


def _window_kernel(sink_ref, q_ref, k_ref, v_ref, kc_ref, vc_ref, o_ref):
    i = pl.program_id(1)
    tq = q_ref.shape[0]
    seq = k_ref.shape[0]
    span = tq + 2 * SW_WINDOW
    start = pl.multiple_of(jnp.clip(i * tq - SW_WINDOW, 0, seq - span), LANES)
    q = q_ref[...]
    kw = k_ref[pl.ds(start, span), :]
    vw = v_ref[pl.ds(start, span), :]
    qpos = i * tq + lax.broadcasted_iota(jnp.int32, (2 * tq, span), 0) % tq
    kpos = start + lax.broadcasted_iota(jnp.int32, (2 * tq, span), 1)
    valid = jnp.abs(kpos - qpos) <= SW_WINDOW
    for g in range(2):
        o = _gqa_group(q, g, [kw, kc_ref[...]], [vw, vc_ref[...]],
                       sink_pair=(sink_ref[2 * g], sink_ref[2 * g + 1]), masks=[valid, None])
        o_ref[:, g * LANES:(g + 1) * LANES] = o.astype(BF16)


def _window_call(sink, qd, kd, vd, kd_c, vd_c, *, batch):
    seq = qd.shape[0] // batch
    n_ctx = kd_c.shape[0] // batch
    tq = 128
    nq = seq // tq
    return pl.pallas_call(
        _window_kernel,
        grid=(batch, nq),
        in_specs=[
            pl.BlockSpec(memory_space=pltpu.SMEM),
            pl.BlockSpec((tq, 256), lambda b, i: (b * nq + i, 0)),
            pl.BlockSpec((seq, 128), lambda b, i: (b, 0)),
            pl.BlockSpec((seq, 128), lambda b, i: (b, 0)),
            pl.BlockSpec((n_ctx, 128), lambda b, i: (b, 0)),
            pl.BlockSpec((n_ctx, 128), lambda b, i: (b, 0)),
        ],
        out_specs=pl.BlockSpec((tq, 256), lambda b, i: (b * nq + i, 0)),
        out_shape=jax.ShapeDtypeStruct(qd.shape, BF16),
        name="window_attn",
        compiler_params=_cparams("arbitrary", "arbitrary"),
    )(sink, qd, kd, vd, kd_c, vd_c)


def _ctx_attn_kernel(sink_ref, qa_ref, ka_ref, va_ref, qb_ref, kb_ref, vb_ref, qd_ref, kd_ref, vd_ref,
                     oa_ref, ob_ref, od_ref):
    qa, ka, va = qa_ref[...], ka_ref[...], va_ref[...]
    outs = [_softmax_av([_dot_nt(_head(qa, h), _head(ka, h))], [_head(va, h)]) for h in range(4)]
    oa_ref[...] = jnp.concatenate(outs, axis=1).astype(BF16)
    for g in range(2):
        ob_ref[:, g * LANES:(g + 1) * LANES] = _gqa_group(
            qb_ref[...], g, [kb_ref[...]], [vb_ref[...]]).astype(BF16)
        od_ref[:, g * LANES:(g + 1) * LANES] = _gqa_group(
            qd_ref[...], g, [kd_ref[...]], [vd_ref[...]],
            sink_pair=(sink_ref[2 * g], sink_ref[2 * g + 1])).astype(BF16)


def _ctx_attn_call(sink, cx, *, batch):
    n_ctx = cx["qa"].shape[0] // batch
    blk = lambda w: pl.BlockSpec((n_ctx, w), lambda b: (b, 0))
    names = ("qa", "ka", "va", "qb", "kb", "vb", "qd", "kd", "vd")
    widths = (256, 256, 256, 256, 128, 128, 256, 128, 128)
    shape = jax.ShapeDtypeStruct(cx["qa"].shape, BF16)
    return pl.pallas_call(
        _ctx_attn_kernel,
        grid=(batch,),
        in_specs=[pl.BlockSpec(memory_space=pltpu.SMEM)] + [blk(w) for w in widths],
        out_specs=[blk(256)] * 3,
        out_shape=[shape] * 3,
        name="ctx_attn",
        compiler_params=_cparams("arbitrary"),
    )(sink, *[cx[n] for n in names])


def _s5_discretise(lam_re, lam_im, log_step, b_re, b_im):
    step = jnp.exp(log_step.astype(F32))[:, None]
    lam_re = lam_re.astype(F32)
    lam_im = lam_im.astype(F32)
    mag = jnp.exp(lam_re * step)
    ab_re = mag * jnp.cos(lam_im * step)
    ab_im = mag * jnp.sin(lam_im * step)
    den = lam_re * lam_re + lam_im * lam_im
    num_re = ab_re - 1.0
    f_re = ((num_re * lam_re + ab_im * lam_im) / den)[..., None]
    f_im = ((ab_im * lam_re - num_re * lam_im) / den)[..., None]
    b_re = b_re.astype(F32)
    b_im = b_im.astype(F32)
    return ab_re, ab_im, f_re * b_re - f_im * b_im, f_re * b_im + f_im * b_re


def _s5_operands(lam_re, lam_im, log_step, b_re, b_im, c_re, c_im):
    ab_re, ab_im, bb_re, bb_im = _s5_discretise(lam_re, lam_im, log_step, b_re, b_im)
    eye = jnp.eye(SSM_GROUPS, dtype=F32)
    bd_in = lambda m: jnp.einsum("gph,gk->ghkp", m, eye).reshape(SSM_WIDTH, SSM_FLAT)
    bd_out = lambda m: jnp.einsum("ghp,gk->kpgh", m.astype(F32), eye).reshape(SSM_FLAT, SSM_WIDTH)
    a = jnp.stack([ab_re.reshape(SSM_FLAT), ab_im.reshape(SSM_FLAT)])
    bbd = jnp.concatenate([bd_in(bb_re), bd_in(bb_im)], axis=1).astype(BF16)
    cbd = jnp.concatenate([bd_out(c_re), -bd_out(c_im)], axis=0).astype(BF16)
    return a, bbd, cbd


def _s5_kernel(*refs, reverse, final, batch, n_ctx_chunks):
    if final:
        (uc_ref, ul_ref, pc_ref, pl_ref, a_ref, bbd_ref, cbd_ref, d_ref, wglu_ref,
         oc_ref, ol_ref, bu_ref, h_ref) = refs
    else:
        uc_ref, ul_ref, a_ref, bbd_ref, cbd_ref, oc_ref, ol_ref, bu_ref, h_ref = refs
        pc_ref = pl_ref = d_ref = wglu_ref = None
    c = pl.program_id(0)
    steps = bu_ref.shape[0] // batch

    @pl.when(c == 0)
    def _():
        h_ref[...] = jnp.zeros_like(h_ref)

    def run(u_ref, prev_ref, o_ref):
        u = u_ref[...]
        bu_ref[...] = _dot(u.astype(BF16), bbd_ref[...])
        ar = jnp.broadcast_to(a_ref[0:1, :], (batch, SSM_FLAT))
        ai = jnp.broadcast_to(a_ref[1:2, :], (batch, SSM_FLAT))

        def step(i, h):
            t = steps - 1 - i if reverse else i
            row = pl.multiple_of(t * batch, batch)
            b = bu_ref[pl.ds(row, batch), :]
            hr, hi = h[:, :SSM_FLAT], h[:, SSM_FLAT:]
            nr = ar * hr - ai * hi + b[:, :SSM_FLAT]
            ni = ar * hi + ai * hr + b[:, SSM_FLAT:]
            hn = jnp.concatenate([nr, ni], axis=1)
            bu_ref[pl.ds(row, batch), :] = hn
            return hn

        h_ref[...] = lax.fori_loop(0, steps, step, h_ref[...])
        y = _dot(bu_ref[...].astype(BF16), cbd_ref[...])
        if not final:
            o_ref[...] = y
        else:
            y = jax.nn.gelu(y + prev_ref[...] + d_ref[...] * u)
            z = _dot(y.astype(BF16), wglu_ref[...])
            o_ref[...] = (y * jax.nn.sigmoid(z)).astype(BF16)

    @pl.when(c < n_ctx_chunks)
    def _():
        run(uc_ref, pc_ref, oc_ref)

    @pl.when(c >= n_ctx_chunks)
    def _():
        run(ul_ref, pl_ref, ol_ref)


def _s5_pass(u_ctx, u_lat, prev, a, bbd, cbd, d_skip, w_glu, *, batch, reverse):
    final = prev is not None
    t_chunk = 128
    rows = t_chunk * batch
    ncc = u_ctx.shape[0] // t_chunk
    nlc = u_lat.shape[0] // t_chunk
    flat = lambda x: x.reshape(x.shape[0] * batch, SSM_WIDTH)
    if reverse:
        ctx_idx = lambda c: (jnp.maximum(ncc - 1 - c, 0), 0)
        lat_idx = lambda c: (nlc - 1 - jnp.maximum(c - ncc, 0), 0)
    else:
        ctx_idx = lambda c: (jnp.minimum(c, ncc - 1), 0)
        lat_idx = lambda c: (jnp.maximum(c - ncc, 0), 0)
    full = lambda shape: pl.BlockSpec(shape, lambda c: (0,) * len(shape))
    u_specs = [pl.BlockSpec((rows, SSM_WIDTH), ctx_idx), pl.BlockSpec((rows, SSM_WIDTH), lat_idx)]
    par_specs = [full((2, SSM_FLAT)), full((SSM_WIDTH, 2 * SSM_FLAT)), full((2 * SSM_FLAT, SSM_WIDTH))]
    args = [flat(u_ctx), flat(u_lat)]
    in_specs = list(u_specs)
    if final:
        args += [flat(prev[0]), flat(prev[1])]
        in_specs += u_specs
    args += [a, bbd, cbd]
    in_specs += par_specs
    if final:
        args += [d_skip, w_glu]
        in_specs += [full((1, SSM_WIDTH)), full((SSM_WIDTH, SSM_WIDTH))]
    dt = BF16 if final else F32
    oc, ol = pl.pallas_call(
        functools.partial(_s5_kernel, reverse=reverse, final=final, batch=batch, n_ctx_chunks=ncc),
        grid=(ncc + nlc,),
        in_specs=in_specs,
        out_specs=u_specs,
        out_shape=[jax.ShapeDtypeStruct((u_ctx.shape[0] * batch, SSM_WIDTH), dt),
                   jax.ShapeDtypeStruct((u_lat.shape[0] * batch, SSM_WIDTH), dt)],
        scratch_shapes=[pltpu.VMEM((rows, 2 * SSM_FLAT), F32), pltpu.VMEM((batch, 2 * SSM_FLAT), F32)],
        name="s5_bwd_glu" if final else "s5_fwd",
        compiler_params=_cparams("arbitrary"),
    )(*args)
    return oc.reshape(u_ctx.shape), ol.reshape(u_lat.shape)


def _out_proj_kernel(oa_ref, ob_ref, oc_ref, od_ref, w_ref, x_ref, g1_ref, sh2_ref, sc2_ref,
                     lng_ref, lnb_ref, h_ref, f_ref):
    y = (_dot(oa_ref[...], w_ref[0:256, :]) + _dot(ob_ref[...], w_ref[256:512, :])
         + _dot(oc_ref[...], w_ref[512:768, :]) + _dot(od_ref[...], w_ref[768:1024, :]))
    h1 = _layer_norm(DEEPNORM_ALPHA * x_ref[...] + g1_ref[...] * y, lng_ref[...], lnb_ref[...])
    h_ref[...] = h1
    f_ref[...] = (h1 * (1.0 + sc2_ref[...]) + sh2_ref[...]).astype(BF16)


def _out_proj_call(oa, ob, oc_tm, od, w_out, x, modr, layer, lng, lnb, *, batch, is_ctx):
    rows = x.shape[0]
    per_batch = rows // batch
    tm = 256 if is_ctx else 512
    nb = per_batch // tm
    row_fn = (lambda i: batch) if is_ctx else (lambda i: i // nb)
    rowblk = lambda w: pl.BlockSpec((tm, w), lambda i: (i, 0))
    full = lambda shape: pl.BlockSpec(shape, lambda i: (0,) * len(shape))
    return pl.pallas_call(
        _out_proj_kernel,
        grid=(rows // tm,),
        in_specs=[rowblk(256), rowblk(256),
                  pl.BlockSpec((tm, SSM_WIDTH), lambda i: (i % nb, i // nb)),
                  rowblk(256), full((D_MODEL, D_MODEL)), rowblk(D_MODEL),
                  _mod_spec(layer, 2, row_fn), _mod_spec(layer, 3, row_fn), _mod_spec(layer, 4, row_fn),
                  full((1, D_MODEL)), full((1, D_MODEL))],
        out_specs=[rowblk(D_MODEL), rowblk(D_MODEL)],
        out_shape=[jax.ShapeDtypeStruct((rows, D_MODEL), F32), jax.ShapeDtypeStruct((rows, D_MODEL), BF16)],
        name="out_proj_ctx" if is_ctx else "out_proj",
        compiler_params=_cparams("arbitrary"),
    )(oa, ob, oc_tm, od, w_out, x, modr, modr, modr, lng, lnb)


def _ffn_kernel(f_ref, wg_ref, wu_ref, wd_ref, h_ref, g2_ref, lng_ref, lnb_ref, o_ref, acc_ref):
    j = pl.program_id(1)

    @pl.when(j == 0)
    def _():
        acc_ref[...] = jnp.zeros_like(acc_ref)

    f = f_ref[...]
    g = _dot(f, wg_ref[...])
    u = _dot(f, wu_ref[...])
    acc_ref[...] += _dot((g * jax.nn.sigmoid(g) * u).astype(BF16), wd_ref[...])

    @pl.when(j == pl.num_programs(1) - 1)
    def _():
        o_ref[...] = _layer_norm(DEEPNORM_ALPHA * h_ref[...] + g2_ref[...] * acc_ref[...],
                                 lng_ref[...], lnb_ref[...])


def _ffn_call(f_in, wg, wu, wd, h1, modr, layer, lng, lnb, *, batch, is_ctx):
    rows = h1.shape[0]
    per_batch = rows // batch
    tm = 256 if is_ctx else 512
    d_ff = wg.shape[1]
    tf = d_ff // 2
    nb = per_batch // tm
    row_fn = (lambda i: batch) if is_ctx else (lambda i: i // nb)
    rowblk = lambda w: pl.BlockSpec((tm, w), lambda i, j: (i, 0))
    full = lambda shape: pl.BlockSpec(shape, lambda i, j: (0,) * len(shape))
    return pl.pallas_call(
        _ffn_kernel,
        grid=(rows // tm, d_ff // tf),
        in_specs=[rowblk(D_MODEL),
                  pl.BlockSpec((D_MODEL, tf), lambda i, j: (0, j)),
                  pl.BlockSpec((D_MODEL, tf), lambda i, j: (0, j)),
                  pl.BlockSpec((tf, D_MODEL), lambda i, j: (j, 0)),
                  rowblk(D_MODEL), _mod_spec(layer, 5, row_fn), full((1, D_MODEL)), full((1, D_MODEL))],
        out_specs=rowblk(D_MODEL),
        out_shape=jax.ShapeDtypeStruct((rows, D_MODEL), F32),
        scratch_shapes=[pltpu.VMEM((tm, D_MODEL), F32)],
        name="ffn_ctx" if is_ctx else "ffn",
        compiler_params=_cparams("arbitrary", "arbitrary"),
    )(f_in, wg, wu, wd, h1, modr, lng, lnb)


MOE_TM = 1024
MOE_CHUNK = 288


def _router_kernel(h_ref, sh2_ref, sc2_ref, wr_ref, br_ref, tri_ref, gate_ref, rank_ref, cnt_ref):
    f = h_ref[...] * (1.0 + sc2_ref[...]) + sh2_ref[...]
    f_hi, f_lo = _split_bf16(f)
    w_hi, w_lo = _split_bf16(wr_ref[...])
    logits = _dot_nt(w_hi, f_hi) + _dot_nt(w_hi, f_lo) + _dot_nt(w_lo, f_hi) + br_ref[...]
    ie = lax.broadcasted_iota(jnp.int32, logits.shape, 0)
    m1 = jnp.max(logits, axis=0, keepdims=True)
    i1 = jnp.min(jnp.where(logits == m1, ie, N_EXPERTS), axis=0, keepdims=True)
    rest = jnp.where(ie == i1, -jnp.inf, logits)
    m2 = jnp.max(rest, axis=0, keepdims=True)
    i2 = jnp.min(jnp.where(rest == m2, ie, N_EXPERTS), axis=0, keepdims=True)
    e2 = jnp.exp(m2 - m1)
    den = 1.0 + e2
    gate_ref[...] = jnp.where(ie == i1, 1.0 / den, 0.0) + jnp.where(ie == i2, e2 / den, 0.0)
    sel = jnp.where((ie == i1) | (ie == i2), 1.0, 0.0)
    rank = _dot(sel.astype(BF16), tri_ref[...])
    rank_ref[...] = jnp.where(sel > 0.0, rank, -1.0)
    cnt = jnp.sum(sel, axis=1, keepdims=True)
    cnt_ref[...] = jnp.broadcast_to(cnt, cnt_ref.shape).astype(jnp.int32)


def _router_call(h1, modr, layer, wr_t, b_r, tri, *, batch, is_ctx):
    rows = h1.shape[0]
    tm = MOE_TM
    nblk = rows // tm
    nb = (rows // batch) // tm if not is_ctx else 1
    row_fn = (lambda i: batch) if is_ctx else (lambda i: i // nb)
    full = lambda shape: pl.BlockSpec(shape, lambda i: (0,) * len(shape))
    blk3 = lambda w: pl.BlockSpec((None, N_EXPERTS, w), lambda i: (i, 0, 0))
    return pl.pallas_call(
        _router_kernel,
        grid=(nblk,),
        in_specs=[pl.BlockSpec((tm, D_MODEL), lambda i: (i, 0)),
                  _mod_spec(layer, 3, row_fn), _mod_spec(layer, 4, row_fn),
                  full((N_EXPERTS, D_MODEL)), full((N_EXPERTS, 1)), full((tm, tm))],
        out_specs=[blk3(tm), blk3(tm), blk3(LANES)],
        out_shape=[jax.ShapeDtypeStruct((nblk, N_EXPERTS, tm), F32),
                   jax.ShapeDtypeStruct((nblk, N_EXPERTS, tm), F32),
                   jax.ShapeDtypeStruct((nblk, N_EXPERTS, LANES), jnp.int32)],
        name="router_ctx" if is_ctx else "router",
        compiler_params=_cparams("arbitrary"),
    )(h1, modr, modr, wr_t, b_r, tri)


def _moe_kernel(cnt_ref, f_ref, gate_ref, rank_ref, wg_ref, wu_ref, wd_ref, h_ref, g2_ref,
                lng_ref, lnb_ref, o_ref, acc_ref, xg_ref, yg_ref):
    i, e, j = pl.program_id(0), pl.program_id(1), pl.program_id(2)
    tm = f_ref.shape[0]
    last_j = pl.num_programs(2) - 1
    n_rows = cnt_ref[i * N_EXPERTS + e]
    n_chunks = lax.div(n_rows + (MOE_CHUNK - 1), MOE_CHUNK)

    @pl.when((e == 0) & (j == 0))
    def _():
        acc_ref[...] = jnp.zeros_like(acc_ref)

    rank_row = rank_ref[pl.ds(e, 1), :]
    gate_row = gate_ref[pl.ds(e, 1), :]

    def chunk(c, carry):
        base = pl.multiple_of(c * MOE_CHUNK, 32)
        rows = pl.ds(base, MOE_CHUNK)

        def one_hot():
            slot = (lax.broadcasted_iota(jnp.int32, (MOE_CHUNK, tm), 0) + base).astype(F32)
            return rank_row == slot

        @pl.when(j == 0)
        def _():
            sel = jnp.where(one_hot(), 1.0, 0.0).astype(BF16)
            xg_ref[rows, :] = _dot(sel, f_ref[...]).astype(BF16)

        xc = xg_ref[rows, :]
        g = _dot(xc, wg_ref[...])
        u = _dot(xc, wu_ref[...])
        y = _dot((g * jax.nn.sigmoid(g) * u).astype(BF16), wd_ref[...])

        @pl.when(j == 0)
        def _():
            yg_ref[rows, :] = y

        @pl.when((j > 0) & (j < last_j))
        def _():
            yg_ref[rows, :] += y

        @pl.when(j == last_j)
        def _():
            hit = one_hot()
            sel = jnp.where(hit, 1.0, 0.0).astype(BF16)
            row_gate = jnp.sum(jnp.where(hit, gate_row, 0.0), axis=1, keepdims=True)
            y_hi, y_lo = _split_bf16((yg_ref[rows, :] + y) * row_gate)
            acc_ref[...] += _dot_tn(sel, y_hi) + _dot_tn(sel, y_lo)

        return carry

    lax.fori_loop(0, n_chunks, chunk, 0)

    @pl.when((e == pl.num_programs(1) - 1) & (j == pl.num_programs(2) - 1))
    def _():
        o_ref[...] = _layer_norm(DEEPNORM_ALPHA * h_ref[...] + g2_ref[...] * acc_ref[...],
                                 lng_ref[...], lnb_ref[...])


def _moe_call(cnt, f_in, gate_t, rank_t, wg, wu, wd, h1, modr, layer, lng, lnb, *, batch, is_ctx):
    rows = h1.shape[0]
    tm = MOE_TM
    d_ff = wg.shape[2]
    tf = d_ff // 4
    max_rows = pl.cdiv(tm, MOE_CHUNK) * MOE_CHUNK
    nb = (rows // batch) // tm if not is_ctx else 1
    row_fn = (lambda i: batch) if is_ctx else (lambda i: i // nb)
    rowblk = lambda w: pl.BlockSpec((tm, w), lambda i, e, j, c: (i, 0))
    full = lambda shape: pl.BlockSpec(shape, lambda i, e, j, c: (0,) * len(shape))
    blk3 = pl.BlockSpec((None, N_EXPERTS, tm), lambda i, e, j, c: (i, 0, 0))
    grid_spec = pltpu.PrefetchScalarGridSpec(
        num_scalar_prefetch=1,
        grid=(rows // tm, N_EXPERTS, d_ff // tf),
        in_specs=[rowblk(D_MODEL), blk3, blk3,
                  pl.BlockSpec((None, D_MODEL, tf), lambda i, e, j, c: (e, 0, j)),
                  pl.BlockSpec((None, D_MODEL, tf), lambda i, e, j, c: (e, 0, j)),
                  pl.BlockSpec((None, tf, D_MODEL), lambda i, e, j, c: (e, j, 0)),
                  rowblk(D_MODEL), _mod_spec(layer, 5, row_fn), full((1, D_MODEL)), full((1, D_MODEL))],
        out_specs=rowblk(D_MODEL),
        scratch_shapes=[pltpu.VMEM((tm, D_MODEL), F32), pltpu.VMEM((max_rows, D_MODEL), BF16),
                        pltpu.VMEM((max_rows, D_MODEL), F32)],
    )
    return pl.pallas_call(
        _moe_kernel,
        grid_spec=grid_spec,
        out_shape=jax.ShapeDtypeStruct((rows, D_MODEL), F32),
        name="moe_ctx" if is_ctx else "moe",
        compiler_params=_cparams("arbitrary", "arbitrary", "arbitrary"),
    )(cnt, f_in, gate_t, rank_t, wg, wu, wd, h1, modr, lng, lnb)


def _rope_tables(seq):
    pos = jnp.arange(seq, dtype=jnp.int32)
    row = (pos // GRID_W).astype(F32)
    col = (pos % GRID_W).astype(F32)
    n_freq = HEAD_DIM // 4
    inv_freq = ROPE_THETA ** (-jnp.arange(n_freq, dtype=F32) / n_freq)
    ang = jnp.concatenate([row[:, None] * inv_freq, col[:, None] * inv_freq], axis=-1)
    cos, sin = jnp.cos(ang), jnp.sin(ang)
    reps = LANES // (HEAD_DIM // 2)
    sign = jnp.tile(jnp.concatenate([-jnp.ones(HEAD_DIM // 2, F32), jnp.ones(HEAD_DIM // 2, F32)]),
                    LANES // HEAD_DIM)
    return jnp.tile(cos, (1, reps)), jnp.tile(sin, (1, reps)) * sign


def kernel(x, c, ctx, c_ctx, ada_w, ada_b, w_in, w_out, na_rpb, ga_q_norm, ga_k_norm,
           ssm_lambda_re, ssm_lambda_im, ssm_log_step, ssm_b_re, ssm_b_im, ssm_c_re, ssm_c_im,
           ssm_d, ssm_w_glu, sw_sink, ln1_g, ln1_b, ln2_g, ln2_b,
           ffn_w_gate, ffn_w_up, ffn_w_down,
           moe_w_router, moe_b_router, moe_w_gate, moe_w_up, moe_w_down):
    batch, seq, _ = x.shape
    n_ctx = ctx.shape[1]
    assert batch < MOD_ROWS and seq % 512 == 0 and n_ctx % 256 == 0

    c_all = jnp.zeros((MOD_ROWS, D_MODEL), F32).at[:batch].set(c).at[batch].set(c_ctx)
    modr = _mods_call(c_all, ada_w, ada_b)

    cos_t, sin_t = _rope_tables(seq)
    seg = jnp.kron(jnp.eye(256 // HEAD_DIM, dtype=F32), jnp.ones((HEAD_DIM, HEAD_DIM), F32)).astype(BF16)
    tri = (jnp.arange(MOE_TM)[:, None] < jnp.arange(MOE_TM)[None, :]).astype(BF16)

    h = x.reshape(batch * seq, D_MODEL)
    hc = ctx.reshape(batch * n_ctx, D_MODEL)
    names = ("qa", "ka", "va", "qb", "kb", "vb", "u", "qd", "kd", "vd")

    for layer in range(DEPTH):
        need_ctx = layer < DEPTH - 1
        w_in_l = w_in[layer].astype(BF16)
        w_out_l = w_out[layer].astype(BF16)
        gq = jnp.tile(ga_q_norm[layer].astype(F32), 256 // HEAD_DIM)[None, :]
        gk = jnp.tile(ga_k_norm[layer].astype(F32), 128 // HEAD_DIM)[None, :]
        sink = sw_sink[layer].astype(F32)
        ln1 = (ln1_g[layer][None, :], ln1_b[layer][None, :])
        ln2 = (ln2_g[layer][None, :], ln2_b[layer][None, :])

        lat = dict(zip(names, _in_proj_call(h, modr, layer, w_in_l, cos_t, sin_t, seg, gq, gk,
                                            batch=batch, is_ctx=False)))
        cx = dict(zip(names, _in_proj_call(hc, modr, layer, w_in_l, cos_t, sin_t, seg, gq, gk,
                                           batch=batch, is_ctx=True)))

        out_a = _na_call(lat["qa"], lat["ka"], lat["va"], cx["ka"], cx["va"],
                         _na_bias_table(na_rpb[layer]), batch=batch)
        out_b = _global_call(lat["qb"], lat["kb"], lat["vb"], cx["kb"], cx["vb"], batch=batch)
        out_d = _window_call(sink, lat["qd"], lat["kd"], lat["vd"], cx["kd"], cx["vd"], batch=batch)

        dirs = [_s5_operands(ssm_lambda_re[layer, d], ssm_lambda_im[layer, d], ssm_log_step[layer, d],
                             ssm_b_re[layer, d], ssm_b_im[layer, d], ssm_c_re[layer, d], ssm_c_im[layer, d])
                for d in range(2)]
        fwd = _s5_pass(cx["u"], lat["u"], None, *dirs[0], None, None, batch=batch, reverse=False)
        ctx_c, out_c = _s5_pass(cx["u"], lat["u"], fwd, *dirs[1], ssm_d[layer].astype(F32)[None, :],
                                ssm_w_glu[layer].astype(BF16), batch=batch, reverse=True)

        h1, f_in = _out_proj_call(out_a, out_b, out_c, out_d, w_out_l, h, modr, layer, *ln1,
                                  batch=batch, is_ctx=False)
        streams = [(h1, f_in, False)]
        if need_ctx:
            ctx_a, ctx_b, ctx_d = _ctx_attn_call(sink, cx, batch=batch)
            hc1, fc_in = _out_proj_call(ctx_a, ctx_b, ctx_c, ctx_d, w_out_l, hc, modr, layer, *ln1,
                                        batch=batch, is_ctx=True)
            streams.append((hc1, fc_in, True))

        i = layer // 2
        outs = []
        if layer % 2 == 0:
            wg, wu, wd = (ffn_w_gate[i].astype(BF16), ffn_w_up[i].astype(BF16), ffn_w_down[i].astype(BF16))
            for s1, sf, is_ctx in streams:
                outs.append(_ffn_call(sf, wg, wu, wd, s1, modr, layer, *ln2, batch=batch, is_ctx=is_ctx))
        else:
            wg, wu, wd = (moe_w_gate[i].astype(BF16), moe_w_up[i].astype(BF16), moe_w_down[i].astype(BF16))
            wr_t = moe_w_router[i].astype(F32).T
            b_r = moe_b_router[i].astype(F32)[:, None]
            for s1, sf, is_ctx in streams:
                gate_t, rank_t, cnt = _router_call(s1, modr, layer, wr_t, b_r, tri, batch=batch, is_ctx=is_ctx)
                outs.append(_moe_call(cnt[:, :, 0].reshape(-1), sf, gate_t, rank_t, wg, wu, wd, s1, modr,
                                      layer, *ln2, batch=batch, is_ctx=is_ctx))
        h = outs[0]
        if need_ctx:
            hc = outs[1]
    return h.reshape(batch, seq, D_MODEL)
```

```python
import functools

import jax
import jax.numpy as jnp
from jax import lax
from jax.experimental import pallas as pl
from jax.experimental.pallas import tpu as pltpu

F32 = jnp.float32
BF16 = jnp.bfloat16

D_MODEL = 1024
DEPTH = 4
GRID_W = 64
HEAD_DIM = 64
NA_ROWS = 8
NA_COLS = 16
SSM_GROUPS = 16
SSM_GROUP_CH = 16
SSM_STATE = 64
SSM_WIDTH = SSM_GROUPS * SSM_GROUP_CH
SSM_FLAT = SSM_GROUPS * SSM_STATE
SW_WINDOW = 128
N_EXPERTS = 8
ADA_CHUNKS = 6
ROPE_THETA = 10000.0
LN_EPS = 1e-6
RMS_EPS = 1e-6
NEG_INF = -1e30
DEEPNORM_ALPHA = (2 * DEPTH) ** 0.25
ATT_SCALE = HEAD_DIM ** -0.5

C_QA, C_KA, C_VA, C_QB, C_KB, C_VB, C_U, C_QD, C_KD, C_VD, IN_WIDTH = (
    0, 256, 512, 768, 1024, 1152, 1280, 1536, 1792, 1920, 2048)

LANES = 128
SUBLANES = 8
MOD_ROWS = 16
VMEM_LIMIT = 56 * 1024 * 1024


def _cparams(*sem):
    return pltpu.CompilerParams(dimension_semantics=sem, vmem_limit_bytes=VMEM_LIMIT)


def _dot(a, b):
    return jnp.dot(a, b, preferred_element_type=F32)


def _dot_nt(a, b):
    return lax.dot_general(a, b, (((1,), (1,)), ((), ())), preferred_element_type=F32)


def _dot_tn(a, b):
    return lax.dot_general(a, b, (((0,), (0,)), ((), ())), preferred_element_type=F32)


def _split_bf16(x):
    hi = x.astype(BF16)
    lo = (x - hi.astype(F32)).astype(BF16)
    return hi, lo


def _layer_norm(z, g, b):
    zc = z - jnp.mean(z, axis=-1, keepdims=True)
    y = zc * lax.rsqrt(jnp.mean(zc * zc, axis=-1, keepdims=True) + LN_EPS)
    return y * g + b


def _mods_kernel(c_ref, w_ref, b_ref, o_ref):
    c = c_ref[...]
    act = (c * jax.nn.sigmoid(c)).astype(BF16)
    o_ref[...] = _dot(act, w_ref[...].astype(BF16)) + b_ref[...]


def _mods_call(c_all, ada_w, ada_b):
    tn = 1536
    n = ADA_CHUNKS * D_MODEL
    out = pl.pallas_call(
        _mods_kernel,
        grid=(DEPTH, n // tn),
        in_specs=[
            pl.BlockSpec((MOD_ROWS, D_MODEL), lambda l, j: (0, 0)),
            pl.BlockSpec((None, D_MODEL, tn), lambda l, j: (l, 0, j)),
            pl.BlockSpec((None, 1, tn), lambda l, j: (l, 0, j)),
        ],
        out_specs=pl.BlockSpec((None, MOD_ROWS, tn), lambda l, j: (l, 0, j)),
        out_shape=jax.ShapeDtypeStruct((DEPTH, MOD_ROWS, n), F32),
        name="mods",
        compiler_params=_cparams("arbitrary", "arbitrary"),
    )(c_all, ada_w, ada_b.reshape(DEPTH, 1, n))
    return out.reshape(DEPTH * MOD_ROWS * ADA_CHUNKS, 1, D_MODEL)


def _mod_spec(layer, chunk, row_fn):
    def index(i, *_):
        return ((layer * MOD_ROWS + row_fn(i)) * ADA_CHUNKS + chunk, 0, 0)
    return pl.BlockSpec((None, 1, D_MODEL), index)


def _seg_rms(x, seg, g):
    hi, lo = _split_bf16(x * x)
    ss = _dot(hi, seg) + _dot(lo, seg)
    return x * lax.rsqrt(ss * (1.0 / HEAD_DIM) + RMS_EPS) * g


def _rope(x, cos, sin_signed, first_half):
    outs = []
    for j in range(x.shape[1] // LANES):
        xs = x[:, j * LANES:(j + 1) * LANES]
        partner = jnp.where(first_half,
                            pltpu.roll(xs, LANES - HEAD_DIM // 2, 1),
                            pltpu.roll(xs, HEAD_DIM // 2, 1))
        outs.append(xs * cos + partner * sin_signed)
    return outs[0] if len(outs) == 1 else jnp.concatenate(outs, axis=1)


def _in_proj_kernel(x_ref, sh_ref, sc_ref, w_ref, cos_ref, sin_ref, seg_ref, gq_ref, gk_ref,
                    qa_ref, ka_ref, va_ref, qb_ref, kb_ref, vb_ref, u_ref, qd_ref, kd_ref, vd_ref,
                    *, rope):
    a = (x_ref[...] * (1.0 + sc_ref[...]) + sh_ref[...]).astype(BF16)

    def proj(c0, c1):
        return _dot(a, w_ref[:, c0:c1])

    qa_ref[...] = (proj(C_QA, C_KA) * ATT_SCALE).astype(BF16)
    ka_ref[...] = proj(C_KA, C_VA).astype(BF16)
    va_ref[...] = proj(C_VA, C_QB).astype(BF16)
    vb_ref[...] = proj(C_VB, C_U).astype(BF16)
    u_ref[...] = proj(C_U, C_QD)
    vd_ref[...] = proj(C_VD, IN_WIDTH).astype(BF16)

    seg = seg_ref[...]
    qb = _seg_rms(proj(C_QB, C_KB), seg, gq_ref[...])
    kb = _seg_rms(proj(C_KB, C_VB), seg[:LANES, :LANES], gk_ref[...])
    qd = proj(C_QD, C_KD)
    kd = proj(C_KD, C_VD)
    if rope:
        cos = cos_ref[...]
        sin = sin_ref[...]
        lane = lax.broadcasted_iota(jnp.int32, cos.shape, 1)
        first_half = (lane % HEAD_DIM) < (HEAD_DIM // 2)
        qb = _rope(qb, cos, sin, first_half)
        kb = _rope(kb, cos, sin, first_half)
        qd = _rope(qd, cos, sin, first_half)
        kd = _rope(kd, cos, sin, first_half)
    qb_ref[...] = (qb * ATT_SCALE).astype(BF16)
    kb_ref[...] = kb.astype(BF16)
    qd_ref[...] = (qd * ATT_SCALE).astype(BF16)
    kd_ref[...] = kd.astype(BF16)


def _in_proj_call(x, modr, layer, w_in, cos_t, sin_t, seg, gq, gk, *, batch, is_ctx):
    rows = x.shape[0]
    per_batch = rows // batch
    tm = 256 if is_ctx else 512
    nb = per_batch // tm
    row_fn = (lambda i: batch) if is_ctx else (lambda i: i // nb)
    table_spec = pl.BlockSpec((tm, LANES), (lambda i: (0, 0)) if is_ctx else (lambda i: (i % nb, 0)))
    full = lambda shape: pl.BlockSpec(shape, lambda i: (0,) * len(shape))
    rowblk = lambda w: pl.BlockSpec((tm, w), lambda i: (i, 0))
    wide = lambda w, dt: jax.ShapeDtypeStruct((rows, w), dt)
    out_shapes = [wide(256, BF16), wide(256, BF16), wide(256, BF16),
                  wide(256, BF16), wide(128, BF16), wide(128, BF16),
                  jax.ShapeDtypeStruct((per_batch, batch * SSM_WIDTH), F32),
                  wide(256, BF16), wide(128, BF16), wide(128, BF16)]
    out_specs = [rowblk(256), rowblk(256), rowblk(256), rowblk(256), rowblk(128), rowblk(128),
                 pl.BlockSpec((tm, SSM_WIDTH), lambda i: (i % nb, i // nb)),
                 rowblk(256), rowblk(128), rowblk(128)]
    return pl.pallas_call(
        functools.partial(_in_proj_kernel, rope=not is_ctx),
        grid=(rows // tm,),
        in_specs=[rowblk(D_MODEL), _mod_spec(layer, 0, row_fn), _mod_spec(layer, 1, row_fn),
                  full((D_MODEL, IN_WIDTH)), table_spec, table_spec, full((256, 256)),
                  full((1, 256)), full((1, 128))],
        out_specs=out_specs,
        out_shape=out_shapes,
        name="in_proj_ctx" if is_ctx else "in_proj",
        compiler_params=_cparams("arbitrary"),
    )(x, modr, modr, w_in, cos_t, sin_t, seg, gq, gk)


def _softmax_av(scores, values, sink=None):
    m = functools.reduce(jnp.maximum, [jnp.max(s, axis=-1, keepdims=True) for s in scores])
    if sink is not None:
        m = jnp.maximum(m, sink)
    es = [jnp.exp(s - m) for s in scores]
    den = functools.reduce(jnp.add, [jnp.sum(e, axis=-1, keepdims=True) for e in es])
    if sink is not None:
        den = den + jnp.exp(sink - m)
    o = functools.reduce(jnp.add, [_dot(e.astype(BF16), v) for e, v in zip(es, values)])
    return o * (1.0 / den)


def _head(x, h):
    return x[:, h * HEAD_DIM:(h + 1) * HEAD_DIM]


def _gqa_group(q, g, kparts, vparts, sink_pair=None, masks=None):
    tq = q.shape[0]
    q2 = jnp.concatenate([_head(q, 2 * g), _head(q, 2 * g + 1)], axis=0)
    scores = [_dot_nt(q2, _head(k, g)) for k in kparts]
    if masks is not None:
        scores = [s if mk is None else jnp.where(mk, s, NEG_INF) for s, mk in zip(scores, masks)]
    sink = None
    if sink_pair is not None:
        row = lax.broadcasted_iota(jnp.int32, (2 * tq, 1), 0)
        sink = jnp.where(row < tq, sink_pair[0], sink_pair[1])
    o2 = _softmax_av(scores, [_head(v, g) for v in vparts], sink)
    return jnp.concatenate([o2[:tq], o2[tq:]], axis=1)


NA_QROWS = 4
NA_BAND = NA_ROWS + NA_QROWS - 1


def _na_band_start(j, rows):
    return jnp.clip(j * NA_QROWS - NA_ROWS // 2, 0, rows - NA_BAND)


def _na_kernel(q_ref, k_ref, v_ref, kc_ref, vc_ref, bias_ref, mask_ref, o_ref):
    j = pl.program_id(1)
    rows = k_ref.shape[0] // GRID_W
    start = pl.multiple_of(_na_band_start(j, rows) * GRID_W, GRID_W)
    q = q_ref[...]
    kband = k_ref[pl.ds(start, NA_BAND * GRID_W), :]
    vband = v_ref[pl.ds(start, NA_BAND * GRID_W), :]
    kc = kc_ref[...]
    vc = vc_ref[...]
    valid = mask_ref[...] > 0.5
    outs = []
    for h in range(4):
        qh = _head(q, h)
        s_loc = jnp.where(valid, _dot_nt(qh, _head(kband, h)) + bias_ref[h], NEG_INF)
        s_ctx = _dot_nt(qh, _head(kc, h))
        outs.append(_softmax_av([s_loc, s_ctx], [_head(vband, h), _head(vc, h)]))
    o_ref[...] = jnp.concatenate(outs, axis=1).astype(BF16)


def _na_call(qa, ka, va, ka_c, va_c, bias_tab, mask_tab, *, batch):
    seq = qa.shape[0] // batch
    n_ctx = ka_c.shape[0] // batch
    nj = seq // (NA_QROWS * GRID_W)
    tq = NA_QROWS * GRID_W
    nk = NA_BAND * GRID_W

    def pattern(j):
        return jnp.where(j == 0, 0, jnp.where(j == nj - 1, 2, 1))

    return pl.pallas_call(
        _na_kernel,
        grid=(batch, nj),
        in_specs=[
            pl.BlockSpec((tq, 256), lambda b, j: (b * nj + j, 0)),
            pl.BlockSpec((seq, 256), lambda b, j: (b, 0)),
            pl.BlockSpec((seq, 256), lambda b, j: (b, 0)),
            pl.BlockSpec((n_ctx, 256), lambda b, j: (b, 0)),
            pl.BlockSpec((n_ctx, 256), lambda b, j: (b, 0)),
            pl.BlockSpec((None, 4, tq, nk), lambda b, j: (pattern(j), 0, 0, 0)),
            pl.BlockSpec((None, tq, nk), lambda b, j: (pattern(j), 0, 0)),
        ],
        out_specs=pl.BlockSpec((tq, 256), lambda b, j: (b * nj + j, 0)),
        out_shape=jax.ShapeDtypeStruct(qa.shape, BF16),
        name="na_attn",
        compiler_params=_cparams("arbitrary", "arbitrary"),
    )(qa, ka, va, ka_c, va_c, bias_tab, mask_tab)


def _na_tables(rpb, seq):
    rows = seq // GRID_W
    nj = rows // NA_QROWS
    c_idx = jnp.arange(GRID_W)
    col_start = jnp.clip(c_idx - NA_COLS // 2, 0, GRID_W - NA_COLS)
    col_valid = (c_idx[None, :] >= col_start[:, None]) & (c_idx[None, :] < col_start[:, None] + NA_COLS)
    col_rel = jnp.clip(c_idx[None, :] - c_idx[:, None], 1 - NA_COLS, NA_COLS - 1) + NA_COLS - 1
    blocks = jnp.array([0, 1, nj - 1])
    q_row = blocks[:, None] * NA_QROWS + jnp.arange(NA_QROWS)[None, :]
    k_row = _na_band_start(blocks, rows)[:, None] + jnp.arange(NA_BAND)[None, :]
    row_start = jnp.clip(q_row - NA_ROWS // 2, 0, rows - NA_ROWS)
    row_valid = ((k_row[:, None, :] >= row_start[:, :, None])
                 & (k_row[:, None, :] < row_start[:, :, None] + NA_ROWS))
    row_rel = jnp.clip(k_row[:, None, :] - q_row[:, :, None] + NA_ROWS - 1, 0, 2 * NA_ROWS - 2)
    row_sel = jax.nn.one_hot(row_rel, 2 * NA_ROWS - 1, dtype=F32)
    col_sel = jax.nn.one_hot(col_rel, 2 * NA_COLS - 1, dtype=F32)
    exact = lax.Precision.HIGHEST
    by_col = jnp.einsum("hab,qkb->haqk", rpb.astype(F32), col_sel, precision=exact)
    bias = jnp.einsum("pria,haqk->phrqik", row_sel, by_col, precision=exact)
    valid = row_valid[:, :, None, :, None] & col_valid[None, None, :, None, :]
    tq, nk = NA_QROWS * GRID_W, NA_BAND * GRID_W
    return bias.reshape(3, 4, tq, nk), valid.astype(F32).reshape(3, tq, nk)


def _global_kernel(q_ref, k_ref, v_ref, kc_ref, vc_ref, o_ref):
    q = q_ref[...]
    kparts = [kc_ref[...], k_ref[...]]
    vparts = [vc_ref[...], v_ref[...]]
    for g in range(2):
        o_ref[:, g * LANES:(g + 1) * LANES] = _gqa_group(q, g, kparts, vparts).astype(BF16)


def _global_call(qb, kb, vb, kb_c, vb_c, *, batch):
    seq = qb.shape[0] // batch
    n_ctx = kb_c.shape[0] // batch
    tq = 256
    nq = seq // tq
    return pl.pallas_call(
        _global_kernel,
        grid=(batch, nq),
        in_specs=[
            pl.BlockSpec((tq, 256), lambda b, i: (b * nq + i, 0)),
            pl.BlockSpec((seq, 128), lambda b, i: (b, 0)),
            pl.BlockSpec((seq, 128), lambda b, i: (b, 0)),
            pl.BlockSpec((n_ctx, 128), lambda b, i: (b, 0)),
            pl.BlockSpec((n_ctx, 128), lambda b, i: (b, 0)),
        ],
        out_specs=pl.BlockSpec((tq, 256), lambda b, i: (b * nq + i, 0)),
        out_shape=jax.ShapeDtypeStruct(qb.shape, BF16),
        name="global_attn",
        compiler_params=_cparams("arbitrary", "arbitrary"),
    )(qb, kb, vb, kb_c, vb_c)


SW_TQ = 256
SW_SPAN = SW_TQ + 2 * SW_WINDOW


def _window_start(i, seq):
    return jnp.clip(i * SW_TQ - SW_WINDOW, 0, seq - SW_SPAN)


def _window_kernel(sink_ref, q_ref, k_ref, v_ref, kc_ref, vc_ref, mask_ref, o_ref):
    i = pl.program_id(1)
    start = pl.multiple_of(_window_start(i, k_ref.shape[0]), LANES)
    q = q_ref[...]
    kw = k_ref[pl.ds(start, SW_SPAN), :]
    vw = v_ref[pl.ds(start, SW_SPAN), :]
    valid = mask_ref[...] > 0.5
    for g in range(2):
        o = _gqa_group(q, g, [kw, kc_ref[...]], [vw, vc_ref[...]],
                       sink_pair=(sink_ref[2 * g], sink_ref[2 * g + 1]), masks=[valid, None])
        o_ref[:, g * LANES:(g + 1) * LANES] = o.astype(BF16)


def _window_mask_table(seq):
    nq = seq // SW_TQ
    blocks = jnp.array([0, 1, nq - 1])
    qpos = blocks[:, None] * SW_TQ + jnp.arange(SW_TQ)[None, :]
    kpos = _window_start(blocks, seq)[:, None] + jnp.arange(SW_SPAN)[None, :]
    valid = jnp.abs(kpos[:, None, :] - qpos[:, :, None]) <= SW_WINDOW
    return jnp.tile(valid.astype(F32), (1, 2, 1))


def _window_call(sink, qd, kd, vd, kd_c, vd_c, mask_tab, *, batch):
    seq = qd.shape[0] // batch
    n_ctx = kd_c.shape[0] // batch
    nq = seq // SW_TQ

    def pattern(i):
        return jnp.where(i == 0, 0, jnp.where(i == nq - 1, 2, 1))

    return pl.pallas_call(
        _window_kernel,
        grid=(batch, nq),
        in_specs=[
            pl.BlockSpec(memory_space=pltpu.SMEM),
            pl.BlockSpec((SW_TQ, 256), lambda b, i: (b * nq + i, 0)),
            pl.BlockSpec((seq, 128), lambda b, i: (b, 0)),
            pl.BlockSpec((seq, 128), lambda b, i: (b, 0)),
            pl.BlockSpec((n_ctx, 128), lambda b, i: (b, 0)),
            pl.BlockSpec((n_ctx, 128), lambda b, i: (b, 0)),
            pl.BlockSpec((None, 2 * SW_TQ, SW_SPAN), lambda b, i: (pattern(i), 0, 0)),
        ],
        out_specs=pl.BlockSpec((SW_TQ, 256), lambda b, i: (b * nq + i, 0)),
        out_shape=jax.ShapeDtypeStruct(qd.shape, BF16),
        name="window_attn",
        compiler_params=_cparams("arbitrary", "arbitrary"),
    )(sink, qd, kd, vd, kd_c, vd_c, mask_tab)


def _ctx_attn_kernel(sink_ref, qa_ref, ka_ref, va_ref, qb_ref, kb_ref, vb_ref, qd_ref, kd_ref, vd_ref,
                     oa_ref, ob_ref, od_ref):
    qa, ka, va = qa_ref[...], ka_ref[...], va_ref[...]
    outs = [_softmax_av([_dot_nt(_head(qa, h), _head(ka, h))], [_head(va, h)]) for h in range(4)]
    oa_ref[...] = jnp.concatenate(outs, axis=1).astype(BF16)
    for g in range(2):
        ob_ref[:, g * LANES:(g + 1) * LANES] = _gqa_group(
            qb_ref[...], g, [kb_ref[...]], [vb_ref[...]]).astype(BF16)
        od_ref[:, g * LANES:(g + 1) * LANES] = _gqa_group(
            qd_ref[...], g, [kd_ref[...]], [vd_ref[...]],
            sink_pair=(sink_ref[2 * g], sink_ref[2 * g + 1])).astype(BF16)


def _ctx_attn_call(sink, cx, *, batch):
    n_ctx = cx["qa"].shape[0] // batch
    blk = lambda w: pl.BlockSpec((n_ctx, w), lambda b: (b, 0))
    names = ("qa", "ka", "va", "qb", "kb", "vb", "qd", "kd", "vd")
    widths = (256, 256, 256, 256, 128, 128, 256, 128, 128)
    shape = jax.ShapeDtypeStruct(cx["qa"].shape, BF16)
    return pl.pallas_call(
        _ctx_attn_kernel,
        grid=(batch,),
        in_specs=[pl.BlockSpec(memory_space=pltpu.SMEM)] + [blk(w) for w in widths],
        out_specs=[blk(256)] * 3,
        out_shape=[shape] * 3,
        name="ctx_attn",
        compiler_params=_cparams("arbitrary"),
    )(sink, *[cx[n] for n in names])


def _s5_discretise(lam_re, lam_im, log_step, b_re, b_im):
    step = jnp.exp(log_step.astype(F32))[:, None]
    lam_re = lam_re.astype(F32)
    lam_im = lam_im.astype(F32)
    mag = jnp.exp(lam_re * step)
    ab_re = mag * jnp.cos(lam_im * step)
    ab_im = mag * jnp.sin(lam_im * step)
    den = lam_re * lam_re + lam_im * lam_im
    num_re = ab_re - 1.0
    f_re = ((num_re * lam_re + ab_im * lam_im) / den)[..., None]
    f_im = ((ab_im * lam_re - num_re * lam_im) / den)[..., None]
    b_re = b_re.astype(F32)
    b_im = b_im.astype(F32)
    return ab_re, ab_im, f_re * b_re - f_im * b_im, f_re * b_im + f_im * b_re


def _s5_operands(lam_re, lam_im, log_step, b_re, b_im, c_re, c_im):
    ab_re, ab_im, bb_re, bb_im = _s5_discretise(lam_re, lam_im, log_step, b_re, b_im)
    eye = jnp.eye(SSM_GROUPS, dtype=F32)
    bd_in = lambda m: jnp.einsum("gph,gk->ghkp", m, eye).reshape(SSM_WIDTH, SSM_FLAT)
    bd_out = lambda m: jnp.einsum("ghp,gk->kpgh", m.astype(F32), eye).reshape(SSM_FLAT, SSM_WIDTH)
    a = jnp.stack([ab_re.reshape(SSM_FLAT), ab_im.reshape(SSM_FLAT)])
    bbd = jnp.concatenate([bd_in(bb_re), bd_in(bb_im)], axis=1).astype(BF16)
    cbd = jnp.concatenate([bd_out(c_re), -bd_out(c_im)], axis=0).astype(BF16)
    return a, bbd, cbd


def _s5_kernel(*refs, reverse, final, batch, n_ctx_chunks):
    if final:
        (uc_ref, ul_ref, pc_ref, pl_ref, a_ref, bbd_ref, cbd_ref, d_ref, wglu_ref,
         oc_ref, ol_ref, bu_ref, h_ref) = refs
    else:
        uc_ref, ul_ref, a_ref, bbd_ref, cbd_ref, oc_ref, ol_ref, bu_ref, h_ref = refs
        pc_ref = pl_ref = d_ref = wglu_ref = None
    c = pl.program_id(0)
    steps = bu_ref.shape[0] // batch

    @pl.when(c == 0)
    def _():
        h_ref[...] = jnp.zeros_like(h_ref)

    def run(u_ref, prev_ref, o_ref):
        u = u_ref[...]
        bu_ref[...] = _dot(u.astype(BF16), bbd_ref[...])
        ar = jnp.broadcast_to(a_ref[0:1, :], (batch, SSM_FLAT))
        ai = jnp.broadcast_to(a_ref[1:2, :], (batch, SSM_FLAT))

        def step(i, h):
            t = steps - 1 - i if reverse else i
            row = pl.multiple_of(t * batch, batch)
            b = bu_ref[pl.ds(row, batch), :]
            hr, hi = h[:, :SSM_FLAT], h[:, SSM_FLAT:]
            nr = ar * hr - ai * hi + b[:, :SSM_FLAT]
            ni = ar * hi + ai * hr + b[:, SSM_FLAT:]
            hn = jnp.concatenate([nr, ni], axis=1)
            bu_ref[pl.ds(row, batch), :] = hn
            return hn

        h_ref[...] = lax.fori_loop(0, steps, step, h_ref[...], unroll=4)
        y = _dot(bu_ref[...].astype(BF16), cbd_ref[...])
        if not final:
            o_ref[...] = y
        else:
            y = jax.nn.gelu(y + prev_ref[...] + d_ref[...] * u)
            z = _dot(y.astype(BF16), wglu_ref[...])
            o_ref[...] = (y * jax.nn.sigmoid(z)).astype(BF16)

    @pl.when(c < n_ctx_chunks)
    def _():
        run(uc_ref, pc_ref, oc_ref)

    @pl.when(c >= n_ctx_chunks)
    def _():
        run(ul_ref, pl_ref, ol_ref)


def _s5_pass(u_ctx, u_lat, prev, a, bbd, cbd, d_skip, w_glu, *, batch, reverse):
    final = prev is not None
    t_chunk = 128
    rows = t_chunk * batch
    ncc = u_ctx.shape[0] // t_chunk
    nlc = u_lat.shape[0] // t_chunk
    flat = lambda x: x.reshape(x.shape[0] * batch, SSM_WIDTH)
    if reverse:
        ctx_idx = lambda c: (jnp.maximum(ncc - 1 - c, 0), 0)
        lat_idx = lambda c: (nlc - 1 - jnp.maximum(c - ncc, 0), 0)
    else:
        ctx_idx = lambda c: (jnp.minimum(c, ncc - 1), 0)
        lat_idx = lambda c: (jnp.maximum(c - ncc, 0), 0)
    full = lambda shape: pl.BlockSpec(shape, lambda c: (0,) * len(shape))
    u_specs = [pl.BlockSpec((rows, SSM_WIDTH), ctx_idx), pl.BlockSpec((rows, SSM_WIDTH), lat_idx)]
    par_specs = [full((2, SSM_FLAT)), full((SSM_WIDTH, 2 * SSM_FLAT)), full((2 * SSM_FLAT, SSM_WIDTH))]
    args = [flat(u_ctx), flat(u_lat)]
    in_specs = list(u_specs)
    if final:
        args += [flat(prev[0]), flat(prev[1])]
        in_specs += u_specs
    args += [a, bbd, cbd]
    in_specs += par_specs
    if final:
        args += [d_skip, w_glu]
        in_specs += [full((1, SSM_WIDTH)), full((SSM_WIDTH, SSM_WIDTH))]
    dt = BF16 if final else F32
    oc, ol = pl.pallas_call(
        functools.partial(_s5_kernel, reverse=reverse, final=final, batch=batch, n_ctx_chunks=ncc),
        grid=(ncc + nlc,),
        in_specs=in_specs,
        out_specs=u_specs,
        out_shape=[jax.ShapeDtypeStruct((u_ctx.shape[0] * batch, SSM_WIDTH), dt),
                   jax.ShapeDtypeStruct((u_lat.shape[0] * batch, SSM_WIDTH), dt)],
        scratch_shapes=[pltpu.VMEM((rows, 2 * SSM_FLAT), F32), pltpu.VMEM((batch, 2 * SSM_FLAT), F32)],
        name="s5_bwd_glu" if final else "s5_fwd",
        compiler_params=_cparams("arbitrary"),
    )(*args)
    return oc.reshape(u_ctx.shape), ol.reshape(u_lat.shape)


def _out_proj_kernel(oa_ref, ob_ref, oc_ref, od_ref, w_ref, x_ref, g1_ref, sh2_ref, sc2_ref,
                     lng_ref, lnb_ref, h_ref, f_ref):
    y = (_dot(oa_ref[...], w_ref[0:256, :]) + _dot(ob_ref[...], w_ref[256:512, :])
         + _dot(oc_ref[...], w_ref[512:768, :]) + _dot(od_ref[...], w_ref[768:1024, :]))
    h1 = _layer_norm(DEEPNORM_ALPHA * x_ref[...] + g1_ref[...] * y, lng_ref[...], lnb_ref[...])
    h_ref[...] = h1
    f_ref[...] = (h1 * (1.0 + sc2_ref[...]) + sh2_ref[...]).astype(BF16)


def _out_proj_call(oa, ob, oc_tm, od, w_out, x, modr, layer, lng, lnb, *, batch, is_ctx):
    rows = x.shape[0]
    per_batch = rows // batch
    tm = 256 if is_ctx else 512
    nb = per_batch // tm
    row_fn = (lambda i: batch) if is_ctx else (lambda i: i // nb)
    rowblk = lambda w: pl.BlockSpec((tm, w), lambda i: (i, 0))
    full = lambda shape: pl.BlockSpec(shape, lambda i: (0,) * len(shape))
    return pl.pallas_call(
        _out_proj_kernel,
        grid=(rows // tm,),
        in_specs=[rowblk(256), rowblk(256),
                  pl.BlockSpec((tm, SSM_WIDTH), lambda i: (i % nb, i // nb)),
                  rowblk(256), full((D_MODEL, D_MODEL)), rowblk(D_MODEL),
                  _mod_spec(layer, 2, row_fn), _mod_spec(layer, 3, row_fn), _mod_spec(layer, 4, row_fn),
                  full((1, D_MODEL)), full((1, D_MODEL))],
        out_specs=[rowblk(D_MODEL), rowblk(D_MODEL)],
        out_shape=[jax.ShapeDtypeStruct((rows, D_MODEL), F32), jax.ShapeDtypeStruct((rows, D_MODEL), BF16)],
        name="out_proj_ctx" if is_ctx else "out_proj",
        compiler_params=_cparams("arbitrary"),
    )(oa, ob, oc_tm, od, w_out, x, modr, modr, modr, lng, lnb)


def _ffn_kernel(f_ref, wg_ref, wu_ref, wd_ref, h_ref, g2_ref, lng_ref, lnb_ref, o_ref, acc_ref):
    j = pl.program_id(1)

    @pl.when(j == 0)
    def _():
        acc_ref[...] = jnp.zeros_like(acc_ref)

    f = f_ref[...]
    g = _dot(f, wg_ref[...])
    u = _dot(f, wu_ref[...])
    acc_ref[...] += _dot((g * jax.nn.sigmoid(g) * u).astype(BF16), wd_ref[...])

    @pl.when(j == pl.num_programs(1) - 1)
    def _():
        o_ref[...] = _layer_norm(DEEPNORM_ALPHA * h_ref[...] + g2_ref[...] * acc_ref[...],
                                 lng_ref[...], lnb_ref[...])


def _ffn_call(f_in, wg, wu, wd, h1, modr, layer, lng, lnb, *, batch, is_ctx):
    rows = h1.shape[0]
    per_batch = rows // batch
    tm = 256 if is_ctx else 512
    d_ff = wg.shape[1]
    tf = d_ff // 2
    nb = per_batch // tm
    row_fn = (lambda i: batch) if is_ctx else (lambda i: i // nb)
    rowblk = lambda w: pl.BlockSpec((tm, w), lambda i, j: (i, 0))
    full = lambda shape: pl.BlockSpec(shape, lambda i, j: (0,) * len(shape))
    return pl.pallas_call(
        _ffn_kernel,
        grid=(rows // tm, d_ff // tf),
        in_specs=[rowblk(D_MODEL),
                  pl.BlockSpec((D_MODEL, tf), lambda i, j: (0, j)),
                  pl.BlockSpec((D_MODEL, tf), lambda i, j: (0, j)),
                  pl.BlockSpec((tf, D_MODEL), lambda i, j: (j, 0)),
                  rowblk(D_MODEL), _mod_spec(layer, 5, row_fn), full((1, D_MODEL)), full((1, D_MODEL))],
        out_specs=rowblk(D_MODEL),
        out_shape=jax.ShapeDtypeStruct((rows, D_MODEL), F32),
        scratch_shapes=[pltpu.VMEM((tm, D_MODEL), F32)],
        name="ffn_ctx" if is_ctx else "ffn",
        compiler_params=_cparams("arbitrary", "arbitrary"),
    )(f_in, wg, wu, wd, h1, modr, lng, lnb)


MOE_TM = 1024
MOE_CHUNK = 288


def _router_kernel(h_ref, sh2_ref, sc2_ref, wr_ref, br_ref, tri_ref, gate_ref, rank_ref, cnt_ref):
    f = h_ref[...] * (1.0 + sc2_ref[...]) + sh2_ref[...]
    f_hi, f_lo = _split_bf16(f)
    w_hi, w_lo = _split_bf16(wr_ref[...])
    logits = _dot_nt(w_hi, f_hi) + _dot_nt(w_hi, f_lo) + _dot_nt(w_lo, f_hi) + br_ref[...]
    ie = lax.broadcasted_iota(jnp.int32, logits.shape, 0)
    m1 = jnp.max(logits, axis=0, keepdims=True)
    i1 = jnp.min(jnp.where(logits == m1, ie, N_EXPERTS), axis=0, keepdims=True)
    rest = jnp.where(ie == i1, -jnp.inf, logits)
    m2 = jnp.max(rest, axis=0, keepdims=True)
    i2 = jnp.min(jnp.where(rest == m2, ie, N_EXPERTS), axis=0, keepdims=True)
    e2 = jnp.exp(m2 - m1)
    den = 1.0 + e2
    gate_ref[...] = jnp.where(ie == i1, 1.0 / den, 0.0) + jnp.where(ie == i2, e2 / den, 0.0)
    sel = jnp.where((ie == i1) | (ie == i2), 1.0, 0.0)
    rank = _dot(sel.astype(BF16), tri_ref[...])
    rank_ref[...] = jnp.where(sel > 0.0, rank, -1.0)
    cnt = jnp.sum(sel, axis=1, keepdims=True)
    cnt_ref[...] = jnp.broadcast_to(cnt, cnt_ref.shape).astype(jnp.int32)


def _router_call(h1, modr, layer, wr_t, b_r, tri, *, batch, is_ctx):
    rows = h1.shape[0]
    tm = MOE_TM
    nblk = rows // tm
    nb = (rows // batch) // tm if not is_ctx else 1
    row_fn = (lambda i: batch) if is_ctx else (lambda i: i // nb)
    full = lambda shape: pl.BlockSpec(shape, lambda i: (0,) * len(shape))
    blk3 = lambda w: pl.BlockSpec((None, N_EXPERTS, w), lambda i: (i, 0, 0))
    return pl.pallas_call(
        _router_kernel,
        grid=(nblk,),
        in_specs=[pl.BlockSpec((tm, D_MODEL), lambda i: (i, 0)),
                  _mod_spec(layer, 3, row_fn), _mod_spec(layer, 4, row_fn),
                  full((N_EXPERTS, D_MODEL)), full((N_EXPERTS, 1)), full((tm, tm))],
        out_specs=[blk3(tm), blk3(tm), blk3(LANES)],
        out_shape=[jax.ShapeDtypeStruct((nblk, N_EXPERTS, tm), F32),
                   jax.ShapeDtypeStruct((nblk, N_EXPERTS, tm), F32),
                   jax.ShapeDtypeStruct((nblk, N_EXPERTS, LANES), jnp.int32)],
        name="router_ctx" if is_ctx else "router",
        compiler_params=_cparams("arbitrary"),
    )(h1, modr, modr, wr_t, b_r, tri)


def _moe_kernel(cnt_ref, f_ref, gate_ref, rank_ref, wg_ref, wu_ref, wd_ref, h_ref, g2_ref,
                lng_ref, lnb_ref, o_ref, acc_ref, xg_ref, yg_ref):
    i, e, j = pl.program_id(0), pl.program_id(1), pl.program_id(2)
    tm = f_ref.shape[0]
    last_j = pl.num_programs(2) - 1
    n_rows = cnt_ref[i * N_EXPERTS + e]
    n_chunks = lax.div(n_rows + (MOE_CHUNK - 1), MOE_CHUNK)

    @pl.when((e == 0) & (j == 0))
    def _():
        acc_ref[...] = jnp.zeros_like(acc_ref)

    rank_row = rank_ref[pl.ds(e, 1), :]
    gate_row = gate_ref[pl.ds(e, 1), :]

    def chunk(c, carry):
        base = pl.multiple_of(c * MOE_CHUNK, 32)
        rows = pl.ds(base, MOE_CHUNK)

        def one_hot():
            slot = (lax.broadcasted_iota(jnp.int32, (MOE_CHUNK, tm), 0) + base).astype(F32)
            return rank_row == slot

        @pl.when(j == 0)
        def _():
            sel = jnp.where(one_hot(), 1.0, 0.0).astype(BF16)
            xg_ref[rows, :] = _dot(sel, f_ref[...]).astype(BF16)

        xc = xg_ref[rows, :]
        g = _dot(xc, wg_ref[...])
        u = _dot(xc, wu_ref[...])
        y = _dot((g * jax.nn.sigmoid(g) * u).astype(BF16), wd_ref[...])

        @pl.when(j == 0)
        def _():
            yg_ref[rows, :] = y

        @pl.when((j > 0) & (j < last_j))
        def _():
            yg_ref[rows, :] += y

        @pl.when(j == last_j)
        def _():
            hit = one_hot()
            sel = jnp.where(hit, 1.0, 0.0).astype(BF16)
            row_gate = jnp.sum(jnp.where(hit, gate_row, 0.0), axis=1, keepdims=True)
            y_hi, y_lo = _split_bf16((yg_ref[rows, :] + y) * row_gate)
            acc_ref[...] += _dot_tn(sel, y_hi) + _dot_tn(sel, y_lo)

        return carry

    lax.fori_loop(0, n_chunks, chunk, 0)

    @pl.when((e == pl.num_programs(1) - 1) & (j == pl.num_programs(2) - 1))
    def _():
        o_ref[...] = _layer_norm(DEEPNORM_ALPHA * h_ref[...] + g2_ref[...] * acc_ref[...],
                                 lng_ref[...], lnb_ref[...])


def _moe_call(cnt, f_in, gate_t, rank_t, wg, wu, wd, h1, modr, layer, lng, lnb, *, batch, is_ctx):
    rows = h1.shape[0]
    tm = MOE_TM
    d_ff = wg.shape[2]
    tf = d_ff // 4
    max_rows = pl.cdiv(tm, MOE_CHUNK) * MOE_CHUNK
    nb = (rows // batch) // tm if not is_ctx else 1
    row_fn = (lambda i: batch) if is_ctx else (lambda i: i // nb)
    rowblk = lambda w: pl.BlockSpec((tm, w), lambda i, e, j, c: (i, 0))
    full = lambda shape: pl.BlockSpec(shape, lambda i, e, j, c: (0,) * len(shape))
    blk3 = pl.BlockSpec((None, N_EXPERTS, tm), lambda i, e, j, c: (i, 0, 0))
    grid_spec = pltpu.PrefetchScalarGridSpec(
        num_scalar_prefetch=1,
        grid=(rows // tm, N_EXPERTS, d_ff // tf),
        in_specs=[rowblk(D_MODEL), blk3, blk3,
                  pl.BlockSpec((None, D_MODEL, tf), lambda i, e, j, c: (e, 0, j)),
                  pl.BlockSpec((None, D_MODEL, tf), lambda i, e, j, c: (e, 0, j)),
                  pl.BlockSpec((None, tf, D_MODEL), lambda i, e, j, c: (e, j, 0)),
                  rowblk(D_MODEL), _mod_spec(layer, 5, row_fn), full((1, D_MODEL)), full((1, D_MODEL))],
        out_specs=rowblk(D_MODEL),
        scratch_shapes=[pltpu.VMEM((tm, D_MODEL), F32), pltpu.VMEM((max_rows, D_MODEL), BF16),
                        pltpu.VMEM((max_rows, D_MODEL), F32)],
    )
    return pl.pallas_call(
        _moe_kernel,
        grid_spec=grid_spec,
        out_shape=jax.ShapeDtypeStruct((rows, D_MODEL), F32),
        name="moe_ctx" if is_ctx else "moe",
        compiler_params=_cparams("arbitrary", "arbitrary", "arbitrary"),
    )(cnt, f_in, gate_t, rank_t, wg, wu, wd, h1, modr, lng, lnb)


def _rope_tables(seq):
    pos = jnp.arange(seq, dtype=jnp.int32)
    row = (pos // GRID_W).astype(F32)
    col = (pos % GRID_W).astype(F32)
    n_freq = HEAD_DIM // 4
    inv_freq = ROPE_THETA ** (-jnp.arange(n_freq, dtype=F32) / n_freq)
    ang = jnp.concatenate([row[:, None] * inv_freq, col[:, None] * inv_freq], axis=-1)
    cos, sin = jnp.cos(ang), jnp.sin(ang)
    reps = LANES // (HEAD_DIM // 2)
    sign = jnp.tile(jnp.concatenate([-jnp.ones(HEAD_DIM // 2, F32), jnp.ones(HEAD_DIM // 2, F32)]),
                    LANES // HEAD_DIM)
    return jnp.tile(cos, (1, reps)), jnp.tile(sin, (1, reps)) * sign


def kernel(x, c, ctx, c_ctx, ada_w, ada_b, w_in, w_out, na_rpb, ga_q_norm, ga_k_norm,
           ssm_lambda_re, ssm_lambda_im, ssm_log_step, ssm_b_re, ssm_b_im, ssm_c_re, ssm_c_im,
           ssm_d, ssm_w_glu, sw_sink, ln1_g, ln1_b, ln2_g, ln2_b,
           ffn_w_gate, ffn_w_up, ffn_w_down,
           moe_w_router, moe_b_router, moe_w_gate, moe_w_up, moe_w_down):
    batch, seq, _ = x.shape
    n_ctx = ctx.shape[1]
    assert batch < MOD_ROWS and seq % 512 == 0 and n_ctx % 256 == 0

    c_all = jnp.zeros((MOD_ROWS, D_MODEL), F32).at[:batch].set(c).at[batch].set(c_ctx)
    modr = _mods_call(c_all, ada_w, ada_b)

    cos_t, sin_t = _rope_tables(seq)
    seg = jnp.kron(jnp.eye(256 // HEAD_DIM, dtype=F32), jnp.ones((HEAD_DIM, HEAD_DIM), F32)).astype(BF16)
    tri = (jnp.arange(MOE_TM)[:, None] < jnp.arange(MOE_TM)[None, :]).astype(BF16)
    sw_mask = _window_mask_table(seq)

    h = x.reshape(batch * seq, D_MODEL)
    hc = ctx.reshape(batch * n_ctx, D_MODEL)
    names = ("qa", "ka", "va", "qb", "kb", "vb", "u", "qd", "kd", "vd")

    for layer in range(DEPTH):
        need_ctx = layer < DEPTH - 1
        w_in_l = w_in[layer].astype(BF16)
        w_out_l = w_out[layer].astype(BF16)
        gq = jnp.tile(ga_q_norm[layer].astype(F32), 256 // HEAD_DIM)[None, :]
        gk = jnp.tile(ga_k_norm[layer].astype(F32), 128 // HEAD_DIM)[None, :]
        sink = sw_sink[layer].astype(F32)
        ln1 = (ln1_g[layer][None, :], ln1_b[layer][None, :])
        ln2 = (ln2_g[layer][None, :], ln2_b[layer][None, :])

        lat = dict(zip(names, _in_proj_call(h, modr, layer, w_in_l, cos_t, sin_t, seg, gq, gk,
                                            batch=batch, is_ctx=False)))
        cx = dict(zip(names, _in_proj_call(hc, modr, layer, w_in_l, cos_t, sin_t, seg, gq, gk,
                                           batch=batch, is_ctx=True)))

        na_bias, na_mask = _na_tables(na_rpb[layer], seq)
        out_a = _na_call(lat["qa"], lat["ka"], lat["va"], cx["ka"], cx["va"], na_bias, na_mask, batch=batch)
        out_b = _global_call(lat["qb"], lat["kb"], lat["vb"], cx["kb"], cx["vb"], batch=batch)
        out_d = _window_call(sink, lat["qd"], lat["kd"], lat["vd"], cx["kd"], cx["vd"], sw_mask, batch=batch)

        dirs = [_s5_operands(ssm_lambda_re[layer, d], ssm_lambda_im[layer, d], ssm_log_step[layer, d],
                             ssm_b_re[layer, d], ssm_b_im[layer, d], ssm_c_re[layer, d], ssm_c_im[layer, d])
                for d in range(2)]
        fwd = _s5_pass(cx["u"], lat["u"], None, *dirs[0], None, None, batch=batch, reverse=False)
        ctx_c, out_c = _s5_pass(cx["u"], lat["u"], fwd, *dirs[1], ssm_d[layer].astype(F32)[None, :],
                                ssm_w_glu[layer].astype(BF16), batch=batch, reverse=True)

        h1, f_in = _out_proj_call(out_a, out_b, out_c, out_d, w_out_l, h, modr, layer, *ln1,
                                  batch=batch, is_ctx=False)
        streams = [(h1, f_in, False)]
        if need_ctx:
            ctx_a, ctx_b, ctx_d = _ctx_attn_call(sink, cx, batch=batch)
            hc1, fc_in = _out_proj_call(ctx_a, ctx_b, ctx_c, ctx_d, w_out_l, hc, modr, layer, *ln1,
                                        batch=batch, is_ctx=True)
            streams.append((hc1, fc_in, True))

        i = layer // 2
        outs = []
        if layer % 2 == 0:
            wg, wu, wd = (ffn_w_gate[i].astype(BF16), ffn_w_up[i].astype(BF16), ffn_w_down[i].astype(BF16))
            for s1, sf, is_ctx in streams:
                outs.append(_ffn_call(sf, wg, wu, wd, s1, modr, layer, *ln2, batch=batch, is_ctx=is_ctx))
        else:
            wg, wu, wd = (moe_w_gate[i].astype(BF16), moe_w_up[i].astype(BF16), moe_w_down[i].astype(BF16))
            wr_t = moe_w_router[i].astype(F32).T
            b_r = moe_b_router[i].astype(F32)[:, None]
            for s1, sf, is_ctx in streams:
                gate_t, rank_t, cnt = _router_call(s1, modr, layer, wr_t, b_r, tri, batch=batch, is_ctx=is_ctx)
                outs.append(_moe_call(cnt[:, :, 0].reshape(-1), sf, gate_t, rank_t, wg, wu, wd, s1, modr,
                                      layer, *ln2, batch=batch, is_ctx=is_ctx))
        h = outs[0]
        if need_ctx:
            hc = outs[1]
    return h.reshape(batch, seq, D_MODEL)
```

```python
import functools

import jax
import jax.numpy as jnp
from jax import lax
from jax.experimental import pallas as pl
from jax.experimental.pallas import tpu as pltpu

F32 = jnp.float32
BF16 = jnp.bfloat16

D_MODEL = 1024
DEPTH = 4
GRID_W = 64
HEAD_DIM = 64
NA_ROWS = 8
NA_COLS = 16
SSM_GROUPS = 16
SSM_GROUP_CH = 16
SSM_STATE = 64
SSM_WIDTH = SSM_GROUPS * SSM_GROUP_CH
SSM_FLAT = SSM_GROUPS * SSM_STATE
SW_WINDOW = 128
N_EXPERTS = 8
ADA_CHUNKS = 6
ROPE_THETA = 10000.0
LN_EPS = 1e-6
RMS_EPS = 1e-6
NEG_INF = -1e30
DEEPNORM_ALPHA = (2 * DEPTH) ** 0.25
ATT_SCALE = HEAD_DIM ** -0.5

C_QA, C_KA, C_VA, C_QB, C_KB, C_VB, C_U, C_QD, C_KD, C_VD, IN_WIDTH = (
    0, 256, 512, 768, 1024, 1152, 1280, 1536, 1792, 1920, 2048)

LANES = 128
SUBLANES = 8
MOD_ROWS = 16
VMEM_LIMIT = 56 * 1024 * 1024


def _cparams(*sem):
    return pltpu.CompilerParams(dimension_semantics=sem, vmem_limit_bytes=VMEM_LIMIT)


def _dot(a, b):
    return jnp.dot(a, b, preferred_element_type=F32)


def _dot_nt(a, b):
    return lax.dot_general(a, b, (((1,), (1,)), ((), ())), preferred_element_type=F32)


def _dot_tn(a, b):
    return lax.dot_general(a, b, (((0,), (0,)), ((), ())), preferred_element_type=F32)


def _split_bf16(x):
    hi = x.astype(BF16)
    lo = (x - hi.astype(F32)).astype(BF16)
    return hi, lo


def _layer_norm(z, g, b):
    zc = z - jnp.mean(z, axis=-1, keepdims=True)
    y = zc * lax.rsqrt(jnp.mean(zc * zc, axis=-1, keepdims=True) + LN_EPS)
    return y * g + b


def _mods_kernel(c_ref, w_ref, b_ref, o_ref):
    c = c_ref[...]
    act = (c * jax.nn.sigmoid(c)).astype(BF16)
    o_ref[...] = _dot(act, w_ref[...].astype(BF16)) + b_ref[...]


def _mods_call(c_all, ada_w, ada_b):
    tn = 1536
    n = ADA_CHUNKS * D_MODEL
    out = pl.pallas_call(
        _mods_kernel,
        grid=(DEPTH, n // tn),
        in_specs=[
            pl.BlockSpec((MOD_ROWS, D_MODEL), lambda l, j: (0, 0)),
            pl.BlockSpec((None, D_MODEL, tn), lambda l, j: (l, 0, j)),
            pl.BlockSpec((None, 1, tn), lambda l, j: (l, 0, j)),
        ],
        out_specs=pl.BlockSpec((None, MOD_ROWS, tn), lambda l, j: (l, 0, j)),
        out_shape=jax.ShapeDtypeStruct((DEPTH, MOD_ROWS, n), F32),
        name="mods",
        compiler_params=_cparams("arbitrary", "arbitrary"),
    )(c_all, ada_w, ada_b.reshape(DEPTH, 1, n))
    return out.reshape(DEPTH * MOD_ROWS * ADA_CHUNKS, 1, D_MODEL)


def _mod_spec(layer, chunk, row_fn):
    def index(i, *_):
        return ((layer * MOD_ROWS + row_fn(i)) * ADA_CHUNKS + chunk, 0, 0)
    return pl.BlockSpec((None, 1, D_MODEL), index)


def _seg_rms(x, seg, g):
    hi, lo = _split_bf16(x * x)
    ss = _dot(hi, seg) + _dot(lo, seg)
    return x * lax.rsqrt(ss * (1.0 / HEAD_DIM) + RMS_EPS) * g


def _rope(x, cos, sin_signed, first_half):
    outs = []
    for j in range(x.shape[1] // LANES):
        xs = x[:, j * LANES:(j + 1) * LANES]
        partner = jnp.where(first_half,
                            pltpu.roll(xs, LANES - HEAD_DIM // 2, 1),
                            pltpu.roll(xs, HEAD_DIM // 2, 1))
        outs.append(xs * cos + partner * sin_signed)
    return outs[0] if len(outs) == 1 else jnp.concatenate(outs, axis=1)


def _pad_heads(x, lane, ones_lane=False):
    low = lane < HEAD_DIM
    fill = jnp.where(lane == HEAD_DIM, 1.0, 0.0) if ones_lane else 0.0
    outs = []
    for j in range(x.shape[1] // LANES):
        xs = x[:, j * LANES:(j + 1) * LANES]
        outs.append(jnp.where(low, xs, fill))
        outs.append(jnp.where(low, pltpu.roll(xs, HEAD_DIM, 1), fill))
    return jnp.concatenate(outs, axis=1).astype(BF16)


def _in_proj_kernel(x_ref, sh_ref, sc_ref, w_ref, cos_ref, sin_ref, seg_ref, gq_ref, gk_ref,
                    qa_ref, ka_ref, va_ref, qb_ref, kb_ref, vb_ref, u_ref, qd_ref, kd_ref, vd_ref,
                    *, rope):
    a = (x_ref[...] * (1.0 + sc_ref[...]) + sh_ref[...]).astype(BF16)
    lane = lax.broadcasted_iota(jnp.int32, (a.shape[0], LANES), 1)

    def proj(c0, c1):
        return _dot(a, w_ref[:, c0:c1])

    qa_ref[...] = _pad_heads(proj(C_QA, C_KA) * ATT_SCALE, lane)
    ka_ref[...] = _pad_heads(proj(C_KA, C_VA), lane)
    va_ref[...] = _pad_heads(proj(C_VA, C_QB), lane, ones_lane=True)
    vb_ref[...] = _pad_heads(proj(C_VB, C_U), lane, ones_lane=True)
    u_ref[...] = proj(C_U, C_QD)
    vd_ref[...] = _pad_heads(proj(C_VD, IN_WIDTH), lane, ones_lane=True)

    seg = seg_ref[...]
    qb = _seg_rms(proj(C_QB, C_KB), seg, gq_ref[...])
    kb = _seg_rms(proj(C_KB, C_VB), seg[:LANES, :LANES], gk_ref[...])
    qd = proj(C_QD, C_KD)
    kd = proj(C_KD, C_VD)
    if rope:
        cos = cos_ref[...]
        sin = sin_ref[...]
        first_half = (lane % HEAD_DIM) < (HEAD_DIM // 2)
        qb = _rope(qb, cos, sin, first_half)
        kb = _rope(kb, cos, sin, first_half)
        qd = _rope(qd, cos, sin, first_half)
        kd = _rope(kd, cos, sin, first_half)
    qb_ref[...] = _pad_heads(qb * ATT_SCALE, lane)
    kb_ref[...] = _pad_heads(kb, lane)
    qd_ref[...] = _pad_heads(qd * ATT_SCALE, lane)
    kd_ref[...] = _pad_heads(kd, lane)


def _in_proj_call(x, modr, layer, w_in, cos_t, sin_t, seg, gq, gk, *, batch, is_ctx):
    rows = x.shape[0]
    per_batch = rows // batch
    tm = 256 if is_ctx else 512
    nb = per_batch // tm
    row_fn = (lambda i: batch) if is_ctx else (lambda i: i // nb)
    table_spec = pl.BlockSpec((tm, LANES), (lambda i: (0, 0)) if is_ctx else (lambda i: (i % nb, 0)))
    full = lambda shape: pl.BlockSpec(shape, lambda i: (0,) * len(shape))
    rowblk = lambda w: pl.BlockSpec((tm, w), lambda i: (i, 0))
    wide = lambda w, dt: jax.ShapeDtypeStruct((rows, w), dt)
    out_shapes = [wide(512, BF16), wide(512, BF16), wide(512, BF16),
                  wide(512, BF16), wide(256, BF16), wide(256, BF16),
                  jax.ShapeDtypeStruct((per_batch, batch * SSM_WIDTH), F32),
                  wide(512, BF16), wide(256, BF16), wide(256, BF16)]
    out_specs = [rowblk(512), rowblk(512), rowblk(512), rowblk(512), rowblk(256), rowblk(256),
                 pl.BlockSpec((tm, SSM_WIDTH), lambda i: (i % nb, i // nb)),
                 rowblk(512), rowblk(256), rowblk(256)]
    return pl.pallas_call(
        functools.partial(_in_proj_kernel, rope=not is_ctx),
        grid=(rows // tm,),
        in_specs=[rowblk(D_MODEL), _mod_spec(layer, 0, row_fn), _mod_spec(layer, 1, row_fn),
                  full((D_MODEL, IN_WIDTH)), table_spec, table_spec, full((256, 256)),
                  full((1, 256)), full((1, 128))],
        out_specs=out_specs,
        out_shape=out_shapes,
        name="in_proj_ctx" if is_ctx else "in_proj",
        compiler_params=_cparams("arbitrary"),
    )(x, modr, modr, w_in, cos_t, sin_t, seg, gq, gk)


KV_CHUNK = 256


def _tile(h):
    return slice(h * LANES, (h + 1) * LANES)


def _attend(q, parts, sink=None):
    def score(k_fn, mod_fn, c):
        s = _dot_nt(q, k_fn(c))
        return s if mod_fn is None else mod_fn(s, c)

    tiles = [(k_fn, v_fn, mod_fn, c) for k_fn, v_fn, n, mod_fn in parts for c in range(0, n, KV_CHUNK)]
    m = sink
    acc = None
    for k_fn, v_fn, mod_fn, c in tiles:
        s = score(k_fn, mod_fn, c)
        m_tile = jnp.max(s, axis=-1, keepdims=True)
        m_new = m_tile if m is None else jnp.maximum(m, m_tile)
        pv = _dot(jnp.exp(s - m_new).astype(BF16), v_fn(c))
        acc = pv if acc is None else acc * jnp.exp(m - m_new) + pv
        m = m_new
    den = acc[:, HEAD_DIM:HEAD_DIM + 1]
    if sink is not None:
        den = den + jnp.exp(sink - m)
    return acc * (1.0 / den)


def _pack_pair(a, b):
    lane = lax.broadcasted_iota(jnp.int32, a.shape, 1)
    return jnp.where(lane < HEAD_DIM, a, pltpu.roll(b, HEAD_DIM, 1)).astype(BF16)


def _ref_part(k_ref, v_ref, t, n_keys, start=0, mod_fn=None):
    return (lambda c: k_ref[pl.ds(start + c, KV_CHUNK), _tile(t)],
            lambda c: v_ref[pl.ds(start + c, KV_CHUNK), _tile(t)], n_keys, mod_fn)


def _gqa_group(q_ref, g, parts, sink_pair=None):
    tq = q_ref.shape[0]
    q2 = jnp.concatenate([q_ref[:, _tile(2 * g)], q_ref[:, _tile(2 * g + 1)]], axis=0)
    sink = None
    if sink_pair is not None:
        row = lax.broadcasted_iota(jnp.int32, (2 * tq, 1), 0)
        sink = jnp.where(row < tq, sink_pair[0], sink_pair[1])
    o2 = _attend(q2, parts, sink)
    return _pack_pair(o2[:tq], o2[tq:])


NA_QROWS = 4
NA_BAND = 12


def _na_band_start(j, rows):
    return jnp.clip(j * NA_QROWS - NA_ROWS // 2, 0, rows - NA_BAND)


def _na_kernel(q_ref, k_ref, v_ref, kc_ref, vc_ref, bias_ref, mask_ref, o_ref):
    j = pl.program_id(1)
    rows = k_ref.shape[0] // GRID_W
    start = pl.multiple_of(_na_band_start(j, rows) * GRID_W, GRID_W)
    outs = []
    for h in range(4):
        def local_scores(s, c, h=h):
            cols = slice(c, c + KV_CHUNK)
            return jnp.where(mask_ref[:, cols] > 0.5, s + bias_ref[h, :, cols], NEG_INF)

        parts = [_ref_part(k_ref, v_ref, h, NA_BAND * GRID_W, start, local_scores),
                 _ref_part(kc_ref, vc_ref, h, kc_ref.shape[0])]
        outs.append(_attend(q_ref[:, _tile(h)], parts))
    o_ref[:, _tile(0)] = _pack_pair(outs[0], outs[1])
    o_ref[:, _tile(1)] = _pack_pair(outs[2], outs[3])


def _na_call(qa, ka, va, ka_c, va_c, bias_tab, mask_tab, *, batch):
    seq = qa.shape[0] // batch
    n_ctx = ka_c.shape[0] // batch
    nj = seq // (NA_QROWS * GRID_W)
    tq = NA_QROWS * GRID_W
    nk = NA_BAND * GRID_W

    def pattern(j):
        return jnp.where(j == 0, 0, jnp.where(j == nj - 1, 2, 1))

    return pl.pallas_call(
        _na_kernel,
        grid=(batch, nj),
        in_specs=[
            pl.BlockSpec((tq, 512), lambda b, j: (b * nj + j, 0)),
            pl.BlockSpec((seq, 512), lambda b, j: (b, 0)),
            pl.BlockSpec((seq, 512), lambda b, j: (b, 0)),
            pl.BlockSpec((n_ctx, 512), lambda b, j: (b, 0)),
            pl.BlockSpec((n_ctx, 512), lambda b, j: (b, 0)),
            pl.BlockSpec((None, 4, tq, nk), lambda b, j: (pattern(j), 0, 0, 0)),
            pl.BlockSpec((None, tq, nk), lambda b, j: (pattern(j), 0, 0)),
        ],
        out_specs=pl.BlockSpec((tq, 256), lambda b, j: (b * nj + j, 0)),
        out_shape=jax.ShapeDtypeStruct((qa.shape[0], 256), BF16),
        name="na_attn",
        compiler_params=_cparams("arbitrary", "arbitrary"),
    )(qa, ka, va, ka_c, va_c, bias_tab, mask_tab)


def _na_tables(rpb, seq):
    rows = seq // GRID_W
    nj = rows // NA_QROWS
    c_idx = jnp.arange(GRID_W)
    col_start = jnp.clip(c_idx - NA_COLS // 2, 0, GRID_W - NA_COLS)
    col_valid = (c_idx[None, :] >= col_start[:, None]) & (c_idx[None, :] < col_start[:, None] + NA_COLS)
    col_rel = jnp.clip(c_idx[None, :] - c_idx[:, None], 1 - NA_COLS, NA_COLS - 1) + NA_COLS - 1
    blocks = jnp.array([0, 1, nj - 1])
    q_row = blocks[:, None] * NA_QROWS + jnp.arange(NA_QROWS)[None, :]
    k_row = _na_band_start(blocks, rows)[:, None] + jnp.arange(NA_BAND)[None, :]
    row_start = jnp.clip(q_row - NA_ROWS // 2, 0, rows - NA_ROWS)
    row_valid = ((k_row[:, None, :] >= row_start[:, :, None])
                 & (k_row[:, None, :] < row_start[:, :, None] + NA_ROWS))
    row_rel = jnp.clip(k_row[:, None, :] - q_row[:, :, None] + NA_ROWS - 1, 0, 2 * NA_ROWS - 2)
    row_sel = jax.nn.one_hot(row_rel, 2 * NA_ROWS - 1, dtype=F32)
    col_sel = jax.nn.one_hot(col_rel, 2 * NA_COLS - 1, dtype=F32)
    exact = lax.Precision.HIGHEST
    by_col = jnp.einsum("hab,qkb->haqk", rpb.astype(F32), col_sel, precision=exact)
    bias = jnp.einsum("pria,haqk->phrqik", row_sel, by_col, precision=exact)
    valid = row_valid[:, :, None, :, None] & col_valid[None, None, :, None, :]
    tq, nk = NA_QROWS * GRID_W, NA_BAND * GRID_W
    return bias.reshape(3, 4, tq, nk), valid.astype(F32).reshape(3, tq, nk)


def _global_kernel(q_ref, k_ref, v_ref, kc_ref, vc_ref, o_ref):
    for g in range(2):
        parts = [_ref_part(kc_ref, vc_ref, g, kc_ref.shape[0]), _ref_part(k_ref, v_ref, g, k_ref.shape[0])]
        o_ref[:, _tile(g)] = _gqa_group(q_ref, g, parts)


def _global_call(qb, kb, vb, kb_c, vb_c, *, batch):
    seq = qb.shape[0] // batch
    n_ctx = kb_c.shape[0] // batch
    tq = 256
    nq = seq // tq
    return pl.pallas_call(
        _global_kernel,
        grid=(batch, nq),
        in_specs=[
            pl.BlockSpec((tq, 512), lambda b, i: (b * nq + i, 0)),
            pl.BlockSpec((seq, 256), lambda b, i: (b, 0)),
            pl.BlockSpec((seq, 256), lambda b, i: (b, 0)),
            pl.BlockSpec((n_ctx, 256), lambda b, i: (b, 0)),
            pl.BlockSpec((n_ctx, 256), lambda b, i: (b, 0)),
        ],
        out_specs=pl.BlockSpec((tq, 256), lambda b, i: (b * nq + i, 0)),
        out_shape=jax.ShapeDtypeStruct((qb.shape[0], 256), BF16),
        name="global_attn",
        compiler_params=_cparams("arbitrary", "arbitrary"),
    )(qb, kb, vb, kb_c, vb_c)


SW_TQ = 256
SW_SPAN = SW_TQ + 2 * SW_WINDOW


def _window_start(i, seq):
    return jnp.clip(i * SW_TQ - SW_WINDOW, 0, seq - SW_SPAN)


def _window_kernel(sink_ref, q_ref, k_ref, v_ref, kc_ref, vc_ref, mask_ref, o_ref):
    i = pl.program_id(1)
    start = pl.multiple_of(_window_start(i, k_ref.shape[0]), LANES)

    def in_window(s, c):
        return jnp.where(mask_ref[:, c:c + KV_CHUNK] > 0.5, s, NEG_INF)

    for g in range(2):
        parts = [_ref_part(k_ref, v_ref, g, SW_SPAN, start, in_window),
                 _ref_part(kc_ref, vc_ref, g, kc_ref.shape[0])]
        o_ref[:, _tile(g)] = _gqa_group(q_ref, g, parts, sink_pair=(sink_ref[2 * g], sink_ref[2 * g + 1]))


def _window_mask_table(seq):
    nq = seq // SW_TQ
    blocks = jnp.array([0, 1, nq - 1])
    qpos = blocks[:, None] * SW_TQ + jnp.arange(SW_TQ)[None, :]
    kpos = _window_start(blocks, seq)[:, None] + jnp.arange(SW_SPAN)[None, :]
    valid = jnp.abs(kpos[:, None, :] - qpos[:, :, None]) <= SW_WINDOW
    return jnp.tile(valid.astype(F32), (1, 2, 1))


def _window_call(sink, qd, kd, vd, kd_c, vd_c, mask_tab, *, batch):
    seq = qd.shape[0] // batch
    n_ctx = kd_c.shape[0] // batch
    nq = seq // SW_TQ

    def pattern(i):
        return jnp.where(i == 0, 0, jnp.where(i == nq - 1, 2, 1))

    return pl.pallas_call(
        _window_kernel,
        grid=(batch, nq),
        in_specs=[
            pl.BlockSpec(memory_space=pltpu.SMEM),
            pl.BlockSpec((SW_TQ, 512), lambda b, i: (b * nq + i, 0)),
            pl.BlockSpec((seq, 256), lambda b, i: (b, 0)),
            pl.BlockSpec((seq, 256), lambda b, i: (b, 0)),
            pl.BlockSpec((n_ctx, 256), lambda b, i: (b, 0)),
            pl.BlockSpec((n_ctx, 256), lambda b, i: (b, 0)),
            pl.BlockSpec((None, 2 * SW_TQ, SW_SPAN), lambda b, i: (pattern(i), 0, 0)),
        ],
        out_specs=pl.BlockSpec((SW_TQ, 256), lambda b, i: (b * nq + i, 0)),
        out_shape=jax.ShapeDtypeStruct((qd.shape[0], 256), BF16),
        name="window_attn",
        compiler_params=_cparams("arbitrary", "arbitrary"),
    )(sink, qd, kd, vd, kd_c, vd_c, mask_tab)


def _ctx_attn_kernel(sink_ref, qa_ref, ka_ref, va_ref, qb_ref, kb_ref, vb_ref, qd_ref, kd_ref, vd_ref,
                     oa_ref, ob_ref, od_ref):
    n = ka_ref.shape[0]
    outs = [_attend(qa_ref[:, _tile(h)], [_ref_part(ka_ref, va_ref, h, n)]) for h in range(4)]
    oa_ref[:, _tile(0)] = _pack_pair(outs[0], outs[1])
    oa_ref[:, _tile(1)] = _pack_pair(outs[2], outs[3])
    for g in range(2):
        ob_ref[:, _tile(g)] = _gqa_group(qb_ref, g, [_ref_part(kb_ref, vb_ref, g, n)])
        od_ref[:, _tile(g)] = _gqa_group(qd_ref, g, [_ref_part(kd_ref, vd_ref, g, n)],
                                         sink_pair=(sink_ref[2 * g], sink_ref[2 * g + 1]))


def _ctx_attn_call(sink, cx, *, batch):
    n_ctx = cx["qa"].shape[0] // batch
    blk = lambda w: pl.BlockSpec((n_ctx, w), lambda b: (b, 0))
    names = ("qa", "ka", "va", "qb", "kb", "vb", "qd", "kd", "vd")
    widths = (512, 512, 512, 512, 256, 256, 512, 256, 256)
    shape = jax.ShapeDtypeStruct((cx["qa"].shape[0], 256), BF16)
    return pl.pallas_call(
        _ctx_attn_kernel,
        grid=(batch,),
        in_specs=[pl.BlockSpec(memory_space=pltpu.SMEM)] + [blk(w) for w in widths],
        out_specs=[blk(256)] * 3,
        out_shape=[shape] * 3,
        name="ctx_attn",
        compiler_params=_cparams("arbitrary"),
    )(sink, *[cx[n] for n in names])


def _s5_discretise(lam_re, lam_im, log_step, b_re, b_im):
    step = jnp.exp(log_step.astype(F32))[:, None]
    lam_re = lam_re.astype(F32)
    lam_im = lam_im.astype(F32)
    mag = jnp.exp(lam_re * step)
    ab_re = mag * jnp.cos(lam_im * step)
    ab_im = mag * jnp.sin(lam_im * step)
    den = lam_re * lam_re + lam_im * lam_im
    num_re = ab_re - 1.0
    f_re = ((num_re * lam_re + ab_im * lam_im) / den)[..., None]
    f_im = ((ab_im * lam_re - num_re * lam_im) / den)[..., None]
    b_re = b_re.astype(F32)
    b_im = b_im.astype(F32)
    return ab_re, ab_im, f_re * b_re - f_im * b_im, f_re * b_im + f_im * b_re


def _s5_operands(lam_re, lam_im, log_step, b_re, b_im, c_re, c_im):
    ab_re, ab_im, bb_re, bb_im = _s5_discretise(lam_re, lam_im, log_step, b_re, b_im)
    eye = jnp.eye(SSM_GROUPS, dtype=F32)
    bd_in = lambda m: jnp.einsum("gph,gk->ghkp", m, eye).reshape(SSM_WIDTH, SSM_FLAT)
    bd_out = lambda m: jnp.einsum("ghp,gk->kpgh", m.astype(F32), eye).reshape(SSM_FLAT, SSM_WIDTH)
    a = jnp.stack([ab_re.reshape(SSM_FLAT), ab_im.reshape(SSM_FLAT)])
    bbd = jnp.concatenate([bd_in(bb_re), bd_in(bb_im)], axis=1).astype(BF16)
    cbd = jnp.concatenate([bd_out(c_re), -bd_out(c_im)], axis=0).astype(BF16)
    return a, bbd, cbd


def _s5_kernel(*refs, reverse, final, batch, n_ctx_chunks):
    if final:
        (uc_ref, ul_ref, pc_ref, pl_ref, a_ref, bbd_ref, cbd_ref, d_ref, wglu_ref,
         oc_ref, ol_ref, bu_ref, h_ref) = refs
    else:
        uc_ref, ul_ref, a_ref, bbd_ref, cbd_ref, oc_ref, ol_ref, bu_ref, h_ref = refs
        pc_ref = pl_ref = d_ref = wglu_ref = None
    c = pl.program_id(0)
    steps = bu_ref.shape[0] // batch

    @pl.when(c == 0)
    def _():
        h_ref[...] = jnp.zeros_like(h_ref)

    def run(u_ref, prev_ref, o_ref):
        u = u_ref[...]
        bu_ref[...] = _dot(u.astype(BF16), bbd_ref[...])
        ar = jnp.broadcast_to(a_ref[0:1, :], (batch, SSM_FLAT))
        ai = jnp.broadcast_to(a_ref[1:2, :], (batch, SSM_FLAT))

        def step(i, h):
            t = steps - 1 - i if reverse else i
            row = pl.multiple_of(t * batch, batch)
            b = bu_ref[pl.ds(row, batch), :]
            hr, hi = h[:, :SSM_FLAT], h[:, SSM_FLAT:]
            nr = ar * hr - ai * hi + b[:, :SSM_FLAT]
            ni = ar * hi + ai * hr + b[:, SSM_FLAT:]
            hn = jnp.concatenate([nr, ni], axis=1)
            bu_ref[pl.ds(row, batch), :] = hn
            return hn

        h_ref[...] = lax.fori_loop(0, steps, step, h_ref[...], unroll=4)
        y = _dot(bu_ref[...].astype(BF16), cbd_ref[...])
        if not final:
            o_ref[...] = y
        else:
            y = jax.nn.gelu(y + prev_ref[...] + d_ref[...] * u)
            z = _dot(y.astype(BF16), wglu_ref[...])
            o_ref[...] = (y * jax.nn.sigmoid(z)).astype(BF16)

    @pl.when(c < n_ctx_chunks)
    def _():
        run(uc_ref, pc_ref, oc_ref)

    @pl.when(c >= n_ctx_chunks)
    def _():
        run(ul_ref, pl_ref, ol_ref)


def _s5_pass(u_ctx, u_lat, prev, a, bbd, cbd, d_skip, w_glu, *, batch, reverse):
    final = prev is not None
    t_chunk = 128
    rows = t_chunk * batch
    ncc = u_ctx.shape[0] // t_chunk
    nlc = u_lat.shape[0] // t_chunk
    flat = lambda x: x.reshape(x.shape[0] * batch, SSM_WIDTH)
    if reverse:
        ctx_idx = lambda c: (jnp.maximum(ncc - 1 - c, 0), 0)
        lat_idx = lambda c: (nlc - 1 - jnp.maximum(c - ncc, 0), 0)
    else:
        ctx_idx = lambda c: (jnp.minimum(c, ncc - 1), 0)
        lat_idx = lambda c: (jnp.maximum(c - ncc, 0), 0)
    full = lambda shape: pl.BlockSpec(shape, lambda c: (0,) * len(shape))
    u_specs = [pl.BlockSpec((rows, SSM_WIDTH), ctx_idx), pl.BlockSpec((rows, SSM_WIDTH), lat_idx)]
    par_specs = [full((2, SSM_FLAT)), full((SSM_WIDTH, 2 * SSM_FLAT)), full((2 * SSM_FLAT, SSM_WIDTH))]
    args = [flat(u_ctx), flat(u_lat)]
    in_specs = list(u_specs)
    if final:
        args += [flat(prev[0]), flat(prev[1])]
        in_specs += u_specs
    args += [a, bbd, cbd]
    in_specs += par_specs
    if final:
        args += [d_skip, w_glu]
        in_specs += [full((1, SSM_WIDTH)), full((SSM_WIDTH, SSM_WIDTH))]
    dt = BF16 if final else F32
    oc, ol = pl.pallas_call(
        functools.partial(_s5_kernel, reverse=reverse, final=final, batch=batch, n_ctx_chunks=ncc),
        grid=(ncc + nlc,),
        in_specs=in_specs,
        out_specs=u_specs,
        out_shape=[jax.ShapeDtypeStruct((u_ctx.shape[0] * batch, SSM_WIDTH), dt),
                   jax.ShapeDtypeStruct((u_lat.shape[0] * batch, SSM_WIDTH), dt)],
        scratch_shapes=[pltpu.VMEM((rows, 2 * SSM_FLAT), F32), pltpu.VMEM((batch, 2 * SSM_FLAT), F32)],
        name="s5_bwd_glu" if final else "s5_fwd",
        compiler_params=_cparams("arbitrary"),
    )(*args)
    return oc.reshape(u_ctx.shape), ol.reshape(u_lat.shape)


def _out_proj_kernel(oa_ref, ob_ref, oc_ref, od_ref, w_ref, x_ref, g1_ref, sh2_ref, sc2_ref,
                     lng_ref, lnb_ref, h_ref, f_ref):
    y = (_dot(oa_ref[...], w_ref[0:256, :]) + _dot(ob_ref[...], w_ref[256:512, :])
         + _dot(oc_ref[...], w_ref[512:768, :]) + _dot(od_ref[...], w_ref[768:1024, :]))
    h1 = _layer_norm(DEEPNORM_ALPHA * x_ref[...] + g1_ref[...] * y, lng_ref[...], lnb_ref[...])
    h_ref[...] = h1
    f_ref[...] = (h1 * (1.0 + sc2_ref[...]) + sh2_ref[...]).astype(BF16)


def _out_proj_call(oa, ob, oc_tm, od, w_out, x, modr, layer, lng, lnb, *, batch, is_ctx):
    rows = x.shape[0]
    per_batch = rows // batch
    tm = 256 if is_ctx else 512
    nb = per_batch // tm
    row_fn = (lambda i: batch) if is_ctx else (lambda i: i // nb)
    rowblk = lambda w: pl.BlockSpec((tm, w), lambda i: (i, 0))
    full = lambda shape: pl.BlockSpec(shape, lambda i: (0,) * len(shape))
    return pl.pallas_call(
        _out_proj_kernel,
        grid=(rows // tm,),
        in_specs=[rowblk(256), rowblk(256),
                  pl.BlockSpec((tm, SSM_WIDTH), lambda i: (i % nb, i // nb)),
                  rowblk(256), full((D_MODEL, D_MODEL)), rowblk(D_MODEL),
                  _mod_spec(layer, 2, row_fn), _mod_spec(layer, 3, row_fn), _mod_spec(layer, 4, row_fn),
                  full((1, D_MODEL)), full((1, D_MODEL))],
        out_specs=[rowblk(D_MODEL), rowblk(D_MODEL)],
        out_shape=[jax.ShapeDtypeStruct((rows, D_MODEL), F32), jax.ShapeDtypeStruct((rows, D_MODEL), BF16)],
        name="out_proj_ctx" if is_ctx else "out_proj",
        compiler_params=_cparams("arbitrary"),
    )(oa, ob, oc_tm, od, w_out, x, modr, modr, modr, lng, lnb)


def _ffn_kernel(f_ref, wg_ref, wu_ref, wd_ref, h_ref, g2_ref, lng_ref, lnb_ref, o_ref, acc_ref):
    j = pl.program_id(1)

    @pl.when(j == 0)
    def _():
        acc_ref[...] = jnp.zeros_like(acc_ref)

    f = f_ref[...]
    g = _dot(f, wg_ref[...])
    u = _dot(f, wu_ref[...])
    acc_ref[...] += _dot((g * jax.nn.sigmoid(g) * u).astype(BF16), wd_ref[...])

    @pl.when(j == pl.num_programs(1) - 1)
    def _():
        o_ref[...] = _layer_norm(DEEPNORM_ALPHA * h_ref[...] + g2_ref[...] * acc_ref[...],
                                 lng_ref[...], lnb_ref[...])


def _ffn_call(f_in, wg, wu, wd, h1, modr, layer, lng, lnb, *, batch, is_ctx):
    rows = h1.shape[0]
    per_batch = rows // batch
    tm = 256 if is_ctx else 512
    d_ff = wg.shape[1]
    tf = d_ff // 2
    nb = per_batch // tm
    row_fn = (lambda i: batch) if is_ctx else (lambda i: i // nb)
    rowblk = lambda w: pl.BlockSpec((tm, w), lambda i, j: (i, 0))
    full = lambda shape: pl.BlockSpec(shape, lambda i, j: (0,) * len(shape))
    return pl.pallas_call(
        _ffn_kernel,
        grid=(rows // tm, d_ff // tf),
        in_specs=[rowblk(D_MODEL),
                  pl.BlockSpec((D_MODEL, tf), lambda i, j: (0, j)),
                  pl.BlockSpec((D_MODEL, tf), lambda i, j: (0, j)),
                  pl.BlockSpec((tf, D_MODEL), lambda i, j: (j, 0)),
                  rowblk(D_MODEL), _mod_spec(layer, 5, row_fn), full((1, D_MODEL)), full((1, D_MODEL))],
        out_specs=rowblk(D_MODEL),
        out_shape=jax.ShapeDtypeStruct((rows, D_MODEL), F32),
        scratch_shapes=[pltpu.VMEM((tm, D_MODEL), F32)],
        name="ffn_ctx" if is_ctx else "ffn",
        compiler_params=_cparams("arbitrary", "arbitrary"),
    )(f_in, wg, wu, wd, h1, modr, lng, lnb)


MOE_TM = 1024
MOE_CHUNK = 288


def _router_kernel(h_ref, sh2_ref, sc2_ref, wr_ref, br_ref, tri_ref, gate_ref, rank_ref, cnt_ref):
    f = h_ref[...] * (1.0 + sc2_ref[...]) + sh2_ref[...]
    f_hi, f_lo = _split_bf16(f)
    w_hi, w_lo = _split_bf16(wr_ref[...])
    logits = _dot_nt(w_hi, f_hi) + _dot_nt(w_hi, f_lo) + _dot_nt(w_lo, f_hi) + br_ref[...]
    ie = lax.broadcasted_iota(jnp.int32, logits.shape, 0)
    m1 = jnp.max(logits, axis=0, keepdims=True)
    i1 = jnp.min(jnp.where(logits == m1, ie, N_EXPERTS), axis=0, keepdims=True)
    rest = jnp.where(ie == i1, -jnp.inf, logits)
    m2 = jnp.max(rest, axis=0, keepdims=True)
    i2 = jnp.min(jnp.where(rest == m2, ie, N_EXPERTS), axis=0, keepdims=True)
    e2 = jnp.exp(m2 - m1)
    den = 1.0 + e2
    gate_ref[...] = jnp.where(ie == i1, 1.0 / den, 0.0) + jnp.where(ie == i2, e2 / den, 0.0)
    sel = jnp.where((ie == i1) | (ie == i2), 1.0, 0.0)
    rank = _dot(sel.astype(BF16), tri_ref[...])
    rank_ref[...] = jnp.where(sel > 0.0, rank, -1.0)
    cnt = jnp.sum(sel, axis=1, keepdims=True)
    cnt_ref[...] = jnp.broadcast_to(cnt, cnt_ref.shape).astype(jnp.int32)


def _router_call(h1, modr, layer, wr_t, b_r, tri, *, batch, is_ctx):
    rows = h1.shape[0]
    tm = MOE_TM
    nblk = rows // tm
    nb = (rows // batch) // tm if not is_ctx else 1
    row_fn = (lambda i: batch) if is_ctx else (lambda i: i // nb)
    full = lambda shape: pl.BlockSpec(shape, lambda i: (0,) * len(shape))
    blk3 = lambda w: pl.BlockSpec((None, N_EXPERTS, w), lambda i: (i, 0, 0))
    return pl.pallas_call(
        _router_kernel,
        grid=(nblk,),
        in_specs=[pl.BlockSpec((tm, D_MODEL), lambda i: (i, 0)),
                  _mod_spec(layer, 3, row_fn), _mod_spec(layer, 4, row_fn),
                  full((N_EXPERTS, D_MODEL)), full((N_EXPERTS, 1)), full((tm, tm))],
        out_specs=[blk3(tm), blk3(tm), blk3(LANES)],
        out_shape=[jax.ShapeDtypeStruct((nblk, N_EXPERTS, tm), F32),
                   jax.ShapeDtypeStruct((nblk, N_EXPERTS, tm), F32),
                   jax.ShapeDtypeStruct((nblk, N_EXPERTS, LANES), jnp.int32)],
        name="router_ctx" if is_ctx else "router",
        compiler_params=_cparams("arbitrary"),
    )(h1, modr, modr, wr_t, b_r, tri)


def _moe_kernel(cnt_ref, f_ref, gate_ref, rank_ref, wg_ref, wu_ref, wd_ref, h_ref, g2_ref,
                lng_ref, lnb_ref, o_ref, acc_ref, xg_ref, yg_ref):
    i, e, j = pl.program_id(0), pl.program_id(1), pl.program_id(2)
    tm = f_ref.shape[0]
    last_j = pl.num_programs(2) - 1
    n_rows = cnt_ref[i * N_EXPERTS + e]
    n_chunks = lax.div(n_rows + (MOE_CHUNK - 1), MOE_CHUNK)

    @pl.when((e == 0) & (j == 0))
    def _():
        acc_ref[...] = jnp.zeros_like(acc_ref)

    rank_row = rank_ref[pl.ds(e, 1), :]
    gate_row = gate_ref[pl.ds(e, 1), :]

    def chunk(c, carry):
        base = pl.multiple_of(c * MOE_CHUNK, 32)
        rows = pl.ds(base, MOE_CHUNK)

        def one_hot():
            slot = (lax.broadcasted_iota(jnp.int32, (MOE_CHUNK, tm), 0) + base).astype(F32)
            return rank_row == slot

        @pl.when(j == 0)
        def _():
            sel = jnp.where(one_hot(), 1.0, 0.0).astype(BF16)
            xg_ref[rows, :] = _dot(sel, f_ref[...]).astype(BF16)

        xc = xg_ref[rows, :]
        g = _dot(xc, wg_ref[...])
        u = _dot(xc, wu_ref[...])
        y = _dot((g * jax.nn.sigmoid(g) * u).astype(BF16), wd_ref[...])

        @pl.when(j == 0)
        def _():
            yg_ref[rows, :] = y

        @pl.when((j > 0) & (j < last_j))
        def _():
            yg_ref[rows, :] += y

        @pl.when(j == last_j)
        def _():
            hit = one_hot()
            sel = jnp.where(hit, 1.0, 0.0).astype(BF16)
            row_gate = jnp.sum(jnp.where(hit, gate_row, 0.0), axis=1, keepdims=True)
            y_hi, y_lo = _split_bf16((yg_ref[rows, :] + y) * row_gate)
            acc_ref[...] += _dot_tn(jnp.concatenate([sel, sel], axis=0), jnp.concatenate([y_hi, y_lo], axis=0))

        return carry

    lax.fori_loop(0, n_chunks, chunk, 0)

    @pl.when((e == pl.num_programs(1) - 1) & (j == pl.num_programs(2) - 1))
    def _():
        o_ref[...] = _layer_norm(DEEPNORM_ALPHA * h_ref[...] + g2_ref[...] * acc_ref[...],
                                 lng_ref[...], lnb_ref[...])


def _moe_call(cnt, f_in, gate_t, rank_t, wg, wu, wd, h1, modr, layer, lng, lnb, *, batch, is_ctx):
    rows = h1.shape[0]
    tm = MOE_TM
    d_ff = wg.shape[2]
    tf = d_ff // 4
    max_rows = pl.cdiv(tm, MOE_CHUNK) * MOE_CHUNK
    nb = (rows // batch) // tm if not is_ctx else 1
    row_fn = (lambda i: batch) if is_ctx else (lambda i: i // nb)
    rowblk = lambda w: pl.BlockSpec((tm, w), lambda i, e, j, c: (i, 0))
    full = lambda shape: pl.BlockSpec(shape, lambda i, e, j, c: (0,) * len(shape))
    blk3 = pl.BlockSpec((None, N_EXPERTS, tm), lambda i, e, j, c: (i, 0, 0))
    grid_spec = pltpu.PrefetchScalarGridSpec(
        num_scalar_prefetch=1,
        grid=(rows // tm, N_EXPERTS, d_ff // tf),
        in_specs=[rowblk(D_MODEL), blk3, blk3,
                  pl.BlockSpec((None, D_MODEL, tf), lambda i, e, j, c: (e, 0, j)),
                  pl.BlockSpec((None, D_MODEL, tf), lambda i, e, j, c: (e, 0, j)),
                  pl.BlockSpec((None, tf, D_MODEL), lambda i, e, j, c: (e, j, 0)),
                  rowblk(D_MODEL), _mod_spec(layer, 5, row_fn), full((1, D_MODEL)), full((1, D_MODEL))],
        out_specs=rowblk(D_MODEL),
        scratch_shapes=[pltpu.VMEM((tm, D_MODEL), F32), pltpu.VMEM((max_rows, D_MODEL), BF16),
                        pltpu.VMEM((max_rows, D_MODEL), F32)],
    )
    return pl.pallas_call(
        _moe_kernel,
        grid_spec=grid_spec,
        out_shape=jax.ShapeDtypeStruct((rows, D_MODEL), F32),
        name="moe_ctx" if is_ctx else "moe",
        compiler_params=_cparams("arbitrary", "arbitrary", "arbitrary"),
    )(cnt, f_in, gate_t, rank_t, wg, wu, wd, h1, modr, lng, lnb)


def _rope_tables(seq):
    pos = jnp.arange(seq, dtype=jnp.int32)
    row = (pos // GRID_W).astype(F32)
    col = (pos % GRID_W).astype(F32)
    n_freq = HEAD_DIM // 4
    inv_freq = ROPE_THETA ** (-jnp.arange(n_freq, dtype=F32) / n_freq)
    ang = jnp.concatenate([row[:, None] * inv_freq, col[:, None] * inv_freq], axis=-1)
    cos, sin = jnp.cos(ang), jnp.sin(ang)
    reps = LANES // (HEAD_DIM // 2)
    sign = jnp.tile(jnp.concatenate([-jnp.ones(HEAD_DIM // 2, F32), jnp.ones(HEAD_DIM // 2, F32)]),
                    LANES // HEAD_DIM)
    return jnp.tile(cos, (1, reps)), jnp.tile(sin, (1, reps)) * sign


def kernel(x, c, ctx, c_ctx, ada_w, ada_b, w_in, w_out, na_rpb, ga_q_norm, ga_k_norm,
           ssm_lambda_re, ssm_lambda_im, ssm_log_step, ssm_b_re, ssm_b_im, ssm_c_re, ssm_c_im,
           ssm_d, ssm_w_glu, sw_sink, ln1_g, ln1_b, ln2_g, ln2_b,
           ffn_w_gate, ffn_w_up, ffn_w_down,
           moe_w_router, moe_b_router, moe_w_gate, moe_w_up, moe_w_down):
    batch, seq, _ = x.shape
    n_ctx = ctx.shape[1]
    assert batch < MOD_ROWS and seq % 512 == 0 and n_ctx % 256 == 0

    c_all = jnp.zeros((MOD_ROWS, D_MODEL), F32).at[:batch].set(c).at[batch].set(c_ctx)
    modr = _mods_call(c_all, ada_w, ada_b)

    cos_t, sin_t = _rope_tables(seq)
    seg = jnp.kron(jnp.eye(256 // HEAD_DIM, dtype=F32), jnp.ones((HEAD_DIM, HEAD_DIM), F32)).astype(BF16)
    tri = (jnp.arange(MOE_TM)[:, None] < jnp.arange(MOE_TM)[None, :]).astype(BF16)
    sw_mask = _window_mask_table(seq)

    h = x.reshape(batch * seq, D_MODEL)
    hc = ctx.reshape(batch * n_ctx, D_MODEL)
    names = ("qa", "ka", "va", "qb", "kb", "vb", "u", "qd", "kd", "vd")

    for layer in range(DEPTH):
        need_ctx = layer < DEPTH - 1
        w_in_l = w_in[layer].astype(BF16)
        w_out_l = w_out[layer].astype(BF16)
        gq = jnp.tile(ga_q_norm[layer].astype(F32), 256 // HEAD_DIM)[None, :]
        gk = jnp.tile(ga_k_norm[layer].astype(F32), 128 // HEAD_DIM)[None, :]
        sink = sw_sink[layer].astype(F32)
        ln1 = (ln1_g[layer][None, :], ln1_b[layer][None, :])
        ln2 = (ln2_g[layer][None, :], ln2_b[layer][None, :])

        lat = dict(zip(names, _in_proj_call(h, modr, layer, w_in_l, cos_t, sin_t, seg, gq, gk,
                                            batch=batch, is_ctx=False)))
        cx = dict(zip(names, _in_proj_call(hc, modr, layer, w_in_l, cos_t, sin_t, seg, gq, gk,
                                           batch=batch, is_ctx=True)))

        na_bias, na_mask = _na_tables(na_rpb[layer], seq)
        out_a = _na_call(lat["qa"], lat["ka"], lat["va"], cx["ka"], cx["va"], na_bias, na_mask, batch=batch)
        out_b = _global_call(lat["qb"], lat["kb"], lat["vb"], cx["kb"], cx["vb"], batch=batch)
        out_d = _window_call(sink, lat["qd"], lat["kd"], lat["vd"], cx["kd"], cx["vd"], sw_mask, batch=batch)

        dirs = [_s5_operands(ssm_lambda_re[layer, d], ssm_lambda_im[layer, d], ssm_log_step[layer, d],
                             ssm_b_re[layer, d], ssm_b_im[layer, d], ssm_c_re[layer, d], ssm_c_im[layer, d])
                for d in range(2)]
        fwd = _s5_pass(cx["u"], lat["u"], None, *dirs[0], None, None, batch=batch, reverse=False)
        ctx_c, out_c = _s5_pass(cx["u"], lat["u"], fwd, *dirs[1], ssm_d[layer].astype(F32)[None, :],
                                ssm_w_glu[layer].astype(BF16), batch=batch, reverse=True)

        h1, f_in = _out_proj_call(out_a, out_b, out_c, out_d, w_out_l, h, modr, layer, *ln1,
                                  batch=batch, is_ctx=False)
        streams = [(h1, f_in, False)]
        if need_ctx:
            ctx_a, ctx_b, ctx_d = _ctx_attn_call(sink, cx, batch=batch)
            hc1, fc_in = _out_proj_call(ctx_a, ctx_b, ctx_c, ctx_d, w_out_l, hc, modr, layer, *ln1,
                                        batch=batch, is_ctx=True)
            streams.append((hc1, fc_in, True))

        i = layer // 2
        outs = []
        if layer % 2 == 0:
            wg, wu, wd = (ffn_w_gate[i].astype(BF16), ffn_w_up[i].astype(BF16), ffn_w_down[i].astype(BF16))
            for s1, sf, is_ctx in streams:
                outs.append(_ffn_call(sf, wg, wu, wd, s1, modr, layer, *ln2, batch=batch, is_ctx=is_ctx))
        else:
            wg, wu, wd = (moe_w_gate[i].astype(BF16), moe_w_up[i].astype(BF16), moe_w_down[i].astype(BF16))
            wr_t = moe_w_router[i].astype(F32).T
            b_r = moe_b_router[i].astype(F32)[:, None]
            for s1, sf, is_ctx in streams:
                gate_t, rank_t, cnt = _router_call(s1, modr, layer, wr_t, b_r, tri, batch=batch, is_ctx=is_ctx)
                outs.append(_moe_call(cnt[:, :, 0].reshape(-1), sf, gate_t, rank_t, wg, wu, wd, s1, modr,
                                      layer, *ln2, batch=batch, is_ctx=is_ctx))
        h = outs[0]
        if need_ctx:
            hc = outs[1]
    return h.reshape(batch, seq, D_MODEL)
```

```python
import functools

import jax
import jax.numpy as jnp
from jax import lax
from jax.experimental import pallas as pl
from jax.experimental.pallas import tpu as pltpu

F32 = jnp.float32
BF16 = jnp.bfloat16

D_MODEL = 1024
DEPTH = 4
GRID_W = 64
HEAD_DIM = 64
NA_ROWS = 8
NA_COLS = 16
SSM_GROUPS = 16
SSM_GROUP_CH = 16
SSM_STATE = 64
SSM_WIDTH = SSM_GROUPS * SSM_GROUP_CH
SSM_FLAT = SSM_GROUPS * SSM_STATE
SW_WINDOW = 128
N_EXPERTS = 8
ADA_CHUNKS = 6
ROPE_THETA = 10000.0
LN_EPS = 1e-6
RMS_EPS = 1e-6
NEG_INF = -1e30
DEEPNORM_ALPHA = (2 * DEPTH) ** 0.25
ATT_SCALE = HEAD_DIM ** -0.5

C_QA, C_KA, C_VA, C_QB, C_KB, C_VB, C_U, C_QD, C_KD, C_VD, IN_WIDTH = (
    0, 256, 512, 768, 1024, 1152, 1280, 1536, 1792, 1920, 2048)

LANES = 128
SUBLANES = 8
MOD_ROWS = 16
VMEM_LIMIT = 56 * 1024 * 1024


def _cparams(*sem):
    return pltpu.CompilerParams(dimension_semantics=sem, vmem_limit_bytes=VMEM_LIMIT)


def _dot(a, b):
    return jnp.dot(a, b, preferred_element_type=F32)


def _dot_nt(a, b):
    return lax.dot_general(a, b, (((1,), (1,)), ((), ())), preferred_element_type=F32)


def _dot_tn(a, b):
    return lax.dot_general(a, b, (((0,), (0,)), ((), ())), preferred_element_type=F32)


def _split_bf16(x):
    hi = x.astype(BF16)
    lo = (x - hi.astype(F32)).astype(BF16)
    return hi, lo


def _layer_norm(z, g, b):
    zc = z - jnp.mean(z, axis=-1, keepdims=True)
    y = zc * lax.rsqrt(jnp.mean(zc * zc, axis=-1, keepdims=True) + LN_EPS)
    return y * g + b


def _mods_kernel(c_ref, w_ref, b_ref, o_ref):
    c = c_ref[...]
    act = (c * jax.nn.sigmoid(c)).astype(BF16)
    o_ref[...] = _dot(act, w_ref[...].astype(BF16)) + b_ref[...]


def _mods_call(c_all, ada_w, ada_b):
    tn = 1536
    n = ADA_CHUNKS * D_MODEL
    out = pl.pallas_call(
        _mods_kernel,
        grid=(DEPTH, n // tn),
        in_specs=[
            pl.BlockSpec((MOD_ROWS, D_MODEL), lambda l, j: (0, 0)),
            pl.BlockSpec((None, D_MODEL, tn), lambda l, j: (l, 0, j)),
            pl.BlockSpec((None, 1, tn), lambda l, j: (l, 0, j)),
        ],
        out_specs=pl.BlockSpec((None, MOD_ROWS, tn), lambda l, j: (l, 0, j)),
        out_shape=jax.ShapeDtypeStruct((DEPTH, MOD_ROWS, n), F32),
        name="mods",
        compiler_params=_cparams("arbitrary", "arbitrary"),
    )(c_all, ada_w, ada_b.reshape(DEPTH, 1, n))
    return out.reshape(DEPTH * MOD_ROWS * ADA_CHUNKS, 1, D_MODEL)


def _mod_spec(layer, chunk, row_fn):
    def index(i, *_):
        return ((layer * MOD_ROWS + row_fn(i)) * ADA_CHUNKS + chunk, 0, 0)
    return pl.BlockSpec((None, 1, D_MODEL), index)


def _seg_rms(x, seg, g):
    hi, lo = _split_bf16(x * x)
    ss = _dot(hi, seg) + _dot(lo, seg)
    return x * lax.rsqrt(ss * (1.0 / HEAD_DIM) + RMS_EPS) * g


def _rope(x, cos, sin_signed, first_half):
    outs = []
    for j in range(x.shape[1] // LANES):
        xs = x[:, j * LANES:(j + 1) * LANES]
        partner = jnp.where(first_half,
                            pltpu.roll(xs, LANES - HEAD_DIM // 2, 1),
                            pltpu.roll(xs, HEAD_DIM // 2, 1))
        outs.append(xs * cos + partner * sin_signed)
    return outs[0] if len(outs) == 1 else jnp.concatenate(outs, axis=1)


def _pad_heads(x, lane, ones_lane=False):
    low = lane < HEAD_DIM
    fill = jnp.where(lane == HEAD_DIM, 1.0, 0.0) if ones_lane else 0.0
    outs = []
    for j in range(x.shape[1] // LANES):
        xs = x[:, j * LANES:(j + 1) * LANES]
        outs.append(jnp.where(low, xs, fill))
        outs.append(jnp.where(low, pltpu.roll(xs, HEAD_DIM, 1), fill))
    return jnp.concatenate(outs, axis=1).astype(BF16)


def _in_proj_kernel(x_ref, sh_ref, sc_ref, w_ref, cos_ref, sin_ref, seg_ref, gq_ref, gk_ref,
                    qa_ref, ka_ref, va_ref, qb_ref, kb_ref, vb_ref, u_ref, qd_ref, kd_ref, vd_ref,
                    *, rope):
    a = (x_ref[...] * (1.0 + sc_ref[...]) + sh_ref[...]).astype(BF16)
    lane = lax.broadcasted_iota(jnp.int32, (a.shape[0], LANES), 1)

    def proj(c0, c1):
        return _dot(a, w_ref[:, c0:c1])

    qa_ref[...] = _pad_heads(proj(C_QA, C_KA) * ATT_SCALE, lane)
    ka_ref[...] = _pad_heads(proj(C_KA, C_VA), lane)
    va_ref[...] = _pad_heads(proj(C_VA, C_QB), lane, ones_lane=True)
    vb_ref[...] = _pad_heads(proj(C_VB, C_U), lane, ones_lane=True)
    u_ref[...] = proj(C_U, C_QD)
    vd_ref[...] = _pad_heads(proj(C_VD, IN_WIDTH), lane, ones_lane=True)

    seg = seg_ref[...]
    qb = _seg_rms(proj(C_QB, C_KB), seg, gq_ref[...])
    kb = _seg_rms(proj(C_KB, C_VB), seg[:LANES, :LANES], gk_ref[...])
    qd = proj(C_QD, C_KD)
    kd = proj(C_KD, C_VD)
    if rope:
        cos = cos_ref[...]
        sin = sin_ref[...]
        first_half = (lane % HEAD_DIM) < (HEAD_DIM // 2)
        qb = _rope(qb, cos, sin, first_half)
        kb = _rope(kb, cos, sin, first_half)
        qd = _rope(qd, cos, sin, first_half)
        kd = _rope(kd, cos, sin, first_half)
    qb_ref[...] = _pad_heads(qb * ATT_SCALE, lane)
    kb_ref[...] = _pad_heads(kb, lane)
    qd_ref[...] = _pad_heads(qd * ATT_SCALE, lane)
    kd_ref[...] = _pad_heads(kd, lane)


def _in_proj_call(x, modr, layer, w_in, cos_t, sin_t, seg, gq, gk, *, batch, is_ctx):
    rows = x.shape[0]
    per_batch = rows // batch
    tm = 256 if is_ctx else 512
    nb = per_batch // tm
    row_fn = (lambda i: batch) if is_ctx else (lambda i: i // nb)
    table_spec = pl.BlockSpec((tm, LANES), (lambda i: (0, 0)) if is_ctx else (lambda i: (i % nb, 0)))
    full = lambda shape: pl.BlockSpec(shape, lambda i: (0,) * len(shape))
    rowblk = lambda w: pl.BlockSpec((tm, w), lambda i: (i, 0))
    wide = lambda w, dt: jax.ShapeDtypeStruct((rows, w), dt)
    out_shapes = [wide(512, BF16), wide(512, BF16), wide(512, BF16),
                  wide(512, BF16), wide(256, BF16), wide(256, BF16),
                  jax.ShapeDtypeStruct((per_batch, batch * SSM_WIDTH), F32),
                  wide(512, BF16), wide(256, BF16), wide(256, BF16)]
    out_specs = [rowblk(512), rowblk(512), rowblk(512), rowblk(512), rowblk(256), rowblk(256),
                 pl.BlockSpec((tm, SSM_WIDTH), lambda i: (i % nb, i // nb)),
                 rowblk(512), rowblk(256), rowblk(256)]
    return pl.pallas_call(
        functools.partial(_in_proj_kernel, rope=not is_ctx),
        grid=(rows // tm,),
        in_specs=[rowblk(D_MODEL), _mod_spec(layer, 0, row_fn), _mod_spec(layer, 1, row_fn),
                  full((D_MODEL, IN_WIDTH)), table_spec, table_spec, full((256, 256)),
                  full((1, 256)), full((1, 128))],
        out_specs=out_specs,
        out_shape=out_shapes,
        name="in_proj_ctx" if is_ctx else "in_proj",
        compiler_params=_cparams("arbitrary"),
    )(x, modr, modr, w_in, cos_t, sin_t, seg, gq, gk)


KV_CHUNK = 256


def _tile(h):
    return slice(h * LANES, (h + 1) * LANES)


def _attend(q, parts, sink=None):
    def score(k_fn, mod_fn, c):
        s = _dot_nt(q, k_fn(c))
        return s if mod_fn is None else mod_fn(s, c)

    tiles = [(k_fn, v_fn, mod_fn, c) for k_fn, v_fn, n, mod_fn in parts for c in range(0, n, KV_CHUNK)]
    m = sink
    acc = None
    for k_fn, v_fn, mod_fn, c in tiles:
        s = score(k_fn, mod_fn, c)
        m_tile = jnp.max(s, axis=-1, keepdims=True)
        m_new = m_tile if m is None else jnp.maximum(m, m_tile)
        pv = _dot(jnp.exp(s - m_new).astype(BF16), v_fn(c))
        acc = pv if acc is None else acc * jnp.exp(m - m_new) + pv
        m = m_new
    den = acc[:, HEAD_DIM:HEAD_DIM + 1]
    if sink is not None:
        den = den + jnp.exp(sink - m)
    return acc * (1.0 / den)


def _pack_pair(a, b):
    lane = lax.broadcasted_iota(jnp.int32, a.shape, 1)
    return jnp.where(lane < HEAD_DIM, a, pltpu.roll(b, HEAD_DIM, 1)).astype(BF16)


def _ref_part(k_ref, v_ref, t, n_keys, start=0, mod_fn=None):
    return (lambda c: k_ref[pl.ds(start + c, KV_CHUNK), _tile(t)],
            lambda c: v_ref[pl.ds(start + c, KV_CHUNK), _tile(t)], n_keys, mod_fn)


def _gqa_group(q_ref, g, parts, sink_pair=None):
    tq = q_ref.shape[0]
    q2 = jnp.concatenate([q_ref[:, _tile(2 * g)], q_ref[:, _tile(2 * g + 1)]], axis=0)
    sink = None
    if sink_pair is not None:
        row = lax.broadcasted_iota(jnp.int32, (2 * tq, 1), 0)
        sink = jnp.where(row < tq, sink_pair[0], sink_pair[1])
    o2 = _attend(q2, parts, sink)
    return _pack_pair(o2[:tq], o2[tq:])


NA_QROWS = 4
NA_BAND = 12


def _na_band_start(j, rows):
    return jnp.clip(j * NA_QROWS - NA_ROWS // 2, 0, rows - NA_BAND)


def _na_kernel(q_ref, k_ref, v_ref, kc_ref, vc_ref, bias_ref, mask_ref, o_ref):
    j = pl.program_id(1)
    rows = k_ref.shape[0] // GRID_W
    start = pl.multiple_of(_na_band_start(j, rows) * GRID_W, GRID_W)
    outs = []
    for h in range(4):
        def local_scores(s, c, h=h):
            cols = slice(c, c + KV_CHUNK)
            return jnp.where(mask_ref[:, cols] > 0.5, s + bias_ref[h, :, cols], NEG_INF)

        parts = [_ref_part(k_ref, v_ref, h, NA_BAND * GRID_W, start, local_scores),
                 _ref_part(kc_ref, vc_ref, h, kc_ref.shape[0])]
        outs.append(_attend(q_ref[:, _tile(h)], parts))
    o_ref[:, _tile(0)] = _pack_pair(outs[0], outs[1])
    o_ref[:, _tile(1)] = _pack_pair(outs[2], outs[3])


def _na_call(qa, ka, va, ka_c, va_c, bias_tab, mask_tab, *, batch):
    seq = qa.shape[0] // batch
    n_ctx = ka_c.shape[0] // batch
    nj = seq // (NA_QROWS * GRID_W)
    tq = NA_QROWS * GRID_W
    nk = NA_BAND * GRID_W

    def pattern(j):
        return jnp.where(j == 0, 0, jnp.where(j == nj - 1, 2, 1))

    return pl.pallas_call(
        _na_kernel,
        grid=(batch, nj),
        in_specs=[
            pl.BlockSpec((tq, 512), lambda b, j: (b * nj + j, 0)),
            pl.BlockSpec((seq, 512), lambda b, j: (b, 0)),
            pl.BlockSpec((seq, 512), lambda b, j: (b, 0)),
            pl.BlockSpec((n_ctx, 512), lambda b, j: (b, 0)),
            pl.BlockSpec((n_ctx, 512), lambda b, j: (b, 0)),
            pl.BlockSpec((None, 4, tq, nk), lambda b, j: (pattern(j), 0, 0, 0)),
            pl.BlockSpec((None, tq, nk), lambda b, j: (pattern(j), 0, 0)),
        ],
        out_specs=pl.BlockSpec((tq, 256), lambda b, j: (b * nj + j, 0)),
        out_shape=jax.ShapeDtypeStruct((qa.shape[0], 256), BF16),
        name="na_attn",
        compiler_params=_cparams("arbitrary", "arbitrary"),
    )(qa, ka, va, ka_c, va_c, bias_tab, mask_tab)


def _na_tables(rpb, seq):
    rows = seq // GRID_W
    nj = rows // NA_QROWS
    c_idx = jnp.arange(GRID_W)
    col_start = jnp.clip(c_idx - NA_COLS // 2, 0, GRID_W - NA_COLS)
    col_valid = (c_idx[None, :] >= col_start[:, None]) & (c_idx[None, :] < col_start[:, None] + NA_COLS)
    col_rel = jnp.clip(c_idx[None, :] - c_idx[:, None], 1 - NA_COLS, NA_COLS - 1) + NA_COLS - 1
    blocks = jnp.array([0, 1, nj - 1])
    q_row = blocks[:, None] * NA_QROWS + jnp.arange(NA_QROWS)[None, :]
    k_row = _na_band_start(blocks, rows)[:, None] + jnp.arange(NA_BAND)[None, :]
    row_start = jnp.clip(q_row - NA_ROWS // 2, 0, rows - NA_ROWS)
    row_valid = ((k_row[:, None, :] >= row_start[:, :, None])
                 & (k_row[:, None, :] < row_start[:, :, None] + NA_ROWS))
    row_rel = jnp.clip(k_row[:, None, :] - q_row[:, :, None] + NA_ROWS - 1, 0, 2 * NA_ROWS - 2)
    row_sel = jax.nn.one_hot(row_rel, 2 * NA_ROWS - 1, dtype=F32)
    col_sel = jax.nn.one_hot(col_rel, 2 * NA_COLS - 1, dtype=F32)
    exact = lax.Precision.HIGHEST
    by_col = jnp.einsum("hab,qkb->haqk", rpb.astype(F32), col_sel, precision=exact)
    bias = jnp.einsum("pria,haqk->phrqik", row_sel, by_col, precision=exact)
    valid = row_valid[:, :, None, :, None] & col_valid[None, None, :, None, :]
    tq, nk = NA_QROWS * GRID_W, NA_BAND * GRID_W
    return bias.reshape(3, 4, tq, nk), valid.astype(F32).reshape(3, tq, nk)


def _global_kernel(q_ref, k_ref, v_ref, kc_ref, vc_ref, o_ref):
    for g in range(2):
        parts = [_ref_part(kc_ref, vc_ref, g, kc_ref.shape[0]), _ref_part(k_ref, v_ref, g, k_ref.shape[0])]
        o_ref[:, _tile(g)] = _gqa_group(q_ref, g, parts)


def _global_call(qb, kb, vb, kb_c, vb_c, *, batch):
    seq = qb.shape[0] // batch
    n_ctx = kb_c.shape[0] // batch
    tq = 256
    nq = seq // tq
    return pl.pallas_call(
        _global_kernel,
        grid=(batch, nq),
        in_specs=[
            pl.BlockSpec((tq, 512), lambda b, i: (b * nq + i, 0)),
            pl.BlockSpec((seq, 256), lambda b, i: (b, 0)),
            pl.BlockSpec((seq, 256), lambda b, i: (b, 0)),
            pl.BlockSpec((n_ctx, 256), lambda b, i: (b, 0)),
            pl.BlockSpec((n_ctx, 256), lambda b, i: (b, 0)),
        ],
        out_specs=pl.BlockSpec((tq, 256), lambda b, i: (b * nq + i, 0)),
        out_shape=jax.ShapeDtypeStruct((qb.shape[0], 256), BF16),
        name="global_attn",
        compiler_params=_cparams("arbitrary", "arbitrary"),
    )(qb, kb, vb, kb_c, vb_c)


SW_TQ = 256
SW_SPAN = SW_TQ + 2 * SW_WINDOW


def _window_start(i, seq):
    return jnp.clip(i * SW_TQ - SW_WINDOW, 0, seq - SW_SPAN)


def _window_kernel(sink_ref, q_ref, k_ref, v_ref, kc_ref, vc_ref, mask_ref, o_ref):
    i = pl.program_id(1)
    start = pl.multiple_of(_window_start(i, k_ref.shape[0]), LANES)

    def in_window(s, c):
        return jnp.where(mask_ref[:, c:c + KV_CHUNK] > 0.5, s, NEG_INF)

    for g in range(2):
        parts = [_ref_part(k_ref, v_ref, g, SW_SPAN, start, in_window),
                 _ref_part(kc_ref, vc_ref, g, kc_ref.shape[0])]
        o_ref[:, _tile(g)] = _gqa_group(q_ref, g, parts, sink_pair=(sink_ref[2 * g], sink_ref[2 * g + 1]))


def _window_mask_table(seq):
    nq = seq // SW_TQ
    blocks = jnp.array([0, 1, nq - 1])
    qpos = blocks[:, None] * SW_TQ + jnp.arange(SW_TQ)[None, :]
    kpos = _window_start(blocks, seq)[:, None] + jnp.arange(SW_SPAN)[None, :]
    valid = jnp.abs(kpos[:, None, :] - qpos[:, :, None]) <= SW_WINDOW
    return jnp.tile(valid.astype(F32), (1, 2, 1))


def _window_call(sink, qd, kd, vd, kd_c, vd_c, mask_tab, *, batch):
    seq = qd.shape[0] // batch
    n_ctx = kd_c.shape[0] // batch
    nq = seq // SW_TQ

    def pattern(i):
        return jnp.where(i == 0, 0, jnp.where(i == nq - 1, 2, 1))

    return pl.pallas_call(
        _window_kernel,
        grid=(batch, nq),
        in_specs=[
            pl.BlockSpec(memory_space=pltpu.SMEM),
            pl.BlockSpec((SW_TQ, 512), lambda b, i: (b * nq + i, 0)),
            pl.BlockSpec((seq, 256), lambda b, i: (b, 0)),
            pl.BlockSpec((seq, 256), lambda b, i: (b, 0)),
            pl.BlockSpec((n_ctx, 256), lambda b, i: (b, 0)),
            pl.BlockSpec((n_ctx, 256), lambda b, i: (b, 0)),
            pl.BlockSpec((None, 2 * SW_TQ, SW_SPAN), lambda b, i: (pattern(i), 0, 0)),
        ],
        out_specs=pl.BlockSpec((SW_TQ, 256), lambda b, i: (b * nq + i, 0)),
        out_shape=jax.ShapeDtypeStruct((qd.shape[0], 256), BF16),
        name="window_attn",
        compiler_params=_cparams("arbitrary", "arbitrary"),
    )(sink, qd, kd, vd, kd_c, vd_c, mask_tab)


def _ctx_attn_kernel(sink_ref, qa_ref, ka_ref, va_ref, qb_ref, kb_ref, vb_ref, qd_ref, kd_ref, vd_ref,
                     oa_ref, ob_ref, od_ref):
    n = ka_ref.shape[0]
    outs = [_attend(qa_ref[:, _tile(h)], [_ref_part(ka_ref, va_ref, h, n)]) for h in range(4)]
    oa_ref[:, _tile(0)] = _pack_pair(outs[0], outs[1])
    oa_ref[:, _tile(1)] = _pack_pair(outs[2], outs[3])
    for g in range(2):
        ob_ref[:, _tile(g)] = _gqa_group(qb_ref, g, [_ref_part(kb_ref, vb_ref, g, n)])
        od_ref[:, _tile(g)] = _gqa_group(qd_ref, g, [_ref_part(kd_ref, vd_ref, g, n)],
                                         sink_pair=(sink_ref[2 * g], sink_ref[2 * g + 1]))


def _ctx_attn_call(sink, cx, *, batch):
    n_ctx = cx["qa"].shape[0] // batch
    blk = lambda w: pl.BlockSpec((n_ctx, w), lambda b: (b, 0))
    names = ("qa", "ka", "va", "qb", "kb", "vb", "qd", "kd", "vd")
    widths = (512, 512, 512, 512, 256, 256, 512, 256, 256)
    shape = jax.ShapeDtypeStruct((cx["qa"].shape[0], 256), BF16)
    return pl.pallas_call(
        _ctx_attn_kernel,
        grid=(batch,),
        in_specs=[pl.BlockSpec(memory_space=pltpu.SMEM)] + [blk(w) for w in widths],
        out_specs=[blk(256)] * 3,
        out_shape=[shape] * 3,
        name="ctx_attn",
        compiler_params=_cparams("arbitrary"),
    )(sink, *[cx[n] for n in names])


def _s5_discretise(lam_re, lam_im, log_step, b_re, b_im):
    step = jnp.exp(log_step.astype(F32))[:, None]
    lam_re = lam_re.astype(F32)
    lam_im = lam_im.astype(F32)
    mag = jnp.exp(lam_re * step)
    ab_re = mag * jnp.cos(lam_im * step)
    ab_im = mag * jnp.sin(lam_im * step)
    den = lam_re * lam_re + lam_im * lam_im
    num_re = ab_re - 1.0
    f_re = ((num_re * lam_re + ab_im * lam_im) / den)[..., None]
    f_im = ((ab_im * lam_re - num_re * lam_im) / den)[..., None]
    b_re = b_re.astype(F32)
    b_im = b_im.astype(F32)
    return ab_re, ab_im, f_re * b_re - f_im * b_im, f_re * b_im + f_im * b_re


def _s5_operands(lam_re, lam_im, log_step, b_re, b_im, c_re, c_im):
    ab_re, ab_im, bb_re, bb_im = _s5_discretise(lam_re, lam_im, log_step, b_re, b_im)
    eye = jnp.eye(SSM_GROUPS, dtype=F32)
    bd_in = lambda m: jnp.einsum("gph,gk->ghkp", m, eye).reshape(SSM_WIDTH, SSM_FLAT)
    bd_out = lambda m: jnp.einsum("ghp,gk->kpgh", m.astype(F32), eye).reshape(SSM_FLAT, SSM_WIDTH)
    a = jnp.stack([ab_re.reshape(SSM_FLAT), ab_im.reshape(SSM_FLAT)])
    bbd = jnp.concatenate([bd_in(bb_re), bd_in(bb_im)], axis=1).astype(BF16)
    cbd = jnp.concatenate([bd_out(c_re), -bd_out(c_im)], axis=0).astype(BF16)
    return a, bbd, cbd


def _s5_kernel(*refs, reverse, final, batch, n_ctx_chunks):
    if final:
        (uc_ref, ul_ref, pc_ref, pl_ref, a_ref, bbd_ref, cbd_ref, d_ref, wglu_ref,
         oc_ref, ol_ref, bu_ref, h_ref) = refs
    else:
        uc_ref, ul_ref, a_ref, bbd_ref, cbd_ref, oc_ref, ol_ref, bu_ref, h_ref = refs
        pc_ref = pl_ref = d_ref = wglu_ref = None
    c = pl.program_id(0)
    steps = bu_ref.shape[0] // batch

    @pl.when(c == 0)
    def _():
        h_ref[...] = jnp.zeros_like(h_ref)

    def run(u_ref, prev_ref, o_ref):
        u = u_ref[...]
        bu_ref[...] = _dot(u.astype(BF16), bbd_ref[...])
        ar = jnp.broadcast_to(a_ref[0:1, :], (batch, SSM_FLAT))
        ai = jnp.broadcast_to(a_ref[1:2, :], (batch, SSM_FLAT))

        def step(i, h):
            t = steps - 1 - i if reverse else i
            row = pl.multiple_of(t * batch, batch)
            b = bu_ref[pl.ds(row, batch), :]
            hr, hi = h[:, :SSM_FLAT], h[:, SSM_FLAT:]
            nr = ar * hr - ai * hi + b[:, :SSM_FLAT]
            ni = ar * hi + ai * hr + b[:, SSM_FLAT:]
            hn = jnp.concatenate([nr, ni], axis=1)
            bu_ref[pl.ds(row, batch), :] = hn
            return hn

        h_ref[...] = lax.fori_loop(0, steps, step, h_ref[...], unroll=4)
        y = _dot(bu_ref[...].astype(BF16), cbd_ref[...])
        if not final:
            o_ref[...] = y
        else:
            y = jax.nn.gelu(y + prev_ref[...] + d_ref[...] * u)
            z = _dot(y.astype(BF16), wglu_ref[...])
            o_ref[...] = (y * jax.nn.sigmoid(z)).astype(BF16)

    @pl.when(c < n_ctx_chunks)
    def _():
        run(uc_ref, pc_ref, oc_ref)

    @pl.when(c >= n_ctx_chunks)
    def _():
        run(ul_ref, pl_ref, ol_ref)


def _s5_pass(u_ctx, u_lat, prev, a, bbd, cbd, d_skip, w_glu, *, batch, reverse):
    final = prev is not None
    t_chunk = 128
    rows = t_chunk * batch
    ncc = u_ctx.shape[0] // t_chunk
    nlc = u_lat.shape[0] // t_chunk
    flat = lambda x: x.reshape(x.shape[0] * batch, SSM_WIDTH)
    if reverse:
        ctx_idx = lambda c: (jnp.maximum(ncc - 1 - c, 0), 0)
        lat_idx = lambda c: (nlc - 1 - jnp.maximum(c - ncc, 0), 0)
    else:
        ctx_idx = lambda c: (jnp.minimum(c, ncc - 1), 0)
        lat_idx = lambda c: (jnp.maximum(c - ncc, 0), 0)
    full = lambda shape: pl.BlockSpec(shape, lambda c: (0,) * len(shape))
    u_specs = [pl.BlockSpec((rows, SSM_WIDTH), ctx_idx), pl.BlockSpec((rows, SSM_WIDTH), lat_idx)]
    par_specs = [full((2, SSM_FLAT)), full((SSM_WIDTH, 2 * SSM_FLAT)), full((2 * SSM_FLAT, SSM_WIDTH))]
    args = [flat(u_ctx), flat(u_lat)]
    in_specs = list(u_specs)
    if final:
        args += [flat(prev[0]), flat(prev[1])]
        in_specs += u_specs
    args += [a, bbd, cbd]
    in_specs += par_specs
    if final:
        args += [d_skip, w_glu]
        in_specs += [full((1, SSM_WIDTH)), full((SSM_WIDTH, SSM_WIDTH))]
    dt = BF16 if final else F32
    oc, ol = pl.pallas_call(
        functools.partial(_s5_kernel, reverse=reverse, final=final, batch=batch, n_ctx_chunks=ncc),
        grid=(ncc + nlc,),
        in_specs=in_specs,
        out_specs=u_specs,
        out_shape=[jax.ShapeDtypeStruct((u_ctx.shape[0] * batch, SSM_WIDTH), dt),
                   jax.ShapeDtypeStruct((u_lat.shape[0] * batch, SSM_WIDTH), dt)],
        scratch_shapes=[pltpu.VMEM((rows, 2 * SSM_FLAT), F32), pltpu.VMEM((batch, 2 * SSM_FLAT), F32)],
        name="s5_bwd_glu" if final else "s5_fwd",
        compiler_params=_cparams("arbitrary"),
    )(*args)
    return oc.reshape(u_ctx.shape), ol.reshape(u_lat.shape)


def _out_proj_kernel(oa_ref, ob_ref, oc_ref, od_ref, w_ref, x_ref, g1_ref, sh2_ref, sc2_ref,
                     lng_ref, lnb_ref, h_ref, f_ref):
    y = (_dot(oa_ref[...], w_ref[0:256, :]) + _dot(ob_ref[...], w_ref[256:512, :])
         + _dot(oc_ref[...], w_ref[512:768, :]) + _dot(od_ref[...], w_ref[768:1024, :]))
    h1 = _layer_norm(DEEPNORM_ALPHA * x_ref[...] + g1_ref[...] * y, lng_ref[...], lnb_ref[...])
    h_ref[...] = h1
    f_ref[...] = (h1 * (1.0 + sc2_ref[...]) + sh2_ref[...]).astype(BF16)


def _out_proj_call(oa, ob, oc_tm, od, w_out, x, modr, layer, lng, lnb, *, batch, is_ctx):
    rows = x.shape[0]
    per_batch = rows // batch
    tm = 256 if is_ctx else 512
    nb = per_batch // tm
    row_fn = (lambda i: batch) if is_ctx else (lambda i: i // nb)
    rowblk = lambda w: pl.BlockSpec((tm, w), lambda i: (i, 0))
    full = lambda shape: pl.BlockSpec(shape, lambda i: (0,) * len(shape))
    return pl.pallas_call(
        _out_proj_kernel,
        grid=(rows // tm,),
        in_specs=[rowblk(256), rowblk(256),
                  pl.BlockSpec((tm, SSM_WIDTH), lambda i: (i % nb, i // nb)),
                  rowblk(256), full((D_MODEL, D_MODEL)), rowblk(D_MODEL),
                  _mod_spec(layer, 2, row_fn), _mod_spec(layer, 3, row_fn), _mod_spec(layer, 4, row_fn),
                  full((1, D_MODEL)), full((1, D_MODEL))],
        out_specs=[rowblk(D_MODEL), rowblk(D_MODEL)],
        out_shape=[jax.ShapeDtypeStruct((rows, D_MODEL), F32), jax.ShapeDtypeStruct((rows, D_MODEL), BF16)],
        name="out_proj_ctx" if is_ctx else "out_proj",
        compiler_params=_cparams("arbitrary"),
    )(oa, ob, oc_tm, od, w_out, x, modr, modr, modr, lng, lnb)


def _ffn_kernel(f_ref, wg_ref, wu_ref, wd_ref, h_ref, g2_ref, lng_ref, lnb_ref, o_ref, acc_ref):
    j = pl.program_id(1)

    @pl.when(j == 0)
    def _():
        acc_ref[...] = jnp.zeros_like(acc_ref)

    f = f_ref[...]
    g = _dot(f, wg_ref[...])
    u = _dot(f, wu_ref[...])
    acc_ref[...] += _dot((g * jax.nn.sigmoid(g) * u).astype(BF16), wd_ref[...])

    @pl.when(j == pl.num_programs(1) - 1)
    def _():
        o_ref[...] = _layer_norm(DEEPNORM_ALPHA * h_ref[...] + g2_ref[...] * acc_ref[...],
                                 lng_ref[...], lnb_ref[...])


def _ffn_call(f_in, wg, wu, wd, h1, modr, layer, lng, lnb, *, batch, is_ctx):
    rows = h1.shape[0]
    per_batch = rows // batch
    tm = 256 if is_ctx else 512
    d_ff = wg.shape[1]
    tf = d_ff // 2
    nb = per_batch // tm
    row_fn = (lambda i: batch) if is_ctx else (lambda i: i // nb)
    rowblk = lambda w: pl.BlockSpec((tm, w), lambda i, j: (i, 0))
    full = lambda shape: pl.BlockSpec(shape, lambda i, j: (0,) * len(shape))
    return pl.pallas_call(
        _ffn_kernel,
        grid=(rows // tm, d_ff // tf),
        in_specs=[rowblk(D_MODEL),
                  pl.BlockSpec((D_MODEL, tf), lambda i, j: (0, j)),
                  pl.BlockSpec((D_MODEL, tf), lambda i, j: (0, j)),
                  pl.BlockSpec((tf, D_MODEL), lambda i, j: (j, 0)),
                  rowblk(D_MODEL), _mod_spec(layer, 5, row_fn), full((1, D_MODEL)), full((1, D_MODEL))],
        out_specs=rowblk(D_MODEL),
        out_shape=jax.ShapeDtypeStruct((rows, D_MODEL), F32),
        scratch_shapes=[pltpu.VMEM((tm, D_MODEL), F32)],
        name="ffn_ctx" if is_ctx else "ffn",
        compiler_params=_cparams("arbitrary", "arbitrary"),
    )(f_in, wg, wu, wd, h1, modr, lng, lnb)


MOE_TM = 1024
MOE_CHUNK = 288
MOE_FF_CHUNKS = 4


def _router_kernel(h_ref, sh2_ref, sc2_ref, wr_ref, br_ref, tri_ref, gate_ref, rank_ref, cnt_ref):
    f = h_ref[...] * (1.0 + sc2_ref[...]) + sh2_ref[...]
    f_hi, f_lo = _split_bf16(f)
    w_hi, w_lo = _split_bf16(wr_ref[...])
    logits = _dot_nt(w_hi, f_hi) + _dot_nt(w_hi, f_lo) + _dot_nt(w_lo, f_hi) + br_ref[...]
    ie = lax.broadcasted_iota(jnp.int32, logits.shape, 0)
    m1 = jnp.max(logits, axis=0, keepdims=True)
    i1 = jnp.min(jnp.where(logits == m1, ie, N_EXPERTS), axis=0, keepdims=True)
    rest = jnp.where(ie == i1, -jnp.inf, logits)
    m2 = jnp.max(rest, axis=0, keepdims=True)
    i2 = jnp.min(jnp.where(rest == m2, ie, N_EXPERTS), axis=0, keepdims=True)
    e2 = jnp.exp(m2 - m1)
    den = 1.0 + e2
    gate_ref[...] = jnp.where(ie == i1, 1.0 / den, 0.0) + jnp.where(ie == i2, e2 / den, 0.0)
    sel = jnp.where((ie == i1) | (ie == i2), 1.0, 0.0)
    rank = _dot(sel.astype(BF16), tri_ref[...])
    rank_ref[...] = jnp.where(sel > 0.0, rank, -1.0)
    cnt = jnp.sum(sel, axis=1, keepdims=True)
    cnt_ref[...] = jnp.broadcast_to(cnt, cnt_ref.shape).astype(jnp.int32)


def _router_call(h1, modr, layer, wr_t, b_r, tri, *, batch, is_ctx):
    rows = h1.shape[0]
    tm = MOE_TM
    nblk = rows // tm
    nb = (rows // batch) // tm if not is_ctx else 1
    row_fn = (lambda i: batch) if is_ctx else (lambda i: i // nb)
    full = lambda shape: pl.BlockSpec(shape, lambda i: (0,) * len(shape))
    blk3 = lambda w: pl.BlockSpec((None, N_EXPERTS, w), lambda i: (i, 0, 0))
    return pl.pallas_call(
        _router_kernel,
        grid=(nblk,),
        in_specs=[pl.BlockSpec((tm, D_MODEL), lambda i: (i, 0)),
                  _mod_spec(layer, 3, row_fn), _mod_spec(layer, 4, row_fn),
                  full((N_EXPERTS, D_MODEL)), full((N_EXPERTS, 1)), full((tm, tm))],
        out_specs=[blk3(tm), blk3(tm), blk3(LANES)],
        out_shape=[jax.ShapeDtypeStruct((nblk, N_EXPERTS, tm), F32),
                   jax.ShapeDtypeStruct((nblk, N_EXPERTS, tm), F32),
                   jax.ShapeDtypeStruct((nblk, N_EXPERTS, LANES), jnp.int32)],
        name="router_ctx" if is_ctx else "router",
        compiler_params=_cparams("arbitrary"),
    )(h1, modr, modr, wr_t, b_r, tri)


def _moe_kernel(cnt_ref, f_ref, gate_ref, rank_ref, wg_ref, wu_ref, wd_ref, h_ref, g2_ref,
                lng_ref, lnb_ref, o_ref, acc_ref, xg_ref, yg_ref):
    i, e, j = pl.program_id(0), pl.program_id(1), pl.program_id(2)
    tm = f_ref.shape[0]
    last_j = pl.num_programs(2) - 1
    n_rows = cnt_ref[i * N_EXPERTS + e]
    n_chunks = lax.div(n_rows + (MOE_CHUNK - 1), MOE_CHUNK)

    @pl.when((e == 0) & (j == 0))
    def _():
        acc_ref[...] = jnp.zeros_like(acc_ref)

    rank_row = rank_ref[pl.ds(e, 1), :]
    gate_row = gate_ref[pl.ds(e, 1), :]

    def chunk(c, carry):
        base = pl.multiple_of(c * MOE_CHUNK, 32)
        rows = pl.ds(base, MOE_CHUNK)

        def one_hot():
            slot = (lax.broadcasted_iota(jnp.int32, (MOE_CHUNK, tm), 0) + base).astype(F32)
            return rank_row == slot

        @pl.when(j == 0)
        def _():
            sel = jnp.where(one_hot(), 1.0, 0.0).astype(BF16)
            xg_ref[rows, :] = _dot(sel, f_ref[...]).astype(BF16)

        xc = xg_ref[rows, :]
        g = _dot(xc, wg_ref[...])
        u = _dot(xc, wu_ref[...])
        y = _dot((g * jax.nn.sigmoid(g) * u).astype(BF16), wd_ref[...])

        @pl.when(j == 0)
        def _():
            yg_ref[rows, :] = y

        @pl.when((j > 0) & (j < last_j))
        def _():
            yg_ref[rows, :] += y

        @pl.when(j == last_j)
        def _():
            hit = one_hot()
            sel = jnp.where(hit, 1.0, 0.0).astype(BF16)
            row_gate = jnp.sum(jnp.where(hit, gate_row, 0.0), axis=1, keepdims=True)
            y_hi, y_lo = _split_bf16((yg_ref[rows, :] + y) * row_gate)
            acc_ref[...] += _dot_tn(jnp.concatenate([sel, sel], axis=0), jnp.concatenate([y_hi, y_lo], axis=0))

        return carry

    lax.fori_loop(0, n_chunks, chunk, 0)

    @pl.when((e == pl.num_programs(1) - 1) & (j == last_j))
    def _():
        o_ref[...] = _layer_norm(DEEPNORM_ALPHA * h_ref[...] + g2_ref[...] * acc_ref[...],
                                 lng_ref[...], lnb_ref[...])


def _expert_chunks(w_gate, w_up, w_down):
    n_e, d, d_ff = w_gate.shape
    tf = d_ff // MOE_FF_CHUNKS
    cols = lambda w: jnp.transpose(w.astype(BF16).reshape(n_e, d, MOE_FF_CHUNKS, tf), (0, 2, 1, 3))
    return cols(w_gate), cols(w_up), w_down.astype(BF16).reshape(n_e, MOE_FF_CHUNKS, tf, d)


def _moe_call(cnt, f_in, gate_t, rank_t, wg, wu, wd, h1, modr, layer, lng, lnb, *, batch, is_ctx):
    rows = h1.shape[0]
    tm = MOE_TM
    tf = wg.shape[3]
    max_rows = pl.cdiv(tm, MOE_CHUNK) * MOE_CHUNK
    nb = (rows // batch) // tm if not is_ctx else 1
    row_fn = (lambda i: batch) if is_ctx else (lambda i: i // nb)
    rowblk = lambda w: pl.BlockSpec((tm, w), lambda i, e, j, c: (i, 0))
    full = lambda shape: pl.BlockSpec(shape, lambda i, e, j, c: (0,) * len(shape))
    blk3 = pl.BlockSpec((None, N_EXPERTS, tm), lambda i, e, j, c: (i, 0, 0))
    slab = lambda r, c: pl.BlockSpec((None, None, r, c), lambda i, e, j, c_: (e, j, 0, 0))
    grid_spec = pltpu.PrefetchScalarGridSpec(
        num_scalar_prefetch=1,
        grid=(rows // tm, N_EXPERTS, MOE_FF_CHUNKS),
        in_specs=[rowblk(D_MODEL), blk3, blk3, slab(D_MODEL, tf), slab(D_MODEL, tf), slab(tf, D_MODEL),
                  rowblk(D_MODEL), _mod_spec(layer, 5, row_fn), full((1, D_MODEL)), full((1, D_MODEL))],
        out_specs=rowblk(D_MODEL),
        scratch_shapes=[pltpu.VMEM((tm, D_MODEL), F32), pltpu.VMEM((max_rows, D_MODEL), BF16),
                        pltpu.VMEM((max_rows, D_MODEL), F32)],
    )
    return pl.pallas_call(
        _moe_kernel,
        grid_spec=grid_spec,
        out_shape=jax.ShapeDtypeStruct((rows, D_MODEL), F32),
        name="moe_ctx" if is_ctx else "moe",
        compiler_params=_cparams("arbitrary", "arbitrary", "arbitrary"),
    )(cnt, f_in, gate_t, rank_t, wg, wu, wd, h1, modr, lng, lnb)


def _rope_tables(seq):
    pos = jnp.arange(seq, dtype=jnp.int32)
    row = (pos // GRID_W).astype(F32)
    col = (pos % GRID_W).astype(F32)
    n_freq = HEAD_DIM // 4
    inv_freq = ROPE_THETA ** (-jnp.arange(n_freq, dtype=F32) / n_freq)
    ang = jnp.concatenate([row[:, None] * inv_freq, col[:, None] * inv_freq], axis=-1)
    cos, sin = jnp.cos(ang), jnp.sin(ang)
    reps = LANES // (HEAD_DIM // 2)
    sign = jnp.tile(jnp.concatenate([-jnp.ones(HEAD_DIM // 2, F32), jnp.ones(HEAD_DIM // 2, F32)]),
                    LANES // HEAD_DIM)
    return jnp.tile(cos, (1, reps)), jnp.tile(sin, (1, reps)) * sign


def kernel(x, c, ctx, c_ctx, ada_w, ada_b, w_in, w_out, na_rpb, ga_q_norm, ga_k_norm,
           ssm_lambda_re, ssm_lambda_im, ssm_log_step, ssm_b_re, ssm_b_im, ssm_c_re, ssm_c_im,
           ssm_d, ssm_w_glu, sw_sink, ln1_g, ln1_b, ln2_g, ln2_b,
           ffn_w_gate, ffn_w_up, ffn_w_down,
           moe_w_router, moe_b_router, moe_w_gate, moe_w_up, moe_w_down):
    batch, seq, _ = x.shape
    n_ctx = ctx.shape[1]
    assert batch < MOD_ROWS and seq % 512 == 0 and n_ctx % 256 == 0

    c_all = jnp.zeros((MOD_ROWS, D_MODEL), F32).at[:batch].set(c).at[batch].set(c_ctx)
    modr = _mods_call(c_all, ada_w, ada_b)

    cos_t, sin_t = _rope_tables(seq)
    seg = jnp.kron(jnp.eye(256 // HEAD_DIM, dtype=F32), jnp.ones((HEAD_DIM, HEAD_DIM), F32)).astype(BF16)
    tri = (jnp.arange(MOE_TM)[:, None] < jnp.arange(MOE_TM)[None, :]).astype(BF16)
    sw_mask = _window_mask_table(seq)

    h = x.reshape(batch * seq, D_MODEL)
    hc = ctx.reshape(batch * n_ctx, D_MODEL)
    names = ("qa", "ka", "va", "qb", "kb", "vb", "u", "qd", "kd", "vd")

    for layer in range(DEPTH):
        need_ctx = layer < DEPTH - 1
        w_in_l = w_in[layer].astype(BF16)
        w_out_l = w_out[layer].astype(BF16)
        gq = jnp.tile(ga_q_norm[layer].astype(F32), 256 // HEAD_DIM)[None, :]
        gk = jnp.tile(ga_k_norm[layer].astype(F32), 128 // HEAD_DIM)[None, :]
        sink = sw_sink[layer].astype(F32)
        ln1 = (ln1_g[layer][None, :], ln1_b[layer][None, :])
        ln2 = (ln2_g[layer][None, :], ln2_b[layer][None, :])

        lat = dict(zip(names, _in_proj_call(h, modr, layer, w_in_l, cos_t, sin_t, seg, gq, gk,
                                            batch=batch, is_ctx=False)))
        cx = dict(zip(names, _in_proj_call(hc, modr, layer, w_in_l, cos_t, sin_t, seg, gq, gk,
                                           batch=batch, is_ctx=True)))

        na_bias, na_mask = _na_tables(na_rpb[layer], seq)
        out_a = _na_call(lat["qa"], lat["ka"], lat["va"], cx["ka"], cx["va"], na_bias, na_mask, batch=batch)
        out_b = _global_call(lat["qb"], lat["kb"], lat["vb"], cx["kb"], cx["vb"], batch=batch)
        out_d = _window_call(sink, lat["qd"], lat["kd"], lat["vd"], cx["kd"], cx["vd"], sw_mask, batch=batch)

        dirs = [_s5_operands(ssm_lambda_re[layer, d], ssm_lambda_im[layer, d], ssm_log_step[layer, d],
                             ssm_b_re[layer, d], ssm_b_im[layer, d], ssm_c_re[layer, d], ssm_c_im[layer, d])
                for d in range(2)]
        fwd = _s5_pass(cx["u"], lat["u"], None, *dirs[0], None, None, batch=batch, reverse=False)
        ctx_c, out_c = _s5_pass(cx["u"], lat["u"], fwd, *dirs[1], ssm_d[layer].astype(F32)[None, :],
                                ssm_w_glu[layer].astype(BF16), batch=batch, reverse=True)

        routed = layer % 2 == 1
        h1, f_in = _out_proj_call(out_a, out_b, out_c, out_d, w_out_l, h, modr, layer, *ln1,
                                  batch=batch, is_ctx=False)
        streams = [(h1, f_in, False)]
        if need_ctx:
            ctx_a, ctx_b, ctx_d = _ctx_attn_call(sink, cx, batch=batch)
            hc1, fc_in = _out_proj_call(ctx_a, ctx_b, ctx_c, ctx_d, w_out_l, hc, modr, layer, *ln1,
                                        batch=batch, is_ctx=True)
            streams.append((hc1, fc_in, True))

        i = layer // 2
        outs = []
        if not routed:
            wg, wu, wd = (ffn_w_gate[i].astype(BF16), ffn_w_up[i].astype(BF16), ffn_w_down[i].astype(BF16))
            for s1, sf, is_ctx in streams:
                outs.append(_ffn_call(sf, wg, wu, wd, s1, modr, layer, *ln2, batch=batch, is_ctx=is_ctx))
        else:
            wg, wu, wd = _expert_chunks(moe_w_gate[i], moe_w_up[i], moe_w_down[i])
            wr_t = moe_w_router[i].astype(F32).T
            b_r = moe_b_router[i].astype(F32)[:, None]
            for s1, sf, is_ctx in streams:
                gate_t, rank_t, cnt = _router_call(s1, modr, layer, wr_t, b_r, tri, batch=batch, is_ctx=is_ctx)
                outs.append(_moe_call(cnt[:, :, 0].reshape(-1), sf, gate_t, rank_t, wg, wu, wd, s1, modr,
                                      layer, *ln2, batch=batch, is_ctx=is_ctx))
        h = outs[0]
        if need_ctx:
            hc = outs[1]
    return h.reshape(batch, seq, D_MODEL)
```

```python
import functools

import jax
import jax.numpy as jnp
from jax import lax
from jax.experimental import pallas as pl
from jax.experimental.pallas import tpu as pltpu

F32 = jnp.float32
BF16 = jnp.bfloat16

D_MODEL = 1024
DEPTH = 4
GRID_W = 64
HEAD_DIM = 64
NA_ROWS = 8
NA_COLS = 16
SSM_GROUPS = 16
SSM_GROUP_CH = 16
SSM_STATE = 64
SSM_WIDTH = SSM_GROUPS * SSM_GROUP_CH
SSM_FLAT = SSM_GROUPS * SSM_STATE
SW_WINDOW = 128
N_EXPERTS = 8
ADA_CHUNKS = 6
ROPE_THETA = 10000.0
LN_EPS = 1e-6
RMS_EPS = 1e-6
NEG_INF = -1e30
DEEPNORM_ALPHA = (2 * DEPTH) ** 0.25
ATT_SCALE = HEAD_DIM ** -0.5

C_QA, C_KA, C_VA, C_QB, C_KB, C_VB, C_U, C_QD, C_KD, C_VD, IN_WIDTH = (
    0, 256, 512, 768, 1024, 1152, 1280, 1536, 1792, 1920, 2048)

LANES = 128
SUBLANES = 8
MOD_ROWS = 16
VMEM_LIMIT = 56 * 1024 * 1024


def _cparams(*sem):
    return pltpu.CompilerParams(dimension_semantics=sem, vmem_limit_bytes=VMEM_LIMIT)


def _dot(a, b):
    return jnp.dot(a, b, preferred_element_type=F32)


def _dot_nt(a, b):
    return lax.dot_general(a, b, (((1,), (1,)), ((), ())), preferred_element_type=F32)


def _dot_tn(a, b):
    return lax.dot_general(a, b, (((0,), (0,)), ((), ())), preferred_element_type=F32)


def _split_bf16(x):
    hi = x.astype(BF16)
    lo = (x - hi.astype(F32)).astype(BF16)
    return hi, lo


def _layer_norm(z, g, b):
    zc = z - jnp.mean(z, axis=-1, keepdims=True)
    y = zc * lax.rsqrt(jnp.mean(zc * zc, axis=-1, keepdims=True) + LN_EPS)
    return y * g + b


def _mods_kernel(c_ref, w_ref, b_ref, o_ref):
    c = c_ref[...]
    act = (c * jax.nn.sigmoid(c)).astype(BF16)
    o_ref[...] = _dot(act, w_ref[...].astype(BF16)) + b_ref[...]


def _mods_call(c_all, ada_w, ada_b):
    tn = 1536
    n = ADA_CHUNKS * D_MODEL
    out = pl.pallas_call(
        _mods_kernel,
        grid=(DEPTH, n // tn),
        in_specs=[
            pl.BlockSpec((MOD_ROWS, D_MODEL), lambda l, j: (0, 0)),
            pl.BlockSpec((None, D_MODEL, tn), lambda l, j: (l, 0, j)),
            pl.BlockSpec((None, 1, tn), lambda l, j: (l, 0, j)),
        ],
        out_specs=pl.BlockSpec((None, MOD_ROWS, tn), lambda l, j: (l, 0, j)),
        out_shape=jax.ShapeDtypeStruct((DEPTH, MOD_ROWS, n), F32),
        name="mods",
        compiler_params=_cparams("arbitrary", "arbitrary"),
    )(c_all, ada_w, ada_b.reshape(DEPTH, 1, n))
    return out.reshape(DEPTH * MOD_ROWS * ADA_CHUNKS, 1, D_MODEL)


def _mod_spec(layer, chunk, row_fn):
    def index(i, *_):
        return ((layer * MOD_ROWS + row_fn(i)) * ADA_CHUNKS + chunk, 0, 0)
    return pl.BlockSpec((None, 1, D_MODEL), index)


def _seg_rms(x, seg, g):
    hi, lo = _split_bf16(x * x)
    ss = _dot(hi, seg) + _dot(lo, seg)
    return x * lax.rsqrt(ss * (1.0 / HEAD_DIM) + RMS_EPS) * g


def _rope(x, cos, sin_signed, first_half):
    outs = []
    for j in range(x.shape[1] // LANES):
        xs = x[:, j * LANES:(j + 1) * LANES]
        partner = jnp.where(first_half,
                            pltpu.roll(xs, LANES - HEAD_DIM // 2, 1),
                            pltpu.roll(xs, HEAD_DIM // 2, 1))
        outs.append(xs * cos + partner * sin_signed)
    return outs[0] if len(outs) == 1 else jnp.concatenate(outs, axis=1)


def _pad_heads(x, lane, ones_lane=False):
    low = lane < HEAD_DIM
    fill = jnp.where(lane == HEAD_DIM, 1.0, 0.0) if ones_lane else 0.0
    outs = []
    for j in range(x.shape[1] // LANES):
        xs = x[:, j * LANES:(j + 1) * LANES]
        outs.append(jnp.where(low, xs, fill))
        outs.append(jnp.where(low, pltpu.roll(xs, HEAD_DIM, 1), fill))
    return jnp.concatenate(outs, axis=1).astype(BF16)


def _in_proj_kernel(x_ref, sh_ref, sc_ref, w_ref, cos_ref, sin_ref, seg_ref, gq_ref, gk_ref,
                    qa_ref, ka_ref, va_ref, qb_ref, kb_ref, vb_ref, u_ref, qd_ref, kd_ref, vd_ref,
                    *, rope):
    a = (x_ref[...] * (1.0 + sc_ref[...]) + sh_ref[...]).astype(BF16)
    lane = lax.broadcasted_iota(jnp.int32, (a.shape[0], LANES), 1)

    def proj(c0, c1):
        return _dot(a, w_ref[:, c0:c1])

    qa_ref[...] = _pad_heads(proj(C_QA, C_KA) * ATT_SCALE, lane)
    ka_ref[...] = _pad_heads(proj(C_KA, C_VA), lane)
    va_ref[...] = _pad_heads(proj(C_VA, C_QB), lane, ones_lane=True)
    vb_ref[...] = _pad_heads(proj(C_VB, C_U), lane, ones_lane=True)
    u_ref[...] = proj(C_U, C_QD)
    vd_ref[...] = _pad_heads(proj(C_VD, IN_WIDTH), lane, ones_lane=True)

    seg = seg_ref[...]
    qb = _seg_rms(proj(C_QB, C_KB), seg, gq_ref[...])
    kb = _seg_rms(proj(C_KB, C_VB), seg[:LANES, :LANES], gk_ref[...])
    qd = proj(C_QD, C_KD)
    kd = proj(C_KD, C_VD)
    if rope:
        cos = cos_ref[...]
        sin = sin_ref[...]
        first_half = (lane % HEAD_DIM) < (HEAD_DIM // 2)
        qb = _rope(qb, cos, sin, first_half)
        kb = _rope(kb, cos, sin, first_half)
        qd = _rope(qd, cos, sin, first_half)
        kd = _rope(kd, cos, sin, first_half)
    qb_ref[...] = _pad_heads(qb * ATT_SCALE, lane)
    kb_ref[...] = _pad_heads(kb, lane)
    qd_ref[...] = _pad_heads(qd * ATT_SCALE, lane)
    kd_ref[...] = _pad_heads(kd, lane)


def _in_proj_call(x, modr, layer, w_in, cos_t, sin_t, seg, gq, gk, *, batch, is_ctx):
    rows = x.shape[0]
    per_batch = rows // batch
    tm = 256 if is_ctx else 512
    nb = per_batch // tm
    row_fn = (lambda i: batch) if is_ctx else (lambda i: i // nb)
    table_spec = pl.BlockSpec((tm, LANES), (lambda i: (0, 0)) if is_ctx else (lambda i: (i % nb, 0)))
    full = lambda shape: pl.BlockSpec(shape, lambda i: (0,) * len(shape))
    rowblk = lambda w: pl.BlockSpec((tm, w), lambda i: (i, 0))
    wide = lambda w, dt: jax.ShapeDtypeStruct((rows, w), dt)
    out_shapes = [wide(512, BF16), wide(512, BF16), wide(512, BF16),
                  wide(512, BF16), wide(256, BF16), wide(256, BF16),
                  jax.ShapeDtypeStruct((per_batch, batch * SSM_WIDTH), F32),
                  wide(512, BF16), wide(256, BF16), wide(256, BF16)]
    out_specs = [rowblk(512), rowblk(512), rowblk(512), rowblk(512), rowblk(256), rowblk(256),
                 pl.BlockSpec((tm, SSM_WIDTH), lambda i: (i % nb, i // nb)),
                 rowblk(512), rowblk(256), rowblk(256)]
    return pl.pallas_call(
        functools.partial(_in_proj_kernel, rope=not is_ctx),
        grid=(rows // tm,),
        in_specs=[rowblk(D_MODEL), _mod_spec(layer, 0, row_fn), _mod_spec(layer, 1, row_fn),
                  full((D_MODEL, IN_WIDTH)), table_spec, table_spec, full((256, 256)),
                  full((1, 256)), full((1, 128))],
        out_specs=out_specs,
        out_shape=out_shapes,
        name="in_proj_ctx" if is_ctx else "in_proj",
        compiler_params=_cparams("arbitrary"),
    )(x, modr, modr, w_in, cos_t, sin_t, seg, gq, gk)


KV_CHUNK = 256


def _tile(h):
    return slice(h * LANES, (h + 1) * LANES)


def _attend(q, parts, sink=None):
    def score(k_fn, mod_fn, c):
        s = _dot_nt(q, k_fn(c))
        return s if mod_fn is None else mod_fn(s, c)

    tiles = [(k_fn, v_fn, mod_fn, c) for k_fn, v_fn, n, mod_fn in parts for c in range(0, n, KV_CHUNK)]
    m = sink
    acc = None
    for k_fn, v_fn, mod_fn, c in tiles:
        s = score(k_fn, mod_fn, c)
        m_tile = jnp.max(s, axis=-1, keepdims=True)
        m_new = m_tile if m is None else jnp.maximum(m, m_tile)
        pv = _dot(jnp.exp(s - m_new).astype(BF16), v_fn(c))
        acc = pv if acc is None else acc * jnp.exp(m - m_new) + pv
        m = m_new
    den = acc[:, HEAD_DIM:HEAD_DIM + 1]
    if sink is not None:
        den = den + jnp.exp(sink - m)
    return acc * (1.0 / den)


def _pack_pair(a, b):
    lane = lax.broadcasted_iota(jnp.int32, a.shape, 1)
    return jnp.where(lane < HEAD_DIM, a, pltpu.roll(b, HEAD_DIM, 1)).astype(BF16)


def _ref_part(k_ref, v_ref, t, n_keys, start=0, mod_fn=None):
    return (lambda c: k_ref[pl.ds(start + c, KV_CHUNK), _tile(t)],
            lambda c: v_ref[pl.ds(start + c, KV_CHUNK), _tile(t)], n_keys, mod_fn)


def _gqa_group(q_ref, g, parts, sink_pair=None):
    tq = q_ref.shape[0]
    q2 = jnp.concatenate([q_ref[:, _tile(2 * g)], q_ref[:, _tile(2 * g + 1)]], axis=0)
    sink = None
    if sink_pair is not None:
        row = lax.broadcasted_iota(jnp.int32, (2 * tq, 1), 0)
        sink = jnp.where(row < tq, sink_pair[0], sink_pair[1])
    o2 = _attend(q2, parts, sink)
    return _pack_pair(o2[:tq], o2[tq:])


NA_QROWS = 4
NA_BAND = 12


def _na_band_start(j, rows):
    return jnp.clip(j * NA_QROWS - NA_ROWS // 2, 0, rows - NA_BAND)


def _na_kernel(q_ref, k_ref, v_ref, kc_ref, vc_ref, bias_ref, mask_ref, o_ref):
    j = pl.program_id(1)
    rows = k_ref.shape[0] // GRID_W
    start = pl.multiple_of(_na_band_start(j, rows) * GRID_W, GRID_W)
    outs = []
    for h in range(4):
        def local_scores(s, c, h=h):
            cols = slice(c, c + KV_CHUNK)
            return jnp.where(mask_ref[:, cols] > 0.5, s + bias_ref[h, :, cols], NEG_INF)

        parts = [_ref_part(k_ref, v_ref, h, NA_BAND * GRID_W, start, local_scores),
                 _ref_part(kc_ref, vc_ref, h, kc_ref.shape[0])]
        outs.append(_attend(q_ref[:, _tile(h)], parts))
    o_ref[:, _tile(0)] = _pack_pair(outs[0], outs[1])
    o_ref[:, _tile(1)] = _pack_pair(outs[2], outs[3])


def _na_call(qa, ka, va, ka_c, va_c, bias_tab, mask_tab, *, batch):
    seq = qa.shape[0] // batch
    n_ctx = ka_c.shape[0] // batch
    nj = seq // (NA_QROWS * GRID_W)
    tq = NA_QROWS * GRID_W
    nk = NA_BAND * GRID_W

    def pattern(j):
        return jnp.where(j == 0, 0, jnp.where(j == nj - 1, 2, 1))

    return pl.pallas_call(
        _na_kernel,
        grid=(batch, nj),
        in_specs=[
            pl.BlockSpec((tq, 512), lambda b, j: (b * nj + j, 0)),
            pl.BlockSpec((seq, 512), lambda b, j: (b, 0)),
            pl.BlockSpec((seq, 512), lambda b, j: (b, 0)),
            pl.BlockSpec((n_ctx, 512), lambda b, j: (b, 0)),
            pl.BlockSpec((n_ctx, 512), lambda b, j: (b, 0)),
            pl.BlockSpec((None, 4, tq, nk), lambda b, j: (pattern(j), 0, 0, 0)),
            pl.BlockSpec((None, tq, nk), lambda b, j: (pattern(j), 0, 0)),
        ],
        out_specs=pl.BlockSpec((tq, 256), lambda b, j: (b * nj + j, 0)),
        out_shape=jax.ShapeDtypeStruct((qa.shape[0], 256), BF16),
        name="na_attn",
        compiler_params=_cparams("arbitrary", "arbitrary"),
    )(qa, ka, va, ka_c, va_c, bias_tab, mask_tab)


def _na_tables(rpb, seq):
    rows = seq // GRID_W
    nj = rows // NA_QROWS
    c_idx = jnp.arange(GRID_W)
    col_start = jnp.clip(c_idx - NA_COLS // 2, 0, GRID_W - NA_COLS)
    col_valid = (c_idx[None, :] >= col_start[:, None]) & (c_idx[None, :] < col_start[:, None] + NA_COLS)
    col_rel = jnp.clip(c_idx[None, :] - c_idx[:, None], 1 - NA_COLS, NA_COLS - 1) + NA_COLS - 1
    blocks = jnp.array([0, 1, nj - 1])
    q_row = blocks[:, None] * NA_QROWS + jnp.arange(NA_QROWS)[None, :]
    k_row = _na_band_start(blocks, rows)[:, None] + jnp.arange(NA_BAND)[None, :]
    row_start = jnp.clip(q_row - NA_ROWS // 2, 0, rows - NA_ROWS)
    row_valid = ((k_row[:, None, :] >= row_start[:, :, None])
                 & (k_row[:, None, :] < row_start[:, :, None] + NA_ROWS))
    row_rel = jnp.clip(k_row[:, None, :] - q_row[:, :, None] + NA_ROWS - 1, 0, 2 * NA_ROWS - 2)
    row_sel = jax.nn.one_hot(row_rel, 2 * NA_ROWS - 1, dtype=F32)
    col_sel = jax.nn.one_hot(col_rel, 2 * NA_COLS - 1, dtype=F32)
    exact = lax.Precision.HIGHEST
    by_col = jnp.einsum("hab,qkb->haqk", rpb.astype(F32), col_sel, precision=exact)
    bias = jnp.einsum("pria,haqk->phrqik", row_sel, by_col, precision=exact)
    valid = row_valid[:, :, None, :, None] & col_valid[None, None, :, None, :]
    tq, nk = NA_QROWS * GRID_W, NA_BAND * GRID_W
    return bias.reshape(3, 4, tq, nk), valid.astype(F32).reshape(3, tq, nk)


def _global_kernel(q_ref, k_ref, v_ref, kc_ref, vc_ref, o_ref):
    for g in range(2):
        parts = [_ref_part(kc_ref, vc_ref, g, kc_ref.shape[0]), _ref_part(k_ref, v_ref, g, k_ref.shape[0])]
        o_ref[:, _tile(g)] = _gqa_group(q_ref, g, parts)


def _global_call(qb, kb, vb, kb_c, vb_c, *, batch):
    seq = qb.shape[0] // batch
    n_ctx = kb_c.shape[0] // batch
    tq = 256
    nq = seq // tq
    return pl.pallas_call(
        _global_kernel,
        grid=(batch, nq),
        in_specs=[
            pl.BlockSpec((tq, 512), lambda b, i: (b * nq + i, 0)),
            pl.BlockSpec((seq, 256), lambda b, i: (b, 0)),
            pl.BlockSpec((seq, 256), lambda b, i: (b, 0)),
            pl.BlockSpec((n_ctx, 256), lambda b, i: (b, 0)),
            pl.BlockSpec((n_ctx, 256), lambda b, i: (b, 0)),
        ],
        out_specs=pl.BlockSpec((tq, 256), lambda b, i: (b * nq + i, 0)),
        out_shape=jax.ShapeDtypeStruct((qb.shape[0], 256), BF16),
        name="global_attn",
        compiler_params=_cparams("arbitrary", "arbitrary"),
    )(qb, kb, vb, kb_c, vb_c)


SW_TQ = 256
SW_SPAN = SW_TQ + 2 * SW_WINDOW


def _window_start(i, seq):
    return jnp.clip(i * SW_TQ - SW_WINDOW, 0, seq - SW_SPAN)


def _window_kernel(sink_ref, q_ref, k_ref, v_ref, kc_ref, vc_ref, mask_ref, o_ref):
    i = pl.program_id(1)
    start = pl.multiple_of(_window_start(i, k_ref.shape[0]), LANES)

    def in_window(s, c):
        return jnp.where(mask_ref[:, c:c + KV_CHUNK] > 0.5, s, NEG_INF)

    for g in range(2):
        parts = [_ref_part(k_ref, v_ref, g, SW_SPAN, start, in_window),
                 _ref_part(kc_ref, vc_ref, g, kc_ref.shape[0])]
        o_ref[:, _tile(g)] = _gqa_group(q_ref, g, parts, sink_pair=(sink_ref[2 * g], sink_ref[2 * g + 1]))


def _window_mask_table(seq):
    nq = seq // SW_TQ
    blocks = jnp.array([0, 1, nq - 1])
    qpos = blocks[:, None] * SW_TQ + jnp.arange(SW_TQ)[None, :]
    kpos = _window_start(blocks, seq)[:, None] + jnp.arange(SW_SPAN)[None, :]
    valid = jnp.abs(kpos[:, None, :] - qpos[:, :, None]) <= SW_WINDOW
    return jnp.tile(valid.astype(F32), (1, 2, 1))


def _window_call(sink, qd, kd, vd, kd_c, vd_c, mask_tab, *, batch):
    seq = qd.shape[0] // batch
    n_ctx = kd_c.shape[0] // batch
    nq = seq // SW_TQ

    def pattern(i):
        return jnp.where(i == 0, 0, jnp.where(i == nq - 1, 2, 1))

    return pl.pallas_call(
        _window_kernel,
        grid=(batch, nq),
        in_specs=[
            pl.BlockSpec(memory_space=pltpu.SMEM),
            pl.BlockSpec((SW_TQ, 512), lambda b, i: (b * nq + i, 0)),
            pl.BlockSpec((seq, 256), lambda b, i: (b, 0)),
            pl.BlockSpec((seq, 256), lambda b, i: (b, 0)),
            pl.BlockSpec((n_ctx, 256), lambda b, i: (b, 0)),
            pl.BlockSpec((n_ctx, 256), lambda b, i: (b, 0)),
            pl.BlockSpec((None, 2 * SW_TQ, SW_SPAN), lambda b, i: (pattern(i), 0, 0)),
        ],
        out_specs=pl.BlockSpec((SW_TQ, 256), lambda b, i: (b * nq + i, 0)),
        out_shape=jax.ShapeDtypeStruct((qd.shape[0], 256), BF16),
        name="window_attn",
        compiler_params=_cparams("arbitrary", "arbitrary"),
    )(sink, qd, kd, vd, kd_c, vd_c, mask_tab)


def _ctx_attn_kernel(sink_ref, qa_ref, ka_ref, va_ref, qb_ref, kb_ref, vb_ref, qd_ref, kd_ref, vd_ref,
                     oa_ref, ob_ref, od_ref):
    n = ka_ref.shape[0]
    outs = [_attend(qa_ref[:, _tile(h)], [_ref_part(ka_ref, va_ref, h, n)]) for h in range(4)]
    oa_ref[:, _tile(0)] = _pack_pair(outs[0], outs[1])
    oa_ref[:, _tile(1)] = _pack_pair(outs[2], outs[3])
    for g in range(2):
        ob_ref[:, _tile(g)] = _gqa_group(qb_ref, g, [_ref_part(kb_ref, vb_ref, g, n)])
        od_ref[:, _tile(g)] = _gqa_group(qd_ref, g, [_ref_part(kd_ref, vd_ref, g, n)],
                                         sink_pair=(sink_ref[2 * g], sink_ref[2 * g + 1]))


def _ctx_attn_call(sink, cx, *, batch):
    n_ctx = cx["qa"].shape[0] // batch
    blk = lambda w: pl.BlockSpec((n_ctx, w), lambda b: (b, 0))
    names = ("qa", "ka", "va", "qb", "kb", "vb", "qd", "kd", "vd")
    widths = (512, 512, 512, 512, 256, 256, 512, 256, 256)
    shape = jax.ShapeDtypeStruct((cx["qa"].shape[0], 256), BF16)
    return pl.pallas_call(
        _ctx_attn_kernel,
        grid=(batch,),
        in_specs=[pl.BlockSpec(memory_space=pltpu.SMEM)] + [blk(w) for w in widths],
        out_specs=[blk(256)] * 3,
        out_shape=[shape] * 3,
        name="ctx_attn",
        compiler_params=_cparams("arbitrary"),
    )(sink, *[cx[n] for n in names])


def _s5_discretise(lam_re, lam_im, log_step, b_re, b_im):
    step = jnp.exp(log_step.astype(F32))[:, None]
    lam_re = lam_re.astype(F32)
    lam_im = lam_im.astype(F32)
    mag = jnp.exp(lam_re * step)
    ab_re = mag * jnp.cos(lam_im * step)
    ab_im = mag * jnp.sin(lam_im * step)
    den = lam_re * lam_re + lam_im * lam_im
    num_re = ab_re - 1.0
    f_re = ((num_re * lam_re + ab_im * lam_im) / den)[..., None]
    f_im = ((ab_im * lam_re - num_re * lam_im) / den)[..., None]
    b_re = b_re.astype(F32)
    b_im = b_im.astype(F32)
    return ab_re, ab_im, f_re * b_re - f_im * b_im, f_re * b_im + f_im * b_re


def _s5_operands(lam_re, lam_im, log_step, b_re, b_im, c_re, c_im):
    ab_re, ab_im, bb_re, bb_im = _s5_discretise(lam_re, lam_im, log_step, b_re, b_im)
    eye = jnp.eye(SSM_GROUPS, dtype=F32)
    bd_in = lambda m: jnp.einsum("gph,gk->ghkp", m, eye).reshape(SSM_WIDTH, SSM_FLAT)
    bd_out = lambda m: jnp.einsum("ghp,gk->kpgh", m.astype(F32), eye).reshape(SSM_FLAT, SSM_WIDTH)
    a = jnp.stack([ab_re.reshape(SSM_FLAT), ab_im.reshape(SSM_FLAT)])
    bbd = jnp.concatenate([bd_in(bb_re), bd_in(bb_im)], axis=1).astype(BF16)
    cbd = jnp.concatenate([bd_out(c_re), -bd_out(c_im)], axis=0).astype(BF16)
    return a, bbd, cbd


def _s5_kernel(*refs, reverse, final, batch, n_ctx_chunks):
    if final:
        (uc_ref, ul_ref, pc_ref, pl_ref, a_ref, bbd_ref, cbd_ref, d_ref, wglu_ref,
         oc_ref, ol_ref, bu_ref, h_ref) = refs
    else:
        uc_ref, ul_ref, a_ref, bbd_ref, cbd_ref, oc_ref, ol_ref, bu_ref, h_ref = refs
        pc_ref = pl_ref = d_ref = wglu_ref = None
    c = pl.program_id(0)
    steps = bu_ref.shape[0] // batch

    @pl.when(c == 0)
    def _():
        h_ref[...] = jnp.zeros_like(h_ref)

    def run(u_ref, prev_ref, o_ref):
        u = u_ref[...]
        bu_ref[...] = _dot(u.astype(BF16), bbd_ref[...])
        ar = jnp.broadcast_to(a_ref[0:1, :], (batch, SSM_FLAT))
        ai = jnp.broadcast_to(a_ref[1:2, :], (batch, SSM_FLAT))

        def step(i, h):
            t = steps - 1 - i if reverse else i
            row = pl.multiple_of(t * batch, batch)
            b = bu_ref[pl.ds(row, batch), :]
            hr, hi = h[:, :SSM_FLAT], h[:, SSM_FLAT:]
            nr = ar * hr - ai * hi + b[:, :SSM_FLAT]
            ni = ar * hi + ai * hr + b[:, SSM_FLAT:]
            hn = jnp.concatenate([nr, ni], axis=1)
            bu_ref[pl.ds(row, batch), :] = hn
            return hn

        h_ref[...] = lax.fori_loop(0, steps, step, h_ref[...], unroll=4)
        y = _dot(bu_ref[...].astype(BF16), cbd_ref[...])
        if not final:
            o_ref[...] = y
        else:
            y = jax.nn.gelu(y + prev_ref[...] + d_ref[...] * u)
            z = _dot(y.astype(BF16), wglu_ref[...])
            o_ref[...] = (y * jax.nn.sigmoid(z)).astype(BF16)

    @pl.when(c < n_ctx_chunks)
    def _():
        run(uc_ref, pc_ref, oc_ref)

    @pl.when(c >= n_ctx_chunks)
    def _():
        run(ul_ref, pl_ref, ol_ref)


def _s5_pass(u_ctx, u_lat, prev, a, bbd, cbd, d_skip, w_glu, *, batch, reverse):
    final = prev is not None
    t_chunk = 128
    rows = t_chunk * batch
    ncc = u_ctx.shape[0] // t_chunk
    nlc = u_lat.shape[0] // t_chunk
    flat = lambda x: x.reshape(x.shape[0] * batch, SSM_WIDTH)
    if reverse:
        ctx_idx = lambda c: (jnp.maximum(ncc - 1 - c, 0), 0)
        lat_idx = lambda c: (nlc - 1 - jnp.maximum(c - ncc, 0), 0)
    else:
        ctx_idx = lambda c: (jnp.minimum(c, ncc - 1), 0)
        lat_idx = lambda c: (jnp.maximum(c - ncc, 0), 0)
    full = lambda shape: pl.BlockSpec(shape, lambda c: (0,) * len(shape))
    u_specs = [pl.BlockSpec((rows, SSM_WIDTH), ctx_idx), pl.BlockSpec((rows, SSM_WIDTH), lat_idx)]
    par_specs = [full((2, SSM_FLAT)), full((SSM_WIDTH, 2 * SSM_FLAT)), full((2 * SSM_FLAT, SSM_WIDTH))]
    args = [flat(u_ctx), flat(u_lat)]
    in_specs = list(u_specs)
    if final:
        args += [flat(prev[0]), flat(prev[1])]
        in_specs += u_specs
    args += [a, bbd, cbd]
    in_specs += par_specs
    if final:
        args += [d_skip, w_glu]
        in_specs += [full((1, SSM_WIDTH)), full((SSM_WIDTH, SSM_WIDTH))]
    dt = BF16 if final else F32
    oc, ol = pl.pallas_call(
        functools.partial(_s5_kernel, reverse=reverse, final=final, batch=batch, n_ctx_chunks=ncc),
        grid=(ncc + nlc,),
        in_specs=in_specs,
        out_specs=u_specs,
        out_shape=[jax.ShapeDtypeStruct((u_ctx.shape[0] * batch, SSM_WIDTH), dt),
                   jax.ShapeDtypeStruct((u_lat.shape[0] * batch, SSM_WIDTH), dt)],
        scratch_shapes=[pltpu.VMEM((rows, 2 * SSM_FLAT), F32), pltpu.VMEM((batch, 2 * SSM_FLAT), F32)],
        name="s5_bwd_glu" if final else "s5_fwd",
        compiler_params=_cparams("arbitrary"),
    )(*args)
    return oc.reshape(u_ctx.shape), ol.reshape(u_lat.shape)


def _bf16_bits(x):
    return lax.bitcast_convert_type(x.astype(BF16).astype(F32), jnp.uint32)


def _pack_bf16_pair(a, b):
    return _bf16_bits(a) | lax.shift_right_logical(_bf16_bits(b), jnp.uint32(16))


def _unpack_bf16_pair(u):
    a = lax.bitcast_convert_type(u & jnp.uint32(0xFFFF0000), F32)
    b = lax.bitcast_convert_type(lax.shift_left(u, jnp.uint32(16)), F32)
    return a.astype(BF16), b.astype(BF16)


def _out_proj_kernel(oa_ref, ob_ref, oc_ref, od_ref, w_ref, x_ref, g1_ref, sh2_ref, sc2_ref,
                     lng_ref, lnb_ref, h_ref, f_ref, *, packed):
    y = (_dot(oa_ref[...], w_ref[0:256, :]) + _dot(ob_ref[...], w_ref[256:512, :])
         + _dot(oc_ref[...], w_ref[512:768, :]) + _dot(od_ref[...], w_ref[768:1024, :]))
    h1 = _layer_norm(DEEPNORM_ALPHA * x_ref[...] + g1_ref[...] * y, lng_ref[...], lnb_ref[...])
    h_ref[...] = h1
    f = h1 * (1.0 + sc2_ref[...]) + sh2_ref[...]
    if packed:
        f_ref[...] = _pack_bf16_pair(f[:, :D_MODEL // 2], f[:, D_MODEL // 2:])
    else:
        f_ref[...] = f.astype(BF16)


def _out_proj_call(oa, ob, oc_tm, od, w_out, x, modr, layer, lng, lnb, *, batch, is_ctx, packed):
    rows = x.shape[0]
    per_batch = rows // batch
    tm = 256 if is_ctx else 512
    nb = per_batch // tm
    row_fn = (lambda i: batch) if is_ctx else (lambda i: i // nb)
    rowblk = lambda w: pl.BlockSpec((tm, w), lambda i: (i, 0))
    full = lambda shape: pl.BlockSpec(shape, lambda i: (0,) * len(shape))
    f_width, f_dtype = (D_MODEL // 2, jnp.uint32) if packed else (D_MODEL, BF16)
    return pl.pallas_call(
        functools.partial(_out_proj_kernel, packed=packed),
        grid=(rows // tm,),
        in_specs=[rowblk(256), rowblk(256),
                  pl.BlockSpec((tm, SSM_WIDTH), lambda i: (i % nb, i // nb)),
                  rowblk(256), full((D_MODEL, D_MODEL)), rowblk(D_MODEL),
                  _mod_spec(layer, 2, row_fn), _mod_spec(layer, 3, row_fn), _mod_spec(layer, 4, row_fn),
                  full((1, D_MODEL)), full((1, D_MODEL))],
        out_specs=[rowblk(D_MODEL), rowblk(f_width)],
        out_shape=[jax.ShapeDtypeStruct((rows, D_MODEL), F32), jax.ShapeDtypeStruct((rows, f_width), f_dtype)],
        name="out_proj_ctx" if is_ctx else "out_proj",
        compiler_params=_cparams("arbitrary"),
    )(oa, ob, oc_tm, od, w_out, x, modr, modr, modr, lng, lnb)


def _ffn_kernel(f_ref, wg_ref, wu_ref, wd_ref, h_ref, g2_ref, lng_ref, lnb_ref, o_ref, acc_ref):
    j = pl.program_id(1)

    @pl.when(j == 0)
    def _():
        acc_ref[...] = jnp.zeros_like(acc_ref)

    f = f_ref[...]
    g = _dot(f, wg_ref[...])
    u = _dot(f, wu_ref[...])
    acc_ref[...] += _dot((g * jax.nn.sigmoid(g) * u).astype(BF16), wd_ref[...])

    @pl.when(j == pl.num_programs(1) - 1)
    def _():
        o_ref[...] = _layer_norm(DEEPNORM_ALPHA * h_ref[...] + g2_ref[...] * acc_ref[...],
                                 lng_ref[...], lnb_ref[...])


def _ffn_call(f_in, wg, wu, wd, h1, modr, layer, lng, lnb, *, batch, is_ctx):
    rows = h1.shape[0]
    per_batch = rows // batch
    tm = 256 if is_ctx else 512
    d_ff = wg.shape[1]
    tf = d_ff // 2
    nb = per_batch // tm
    row_fn = (lambda i: batch) if is_ctx else (lambda i: i // nb)
    rowblk = lambda w: pl.BlockSpec((tm, w), lambda i, j: (i, 0))
    full = lambda shape: pl.BlockSpec(shape, lambda i, j: (0,) * len(shape))
    return pl.pallas_call(
        _ffn_kernel,
        grid=(rows // tm, d_ff // tf),
        in_specs=[rowblk(D_MODEL),
                  pl.BlockSpec((D_MODEL, tf), lambda i, j: (0, j)),
                  pl.BlockSpec((D_MODEL, tf), lambda i, j: (0, j)),
                  pl.BlockSpec((tf, D_MODEL), lambda i, j: (j, 0)),
                  rowblk(D_MODEL), _mod_spec(layer, 5, row_fn), full((1, D_MODEL)), full((1, D_MODEL))],
        out_specs=rowblk(D_MODEL),
        out_shape=jax.ShapeDtypeStruct((rows, D_MODEL), F32),
        scratch_shapes=[pltpu.VMEM((tm, D_MODEL), F32)],
        name="ffn_ctx" if is_ctx else "ffn",
        compiler_params=_cparams("arbitrary", "arbitrary"),
    )(f_in, wg, wu, wd, h1, modr, lng, lnb)


MOE_TM = 1024
MOE_TILE = 512
MOE_PASS_TILES = 10
MOE_FF_CHUNKS = 7
ROUTE_ROWS = 8


def _router_kernel(h_ref, sh2_ref, sc2_ref, wr_ref, br_ref, tri_ref, route_ref, gate_ref, cnt_ref):
    tm = h_ref.shape[0]
    f = h_ref[...] * (1.0 + sc2_ref[...]) + sh2_ref[...]
    f_hi, f_lo = _split_bf16(f)
    w_hi, w_lo = _split_bf16(wr_ref[...])
    logits = _dot_nt(w_hi, f_hi) + _dot_nt(w_hi, f_lo) + _dot_nt(w_lo, f_hi) + br_ref[...]
    ie = lax.broadcasted_iota(jnp.int32, logits.shape, 0)
    m1 = jnp.max(logits, axis=0, keepdims=True)
    i1 = jnp.min(jnp.where(logits == m1, ie, N_EXPERTS), axis=0, keepdims=True)
    rest = jnp.where(ie == i1, -jnp.inf, logits)
    m2 = jnp.max(rest, axis=0, keepdims=True)
    i2 = jnp.min(jnp.where(rest == m2, ie, N_EXPERTS), axis=0, keepdims=True)
    e2 = jnp.exp(m2 - m1)
    den = 1.0 + e2
    sel = jnp.where((ie == i1) | (ie == i2), 1.0, 0.0)
    rank = _dot(sel.astype(BF16), tri_ref[...])
    r1 = jnp.sum(jnp.where(ie == i1, rank, 0.0), axis=0, keepdims=True).astype(jnp.int32)
    r2 = jnp.sum(jnp.where(ie == i2, rank, 0.0), axis=0, keepdims=True).astype(jnp.int32)
    row = lax.broadcasted_iota(jnp.int32, (ROUTE_ROWS, tm), 0)
    route_ref[...] = jnp.where(row == 0, i1, jnp.where(row == 1, i2, jnp.where(row == 2, r1,
                               jnp.where(row == 3, r2, 0))))
    lrow = lax.broadcasted_iota(jnp.int32, (LANES, tm), 0)
    gate_ref[...] = jnp.where(lrow == 0, 1.0 / den, jnp.where(lrow == 1, e2 / den, 0.0)).T
    cnt = jnp.sum(sel, axis=1, keepdims=True)
    cnt_ref[...] = jnp.broadcast_to(cnt, cnt_ref.shape).astype(jnp.int32)


def _router_call(h1, modr, layer, wr_t, b_r, tri, *, batch, is_ctx):
    rows = h1.shape[0]
    tm = MOE_TM
    nblk = rows // tm
    nb = (rows // batch) // tm if not is_ctx else 1
    row_fn = (lambda i: batch) if is_ctx else (lambda i: i // nb)
    full = lambda shape: pl.BlockSpec(shape, lambda i: (0,) * len(shape))
    return pl.pallas_call(
        _router_kernel,
        grid=(nblk,),
        in_specs=[pl.BlockSpec((tm, D_MODEL), lambda i: (i, 0)),
                  _mod_spec(layer, 3, row_fn), _mod_spec(layer, 4, row_fn),
                  full((N_EXPERTS, D_MODEL)), full((N_EXPERTS, 1)), full((tm, tm))],
        out_specs=[pl.BlockSpec((None, ROUTE_ROWS, tm), lambda i: (i, 0, 0)),
                   pl.BlockSpec((tm, LANES), lambda i: (i, 0)),
                   pl.BlockSpec((None, N_EXPERTS, LANES), lambda i: (i, 0, 0))],
        out_shape=[jax.ShapeDtypeStruct((nblk, ROUTE_ROWS, tm), jnp.int32),
                   jax.ShapeDtypeStruct((rows, LANES), F32),
                   jax.ShapeDtypeStruct((nblk, N_EXPERTS, LANES), jnp.int32)],
        name="router_ctx" if is_ctx else "router",
        compiler_params=_cparams("arbitrary"),
    )(h1, modr, modr, wr_t, b_r, tri)


def _route_plan(route, cnt, n_tiles, n_passes):
    total = jnp.sum(cnt, axis=0)
    tiles = lax.div(total + (MOE_TILE - 1), MOE_TILE)
    tile_start = jnp.cumsum(tiles) - tiles
    base = tile_start[None, :] * MOE_TILE + jnp.cumsum(cnt, axis=0) - cnt
    choice = jax.nn.one_hot(route[:, 0:2, :], N_EXPERTS, dtype=jnp.int32)
    pos = jnp.sum(choice * base[:, None, None, :], axis=-1) + route[:, 2:4, :]
    passes = lax.div(tiles + (MOE_PASS_TILES - 1), MOE_PASS_TILES)
    pass_end = jnp.cumsum(passes)
    p = jnp.arange(n_passes, dtype=jnp.int32)
    owner = jnp.sum((p[:, None] >= pass_end[None, :]).astype(jnp.int32), axis=1)
    last = jnp.sum((pass_end[-1] - 1 >= pass_end).astype(jnp.int32))
    expert = jnp.where(p < pass_end[-1], owner, last)
    within = (p - (pass_end - passes)[expert]) * MOE_PASS_TILES
    n = jnp.where(p < pass_end[-1], jnp.clip(tiles[expert] - within, 0, MOE_PASS_TILES), 0)
    first = tile_start[expert] + within
    used = jnp.sum(tiles)
    first = jnp.where(p == pass_end[-1], used, first)
    n_zero = jnp.where(p == pass_end[-1], n_tiles - used, 0)
    i32 = lambda a: a.astype(jnp.int32)
    return i32(pos), i32(expert), i32(first), i32(n), i32(n_zero)


def _row_copies(src_of, dst_of, n_rows, sem):
    def copy(t, k):
        return pltpu.make_async_copy(src_of(t, k), dst_of(t, k), sem)

    def start(t, carry):
        copy(t, 0).start()
        copy(t, 1).start()
        return carry

    def wait(t, carry):
        copy(t, 0).wait()
        copy(t, 1).wait()
        return carry

    lax.fori_loop(0, n_rows, start, 0, unroll=8)
    lax.fori_loop(0, n_rows, wait, 0, unroll=8)


def _dispatch_kernel(pos_ref, x_hbm, zeros_hbm, xs_hbm, sem):
    del zeros_hbm
    tm = pos_ref.shape[1]
    row0 = pl.program_id(0) * tm
    _row_copies(lambda t, k: x_hbm.at[pl.ds(row0 + t, 1)],
                lambda t, k: xs_hbm.at[pl.ds(pos_ref[k, t], 1)], tm, sem)


def _dispatch_call(pos, x_packed, n_rows):
    nblk, _, tm = pos.shape
    width = x_packed.shape[1]
    return pl.pallas_call(
        _dispatch_kernel,
        grid=(nblk,),
        in_specs=[pl.BlockSpec((None, 2, tm), lambda i: (i, 0, 0), memory_space=pltpu.SMEM),
                  pl.BlockSpec(memory_space=pl.ANY), pl.BlockSpec(memory_space=pl.ANY)],
        out_specs=pl.BlockSpec(memory_space=pl.ANY),
        out_shape=jax.ShapeDtypeStruct((n_rows, width), x_packed.dtype),
        scratch_shapes=[pltpu.SemaphoreType.DMA(())],
        input_output_aliases={2: 0},
        name="moe_dispatch",
        compiler_params=_cparams("arbitrary"),
    )(pos, x_packed, jnp.zeros((n_rows, width), x_packed.dtype))


def _experts_kernel(pe_ref, pf_ref, pn_ref, pz_ref, xs_hbm, wg_ref, wu_ref, wd_ref, ys_hbm,
                    xa_buf, xb_buf, y_buf, stage_buf, wg_buf, wu_buf, wd_buf, sem):
    del pe_ref
    p, j = pl.program_id(0), pl.program_id(1)
    n_tiles = pn_ref[p]
    first = pf_ref[p]
    half = D_MODEL // 2

    def tile_rows(k):
        return pl.ds(pl.multiple_of((first + k) * MOE_TILE, MOE_TILE), MOE_TILE)

    def store_tiles(n, src_of):
        def store(k):
            return pltpu.make_async_copy(src_of(k), ys_hbm.at[tile_rows(k)], sem)
        lax.fori_loop(0, n, lambda k, c: (store(k).start(), c)[1], 0)
        lax.fori_loop(0, n, lambda k, c: (store(k).wait(), c)[1], 0)

    @pl.when((pz_ref[p] > 0) & (j == 0))
    def _():
        y_buf[0] = jnp.zeros(y_buf.shape[1:], F32)
        store_tiles(pz_ref[p], lambda k: y_buf.at[0])

    @pl.when(n_tiles > 0)
    def _():
        @pl.when(j == 0)
        def _():
            def load(k, carry):
                copy = pltpu.make_async_copy(xs_hbm.at[tile_rows(k)], stage_buf, sem)
                copy.start()
                copy.wait()
                xa_buf[k], xb_buf[k] = _unpack_bf16_pair(stage_buf[...])
                return carry
            lax.fori_loop(0, n_tiles, load, 0)

        wg_buf[...] = wg_ref[...].astype(BF16)
        wu_buf[...] = wu_ref[...].astype(BF16)
        wd_buf[...] = wd_ref[...].astype(BF16)

        def tile(k, carry):
            xa, xb = xa_buf[k], xb_buf[k]
            g = _dot(xa, wg_buf[:half, :]) + _dot(xb, wg_buf[half:, :])
            u = _dot(xa, wu_buf[:half, :]) + _dot(xb, wu_buf[half:, :])
            y = _dot((g * jax.nn.sigmoid(g) * u).astype(BF16), wd_buf[...])

            @pl.when(j == 0)
            def _():
                y_buf[k] = y

            @pl.when(j > 0)
            def _():
                y_buf[k] += y

            return carry
        lax.fori_loop(0, n_tiles, tile, 0)

        @pl.when(j == pl.num_programs(1) - 1)
        def _():
            store_tiles(n_tiles, lambda k: y_buf.at[k])


def _experts_call(pass_expert, pass_first, pass_tiles, pass_zero, xs, w_gate, w_up, w_down):
    n_passes = pass_expert.shape[0]
    d_ff = w_gate.shape[2]
    tf = d_ff // MOE_FF_CHUNKS
    half = D_MODEL // 2
    chunk = lambda p, j, pn: jnp.where(pn[p] > 0, j, MOE_FF_CHUNKS - 1)
    grid_spec = pltpu.PrefetchScalarGridSpec(
        num_scalar_prefetch=4,
        grid=(n_passes, MOE_FF_CHUNKS),
        in_specs=[pl.BlockSpec(memory_space=pl.ANY),
                  pl.BlockSpec((None, D_MODEL, tf), lambda p, j, pe, pf, pn, pz: (pe[p], 0, chunk(p, j, pn))),
                  pl.BlockSpec((None, D_MODEL, tf), lambda p, j, pe, pf, pn, pz: (pe[p], 0, chunk(p, j, pn))),
                  pl.BlockSpec((None, tf, D_MODEL), lambda p, j, pe, pf, pn, pz: (pe[p], chunk(p, j, pn), 0))],
        out_specs=pl.BlockSpec(memory_space=pl.ANY),
        scratch_shapes=[pltpu.VMEM((MOE_PASS_TILES, MOE_TILE, half), BF16),
                        pltpu.VMEM((MOE_PASS_TILES, MOE_TILE, half), BF16),
                        pltpu.VMEM((MOE_PASS_TILES, MOE_TILE, D_MODEL), F32),
                        pltpu.VMEM((MOE_TILE, half), jnp.uint32),
                        pltpu.VMEM((D_MODEL, tf), BF16), pltpu.VMEM((D_MODEL, tf), BF16),
                        pltpu.VMEM((tf, D_MODEL), BF16),
                        pltpu.SemaphoreType.DMA(())],
    )
    return pl.pallas_call(
        _experts_kernel,
        grid_spec=grid_spec,
        out_shape=jax.ShapeDtypeStruct((xs.shape[0], D_MODEL), F32),
        name="moe_experts",
        compiler_params=_cparams("arbitrary", "arbitrary"),
    )(pass_expert, pass_first, pass_tiles, pass_zero, xs, w_gate, w_up, w_down)


def _combine_kernel(pos_ref, ys_hbm, gate_ref, h_ref, g2_ref, lng_ref, lnb_ref, o_ref, buf_ref, sem):
    tm = pos_ref.shape[1]
    _row_copies(lambda t, k: ys_hbm.at[pl.ds(pos_ref[k, t], 1)],
                lambda t, k: buf_ref.at[k, pl.ds(t, 1)], tm, sem)
    gate = gate_ref[...]
    f = gate[:, 0:1] * buf_ref[0] + gate[:, 1:2] * buf_ref[1]
    o_ref[...] = _layer_norm(DEEPNORM_ALPHA * h_ref[...] + g2_ref[...] * f, lng_ref[...], lnb_ref[...])


def _combine_call(pos, ys, gate, h1, modr, layer, lng, lnb, *, batch, is_ctx):
    rows = h1.shape[0]
    nblk, _, tm = pos.shape
    nb = (rows // batch) // tm if not is_ctx else 1
    row_fn = (lambda i: batch) if is_ctx else (lambda i: i // nb)
    rowblk = lambda w: pl.BlockSpec((tm, w), lambda i: (i, 0))
    full = lambda shape: pl.BlockSpec(shape, lambda i: (0,) * len(shape))
    return pl.pallas_call(
        _combine_kernel,
        grid=(nblk,),
        in_specs=[pl.BlockSpec((None, 2, tm), lambda i: (i, 0, 0), memory_space=pltpu.SMEM),
                  pl.BlockSpec(memory_space=pl.ANY), rowblk(LANES), rowblk(D_MODEL),
                  _mod_spec(layer, 5, row_fn), full((1, D_MODEL)), full((1, D_MODEL))],
        out_specs=rowblk(D_MODEL),
        out_shape=jax.ShapeDtypeStruct((rows, D_MODEL), F32),
        scratch_shapes=[pltpu.VMEM((2, tm, D_MODEL), F32), pltpu.SemaphoreType.DMA(())],
        name="moe_combine",
        compiler_params=_cparams("arbitrary"),
    )(pos, ys, gate, h1, modr, lng, lnb)


def _moe_call(f_packed, h1, modr, layer, wr_t, b_r, tri, w_gate, w_up, w_down, lng, lnb, *, batch, is_ctx):
    rows = h1.shape[0]
    n_tiles = (2 * rows) // MOE_TILE + N_EXPERTS
    n_passes = n_tiles // MOE_PASS_TILES + N_EXPERTS
    route, gate, cnt = _router_call(h1, modr, layer, wr_t, b_r, tri, batch=batch, is_ctx=is_ctx)
    pos, *passes = _route_plan(route, cnt[:, :, 0], n_tiles, n_passes)
    xs = _dispatch_call(pos, f_packed, n_tiles * MOE_TILE)
    ys = _experts_call(*passes, xs, w_gate, w_up, w_down)
    return _combine_call(pos, ys, gate, h1, modr, layer, lng, lnb, batch=batch, is_ctx=is_ctx)


def _rope_tables(seq):
    pos = jnp.arange(seq, dtype=jnp.int32)
    row = (pos // GRID_W).astype(F32)
    col = (pos % GRID_W).astype(F32)
    n_freq = HEAD_DIM // 4
    inv_freq = ROPE_THETA ** (-jnp.arange(n_freq, dtype=F32) / n_freq)
    ang = jnp.concatenate([row[:, None] * inv_freq, col[:, None] * inv_freq], axis=-1)
    cos, sin = jnp.cos(ang), jnp.sin(ang)
    reps = LANES // (HEAD_DIM // 2)
    sign = jnp.tile(jnp.concatenate([-jnp.ones(HEAD_DIM // 2, F32), jnp.ones(HEAD_DIM // 2, F32)]),
                    LANES // HEAD_DIM)
    return jnp.tile(cos, (1, reps)), jnp.tile(sin, (1, reps)) * sign


def kernel(x, c, ctx, c_ctx, ada_w, ada_b, w_in, w_out, na_rpb, ga_q_norm, ga_k_norm,
           ssm_lambda_re, ssm_lambda_im, ssm_log_step, ssm_b_re, ssm_b_im, ssm_c_re, ssm_c_im,
           ssm_d, ssm_w_glu, sw_sink, ln1_g, ln1_b, ln2_g, ln2_b,
           ffn_w_gate, ffn_w_up, ffn_w_down,
           moe_w_router, moe_b_router, moe_w_gate, moe_w_up, moe_w_down):
    batch, seq, _ = x.shape
    n_ctx = ctx.shape[1]
    assert batch < MOD_ROWS and seq % 512 == 0 and n_ctx % 256 == 0

    c_all = jnp.zeros((MOD_ROWS, D_MODEL), F32).at[:batch].set(c).at[batch].set(c_ctx)
    modr = _mods_call(c_all, ada_w, ada_b)

    cos_t, sin_t = _rope_tables(seq)
    seg = jnp.kron(jnp.eye(256 // HEAD_DIM, dtype=F32), jnp.ones((HEAD_DIM, HEAD_DIM), F32)).astype(BF16)
    tri = (jnp.arange(MOE_TM)[:, None] < jnp.arange(MOE_TM)[None, :]).astype(BF16)
    sw_mask = _window_mask_table(seq)

    h = x.reshape(batch * seq, D_MODEL)
    hc = ctx.reshape(batch * n_ctx, D_MODEL)
    names = ("qa", "ka", "va", "qb", "kb", "vb", "u", "qd", "kd", "vd")

    for layer in range(DEPTH):
        need_ctx = layer < DEPTH - 1
        w_in_l = w_in[layer].astype(BF16)
        w_out_l = w_out[layer].astype(BF16)
        gq = jnp.tile(ga_q_norm[layer].astype(F32), 256 // HEAD_DIM)[None, :]
        gk = jnp.tile(ga_k_norm[layer].astype(F32), 128 // HEAD_DIM)[None, :]
        sink = sw_sink[layer].astype(F32)
        ln1 = (ln1_g[layer][None, :], ln1_b[layer][None, :])
        ln2 = (ln2_g[layer][None, :], ln2_b[layer][None, :])

        lat = dict(zip(names, _in_proj_call(h, modr, layer, w_in_l, cos_t, sin_t, seg, gq, gk,
                                            batch=batch, is_ctx=False)))
        cx = dict(zip(names, _in_proj_call(hc, modr, layer, w_in_l, cos_t, sin_t, seg, gq, gk,
                                           batch=batch, is_ctx=True)))

        na_bias, na_mask = _na_tables(na_rpb[layer], seq)
        out_a = _na_call(lat["qa"], lat["ka"], lat["va"], cx["ka"], cx["va"], na_bias, na_mask, batch=batch)
        out_b = _global_call(lat["qb"], lat["kb"], lat["vb"], cx["kb"], cx["vb"], batch=batch)
        out_d = _window_call(sink, lat["qd"], lat["kd"], lat["vd"], cx["kd"], cx["vd"], sw_mask, batch=batch)

        dirs = [_s5_operands(ssm_lambda_re[layer, d], ssm_lambda_im[layer, d], ssm_log_step[layer, d],
                             ssm_b_re[layer, d], ssm_b_im[layer, d], ssm_c_re[layer, d], ssm_c_im[layer, d])
                for d in range(2)]
        fwd = _s5_pass(cx["u"], lat["u"], None, *dirs[0], None, None, batch=batch, reverse=False)
        ctx_c, out_c = _s5_pass(cx["u"], lat["u"], fwd, *dirs[1], ssm_d[layer].astype(F32)[None, :],
                                ssm_w_glu[layer].astype(BF16), batch=batch, reverse=True)

        routed = layer % 2 == 1
        h1, f_in = _out_proj_call(out_a, out_b, out_c, out_d, w_out_l, h, modr, layer, *ln1,
                                  batch=batch, is_ctx=False, packed=routed)
        streams = [(h1, f_in, False)]
        if need_ctx:
            ctx_a, ctx_b, ctx_d = _ctx_attn_call(sink, cx, batch=batch)
            hc1, fc_in = _out_proj_call(ctx_a, ctx_b, ctx_c, ctx_d, w_out_l, hc, modr, layer, *ln1,
                                        batch=batch, is_ctx=True, packed=routed)
            streams.append((hc1, fc_in, True))

        i = layer // 2
        outs = []
        if not routed:
            wg, wu, wd = (ffn_w_gate[i].astype(BF16), ffn_w_up[i].astype(BF16), ffn_w_down[i].astype(BF16))
            for s1, sf, is_ctx in streams:
                outs.append(_ffn_call(sf, wg, wu, wd, s1, modr, layer, *ln2, batch=batch, is_ctx=is_ctx))
        else:
            wr_t = moe_w_router[i].astype(F32).T
            b_r = moe_b_router[i].astype(F32)[:, None]
            for s1, sf, is_ctx in streams:
                outs.append(_moe_call(sf, s1, modr, layer, wr_t, b_r, tri, moe_w_gate[i], moe_w_up[i],
                                      moe_w_down[i], *ln2, batch=batch, is_ctx=is_ctx))
        h = outs[0]
        if need_ctx:
            hc = outs[1]
    return h.reshape(batch, seq, D_MODEL)
```

```python
import functools

import jax
import jax.numpy as jnp
from jax import lax
from jax.experimental import pallas as pl
from jax.experimental.pallas import tpu as pltpu

F32 = jnp.float32
BF16 = jnp.bfloat16

D_MODEL = 1024
DEPTH = 4
GRID_W = 64
HEAD_DIM = 64
NA_ROWS = 8
NA_COLS = 16
SSM_GROUPS = 16
SSM_GROUP_CH = 16
SSM_STATE = 64
SSM_WIDTH = SSM_GROUPS * SSM_GROUP_CH
SSM_FLAT = SSM_GROUPS * SSM_STATE
SW_WINDOW = 128
N_EXPERTS = 8
ADA_CHUNKS = 6
ROPE_THETA = 10000.0
LN_EPS = 1e-6
RMS_EPS = 1e-6
NEG_INF = -1e30
DEEPNORM_ALPHA = (2 * DEPTH) ** 0.25
ATT_SCALE = HEAD_DIM ** -0.5

C_QA, C_KA, C_VA, C_QB, C_KB, C_VB, C_U, C_QD, C_KD, C_VD, IN_WIDTH = (
    0, 256, 512, 768, 1024, 1152, 1280, 1536, 1792, 1920, 2048)

LANES = 128
SUBLANES = 8
MOD_ROWS = 16
VMEM_LIMIT = 56 * 1024 * 1024


def _cparams(*sem):
    return pltpu.CompilerParams(dimension_semantics=sem, vmem_limit_bytes=VMEM_LIMIT)


def _dot(a, b):
    return jnp.dot(a, b, preferred_element_type=F32)


def _dot_nt(a, b):
    return lax.dot_general(a, b, (((1,), (1,)), ((), ())), preferred_element_type=F32)


def _dot_tn(a, b):
    return lax.dot_general(a, b, (((0,), (0,)), ((), ())), preferred_element_type=F32)


def _split_bf16(x):
    hi = x.astype(BF16)
    lo = (x - hi.astype(F32)).astype(BF16)
    return hi, lo


def _layer_norm(z, g, b):
    zc = z - jnp.mean(z, axis=-1, keepdims=True)
    y = zc * lax.rsqrt(jnp.mean(zc * zc, axis=-1, keepdims=True) + LN_EPS)
    return y * g + b


def _mods_kernel(c_ref, w_ref, b_ref, o_ref):
    c = c_ref[...]
    act = (c * jax.nn.sigmoid(c)).astype(BF16)
    o_ref[...] = _dot(act, w_ref[...].astype(BF16)) + b_ref[...]


def _mods_call(c_all, ada_w, ada_b):
    tn = 1536
    n = ADA_CHUNKS * D_MODEL
    out = pl.pallas_call(
        _mods_kernel,
        grid=(DEPTH, n // tn),
        in_specs=[
            pl.BlockSpec((MOD_ROWS, D_MODEL), lambda l, j: (0, 0)),
            pl.BlockSpec((None, D_MODEL, tn), lambda l, j: (l, 0, j)),
            pl.BlockSpec((None, 1, tn), lambda l, j: (l, 0, j)),
        ],
        out_specs=pl.BlockSpec((None, MOD_ROWS, tn), lambda l, j: (l, 0, j)),
        out_shape=jax.ShapeDtypeStruct((DEPTH, MOD_ROWS, n), F32),
        name="mods",
        compiler_params=_cparams("arbitrary", "arbitrary"),
    )(c_all, ada_w, ada_b.reshape(DEPTH, 1, n))
    return out.reshape(DEPTH * MOD_ROWS * ADA_CHUNKS, 1, D_MODEL)


def _mod_spec(layer, chunk, row_fn):
    def index(i, *_):
        return ((layer * MOD_ROWS + row_fn(i)) * ADA_CHUNKS + chunk, 0, 0)
    return pl.BlockSpec((None, 1, D_MODEL), index)


def _seg_rms(x, seg, g):
    hi, lo = _split_bf16(x * x)
    ss = _dot(hi, seg) + _dot(lo, seg)
    return x * lax.rsqrt(ss * (1.0 / HEAD_DIM) + RMS_EPS) * g


def _rope(x, cos, sin_signed, first_half):
    outs = []
    for j in range(x.shape[1] // LANES):
        xs = x[:, j * LANES:(j + 1) * LANES]
        partner = jnp.where(first_half,
                            pltpu.roll(xs, LANES - HEAD_DIM // 2, 1),
                            pltpu.roll(xs, HEAD_DIM // 2, 1))
        outs.append(xs * cos + partner * sin_signed)
    return outs[0] if len(outs) == 1 else jnp.concatenate(outs, axis=1)


def _pad_heads(x, lane, ones_lane=False):
    low = lane < HEAD_DIM
    fill = jnp.where(lane == HEAD_DIM, 1.0, 0.0) if ones_lane else 0.0
    outs = []
    for j in range(x.shape[1] // LANES):
        xs = x[:, j * LANES:(j + 1) * LANES]
        outs.append(jnp.where(low, xs, fill))
        outs.append(jnp.where(low, pltpu.roll(xs, HEAD_DIM, 1), fill))
    return jnp.concatenate(outs, axis=1).astype(BF16)


def _in_proj_kernel(x_ref, sh_ref, sc_ref, w_ref, cos_ref, sin_ref, seg_ref, gq_ref, gk_ref,
                    qa_ref, ka_ref, va_ref, qb_ref, kb_ref, vb_ref, u_ref, qd_ref, kd_ref, vd_ref,
                    *, rope):
    a = (x_ref[...] * (1.0 + sc_ref[...]) + sh_ref[...]).astype(BF16)
    lane = lax.broadcasted_iota(jnp.int32, (a.shape[0], LANES), 1)

    def proj(c0, c1):
        return _dot(a, w_ref[:, c0:c1])

    qa_ref[...] = _pad_heads(proj(C_QA, C_KA) * ATT_SCALE, lane)
    ka_ref[...] = _pad_heads(proj(C_KA, C_VA), lane)
    va_ref[...] = _pad_heads(proj(C_VA, C_QB), lane, ones_lane=True)
    vb_ref[...] = _pad_heads(proj(C_VB, C_U), lane, ones_lane=True)
    u_ref[...] = proj(C_U, C_QD)
    vd_ref[...] = _pad_heads(proj(C_VD, IN_WIDTH), lane, ones_lane=True)

    seg = seg_ref[...]
    qb = _seg_rms(proj(C_QB, C_KB), seg, gq_ref[...])
    kb = _seg_rms(proj(C_KB, C_VB), seg[:LANES, :LANES], gk_ref[...])
    qd = proj(C_QD, C_KD)
    kd = proj(C_KD, C_VD)
    if rope:
        cos = cos_ref[...]
        sin = sin_ref[...]
        first_half = (lane % HEAD_DIM) < (HEAD_DIM // 2)
        qb = _rope(qb, cos, sin, first_half)
        kb = _rope(kb, cos, sin, first_half)
        qd = _rope(qd, cos, sin, first_half)
        kd = _rope(kd, cos, sin, first_half)
    qb_ref[...] = _pad_heads(qb * ATT_SCALE, lane)
    kb_ref[...] = _pad_heads(kb, lane)
    qd_ref[...] = _pad_heads(qd * ATT_SCALE, lane)
    kd_ref[...] = _pad_heads(kd, lane)


def _in_proj_call(x, modr, layer, w_in, cos_t, sin_t, seg, gq, gk, *, batch, is_ctx):
    rows = x.shape[0]
    per_batch = rows // batch
    tm = 256 if is_ctx else 512
    nb = per_batch // tm
    row_fn = (lambda i: batch) if is_ctx else (lambda i: i // nb)
    table_spec = pl.BlockSpec((tm, LANES), (lambda i: (0, 0)) if is_ctx else (lambda i: (i % nb, 0)))
    full = lambda shape: pl.BlockSpec(shape, lambda i: (0,) * len(shape))
    rowblk = lambda w: pl.BlockSpec((tm, w), lambda i: (i, 0))
    wide = lambda w, dt: jax.ShapeDtypeStruct((rows, w), dt)
    out_shapes = [wide(512, BF16), wide(512, BF16), wide(512, BF16),
                  wide(512, BF16), wide(256, BF16), wide(256, BF16),
                  jax.ShapeDtypeStruct((per_batch, batch * SSM_WIDTH), F32),
                  wide(512, BF16), wide(256, BF16), wide(256, BF16)]
    out_specs = [rowblk(512), rowblk(512), rowblk(512), rowblk(512), rowblk(256), rowblk(256),
                 pl.BlockSpec((tm, SSM_WIDTH), lambda i: (i % nb, i // nb)),
                 rowblk(512), rowblk(256), rowblk(256)]
    return pl.pallas_call(
        functools.partial(_in_proj_kernel, rope=not is_ctx),
        grid=(rows // tm,),
        in_specs=[rowblk(D_MODEL), _mod_spec(layer, 0, row_fn), _mod_spec(layer, 1, row_fn),
                  full((D_MODEL, IN_WIDTH)), table_spec, table_spec, full((256, 256)),
                  full((1, 256)), full((1, 128))],
        out_specs=out_specs,
        out_shape=out_shapes,
        name="in_proj_ctx" if is_ctx else "in_proj",
        compiler_params=_cparams("arbitrary"),
    )(x, modr, modr, w_in, cos_t, sin_t, seg, gq, gk)


KV_CHUNK = 256


def _tile(h):
    return slice(h * LANES, (h + 1) * LANES)


def _attend(q, parts, sink=None):
    def score(k_fn, mod_fn, c):
        s = _dot_nt(q, k_fn(c))
        return s if mod_fn is None else mod_fn(s, c)

    tiles = [(k_fn, v_fn, mod_fn, c) for k_fn, v_fn, n, mod_fn in parts for c in range(0, n, KV_CHUNK)]
    m = sink
    acc = None
    for k_fn, v_fn, mod_fn, c in tiles:
        s = score(k_fn, mod_fn, c)
        m_tile = jnp.max(s, axis=-1, keepdims=True)
        m_new = m_tile if m is None else jnp.maximum(m, m_tile)
        pv = _dot(jnp.exp(s - m_new).astype(BF16), v_fn(c))
        acc = pv if acc is None else acc * jnp.exp(m - m_new) + pv
        m = m_new
    den = acc[:, HEAD_DIM:HEAD_DIM + 1]
    if sink is not None:
        den = den + jnp.exp(sink - m)
    return acc * (1.0 / den)


def _pack_pair(a, b):
    lane = lax.broadcasted_iota(jnp.int32, a.shape, 1)
    return jnp.where(lane < HEAD_DIM, a, pltpu.roll(b, HEAD_DIM, 1)).astype(BF16)


def _ref_part(k_ref, v_ref, t, n_keys, start=0, mod_fn=None):
    return (lambda c: k_ref[pl.ds(start + c, KV_CHUNK), _tile(t)],
            lambda c: v_ref[pl.ds(start + c, KV_CHUNK), _tile(t)], n_keys, mod_fn)


def _gqa_group(q_ref, g, parts, sink_pair=None):
    tq = q_ref.shape[0]
    q2 = jnp.concatenate([q_ref[:, _tile(2 * g)], q_ref[:, _tile(2 * g + 1)]], axis=0)
    sink = None
    if sink_pair is not None:
        row = lax.broadcasted_iota(jnp.int32, (2 * tq, 1), 0)
        sink = jnp.where(row < tq, sink_pair[0], sink_pair[1])
    o2 = _attend(q2, parts, sink)
    return _pack_pair(o2[:tq], o2[tq:])


NA_QROWS = 4
NA_BAND = 12


def _na_band_start(j, rows):
    return jnp.clip(j * NA_QROWS - NA_ROWS // 2, 0, rows - NA_BAND)


def _na_kernel(q_ref, k_ref, v_ref, kc_ref, vc_ref, bias_ref, mask_ref, o_ref):
    j = pl.program_id(1)
    rows = k_ref.shape[0] // GRID_W
    start = pl.multiple_of(_na_band_start(j, rows) * GRID_W, GRID_W)
    outs = []
    for h in range(4):
        def local_scores(s, c, h=h):
            cols = slice(c, c + KV_CHUNK)
            return jnp.where(mask_ref[:, cols] > 0.5, s + bias_ref[h, :, cols], NEG_INF)

        parts = [_ref_part(k_ref, v_ref, h, NA_BAND * GRID_W, start, local_scores),
                 _ref_part(kc_ref, vc_ref, h, kc_ref.shape[0])]
        outs.append(_attend(q_ref[:, _tile(h)], parts))
    o_ref[:, _tile(0)] = _pack_pair(outs[0], outs[1])
    o_ref[:, _tile(1)] = _pack_pair(outs[2], outs[3])


def _na_call(qa, ka, va, ka_c, va_c, bias_tab, mask_tab, *, batch):
    seq = qa.shape[0] // batch
    n_ctx = ka_c.shape[0] // batch
    nj = seq // (NA_QROWS * GRID_W)
    tq = NA_QROWS * GRID_W
    nk = NA_BAND * GRID_W

    def pattern(j):
        return jnp.where(j == 0, 0, jnp.where(j == nj - 1, 2, 1))

    return pl.pallas_call(
        _na_kernel,
        grid=(batch, nj),
        in_specs=[
            pl.BlockSpec((tq, 512), lambda b, j: (b * nj + j, 0)),
            pl.BlockSpec((seq, 512), lambda b, j: (b, 0)),
            pl.BlockSpec((seq, 512), lambda b, j: (b, 0)),
            pl.BlockSpec((n_ctx, 512), lambda b, j: (b, 0)),
            pl.BlockSpec((n_ctx, 512), lambda b, j: (b, 0)),
            pl.BlockSpec((None, 4, tq, nk), lambda b, j: (pattern(j), 0, 0, 0)),
            pl.BlockSpec((None, tq, nk), lambda b, j: (pattern(j), 0, 0)),
        ],
        out_specs=pl.BlockSpec((tq, 256), lambda b, j: (b * nj + j, 0)),
        out_shape=jax.ShapeDtypeStruct((qa.shape[0], 256), BF16),
        name="na_attn",
        compiler_params=_cparams("arbitrary", "arbitrary"),
    )(qa, ka, va, ka_c, va_c, bias_tab, mask_tab)


def _na_tables(rpb, seq):
    rows = seq // GRID_W
    nj = rows // NA_QROWS
    c_idx = jnp.arange(GRID_W)
    col_start = jnp.clip(c_idx - NA_COLS // 2, 0, GRID_W - NA_COLS)
    col_valid = (c_idx[None, :] >= col_start[:, None]) & (c_idx[None, :] < col_start[:, None] + NA_COLS)
    col_rel = jnp.clip(c_idx[None, :] - c_idx[:, None], 1 - NA_COLS, NA_COLS - 1) + NA_COLS - 1
    blocks = jnp.array([0, 1, nj - 1])
    q_row = blocks[:, None] * NA_QROWS + jnp.arange(NA_QROWS)[None, :]
    k_row = _na_band_start(blocks, rows)[:, None] + jnp.arange(NA_BAND)[None, :]
    row_start = jnp.clip(q_row - NA_ROWS // 2, 0, rows - NA_ROWS)
    row_valid = ((k_row[:, None, :] >= row_start[:, :, None])
                 & (k_row[:, None, :] < row_start[:, :, None] + NA_ROWS))
    row_rel = jnp.clip(k_row[:, None, :] - q_row[:, :, None] + NA_ROWS - 1, 0, 2 * NA_ROWS - 2)
    row_sel = jax.nn.one_hot(row_rel, 2 * NA_ROWS - 1, dtype=F32)
    col_sel = jax.nn.one_hot(col_rel, 2 * NA_COLS - 1, dtype=F32)
    exact = lax.Precision.HIGHEST
    by_col = jnp.einsum("hab,qkb->haqk", rpb.astype(F32), col_sel, precision=exact)
    bias = jnp.einsum("pria,haqk->phrqik", row_sel, by_col, precision=exact)
    valid = row_valid[:, :, None, :, None] & col_valid[None, None, :, None, :]
    tq, nk = NA_QROWS * GRID_W, NA_BAND * GRID_W
    return bias.reshape(3, 4, tq, nk), valid.astype(F32).reshape(3, tq, nk)


def _global_kernel(q_ref, k_ref, v_ref, kc_ref, vc_ref, o_ref):
    for g in range(2):
        parts = [_ref_part(kc_ref, vc_ref, g, kc_ref.shape[0]), _ref_part(k_ref, v_ref, g, k_ref.shape[0])]
        o_ref[:, _tile(g)] = _gqa_group(q_ref, g, parts)


def _global_call(qb, kb, vb, kb_c, vb_c, *, batch):
    seq = qb.shape[0] // batch
    n_ctx = kb_c.shape[0] // batch
    tq = 256
    nq = seq // tq
    return pl.pallas_call(
        _global_kernel,
        grid=(batch, nq),
        in_specs=[
            pl.BlockSpec((tq, 512), lambda b, i: (b * nq + i, 0)),
            pl.BlockSpec((seq, 256), lambda b, i: (b, 0)),
            pl.BlockSpec((seq, 256), lambda b, i: (b, 0)),
            pl.BlockSpec((n_ctx, 256), lambda b, i: (b, 0)),
            pl.BlockSpec((n_ctx, 256), lambda b, i: (b, 0)),
        ],
        out_specs=pl.BlockSpec((tq, 256), lambda b, i: (b * nq + i, 0)),
        out_shape=jax.ShapeDtypeStruct((qb.shape[0], 256), BF16),
        name="global_attn",
        compiler_params=_cparams("arbitrary", "arbitrary"),
    )(qb, kb, vb, kb_c, vb_c)


SW_TQ = 256
SW_SPAN = SW_TQ + 2 * SW_WINDOW


def _window_start(i, seq):
    return jnp.clip(i * SW_TQ - SW_WINDOW, 0, seq - SW_SPAN)


def _window_kernel(sink_ref, q_ref, k_ref, v_ref, kc_ref, vc_ref, mask_ref, o_ref):
    i = pl.program_id(1)
    start = pl.multiple_of(_window_start(i, k_ref.shape[0]), LANES)

    def in_window(s, c):
        return jnp.where(mask_ref[:, c:c + KV_CHUNK] > 0.5, s, NEG_INF)

    for g in range(2):
        parts = [_ref_part(k_ref, v_ref, g, SW_SPAN, start, in_window),
                 _ref_part(kc_ref, vc_ref, g, kc_ref.shape[0])]
        o_ref[:, _tile(g)] = _gqa_group(q_ref, g, parts, sink_pair=(sink_ref[2 * g], sink_ref[2 * g + 1]))


def _window_mask_table(seq):
    nq = seq // SW_TQ
    blocks = jnp.array([0, 1, nq - 1])
    qpos = blocks[:, None] * SW_TQ + jnp.arange(SW_TQ)[None, :]
    kpos = _window_start(blocks, seq)[:, None] + jnp.arange(SW_SPAN)[None, :]
    valid = jnp.abs(kpos[:, None, :] - qpos[:, :, None]) <= SW_WINDOW
    return jnp.tile(valid.astype(F32), (1, 2, 1))


def _window_call(sink, qd, kd, vd, kd_c, vd_c, mask_tab, *, batch):
    seq = qd.shape[0] // batch
    n_ctx = kd_c.shape[0] // batch
    nq = seq // SW_TQ

    def pattern(i):
        return jnp.where(i == 0, 0, jnp.where(i == nq - 1, 2, 1))

    return pl.pallas_call(
        _window_kernel,
        grid=(batch, nq),
        in_specs=[
            pl.BlockSpec(memory_space=pltpu.SMEM),
            pl.BlockSpec((SW_TQ, 512), lambda b, i: (b * nq + i, 0)),
            pl.BlockSpec((seq, 256), lambda b, i: (b, 0)),
            pl.BlockSpec((seq, 256), lambda b, i: (b, 0)),
            pl.BlockSpec((n_ctx, 256), lambda b, i: (b, 0)),
            pl.BlockSpec((n_ctx, 256), lambda b, i: (b, 0)),
            pl.BlockSpec((None, 2 * SW_TQ, SW_SPAN), lambda b, i: (pattern(i), 0, 0)),
        ],
        out_specs=pl.BlockSpec((SW_TQ, 256), lambda b, i: (b * nq + i, 0)),
        out_shape=jax.ShapeDtypeStruct((qd.shape[0], 256), BF16),
        name="window_attn",
        compiler_params=_cparams("arbitrary", "arbitrary"),
    )(sink, qd, kd, vd, kd_c, vd_c, mask_tab)


def _ctx_attn_kernel(sink_ref, qa_ref, ka_ref, va_ref, qb_ref, kb_ref, vb_ref, qd_ref, kd_ref, vd_ref,
                     oa_ref, ob_ref, od_ref):
    n = ka_ref.shape[0]
    outs = [_attend(qa_ref[:, _tile(h)], [_ref_part(ka_ref, va_ref, h, n)]) for h in range(4)]
    oa_ref[:, _tile(0)] = _pack_pair(outs[0], outs[1])
    oa_ref[:, _tile(1)] = _pack_pair(outs[2], outs[3])
    for g in range(2):
        ob_ref[:, _tile(g)] = _gqa_group(qb_ref, g, [_ref_part(kb_ref, vb_ref, g, n)])
        od_ref[:, _tile(g)] = _gqa_group(qd_ref, g, [_ref_part(kd_ref, vd_ref, g, n)],
                                         sink_pair=(sink_ref[2 * g], sink_ref[2 * g + 1]))


def _ctx_attn_call(sink, cx, *, batch):
    n_ctx = cx["qa"].shape[0] // batch
    blk = lambda w: pl.BlockSpec((n_ctx, w), lambda b: (b, 0))
    names = ("qa", "ka", "va", "qb", "kb", "vb", "qd", "kd", "vd")
    widths = (512, 512, 512, 512, 256, 256, 512, 256, 256)
    shape = jax.ShapeDtypeStruct((cx["qa"].shape[0], 256), BF16)
    return pl.pallas_call(
        _ctx_attn_kernel,
        grid=(batch,),
        in_specs=[pl.BlockSpec(memory_space=pltpu.SMEM)] + [blk(w) for w in widths],
        out_specs=[blk(256)] * 3,
        out_shape=[shape] * 3,
        name="ctx_attn",
        compiler_params=_cparams("arbitrary"),
    )(sink, *[cx[n] for n in names])


def _s5_discretise(lam_re, lam_im, log_step, b_re, b_im):
    step = jnp.exp(log_step.astype(F32))[:, None]
    lam_re = lam_re.astype(F32)
    lam_im = lam_im.astype(F32)
    mag = jnp.exp(lam_re * step)
    ab_re = mag * jnp.cos(lam_im * step)
    ab_im = mag * jnp.sin(lam_im * step)
    den = lam_re * lam_re + lam_im * lam_im
    num_re = ab_re - 1.0
    f_re = ((num_re * lam_re + ab_im * lam_im) / den)[..., None]
    f_im = ((ab_im * lam_re - num_re * lam_im) / den)[..., None]
    b_re = b_re.astype(F32)
    b_im = b_im.astype(F32)
    return ab_re, ab_im, f_re * b_re - f_im * b_im, f_re * b_im + f_im * b_re


def _s5_operands(lam_re, lam_im, log_step, b_re, b_im, c_re, c_im):
    ab_re, ab_im, bb_re, bb_im = _s5_discretise(lam_re, lam_im, log_step, b_re, b_im)
    eye = jnp.eye(SSM_GROUPS, dtype=F32)
    bd_in = lambda m: jnp.einsum("gph,gk->ghkp", m, eye).reshape(SSM_WIDTH, SSM_FLAT)
    bd_out = lambda m: jnp.einsum("ghp,gk->kpgh", m.astype(F32), eye).reshape(SSM_FLAT, SSM_WIDTH)
    a = jnp.stack([ab_re.reshape(SSM_FLAT), ab_im.reshape(SSM_FLAT)])
    bbd = jnp.concatenate([bd_in(bb_re), bd_in(bb_im)], axis=1).astype(BF16)
    cbd = jnp.concatenate([bd_out(c_re), -bd_out(c_im)], axis=0).astype(BF16)
    return a, bbd, cbd


def _s5_kernel(*refs, reverse, final, batch, n_ctx_chunks):
    if final:
        (uc_ref, ul_ref, pc_ref, pl_ref, a_ref, bbd_ref, cbd_ref, d_ref, wglu_ref,
         oc_ref, ol_ref, bu_ref, h_ref) = refs
    else:
        uc_ref, ul_ref, a_ref, bbd_ref, cbd_ref, oc_ref, ol_ref, bu_ref, h_ref = refs
        pc_ref = pl_ref = d_ref = wglu_ref = None
    c = pl.program_id(0)
    steps = bu_ref.shape[0] // batch

    @pl.when(c == 0)
    def _():
        h_ref[...] = jnp.zeros_like(h_ref)

    def run(u_ref, prev_ref, o_ref):
        u = u_ref[...]
        bu_ref[...] = _dot(u.astype(BF16), bbd_ref[...])
        ar = jnp.broadcast_to(a_ref[0:1, :], (batch, SSM_FLAT))
        ai = jnp.broadcast_to(a_ref[1:2, :], (batch, SSM_FLAT))

        def step(i, h):
            t = steps - 1 - i if reverse else i
            row = pl.multiple_of(t * batch, batch)
            b = bu_ref[pl.ds(row, batch), :]
            hr, hi = h[:, :SSM_FLAT], h[:, SSM_FLAT:]
            nr = ar * hr - ai * hi + b[:, :SSM_FLAT]
            ni = ar * hi + ai * hr + b[:, SSM_FLAT:]
            hn = jnp.concatenate([nr, ni], axis=1)
            bu_ref[pl.ds(row, batch), :] = hn
            return hn

        h_ref[...] = lax.fori_loop(0, steps, step, h_ref[...], unroll=4)
        y = _dot(bu_ref[...].astype(BF16), cbd_ref[...])
        if not final:
            o_ref[...] = y
        else:
            y = jax.nn.gelu(y + prev_ref[...] + d_ref[...] * u)
            z = _dot(y.astype(BF16), wglu_ref[...])
            o_ref[...] = (y * jax.nn.sigmoid(z)).astype(BF16)

    @pl.when(c < n_ctx_chunks)
    def _():
        run(uc_ref, pc_ref, oc_ref)

    @pl.when(c >= n_ctx_chunks)
    def _():
        run(ul_ref, pl_ref, ol_ref)


def _s5_pass(u_ctx, u_lat, prev, a, bbd, cbd, d_skip, w_glu, *, batch, reverse):
    final = prev is not None
    t_chunk = 128
    rows = t_chunk * batch
    ncc = u_ctx.shape[0] // t_chunk
    nlc = u_lat.shape[0] // t_chunk
    flat = lambda x: x.reshape(x.shape[0] * batch, SSM_WIDTH)
    if reverse:
        ctx_idx = lambda c: (jnp.maximum(ncc - 1 - c, 0), 0)
        lat_idx = lambda c: (nlc - 1 - jnp.maximum(c - ncc, 0), 0)
    else:
        ctx_idx = lambda c: (jnp.minimum(c, ncc - 1), 0)
        lat_idx = lambda c: (jnp.maximum(c - ncc, 0), 0)
    full = lambda shape: pl.BlockSpec(shape, lambda c: (0,) * len(shape))
    u_specs = [pl.BlockSpec((rows, SSM_WIDTH), ctx_idx), pl.BlockSpec((rows, SSM_WIDTH), lat_idx)]
    par_specs = [full((2, SSM_FLAT)), full((SSM_WIDTH, 2 * SSM_FLAT)), full((2 * SSM_FLAT, SSM_WIDTH))]
    args = [flat(u_ctx), flat(u_lat)]
    in_specs = list(u_specs)
    if final:
        args += [flat(prev[0]), flat(prev[1])]
        in_specs += u_specs
    args += [a, bbd, cbd]
    in_specs += par_specs
    if final:
        args += [d_skip, w_glu]
        in_specs += [full((1, SSM_WIDTH)), full((SSM_WIDTH, SSM_WIDTH))]
    dt = BF16 if final else F32
    oc, ol = pl.pallas_call(
        functools.partial(_s5_kernel, reverse=reverse, final=final, batch=batch, n_ctx_chunks=ncc),
        grid=(ncc + nlc,),
        in_specs=in_specs,
        out_specs=u_specs,
        out_shape=[jax.ShapeDtypeStruct((u_ctx.shape[0] * batch, SSM_WIDTH), dt),
                   jax.ShapeDtypeStruct((u_lat.shape[0] * batch, SSM_WIDTH), dt)],
        scratch_shapes=[pltpu.VMEM((rows, 2 * SSM_FLAT), F32), pltpu.VMEM((batch, 2 * SSM_FLAT), F32)],
        name="s5_bwd_glu" if final else "s5_fwd",
        compiler_params=_cparams("arbitrary"),
    )(*args)
    return oc.reshape(u_ctx.shape), ol.reshape(u_lat.shape)


def _bf16_bits(x):
    return lax.bitcast_convert_type(x.astype(BF16).astype(F32), jnp.uint32)


def _pack_bf16_pair(a, b):
    return _bf16_bits(a) | lax.shift_right_logical(_bf16_bits(b), jnp.uint32(16))


def _unpack_bf16_pair(u):
    a = lax.bitcast_convert_type(u & jnp.uint32(0xFFFF0000), F32)
    b = lax.bitcast_convert_type(lax.shift_left(u, jnp.uint32(16)), F32)
    return a.astype(BF16), b.astype(BF16)


def _out_proj_kernel(oa_ref, ob_ref, oc_ref, od_ref, w_ref, x_ref, g1_ref, sh2_ref, sc2_ref,
                     lng_ref, lnb_ref, h_ref, f_ref, *, packed):
    y = (_dot(oa_ref[...], w_ref[0:256, :]) + _dot(ob_ref[...], w_ref[256:512, :])
         + _dot(oc_ref[...], w_ref[512:768, :]) + _dot(od_ref[...], w_ref[768:1024, :]))
    h1 = _layer_norm(DEEPNORM_ALPHA * x_ref[...] + g1_ref[...] * y, lng_ref[...], lnb_ref[...])
    h_ref[...] = h1
    f = h1 * (1.0 + sc2_ref[...]) + sh2_ref[...]
    if packed:
        f_ref[...] = _pack_bf16_pair(f[:, :D_MODEL // 2], f[:, D_MODEL // 2:])
    else:
        f_ref[...] = f.astype(BF16)


def _out_proj_call(oa, ob, oc_tm, od, w_out, x, modr, layer, lng, lnb, *, batch, is_ctx, packed):
    rows = x.shape[0]
    per_batch = rows // batch
    tm = 256 if is_ctx else 512
    nb = per_batch // tm
    row_fn = (lambda i: batch) if is_ctx else (lambda i: i // nb)
    rowblk = lambda w: pl.BlockSpec((tm, w), lambda i: (i, 0))
    full = lambda shape: pl.BlockSpec(shape, lambda i: (0,) * len(shape))
    f_width, f_dtype = (D_MODEL // 2, jnp.uint32) if packed else (D_MODEL, BF16)
    return pl.pallas_call(
        functools.partial(_out_proj_kernel, packed=packed),
        grid=(rows // tm,),
        in_specs=[rowblk(256), rowblk(256),
                  pl.BlockSpec((tm, SSM_WIDTH), lambda i: (i % nb, i // nb)),
                  rowblk(256), full((D_MODEL, D_MODEL)), rowblk(D_MODEL),
                  _mod_spec(layer, 2, row_fn), _mod_spec(layer, 3, row_fn), _mod_spec(layer, 4, row_fn),
                  full((1, D_MODEL)), full((1, D_MODEL))],
        out_specs=[rowblk(D_MODEL), rowblk(f_width)],
        out_shape=[jax.ShapeDtypeStruct((rows, D_MODEL), F32), jax.ShapeDtypeStruct((rows, f_width), f_dtype)],
        name="out_proj_ctx" if is_ctx else "out_proj",
        compiler_params=_cparams("arbitrary"),
    )(oa, ob, oc_tm, od, w_out, x, modr, modr, modr, lng, lnb)


def _ffn_kernel(f_ref, wg_ref, wu_ref, wd_ref, h_ref, g2_ref, lng_ref, lnb_ref, o_ref, acc_ref):
    j = pl.program_id(1)

    @pl.when(j == 0)
    def _():
        acc_ref[...] = jnp.zeros_like(acc_ref)

    f = f_ref[...]
    g = _dot(f, wg_ref[...])
    u = _dot(f, wu_ref[...])
    acc_ref[...] += _dot((g * jax.nn.sigmoid(g) * u).astype(BF16), wd_ref[...])

    @pl.when(j == pl.num_programs(1) - 1)
    def _():
        o_ref[...] = _layer_norm(DEEPNORM_ALPHA * h_ref[...] + g2_ref[...] * acc_ref[...],
                                 lng_ref[...], lnb_ref[...])


def _ffn_call(f_in, wg, wu, wd, h1, modr, layer, lng, lnb, *, batch, is_ctx):
    rows = h1.shape[0]
    per_batch = rows // batch
    tm = 256 if is_ctx else 512
    d_ff = wg.shape[1]
    tf = d_ff // 2
    nb = per_batch // tm
    row_fn = (lambda i: batch) if is_ctx else (lambda i: i // nb)
    rowblk = lambda w: pl.BlockSpec((tm, w), lambda i, j: (i, 0))
    full = lambda shape: pl.BlockSpec(shape, lambda i, j: (0,) * len(shape))
    return pl.pallas_call(
        _ffn_kernel,
        grid=(rows // tm, d_ff // tf),
        in_specs=[rowblk(D_MODEL),
                  pl.BlockSpec((D_MODEL, tf), lambda i, j: (0, j)),
                  pl.BlockSpec((D_MODEL, tf), lambda i, j: (0, j)),
                  pl.BlockSpec((tf, D_MODEL), lambda i, j: (j, 0)),
                  rowblk(D_MODEL), _mod_spec(layer, 5, row_fn), full((1, D_MODEL)), full((1, D_MODEL))],
        out_specs=rowblk(D_MODEL),
        out_shape=jax.ShapeDtypeStruct((rows, D_MODEL), F32),
        scratch_shapes=[pltpu.VMEM((tm, D_MODEL), F32)],
        name="ffn_ctx" if is_ctx else "ffn",
        compiler_params=_cparams("arbitrary", "arbitrary"),
    )(f_in, wg, wu, wd, h1, modr, lng, lnb)


MOE_TM = 1024
MOE_TILE = 512
MOE_PASS_TILES = 10
MOE_FF_CHUNKS = 7
ROUTE_ROWS = 8


def _router_kernel(h_ref, sh2_ref, sc2_ref, wr_ref, br_ref, tri_ref, route_ref, gate_ref, cnt_ref):
    tm = h_ref.shape[0]
    f = h_ref[...] * (1.0 + sc2_ref[...]) + sh2_ref[...]
    f_hi, f_lo = _split_bf16(f)
    w_hi, w_lo = _split_bf16(wr_ref[...])
    logits = _dot_nt(w_hi, f_hi) + _dot_nt(w_hi, f_lo) + _dot_nt(w_lo, f_hi) + br_ref[...]
    ie = lax.broadcasted_iota(jnp.int32, logits.shape, 0)
    m1 = jnp.max(logits, axis=0, keepdims=True)
    i1 = jnp.min(jnp.where(logits == m1, ie, N_EXPERTS), axis=0, keepdims=True)
    rest = jnp.where(ie == i1, -jnp.inf, logits)
    m2 = jnp.max(rest, axis=0, keepdims=True)
    i2 = jnp.min(jnp.where(rest == m2, ie, N_EXPERTS), axis=0, keepdims=True)
    e2 = jnp.exp(m2 - m1)
    den = 1.0 + e2
    sel = jnp.where((ie == i1) | (ie == i2), 1.0, 0.0)
    rank = _dot(sel.astype(BF16), tri_ref[...])
    r1 = jnp.sum(jnp.where(ie == i1, rank, 0.0), axis=0, keepdims=True).astype(jnp.int32)
    r2 = jnp.sum(jnp.where(ie == i2, rank, 0.0), axis=0, keepdims=True).astype(jnp.int32)
    row = lax.broadcasted_iota(jnp.int32, (ROUTE_ROWS, tm), 0)
    route_ref[...] = jnp.where(row == 0, i1, jnp.where(row == 1, i2, jnp.where(row == 2, r1,
                               jnp.where(row == 3, r2, 0))))
    lrow = lax.broadcasted_iota(jnp.int32, (LANES, tm), 0)
    gate_ref[...] = jnp.where(lrow == 0, 1.0 / den, jnp.where(lrow == 1, e2 / den, 0.0)).T
    cnt = jnp.sum(sel, axis=1, keepdims=True)
    cnt_ref[...] = jnp.broadcast_to(cnt, cnt_ref.shape).astype(jnp.int32)


def _router_call(h1, modr, layer, wr_t, b_r, tri, *, batch, is_ctx):
    rows = h1.shape[0]
    tm = MOE_TM
    nblk = rows // tm
    nb = (rows // batch) // tm if not is_ctx else 1
    row_fn = (lambda i: batch) if is_ctx else (lambda i: i // nb)
    full = lambda shape: pl.BlockSpec(shape, lambda i: (0,) * len(shape))
    return pl.pallas_call(
        _router_kernel,
        grid=(nblk,),
        in_specs=[pl.BlockSpec((tm, D_MODEL), lambda i: (i, 0)),
                  _mod_spec(layer, 3, row_fn), _mod_spec(layer, 4, row_fn),
                  full((N_EXPERTS, D_MODEL)), full((N_EXPERTS, 1)), full((tm, tm))],
        out_specs=[pl.BlockSpec((None, ROUTE_ROWS, tm), lambda i: (i, 0, 0)),
                   pl.BlockSpec((tm, LANES), lambda i: (i, 0)),
                   pl.BlockSpec((None, N_EXPERTS, LANES), lambda i: (i, 0, 0))],
        out_shape=[jax.ShapeDtypeStruct((nblk, ROUTE_ROWS, tm), jnp.int32),
                   jax.ShapeDtypeStruct((rows, LANES), F32),
                   jax.ShapeDtypeStruct((nblk, N_EXPERTS, LANES), jnp.int32)],
        name="router_ctx" if is_ctx else "router",
        compiler_params=_cparams("arbitrary"),
    )(h1, modr, modr, wr_t, b_r, tri)


def _route_plan(route, cnt, n_tiles, n_passes):
    total = jnp.sum(cnt, axis=0)
    tiles = lax.div(total + (MOE_TILE - 1), MOE_TILE)
    tile_start = jnp.cumsum(tiles) - tiles
    base = tile_start[None, :] * MOE_TILE + jnp.cumsum(cnt, axis=0) - cnt
    choice = jax.nn.one_hot(route[:, 0:2, :], N_EXPERTS, dtype=jnp.int32)
    pos = jnp.sum(choice * base[:, None, None, :], axis=-1) + route[:, 2:4, :]
    nblk, _, tm = pos.shape
    token = jnp.broadcast_to((jnp.arange(nblk)[:, None, None] * tm + jnp.arange(tm)[None, None, :]), pos.shape)
    src = jnp.zeros((n_tiles * MOE_TILE,), jnp.int32).at[pos.reshape(-1)].set(
        token.reshape(-1).astype(jnp.int32), unique_indices=True, mode="promise_in_bounds")
    src = src.reshape(n_tiles, 1, MOE_TILE)
    passes = lax.div(tiles + (MOE_PASS_TILES - 1), MOE_PASS_TILES)
    pass_end = jnp.cumsum(passes)
    p = jnp.arange(n_passes, dtype=jnp.int32)
    owner = jnp.sum((p[:, None] >= pass_end[None, :]).astype(jnp.int32), axis=1)
    last = jnp.sum((pass_end[-1] - 1 >= pass_end).astype(jnp.int32))
    expert = jnp.where(p < pass_end[-1], owner, last)
    within = (p - (pass_end - passes)[expert]) * MOE_PASS_TILES
    n = jnp.where(p < pass_end[-1], jnp.clip(tiles[expert] - within, 0, MOE_PASS_TILES), 0)
    first = tile_start[expert] + within
    used = jnp.sum(tiles)
    first = jnp.where(p == pass_end[-1], used, first)
    n_zero = jnp.where(p == pass_end[-1], n_tiles - used, 0)
    i32 = lambda a: a.astype(jnp.int32)
    return i32(pos), src, i32(expert), i32(first), i32(n), i32(n_zero)


def _row_copies(src_of, dst_of, n_rows, sem):
    def copy(t, k):
        return pltpu.make_async_copy(src_of(t, k), dst_of(t, k), sem)

    def start(t, carry):
        copy(t, 0).start()
        copy(t, 1).start()
        return carry

    def wait(t, carry):
        copy(t, 0).wait()
        copy(t, 1).wait()
        return carry

    lax.fori_loop(0, n_rows, start, 0, unroll=8)
    lax.fori_loop(0, n_rows, wait, 0, unroll=8)


def _dispatch_kernel(src_ref, x_hbm, xs_ref, sem):
    half = MOE_TILE // 2
    _row_copies(lambda r, k: x_hbm.at[pl.ds(src_ref[0, k * half + r], 1)],
                lambda r, k: xs_ref.at[pl.ds(k * half + r, 1)], half, sem)


def _dispatch_call(src, x_packed):
    n_tiles = src.shape[0]
    width = x_packed.shape[1]
    return pl.pallas_call(
        _dispatch_kernel,
        grid=(n_tiles,),
        in_specs=[pl.BlockSpec((None, 1, MOE_TILE), lambda t: (t, 0, 0), memory_space=pltpu.SMEM),
                  pl.BlockSpec(memory_space=pl.ANY)],
        out_specs=pl.BlockSpec((MOE_TILE, width), lambda t: (t, 0)),
        out_shape=jax.ShapeDtypeStruct((n_tiles * MOE_TILE, width), x_packed.dtype),
        scratch_shapes=[pltpu.SemaphoreType.DMA(())],
        name="moe_dispatch",
        compiler_params=_cparams("arbitrary"),
    )(src, x_packed)


def _experts_kernel(pe_ref, pf_ref, pn_ref, pz_ref, xs_hbm, wg_ref, wu_ref, wd_ref, ys_hbm,
                    xa_buf, xb_buf, y_buf, stage_buf, wg_buf, wu_buf, wd_buf, sem):
    del pe_ref
    p, j = pl.program_id(0), pl.program_id(1)
    n_tiles = pn_ref[p]
    first = pf_ref[p]
    half = D_MODEL // 2

    def tile_rows(k):
        return pl.ds(pl.multiple_of((first + k) * MOE_TILE, MOE_TILE), MOE_TILE)

    def store_tiles(n, src_of):
        def store(k):
            return pltpu.make_async_copy(src_of(k), ys_hbm.at[tile_rows(k)], sem)
        lax.fori_loop(0, n, lambda k, c: (store(k).start(), c)[1], 0)
        lax.fori_loop(0, n, lambda k, c: (store(k).wait(), c)[1], 0)

    @pl.when((pz_ref[p] > 0) & (j == 0))
    def _():
        y_buf[0] = jnp.zeros(y_buf.shape[1:], F32)
        store_tiles(pz_ref[p], lambda k: y_buf.at[0])

    @pl.when(n_tiles > 0)
    def _():
        @pl.when(j == 0)
        def _():
            def load(k, carry):
                copy = pltpu.make_async_copy(xs_hbm.at[tile_rows(k)], stage_buf, sem)
                copy.start()
                copy.wait()
                xa_buf[k], xb_buf[k] = _unpack_bf16_pair(stage_buf[...])
                return carry
            lax.fori_loop(0, n_tiles, load, 0)

        wg_buf[...] = wg_ref[...].astype(BF16)
        wu_buf[...] = wu_ref[...].astype(BF16)
        wd_buf[...] = wd_ref[...].astype(BF16)

        def tile(k, carry):
            xa, xb = xa_buf[k], xb_buf[k]
            g = _dot(xa, wg_buf[:half, :]) + _dot(xb, wg_buf[half:, :])
            u = _dot(xa, wu_buf[:half, :]) + _dot(xb, wu_buf[half:, :])
            y = _dot((g * jax.nn.sigmoid(g) * u).astype(BF16), wd_buf[...])

            @pl.when(j == 0)
            def _():
                y_buf[k] = y

            @pl.when(j > 0)
            def _():
                y_buf[k] += y

            return carry
        lax.fori_loop(0, n_tiles, tile, 0)

        @pl.when(j == pl.num_programs(1) - 1)
        def _():
            store_tiles(n_tiles, lambda k: y_buf.at[k])


def _experts_call(pass_expert, pass_first, pass_tiles, pass_zero, xs, w_gate, w_up, w_down, moe_layer):
    n_passes = pass_expert.shape[0]
    d_ff = w_gate.shape[3]
    tf = d_ff // MOE_FF_CHUNKS
    half = D_MODEL // 2
    chunk = lambda p, j, pn: jnp.where(pn[p] > 0, j, MOE_FF_CHUNKS - 1)
    cols = pl.BlockSpec((None, None, D_MODEL, tf),
                        lambda p, j, pe, pf, pn, pz: (moe_layer, pe[p], 0, chunk(p, j, pn)))
    grid_spec = pltpu.PrefetchScalarGridSpec(
        num_scalar_prefetch=4,
        grid=(n_passes, MOE_FF_CHUNKS),
        in_specs=[pl.BlockSpec(memory_space=pl.ANY), cols, cols,
                  pl.BlockSpec((None, None, tf, D_MODEL),
                               lambda p, j, pe, pf, pn, pz: (moe_layer, pe[p], chunk(p, j, pn), 0))],
        out_specs=pl.BlockSpec(memory_space=pl.ANY),
        scratch_shapes=[pltpu.VMEM((MOE_PASS_TILES, MOE_TILE, half), BF16),
                        pltpu.VMEM((MOE_PASS_TILES, MOE_TILE, half), BF16),
                        pltpu.VMEM((MOE_PASS_TILES, MOE_TILE, D_MODEL), F32),
                        pltpu.VMEM((MOE_TILE, half), jnp.uint32),
                        pltpu.VMEM((D_MODEL, tf), BF16), pltpu.VMEM((D_MODEL, tf), BF16),
                        pltpu.VMEM((tf, D_MODEL), BF16),
                        pltpu.SemaphoreType.DMA(())],
    )
    return pl.pallas_call(
        _experts_kernel,
        grid_spec=grid_spec,
        out_shape=jax.ShapeDtypeStruct((xs.shape[0], D_MODEL), F32),
        name="moe_experts",
        compiler_params=_cparams("arbitrary", "arbitrary"),
    )(pass_expert, pass_first, pass_tiles, pass_zero, xs, w_gate, w_up, w_down)


def _combine_kernel(pos_ref, ys_hbm, gate_ref, h_ref, g2_ref, lng_ref, lnb_ref, o_ref, buf_ref, sem):
    tm = pos_ref.shape[1]
    _row_copies(lambda t, k: ys_hbm.at[pl.ds(pos_ref[k, t], 1)],
                lambda t, k: buf_ref.at[k, pl.ds(t, 1)], tm, sem)
    gate = gate_ref[...]
    f = gate[:, 0:1] * buf_ref[0] + gate[:, 1:2] * buf_ref[1]
    o_ref[...] = _layer_norm(DEEPNORM_ALPHA * h_ref[...] + g2_ref[...] * f, lng_ref[...], lnb_ref[...])


def _combine_call(pos, ys, gate, h1, modr, layer, lng, lnb, *, batch, is_ctx):
    rows = h1.shape[0]
    nblk, _, tm = pos.shape
    nb = (rows // batch) // tm if not is_ctx else 1
    row_fn = (lambda i: batch) if is_ctx else (lambda i: i // nb)
    rowblk = lambda w: pl.BlockSpec((tm, w), lambda i: (i, 0))
    full = lambda shape: pl.BlockSpec(shape, lambda i: (0,) * len(shape))
    return pl.pallas_call(
        _combine_kernel,
        grid=(nblk,),
        in_specs=[pl.BlockSpec((None, 2, tm), lambda i: (i, 0, 0), memory_space=pltpu.SMEM),
                  pl.BlockSpec(memory_space=pl.ANY), rowblk(LANES), rowblk(D_MODEL),
                  _mod_spec(layer, 5, row_fn), full((1, D_MODEL)), full((1, D_MODEL))],
        out_specs=rowblk(D_MODEL),
        out_shape=jax.ShapeDtypeStruct((rows, D_MODEL), F32),
        scratch_shapes=[pltpu.VMEM((2, tm, D_MODEL), F32), pltpu.SemaphoreType.DMA(())],
        name="moe_combine",
        compiler_params=_cparams("arbitrary"),
    )(pos, ys, gate, h1, modr, lng, lnb)


def _moe_call(f_packed, h1, modr, layer, wr_t, b_r, tri, w_gate, w_up, w_down, moe_layer, lng, lnb,
              *, batch, is_ctx):
    rows = h1.shape[0]
    n_tiles = (2 * rows) // MOE_TILE + N_EXPERTS
    n_passes = n_tiles // MOE_PASS_TILES + N_EXPERTS
    route, gate, cnt = _router_call(h1, modr, layer, wr_t, b_r, tri, batch=batch, is_ctx=is_ctx)
    pos, src, *passes = _route_plan(route, cnt[:, :, 0], n_tiles, n_passes)
    xs = _dispatch_call(src, f_packed)
    ys = _experts_call(*passes, xs, w_gate, w_up, w_down, moe_layer)
    return _combine_call(pos, ys, gate, h1, modr, layer, lng, lnb, batch=batch, is_ctx=is_ctx)


def _rope_tables(seq):
    pos = jnp.arange(seq, dtype=jnp.int32)
    row = (pos // GRID_W).astype(F32)
    col = (pos % GRID_W).astype(F32)
    n_freq = HEAD_DIM // 4
    inv_freq = ROPE_THETA ** (-jnp.arange(n_freq, dtype=F32) / n_freq)
    ang = jnp.concatenate([row[:, None] * inv_freq, col[:, None] * inv_freq], axis=-1)
    cos, sin = jnp.cos(ang), jnp.sin(ang)
    reps = LANES // (HEAD_DIM // 2)
    sign = jnp.tile(jnp.concatenate([-jnp.ones(HEAD_DIM // 2, F32), jnp.ones(HEAD_DIM // 2, F32)]),
                    LANES // HEAD_DIM)
    return jnp.tile(cos, (1, reps)), jnp.tile(sin, (1, reps)) * sign


def kernel(x, c, ctx, c_ctx, ada_w, ada_b, w_in, w_out, na_rpb, ga_q_norm, ga_k_norm,
           ssm_lambda_re, ssm_lambda_im, ssm_log_step, ssm_b_re, ssm_b_im, ssm_c_re, ssm_c_im,
           ssm_d, ssm_w_glu, sw_sink, ln1_g, ln1_b, ln2_g, ln2_b,
           ffn_w_gate, ffn_w_up, ffn_w_down,
           moe_w_router, moe_b_router, moe_w_gate, moe_w_up, moe_w_down):
    batch, seq, _ = x.shape
    n_ctx = ctx.shape[1]
    assert batch < MOD_ROWS and seq % 512 == 0 and n_ctx % 256 == 0

    c_all = jnp.zeros((MOD_ROWS, D_MODEL), F32).at[:batch].set(c).at[batch].set(c_ctx)
    modr = _mods_call(c_all, ada_w, ada_b)

    cos_t, sin_t = _rope_tables(seq)
    seg = jnp.kron(jnp.eye(256 // HEAD_DIM, dtype=F32), jnp.ones((HEAD_DIM, HEAD_DIM), F32)).astype(BF16)
    tri = (jnp.arange(MOE_TM)[:, None] < jnp.arange(MOE_TM)[None, :]).astype(BF16)
    sw_mask = _window_mask_table(seq)

    h = x.reshape(batch * seq, D_MODEL)
    hc = ctx.reshape(batch * n_ctx, D_MODEL)
    names = ("qa", "ka", "va", "qb", "kb", "vb", "u", "qd", "kd", "vd")

    for layer in range(DEPTH):
        need_ctx = layer < DEPTH - 1
        w_in_l = w_in[layer].astype(BF16)
        w_out_l = w_out[layer].astype(BF16)
        gq = jnp.tile(ga_q_norm[layer].astype(F32), 256 // HEAD_DIM)[None, :]
        gk = jnp.tile(ga_k_norm[layer].astype(F32), 128 // HEAD_DIM)[None, :]
        sink = sw_sink[layer].astype(F32)
        ln1 = (ln1_g[layer][None, :], ln1_b[layer][None, :])
        ln2 = (ln2_g[layer][None, :], ln2_b[layer][None, :])

        lat = dict(zip(names, _in_proj_call(h, modr, layer, w_in_l, cos_t, sin_t, seg, gq, gk,
                                            batch=batch, is_ctx=False)))
        cx = dict(zip(names, _in_proj_call(hc, modr, layer, w_in_l, cos_t, sin_t, seg, gq, gk,
                                           batch=batch, is_ctx=True)))

        na_bias, na_mask = _na_tables(na_rpb[layer], seq)
        out_a = _na_call(lat["qa"], lat["ka"], lat["va"], cx["ka"], cx["va"], na_bias, na_mask, batch=batch)
        out_b = _global_call(lat["qb"], lat["kb"], lat["vb"], cx["kb"], cx["vb"], batch=batch)
        out_d = _window_call(sink, lat["qd"], lat["kd"], lat["vd"], cx["kd"], cx["vd"], sw_mask, batch=batch)

        dirs = [_s5_operands(ssm_lambda_re[layer, d], ssm_lambda_im[layer, d], ssm_log_step[layer, d],
                             ssm_b_re[layer, d], ssm_b_im[layer, d], ssm_c_re[layer, d], ssm_c_im[layer, d])
                for d in range(2)]
        fwd = _s5_pass(cx["u"], lat["u"], None, *dirs[0], None, None, batch=batch, reverse=False)
        ctx_c, out_c = _s5_pass(cx["u"], lat["u"], fwd, *dirs[1], ssm_d[layer].astype(F32)[None, :],
                                ssm_w_glu[layer].astype(BF16), batch=batch, reverse=True)

        routed = layer % 2 == 1
        h1, f_in = _out_proj_call(out_a, out_b, out_c, out_d, w_out_l, h, modr, layer, *ln1,
                                  batch=batch, is_ctx=False, packed=routed)
        streams = [(h1, f_in, False)]
        if need_ctx:
            ctx_a, ctx_b, ctx_d = _ctx_attn_call(sink, cx, batch=batch)
            hc1, fc_in = _out_proj_call(ctx_a, ctx_b, ctx_c, ctx_d, w_out_l, hc, modr, layer, *ln1,
                                        batch=batch, is_ctx=True, packed=routed)
            streams.append((hc1, fc_in, True))

        i = layer // 2
        outs = []
        if not routed:
            wg, wu, wd = (ffn_w_gate[i].astype(BF16), ffn_w_up[i].astype(BF16), ffn_w_down[i].astype(BF16))
            for s1, sf, is_ctx in streams:
                outs.append(_ffn_call(sf, wg, wu, wd, s1, modr, layer, *ln2, batch=batch, is_ctx=is_ctx))
        else:
            wr_t = moe_w_router[i].astype(F32).T
            b_r = moe_b_router[i].astype(F32)[:, None]
            for s1, sf, is_ctx in streams:
                outs.append(_moe_call(sf, s1, modr, layer, wr_t, b_r, tri, moe_w_gate, moe_w_up, moe_w_down, i,
                                      *ln2, batch=batch, is_ctx=is_ctx))
        h = outs[0]
        if need_ctx:
            hc = outs[1]
    return h.reshape(batch, seq, D_MODEL)
```

```python
import functools

import jax
import jax.numpy as jnp
from jax import lax
from jax.experimental import pallas as pl
from jax.experimental.pallas import tpu as pltpu

F32 = jnp.float32
BF16 = jnp.bfloat16

D_MODEL = 1024
DEPTH = 4
GRID_W = 64
HEAD_DIM = 64
NA_ROWS = 8
NA_COLS = 16
SSM_GROUPS = 16
SSM_GROUP_CH = 16
SSM_STATE = 64
SSM_WIDTH = SSM_GROUPS * SSM_GROUP_CH
SSM_FLAT = SSM_GROUPS * SSM_STATE
SW_WINDOW = 128
N_EXPERTS = 8
ADA_CHUNKS = 6
ROPE_THETA = 10000.0
LN_EPS = 1e-6
RMS_EPS = 1e-6
NEG_INF = -1e30
DEEPNORM_ALPHA = (2 * DEPTH) ** 0.25
ATT_SCALE = HEAD_DIM ** -0.5

C_QA, C_KA, C_VA, C_QB, C_KB, C_VB, C_U, C_QD, C_KD, C_VD, IN_WIDTH = (
    0, 256, 512, 768, 1024, 1152, 1280, 1536, 1792, 1920, 2048)

LANES = 128
SUBLANES = 8
MOD_ROWS = 16
VMEM_LIMIT = 56 * 1024 * 1024


def _cparams(*sem):
    return pltpu.CompilerParams(dimension_semantics=sem, vmem_limit_bytes=VMEM_LIMIT)


def _dot(a, b):
    return jnp.dot(a, b, preferred_element_type=F32)


def _dot_nt(a, b):
    return lax.dot_general(a, b, (((1,), (1,)), ((), ())), preferred_element_type=F32)


def _dot_tn(a, b):
    return lax.dot_general(a, b, (((0,), (0,)), ((), ())), preferred_element_type=F32)


def _split_bf16(x):
    hi = x.astype(BF16)
    lo = (x - hi.astype(F32)).astype(BF16)
    return hi, lo


def _layer_norm(z, g, b):
    zc = z - jnp.mean(z, axis=-1, keepdims=True)
    y = zc * lax.rsqrt(jnp.mean(zc * zc, axis=-1, keepdims=True) + LN_EPS)
    return y * g + b


def _mods_kernel(c_ref, w_ref, b_ref, o_ref):
    c = c_ref[...]
    act = (c * jax.nn.sigmoid(c)).astype(BF16)
    o_ref[...] = _dot(act, w_ref[...].astype(BF16)) + b_ref[...]


def _mods_call(c_all, ada_w, ada_b):
    tn = 1536
    n = ADA_CHUNKS * D_MODEL
    out = pl.pallas_call(
        _mods_kernel,
        grid=(DEPTH, n // tn),
        in_specs=[
            pl.BlockSpec((MOD_ROWS, D_MODEL), lambda l, j: (0, 0)),
            pl.BlockSpec((None, D_MODEL, tn), lambda l, j: (l, 0, j)),
            pl.BlockSpec((None, 1, tn), lambda l, j: (l, 0, j)),
        ],
        out_specs=pl.BlockSpec((None, MOD_ROWS, tn), lambda l, j: (l, 0, j)),
        out_shape=jax.ShapeDtypeStruct((DEPTH, MOD_ROWS, n), F32),
        name="mods",
        compiler_params=_cparams("arbitrary", "arbitrary"),
    )(c_all, ada_w, ada_b.reshape(DEPTH, 1, n))
    return out.reshape(DEPTH * MOD_ROWS * ADA_CHUNKS, 1, D_MODEL)


def _mod_spec(layer, chunk, row_fn):
    def index(i, *_):
        return ((layer * MOD_ROWS + row_fn(i)) * ADA_CHUNKS + chunk, 0, 0)
    return pl.BlockSpec((None, 1, D_MODEL), index)


def _seg_rms(x, seg, g):
    hi, lo = _split_bf16(x * x)
    ss = _dot(hi, seg) + _dot(lo, seg)
    return x * lax.rsqrt(ss * (1.0 / HEAD_DIM) + RMS_EPS) * g


def _rope(x, cos, sin_signed, first_half):
    outs = []
    for j in range(x.shape[1] // LANES):
        xs = x[:, j * LANES:(j + 1) * LANES]
        partner = jnp.where(first_half,
                            pltpu.roll(xs, LANES - HEAD_DIM // 2, 1),
                            pltpu.roll(xs, HEAD_DIM // 2, 1))
        outs.append(xs * cos + partner * sin_signed)
    return outs[0] if len(outs) == 1 else jnp.concatenate(outs, axis=1)


def _pad_heads(x, lane, ones_lane=False):
    low = lane < HEAD_DIM
    fill = jnp.where(lane == HEAD_DIM, 1.0, 0.0) if ones_lane else 0.0
    outs = []
    for j in range(x.shape[1] // LANES):
        xs = x[:, j * LANES:(j + 1) * LANES]
        outs.append(jnp.where(low, xs, fill))
        outs.append(jnp.where(low, pltpu.roll(xs, HEAD_DIM, 1), fill))
    return jnp.concatenate(outs, axis=1).astype(BF16)


def _in_proj_kernel(x_ref, sh_ref, sc_ref, w_ref, cos_ref, sin_ref, seg_ref, gq_ref, gk_ref,
                    qa_ref, ka_ref, va_ref, qb_ref, kb_ref, vb_ref, u_ref, qd_ref, kd_ref, vd_ref,
                    *, rope):
    a = (x_ref[...] * (1.0 + sc_ref[...]) + sh_ref[...]).astype(BF16)
    lane = lax.broadcasted_iota(jnp.int32, (a.shape[0], LANES), 1)

    def proj(c0, c1):
        return _dot(a, w_ref[:, c0:c1])

    qa_ref[...] = _pad_heads(proj(C_QA, C_KA) * ATT_SCALE, lane)
    ka_ref[...] = _pad_heads(proj(C_KA, C_VA), lane)
    va_ref[...] = _pad_heads(proj(C_VA, C_QB), lane, ones_lane=True)
    vb_ref[...] = _pad_heads(proj(C_VB, C_U), lane, ones_lane=True)
    u_ref[...] = proj(C_U, C_QD)
    vd_ref[...] = _pad_heads(proj(C_VD, IN_WIDTH), lane, ones_lane=True)

    seg = seg_ref[...]
    qb = _seg_rms(proj(C_QB, C_KB), seg, gq_ref[...])
    kb = _seg_rms(proj(C_KB, C_VB), seg[:LANES, :LANES], gk_ref[...])
    qd = proj(C_QD, C_KD)
    kd = proj(C_KD, C_VD)
    if rope:
        cos = cos_ref[...]
        sin = sin_ref[...]
        first_half = (lane % HEAD_DIM) < (HEAD_DIM // 2)
        qb = _rope(qb, cos, sin, first_half)
        kb = _rope(kb, cos, sin, first_half)
        qd = _rope(qd, cos, sin, first_half)
        kd = _rope(kd, cos, sin, first_half)
    qb_ref[...] = _pad_heads(qb * ATT_SCALE, lane)
    kb_ref[...] = _pad_heads(kb, lane)
    qd_ref[...] = _pad_heads(qd * ATT_SCALE, lane)
    kd_ref[...] = _pad_heads(kd, lane)


def _in_proj_call(x, modr, layer, w_in, cos_t, sin_t, seg, gq, gk, *, batch, is_ctx):
    rows = x.shape[0]
    per_batch = rows // batch
    tm = 256 if is_ctx else 512
    nb = per_batch // tm
    row_fn = (lambda i: batch) if is_ctx else (lambda i: i // nb)
    table_spec = pl.BlockSpec((tm, LANES), (lambda i: (0, 0)) if is_ctx else (lambda i: (i % nb, 0)))
    full = lambda shape: pl.BlockSpec(shape, lambda i: (0,) * len(shape))
    rowblk = lambda w: pl.BlockSpec((tm, w), lambda i: (i, 0))
    wide = lambda w, dt: jax.ShapeDtypeStruct((rows, w), dt)
    out_shapes = [wide(512, BF16), wide(512, BF16), wide(512, BF16),
                  wide(512, BF16), wide(256, BF16), wide(256, BF16),
                  jax.ShapeDtypeStruct((per_batch, batch * SSM_WIDTH), F32),
                  wide(512, BF16), wide(256, BF16), wide(256, BF16)]
    out_specs = [rowblk(512), rowblk(512), rowblk(512), rowblk(512), rowblk(256), rowblk(256),
                 pl.BlockSpec((tm, SSM_WIDTH), lambda i: (i % nb, i // nb)),
                 rowblk(512), rowblk(256), rowblk(256)]
    return pl.pallas_call(
        functools.partial(_in_proj_kernel, rope=not is_ctx),
        grid=(rows // tm,),
        in_specs=[rowblk(D_MODEL), _mod_spec(layer, 0, row_fn), _mod_spec(layer, 1, row_fn),
                  full((D_MODEL, IN_WIDTH)), table_spec, table_spec, full((256, 256)),
                  full((1, 256)), full((1, 128))],
        out_specs=out_specs,
        out_shape=out_shapes,
        name="in_proj_ctx" if is_ctx else "in_proj",
        compiler_params=_cparams("arbitrary"),
    )(x, modr, modr, w_in, cos_t, sin_t, seg, gq, gk)


KV_CHUNK = 256


def _tile(h):
    return slice(h * LANES, (h + 1) * LANES)


def _attend(q, parts, sink=None):
    def score(k_fn, mod_fn, c):
        s = _dot_nt(q, k_fn(c))
        return s if mod_fn is None else mod_fn(s, c)

    tiles = [(k_fn, v_fn, mod_fn, c) for k_fn, v_fn, n, mod_fn in parts for c in range(0, n, KV_CHUNK)]
    m = sink
    acc = None
    for k_fn, v_fn, mod_fn, c in tiles:
        s = score(k_fn, mod_fn, c)
        m_tile = jnp.max(s, axis=-1, keepdims=True)
        m_new = m_tile if m is None else jnp.maximum(m, m_tile)
        pv = _dot(jnp.exp(s - m_new).astype(BF16), v_fn(c))
        acc = pv if acc is None else acc * jnp.exp(m - m_new) + pv
        m = m_new
    den = acc[:, HEAD_DIM:HEAD_DIM + 1]
    if sink is not None:
        den = den + jnp.exp(sink - m)
    return acc * (1.0 / den)


def _pack_pair(a, b):
    lane = lax.broadcasted_iota(jnp.int32, a.shape, 1)
    return jnp.where(lane < HEAD_DIM, a, pltpu.roll(b, HEAD_DIM, 1)).astype(BF16)


def _ref_part(k_ref, v_ref, t, n_keys, start=0, mod_fn=None):
    return (lambda c: k_ref[pl.ds(start + c, KV_CHUNK), _tile(t)],
            lambda c: v_ref[pl.ds(start + c, KV_CHUNK), _tile(t)], n_keys, mod_fn)


def _gqa_group(q_ref, g, parts, sink_pair=None):
    tq = q_ref.shape[0]
    q2 = jnp.concatenate([q_ref[:, _tile(2 * g)], q_ref[:, _tile(2 * g + 1)]], axis=0)
    sink = None
    if sink_pair is not None:
        row = lax.broadcasted_iota(jnp.int32, (2 * tq, 1), 0)
        sink = jnp.where(row < tq, sink_pair[0], sink_pair[1])
    o2 = _attend(q2, parts, sink)
    return _pack_pair(o2[:tq], o2[tq:])


NA_QROWS = 4
NA_BAND = 12


def _na_band_start(j, rows):
    return jnp.clip(j * NA_QROWS - NA_ROWS // 2, 0, rows - NA_BAND)


def _na_kernel(q_ref, k_ref, v_ref, kc_ref, vc_ref, bias_ref, mask_ref, o_ref):
    j = pl.program_id(1)
    rows = k_ref.shape[0] // GRID_W
    start = pl.multiple_of(_na_band_start(j, rows) * GRID_W, GRID_W)
    outs = []
    for h in range(4):
        def local_scores(s, c, h=h):
            cols = slice(c, c + KV_CHUNK)
            return jnp.where(mask_ref[:, cols] > 0.5, s + bias_ref[h, :, cols], NEG_INF)

        parts = [_ref_part(k_ref, v_ref, h, NA_BAND * GRID_W, start, local_scores),
                 _ref_part(kc_ref, vc_ref, h, kc_ref.shape[0])]
        outs.append(_attend(q_ref[:, _tile(h)], parts))
    o_ref[:, _tile(0)] = _pack_pair(outs[0], outs[1])
    o_ref[:, _tile(1)] = _pack_pair(outs[2], outs[3])


def _na_call(qa, ka, va, ka_c, va_c, bias_tab, mask_tab, *, batch):
    seq = qa.shape[0] // batch
    n_ctx = ka_c.shape[0] // batch
    nj = seq // (NA_QROWS * GRID_W)
    tq = NA_QROWS * GRID_W
    nk = NA_BAND * GRID_W

    def pattern(j):
        return jnp.where(j == 0, 0, jnp.where(j == nj - 1, 2, 1))

    return pl.pallas_call(
        _na_kernel,
        grid=(batch, nj),
        in_specs=[
            pl.BlockSpec((tq, 512), lambda b, j: (b * nj + j, 0)),
            pl.BlockSpec((seq, 512), lambda b, j: (b, 0)),
            pl.BlockSpec((seq, 512), lambda b, j: (b, 0)),
            pl.BlockSpec((n_ctx, 512), lambda b, j: (b, 0)),
            pl.BlockSpec((n_ctx, 512), lambda b, j: (b, 0)),
            pl.BlockSpec((None, 4, tq, nk), lambda b, j: (pattern(j), 0, 0, 0)),
            pl.BlockSpec((None, tq, nk), lambda b, j: (pattern(j), 0, 0)),
        ],
        out_specs=pl.BlockSpec((tq, 256), lambda b, j: (b * nj + j, 0)),
        out_shape=jax.ShapeDtypeStruct((qa.shape[0], 256), BF16),
        name="na_attn",
        compiler_params=_cparams("arbitrary", "arbitrary"),
    )(qa, ka, va, ka_c, va_c, bias_tab, mask_tab)


def _na_tables(rpb, seq):
    rows = seq // GRID_W
    nj = rows // NA_QROWS
    c_idx = jnp.arange(GRID_W)
    col_start = jnp.clip(c_idx - NA_COLS // 2, 0, GRID_W - NA_COLS)
    col_valid = (c_idx[None, :] >= col_start[:, None]) & (c_idx[None, :] < col_start[:, None] + NA_COLS)
    col_rel = jnp.clip(c_idx[None, :] - c_idx[:, None], 1 - NA_COLS, NA_COLS - 1) + NA_COLS - 1
    blocks = jnp.array([0, 1, nj - 1])
    q_row = blocks[:, None] * NA_QROWS + jnp.arange(NA_QROWS)[None, :]
    k_row = _na_band_start(blocks, rows)[:, None] + jnp.arange(NA_BAND)[None, :]
    row_start = jnp.clip(q_row - NA_ROWS // 2, 0, rows - NA_ROWS)
    row_valid = ((k_row[:, None, :] >= row_start[:, :, None])
                 & (k_row[:, None, :] < row_start[:, :, None] + NA_ROWS))
    row_rel = jnp.clip(k_row[:, None, :] - q_row[:, :, None] + NA_ROWS - 1, 0, 2 * NA_ROWS - 2)
    row_sel = jax.nn.one_hot(row_rel, 2 * NA_ROWS - 1, dtype=F32)
    col_sel = jax.nn.one_hot(col_rel, 2 * NA_COLS - 1, dtype=F32)
    exact = lax.Precision.HIGHEST
    by_col = jnp.einsum("hab,qkb->haqk", rpb.astype(F32), col_sel, precision=exact)
    bias = jnp.einsum("pria,haqk->phrqik", row_sel, by_col, precision=exact)
    valid = row_valid[:, :, None, :, None] & col_valid[None, None, :, None, :]
    tq, nk = NA_QROWS * GRID_W, NA_BAND * GRID_W
    return bias.reshape(3, 4, tq, nk), valid.astype(F32).reshape(3, tq, nk)


def _global_kernel(q_ref, k_ref, v_ref, kc_ref, vc_ref, o_ref):
    for g in range(2):
        parts = [_ref_part(kc_ref, vc_ref, g, kc_ref.shape[0]), _ref_part(k_ref, v_ref, g, k_ref.shape[0])]
        o_ref[:, _tile(g)] = _gqa_group(q_ref, g, parts)


def _global_call(qb, kb, vb, kb_c, vb_c, *, batch):
    seq = qb.shape[0] // batch
    n_ctx = kb_c.shape[0] // batch
    tq = 256
    nq = seq // tq
    return pl.pallas_call(
        _global_kernel,
        grid=(batch, nq),
        in_specs=[
            pl.BlockSpec((tq, 512), lambda b, i: (b * nq + i, 0)),
            pl.BlockSpec((seq, 256), lambda b, i: (b, 0)),
            pl.BlockSpec((seq, 256), lambda b, i: (b, 0)),
            pl.BlockSpec((n_ctx, 256), lambda b, i: (b, 0)),
            pl.BlockSpec((n_ctx, 256), lambda b, i: (b, 0)),
        ],
        out_specs=pl.BlockSpec((tq, 256), lambda b, i: (b * nq + i, 0)),
        out_shape=jax.ShapeDtypeStruct((qb.shape[0], 256), BF16),
        name="global_attn",
        compiler_params=_cparams("arbitrary", "arbitrary"),
    )(qb, kb, vb, kb_c, vb_c)


SW_TQ = 256
SW_SPAN = SW_TQ + 2 * SW_WINDOW


def _window_start(i, seq):
    return jnp.clip(i * SW_TQ - SW_WINDOW, 0, seq - SW_SPAN)


def _window_kernel(sink_ref, q_ref, k_ref, v_ref, kc_ref, vc_ref, mask_ref, o_ref):
    i = pl.program_id(1)
    start = pl.multiple_of(_window_start(i, k_ref.shape[0]), LANES)

    def in_window(s, c):
        return jnp.where(mask_ref[:, c:c + KV_CHUNK] > 0.5, s, NEG_INF)

    for g in range(2):
        parts = [_ref_part(k_ref, v_ref, g, SW_SPAN, start, in_window),
                 _ref_part(kc_ref, vc_ref, g, kc_ref.shape[0])]
        o_ref[:, _tile(g)] = _gqa_group(q_ref, g, parts, sink_pair=(sink_ref[2 * g], sink_ref[2 * g + 1]))


def _window_mask_table(seq):
    nq = seq // SW_TQ
    blocks = jnp.array([0, 1, nq - 1])
    qpos = blocks[:, None] * SW_TQ + jnp.arange(SW_TQ)[None, :]
    kpos = _window_start(blocks, seq)[:, None] + jnp.arange(SW_SPAN)[None, :]
    valid = jnp.abs(kpos[:, None, :] - qpos[:, :, None]) <= SW_WINDOW
    return jnp.tile(valid.astype(F32), (1, 2, 1))


def _window_call(sink, qd, kd, vd, kd_c, vd_c, mask_tab, *, batch):
    seq = qd.shape[0] // batch
    n_ctx = kd_c.shape[0] // batch
    nq = seq // SW_TQ

    def pattern(i):
        return jnp.where(i == 0, 0, jnp.where(i == nq - 1, 2, 1))

    return pl.pallas_call(
        _window_kernel,
        grid=(batch, nq),
        in_specs=[
            pl.BlockSpec(memory_space=pltpu.SMEM),
            pl.BlockSpec((SW_TQ, 512), lambda b, i: (b * nq + i, 0)),
            pl.BlockSpec((seq, 256), lambda b, i: (b, 0)),
            pl.BlockSpec((seq, 256), lambda b, i: (b, 0)),
            pl.BlockSpec((n_ctx, 256), lambda b, i: (b, 0)),
            pl.BlockSpec((n_ctx, 256), lambda b, i: (b, 0)),
            pl.BlockSpec((None, 2 * SW_TQ, SW_SPAN), lambda b, i: (pattern(i), 0, 0)),
        ],
        out_specs=pl.BlockSpec((SW_TQ, 256), lambda b, i: (b * nq + i, 0)),
        out_shape=jax.ShapeDtypeStruct((qd.shape[0], 256), BF16),
        name="window_attn",
        compiler_params=_cparams("arbitrary", "arbitrary"),
    )(sink, qd, kd, vd, kd_c, vd_c, mask_tab)


def _ctx_attn_kernel(sink_ref, qa_ref, ka_ref, va_ref, qb_ref, kb_ref, vb_ref, qd_ref, kd_ref, vd_ref,
                     oa_ref, ob_ref, od_ref):
    n = ka_ref.shape[0]
    outs = [_attend(qa_ref[:, _tile(h)], [_ref_part(ka_ref, va_ref, h, n)]) for h in range(4)]
    oa_ref[:, _tile(0)] = _pack_pair(outs[0], outs[1])
    oa_ref[:, _tile(1)] = _pack_pair(outs[2], outs[3])
    for g in range(2):
        ob_ref[:, _tile(g)] = _gqa_group(qb_ref, g, [_ref_part(kb_ref, vb_ref, g, n)])
        od_ref[:, _tile(g)] = _gqa_group(qd_ref, g, [_ref_part(kd_ref, vd_ref, g, n)],
                                         sink_pair=(sink_ref[2 * g], sink_ref[2 * g + 1]))


def _ctx_attn_call(sink, cx, *, batch):
    n_ctx = cx["qa"].shape[0] // batch
    blk = lambda w: pl.BlockSpec((n_ctx, w), lambda b: (b, 0))
    names = ("qa", "ka", "va", "qb", "kb", "vb", "qd", "kd", "vd")
    widths = (512, 512, 512, 512, 256, 256, 512, 256, 256)
    shape = jax.ShapeDtypeStruct((cx["qa"].shape[0], 256), BF16)
    return pl.pallas_call(
        _ctx_attn_kernel,
        grid=(batch,),
        in_specs=[pl.BlockSpec(memory_space=pltpu.SMEM)] + [blk(w) for w in widths],
        out_specs=[blk(256)] * 3,
        out_shape=[shape] * 3,
        name="ctx_attn",
        compiler_params=_cparams("arbitrary"),
    )(sink, *[cx[n] for n in names])


def _s5_discretise(lam_re, lam_im, log_step, b_re, b_im):
    step = jnp.exp(log_step.astype(F32))[:, None]
    lam_re = lam_re.astype(F32)
    lam_im = lam_im.astype(F32)
    mag = jnp.exp(lam_re * step)
    ab_re = mag * jnp.cos(lam_im * step)
    ab_im = mag * jnp.sin(lam_im * step)
    den = lam_re * lam_re + lam_im * lam_im
    num_re = ab_re - 1.0
    f_re = ((num_re * lam_re + ab_im * lam_im) / den)[..., None]
    f_im = ((ab_im * lam_re - num_re * lam_im) / den)[..., None]
    b_re = b_re.astype(F32)
    b_im = b_im.astype(F32)
    return ab_re, ab_im, f_re * b_re - f_im * b_im, f_re * b_im + f_im * b_re


def _s5_operands(lam_re, lam_im, log_step, b_re, b_im, c_re, c_im):
    ab_re, ab_im, bb_re, bb_im = _s5_discretise(lam_re, lam_im, log_step, b_re, b_im)
    eye = jnp.eye(SSM_GROUPS, dtype=F32)
    bd_in = lambda m: jnp.einsum("gph,gk->ghkp", m, eye).reshape(SSM_WIDTH, SSM_FLAT)
    bd_out = lambda m: jnp.einsum("ghp,gk->kpgh", m.astype(F32), eye).reshape(SSM_FLAT, SSM_WIDTH)
    a = jnp.stack([ab_re.reshape(SSM_FLAT), ab_im.reshape(SSM_FLAT)])
    bbd = jnp.concatenate([bd_in(bb_re), bd_in(bb_im)], axis=1).astype(BF16)
    cbd = jnp.concatenate([bd_out(c_re), -bd_out(c_im)], axis=0).astype(BF16)
    return a, bbd, cbd


def _s5_kernel(*refs, reverse, final, batch, n_ctx_chunks):
    if final:
        (uc_ref, ul_ref, pc_ref, pl_ref, a_ref, bbd_ref, cbd_ref, d_ref, wglu_ref,
         oc_ref, ol_ref, bu_ref, h_ref) = refs
    else:
        uc_ref, ul_ref, a_ref, bbd_ref, cbd_ref, oc_ref, ol_ref, bu_ref, h_ref = refs
        pc_ref = pl_ref = d_ref = wglu_ref = None
    c = pl.program_id(0)
    steps = bu_ref.shape[0] // batch

    @pl.when(c == 0)
    def _():
        h_ref[...] = jnp.zeros_like(h_ref)

    def run(u_ref, prev_ref, o_ref):
        u = u_ref[...]
        bu_ref[...] = _dot(u.astype(BF16), bbd_ref[...])
        ar = jnp.broadcast_to(a_ref[0:1, :], (batch, SSM_FLAT))
        ai = jnp.broadcast_to(a_ref[1:2, :], (batch, SSM_FLAT))

        def step(i, h):
            t = steps - 1 - i if reverse else i
            row = pl.multiple_of(t * batch, batch)
            b = bu_ref[pl.ds(row, batch), :]
            hr, hi = h[:, :SSM_FLAT], h[:, SSM_FLAT:]
            nr = ar * hr - ai * hi + b[:, :SSM_FLAT]
            ni = ar * hi + ai * hr + b[:, SSM_FLAT:]
            hn = jnp.concatenate([nr, ni], axis=1)
            bu_ref[pl.ds(row, batch), :] = hn
            return hn

        h_ref[...] = lax.fori_loop(0, steps, step, h_ref[...], unroll=4)
        y = _dot(bu_ref[...].astype(BF16), cbd_ref[...])
        if not final:
            o_ref[...] = y
        else:
            y = jax.nn.gelu(y + prev_ref[...] + d_ref[...] * u)
            z = _dot(y.astype(BF16), wglu_ref[...])
            o_ref[...] = (y * jax.nn.sigmoid(z)).astype(BF16)

    @pl.when(c < n_ctx_chunks)
    def _():
        run(uc_ref, pc_ref, oc_ref)

    @pl.when(c >= n_ctx_chunks)
    def _():
        run(ul_ref, pl_ref, ol_ref)


def _s5_pass(u_ctx, u_lat, prev, a, bbd, cbd, d_skip, w_glu, *, batch, reverse):
    final = prev is not None
    t_chunk = 128
    rows = t_chunk * batch
    ncc = u_ctx.shape[0] // t_chunk
    nlc = u_lat.shape[0] // t_chunk
    flat = lambda x: x.reshape(x.shape[0] * batch, SSM_WIDTH)
    if reverse:
        ctx_idx = lambda c: (jnp.maximum(ncc - 1 - c, 0), 0)
        lat_idx = lambda c: (nlc - 1 - jnp.maximum(c - ncc, 0), 0)
    else:
        ctx_idx = lambda c: (jnp.minimum(c, ncc - 1), 0)
        lat_idx = lambda c: (jnp.maximum(c - ncc, 0), 0)
    full = lambda shape: pl.BlockSpec(shape, lambda c: (0,) * len(shape))
    u_specs = [pl.BlockSpec((rows, SSM_WIDTH), ctx_idx), pl.BlockSpec((rows, SSM_WIDTH), lat_idx)]
    par_specs = [full((2, SSM_FLAT)), full((SSM_WIDTH, 2 * SSM_FLAT)), full((2 * SSM_FLAT, SSM_WIDTH))]
    args = [flat(u_ctx), flat(u_lat)]
    in_specs = list(u_specs)
    if final:
        args += [flat(prev[0]), flat(prev[1])]
        in_specs += u_specs
    args += [a, bbd, cbd]
    in_specs += par_specs
    if final:
        args += [d_skip, w_glu]
        in_specs += [full((1, SSM_WIDTH)), full((SSM_WIDTH, SSM_WIDTH))]
    dt = BF16 if final else F32
    oc, ol = pl.pallas_call(
        functools.partial(_s5_kernel, reverse=reverse, final=final, batch=batch, n_ctx_chunks=ncc),
        grid=(ncc + nlc,),
        in_specs=in_specs,
        out_specs=u_specs,
        out_shape=[jax.ShapeDtypeStruct((u_ctx.shape[0] * batch, SSM_WIDTH), dt),
                   jax.ShapeDtypeStruct((u_lat.shape[0] * batch, SSM_WIDTH), dt)],
        scratch_shapes=[pltpu.VMEM((rows, 2 * SSM_FLAT), F32), pltpu.VMEM((batch, 2 * SSM_FLAT), F32)],
        name="s5_bwd_glu" if final else "s5_fwd",
        compiler_params=_cparams("arbitrary"),
    )(*args)
    return oc.reshape(u_ctx.shape), ol.reshape(u_lat.shape)


def _bf16_bits(x):
    return lax.bitcast_convert_type(x.astype(BF16).astype(F32), jnp.uint32)


def _pack_bf16_pair(a, b):
    return _bf16_bits(a) | lax.shift_right_logical(_bf16_bits(b), jnp.uint32(16))


def _unpack_bf16_pair(u):
    a = lax.bitcast_convert_type(u & jnp.uint32(0xFFFF0000), F32)
    b = lax.bitcast_convert_type(lax.shift_left(u, jnp.uint32(16)), F32)
    return a.astype(BF16), b.astype(BF16)


def _out_proj_kernel(oa_ref, ob_ref, oc_ref, od_ref, w_ref, x_ref, g1_ref, sh2_ref, sc2_ref,
                     lng_ref, lnb_ref, h_ref, f_ref, *, packed):
    y = (_dot(oa_ref[...], w_ref[0:256, :]) + _dot(ob_ref[...], w_ref[256:512, :])
         + _dot(oc_ref[...], w_ref[512:768, :]) + _dot(od_ref[...], w_ref[768:1024, :]))
    h1 = _layer_norm(DEEPNORM_ALPHA * x_ref[...] + g1_ref[...] * y, lng_ref[...], lnb_ref[...])
    h_ref[...] = h1
    f = h1 * (1.0 + sc2_ref[...]) + sh2_ref[...]
    if packed:
        f_ref[...] = _pack_bf16_pair(f[:, :D_MODEL // 2], f[:, D_MODEL // 2:])
    else:
        f_ref[...] = f.astype(BF16)


def _out_proj_call(oa, ob, oc_tm, od, w_out, x, modr, layer, lng, lnb, *, batch, is_ctx, packed):
    rows = x.shape[0]
    per_batch = rows // batch
    tm = 256 if is_ctx else 512
    nb = per_batch // tm
    row_fn = (lambda i: batch) if is_ctx else (lambda i: i // nb)
    rowblk = lambda w: pl.BlockSpec((tm, w), lambda i: (i, 0))
    full = lambda shape: pl.BlockSpec(shape, lambda i: (0,) * len(shape))
    f_width, f_dtype = (D_MODEL // 2, jnp.uint32) if packed else (D_MODEL, BF16)
    return pl.pallas_call(
        functools.partial(_out_proj_kernel, packed=packed),
        grid=(rows // tm,),
        in_specs=[rowblk(256), rowblk(256),
                  pl.BlockSpec((tm, SSM_WIDTH), lambda i: (i % nb, i // nb)),
                  rowblk(256), full((D_MODEL, D_MODEL)), rowblk(D_MODEL),
                  _mod_spec(layer, 2, row_fn), _mod_spec(layer, 3, row_fn), _mod_spec(layer, 4, row_fn),
                  full((1, D_MODEL)), full((1, D_MODEL))],
        out_specs=[rowblk(D_MODEL), rowblk(f_width)],
        out_shape=[jax.ShapeDtypeStruct((rows, D_MODEL), F32), jax.ShapeDtypeStruct((rows, f_width), f_dtype)],
        name="out_proj_ctx" if is_ctx else "out_proj",
        compiler_params=_cparams("arbitrary"),
    )(oa, ob, oc_tm, od, w_out, x, modr, modr, modr, lng, lnb)


def _ffn_kernel(f_ref, wg_ref, wu_ref, wd_ref, h_ref, g2_ref, lng_ref, lnb_ref, o_ref, acc_ref):
    j = pl.program_id(1)

    @pl.when(j == 0)
    def _():
        acc_ref[...] = jnp.zeros_like(acc_ref)

    f = f_ref[...]
    g = _dot(f, wg_ref[...])
    u = _dot(f, wu_ref[...])
    acc_ref[...] += _dot((g * jax.nn.sigmoid(g) * u).astype(BF16), wd_ref[...])

    @pl.when(j == pl.num_programs(1) - 1)
    def _():
        o_ref[...] = _layer_norm(DEEPNORM_ALPHA * h_ref[...] + g2_ref[...] * acc_ref[...],
                                 lng_ref[...], lnb_ref[...])


def _ffn_call(f_in, wg, wu, wd, h1, modr, layer, lng, lnb, *, batch, is_ctx):
    rows = h1.shape[0]
    per_batch = rows // batch
    tm = 256 if is_ctx else 512
    d_ff = wg.shape[1]
    tf = d_ff // 2
    nb = per_batch // tm
    row_fn = (lambda i: batch) if is_ctx else (lambda i: i // nb)
    rowblk = lambda w: pl.BlockSpec((tm, w), lambda i, j: (i, 0))
    full = lambda shape: pl.BlockSpec(shape, lambda i, j: (0,) * len(shape))
    return pl.pallas_call(
        _ffn_kernel,
        grid=(rows // tm, d_ff // tf),
        in_specs=[rowblk(D_MODEL),
                  pl.BlockSpec((D_MODEL, tf), lambda i, j: (0, j)),
                  pl.BlockSpec((D_MODEL, tf), lambda i, j: (0, j)),
                  pl.BlockSpec((tf, D_MODEL), lambda i, j: (j, 0)),
                  rowblk(D_MODEL), _mod_spec(layer, 5, row_fn), full((1, D_MODEL)), full((1, D_MODEL))],
        out_specs=rowblk(D_MODEL),
        out_shape=jax.ShapeDtypeStruct((rows, D_MODEL), F32),
        scratch_shapes=[pltpu.VMEM((tm, D_MODEL), F32)],
        name="ffn_ctx" if is_ctx else "ffn",
        compiler_params=_cparams("arbitrary", "arbitrary"),
    )(f_in, wg, wu, wd, h1, modr, lng, lnb)


MOE_TM = 1024
MOE_TILE = 512
MOE_PASS_TILES = 10
MOE_FF_CHUNKS = 7
ROUTE_ROWS = 8


def _router_kernel(h_ref, sh2_ref, sc2_ref, wr_ref, br_ref, tri_ref, route_ref, gate_ref, cnt_ref):
    tm = h_ref.shape[0]
    f = h_ref[...] * (1.0 + sc2_ref[...]) + sh2_ref[...]
    f_hi, f_lo = _split_bf16(f)
    w_hi, w_lo = _split_bf16(wr_ref[...])
    logits = _dot_nt(w_hi, f_hi) + _dot_nt(w_hi, f_lo) + _dot_nt(w_lo, f_hi) + br_ref[...]
    ie = lax.broadcasted_iota(jnp.int32, logits.shape, 0)
    m1 = jnp.max(logits, axis=0, keepdims=True)
    i1 = jnp.min(jnp.where(logits == m1, ie, N_EXPERTS), axis=0, keepdims=True)
    rest = jnp.where(ie == i1, -jnp.inf, logits)
    m2 = jnp.max(rest, axis=0, keepdims=True)
    i2 = jnp.min(jnp.where(rest == m2, ie, N_EXPERTS), axis=0, keepdims=True)
    e2 = jnp.exp(m2 - m1)
    den = 1.0 + e2
    sel = jnp.where((ie == i1) | (ie == i2), 1.0, 0.0)
    rank = _dot(sel.astype(BF16), tri_ref[...])
    r1 = jnp.sum(jnp.where(ie == i1, rank, 0.0), axis=0, keepdims=True).astype(jnp.int32)
    r2 = jnp.sum(jnp.where(ie == i2, rank, 0.0), axis=0, keepdims=True).astype(jnp.int32)
    row = lax.broadcasted_iota(jnp.int32, (ROUTE_ROWS, tm), 0)
    route_ref[...] = jnp.where(row == 0, i1, jnp.where(row == 1, i2, jnp.where(row == 2, r1,
                               jnp.where(row == 3, r2, 0))))
    lrow = lax.broadcasted_iota(jnp.int32, (LANES, tm), 0)
    gate_ref[...] = jnp.where(lrow == 0, 1.0 / den, jnp.where(lrow == 1, e2 / den, 0.0)).T
    cnt = jnp.sum(sel, axis=1, keepdims=True)
    cnt_ref[...] = jnp.broadcast_to(cnt, cnt_ref.shape).astype(jnp.int32)


def _router_call(h1, modr, layer, wr_t, b_r, tri, *, batch, is_ctx):
    rows = h1.shape[0]
    tm = MOE_TM
    nblk = rows // tm
    nb = (rows // batch) // tm if not is_ctx else 1
    row_fn = (lambda i: batch) if is_ctx else (lambda i: i // nb)
    full = lambda shape: pl.BlockSpec(shape, lambda i: (0,) * len(shape))
    return pl.pallas_call(
        _router_kernel,
        grid=(nblk,),
        in_specs=[pl.BlockSpec((tm, D_MODEL), lambda i: (i, 0)),
                  _mod_spec(layer, 3, row_fn), _mod_spec(layer, 4, row_fn),
                  full((N_EXPERTS, D_MODEL)), full((N_EXPERTS, 1)), full((tm, tm))],
        out_specs=[pl.BlockSpec((None, ROUTE_ROWS, tm), lambda i: (i, 0, 0)),
                   pl.BlockSpec((tm, LANES), lambda i: (i, 0)),
                   pl.BlockSpec((None, N_EXPERTS, LANES), lambda i: (i, 0, 0))],
        out_shape=[jax.ShapeDtypeStruct((nblk, ROUTE_ROWS, tm), jnp.int32),
                   jax.ShapeDtypeStruct((rows, LANES), F32),
                   jax.ShapeDtypeStruct((nblk, N_EXPERTS, LANES), jnp.int32)],
        name="router_ctx" if is_ctx else "router",
        compiler_params=_cparams("arbitrary"),
    )(h1, modr, modr, wr_t, b_r, tri)


def _route_plan(route, cnt, n_tiles, n_passes):
    total = jnp.sum(cnt, axis=0)
    tiles = lax.div(total + (MOE_TILE - 1), MOE_TILE)
    tile_start = jnp.cumsum(tiles) - tiles
    base = tile_start[None, :] * MOE_TILE + jnp.cumsum(cnt, axis=0) - cnt
    choice = jax.nn.one_hot(route[:, 0:2, :], N_EXPERTS, dtype=jnp.int32)
    pos = jnp.sum(choice * base[:, None, None, :], axis=-1) + route[:, 2:4, :]
    nblk, _, tm = pos.shape
    token = jnp.broadcast_to((jnp.arange(nblk)[:, None, None] * tm + jnp.arange(tm)[None, None, :]), pos.shape)
    src = jnp.zeros((n_tiles * MOE_TILE,), jnp.int32).at[pos.reshape(-1)].set(
        token.reshape(-1).astype(jnp.int32), unique_indices=True, mode="promise_in_bounds")
    src = src.reshape(n_tiles, 1, MOE_TILE)
    passes = lax.div(tiles + (MOE_PASS_TILES - 1), MOE_PASS_TILES)
    pass_end = jnp.cumsum(passes)
    p = jnp.arange(n_passes, dtype=jnp.int32)
    owner = jnp.sum((p[:, None] >= pass_end[None, :]).astype(jnp.int32), axis=1)
    last = jnp.sum((pass_end[-1] - 1 >= pass_end).astype(jnp.int32))
    expert = jnp.where(p < pass_end[-1], owner, last)
    within = (p - (pass_end - passes)[expert]) * MOE_PASS_TILES
    n = jnp.where(p < pass_end[-1], jnp.clip(tiles[expert] - within, 0, MOE_PASS_TILES), 0)
    first = tile_start[expert] + within
    used = jnp.sum(tiles)
    first = jnp.where(p == pass_end[-1], used, first)
    n_zero = jnp.where(p == pass_end[-1], n_tiles - used, 0)
    i32 = lambda a: a.astype(jnp.int32)
    return i32(pos), src, i32(used).reshape(1), i32(expert), i32(first), i32(n), i32(n_zero)


def _row_copies(src_of, dst_of, n_rows, sem):
    def copy(t, k):
        return pltpu.make_async_copy(src_of(t, k), dst_of(t, k), sem)

    def start(t, carry):
        copy(t, 0).start()
        copy(t, 1).start()
        return carry

    def wait(t, carry):
        copy(t, 0).wait()
        copy(t, 1).wait()
        return carry

    lax.fori_loop(0, n_rows, start, 0, unroll=8)
    lax.fori_loop(0, n_rows, wait, 0, unroll=8)


def _dispatch_kernel(used_ref, src_ref, x_hbm, xs_ref, sem):
    half = MOE_TILE // 2

    @pl.when(pl.program_id(0) < used_ref[0])
    def _():
        _row_copies(lambda r, k: x_hbm.at[pl.ds(src_ref[0, k * half + r], 1)],
                    lambda r, k: xs_ref.at[pl.ds(k * half + r, 1)], half, sem)

    @pl.when(pl.program_id(0) >= used_ref[0])
    def _():
        xs_ref[...] = jnp.zeros_like(xs_ref)


def _dispatch_call(n_used, src, x_packed):
    n_tiles = src.shape[0]
    width = x_packed.shape[1]
    grid_spec = pltpu.PrefetchScalarGridSpec(
        num_scalar_prefetch=1,
        grid=(n_tiles,),
        in_specs=[pl.BlockSpec((None, 1, MOE_TILE), lambda t, nu: (t, 0, 0), memory_space=pltpu.SMEM),
                  pl.BlockSpec(memory_space=pl.ANY)],
        out_specs=pl.BlockSpec((MOE_TILE, width), lambda t, nu: (t, 0)),
        scratch_shapes=[pltpu.SemaphoreType.DMA(())],
    )
    return pl.pallas_call(
        _dispatch_kernel,
        grid_spec=grid_spec,
        out_shape=jax.ShapeDtypeStruct((n_tiles * MOE_TILE, width), x_packed.dtype),
        name="moe_dispatch",
        compiler_params=_cparams("arbitrary"),
    )(n_used, src, x_packed)


def _experts_kernel(pe_ref, pf_ref, pn_ref, pz_ref, xs_hbm, wg_ref, wu_ref, wd_ref, ys_hbm,
                    xa_buf, xb_buf, y_buf, stage_buf, wg_buf, wu_buf, wd_buf, load_sem, store_sem):
    del pe_ref
    p, j = pl.program_id(0), pl.program_id(1)
    last_j = pl.num_programs(1) - 1
    n_tiles = pn_ref[p]
    first = pf_ref[p]
    half = D_MODEL // 2

    def tile_rows(k):
        return pl.ds(pl.multiple_of((first + k) * MOE_TILE, MOE_TILE), MOE_TILE)

    def load(k):
        slot = lax.rem(k, 2)
        return pltpu.make_async_copy(xs_hbm.at[tile_rows(k)], stage_buf.at[slot], load_sem.at[slot])

    def store(k, src):
        return pltpu.make_async_copy(src, ys_hbm.at[tile_rows(k)], store_sem)

    def for_tiles(n, body):
        lax.fori_loop(0, n, lambda k, c: (body(k), c)[1], 0)

    @pl.when((pz_ref[p] > 0) & (j == 0))
    def _():
        y_buf[0] = jnp.zeros(y_buf.shape[1:], F32)
        for_tiles(pz_ref[p], lambda k: store(k, y_buf.at[0]).start())
        for_tiles(pz_ref[p], lambda k: store(k, y_buf.at[0]).wait())

    @pl.when(n_tiles > 0)
    def _():
        wg_buf[...] = wg_ref[...].astype(BF16)
        wu_buf[...] = wu_ref[...].astype(BF16)
        wd_buf[...] = wd_ref[...].astype(BF16)

        def swiglu(k):
            xa, xb = xa_buf[k], xb_buf[k]
            g = _dot(xa, wg_buf[:half, :]) + _dot(xb, wg_buf[half:, :])
            u = _dot(xa, wu_buf[:half, :]) + _dot(xb, wu_buf[half:, :])
            return _dot((g * jax.nn.sigmoid(g) * u).astype(BF16), wd_buf[...])

        @pl.when(j == 0)
        def _():
            load(0).start()

            def first_chunk(k):
                load(k).wait()

                @pl.when(k + 1 < n_tiles)
                def _():
                    load(k + 1).start()

                xa_buf[k], xb_buf[k] = _unpack_bf16_pair(stage_buf[lax.rem(k, 2)])
                y_buf[k] = swiglu(k)
            for_tiles(n_tiles, first_chunk)

        @pl.when((j > 0) & (j < last_j))
        def _():
            def middle_chunk(k):
                y_buf[k] += swiglu(k)
            for_tiles(n_tiles, middle_chunk)

        @pl.when(j == last_j)
        def _():
            def last_chunk(k):
                y_buf[k] += swiglu(k)
                store(k, y_buf.at[k]).start()
            for_tiles(n_tiles, last_chunk)
            for_tiles(n_tiles, lambda k: store(k, y_buf.at[k]).wait())


def _experts_call(pass_expert, pass_first, pass_tiles, pass_zero, xs, w_gate, w_up, w_down, moe_layer):
    n_passes = pass_expert.shape[0]
    d_ff = w_gate.shape[3]
    tf = d_ff // MOE_FF_CHUNKS
    half = D_MODEL // 2
    chunk = lambda p, j, pn: jnp.where(pn[p] > 0, j, MOE_FF_CHUNKS - 1)
    cols = pl.BlockSpec((None, None, D_MODEL, tf),
                        lambda p, j, pe, pf, pn, pz: (moe_layer, pe[p], 0, chunk(p, j, pn)))
    grid_spec = pltpu.PrefetchScalarGridSpec(
        num_scalar_prefetch=4,
        grid=(n_passes, MOE_FF_CHUNKS),
        in_specs=[pl.BlockSpec(memory_space=pl.ANY), cols, cols,
                  pl.BlockSpec((None, None, tf, D_MODEL),
                               lambda p, j, pe, pf, pn, pz: (moe_layer, pe[p], chunk(p, j, pn), 0))],
        out_specs=pl.BlockSpec(memory_space=pl.ANY),
        scratch_shapes=[pltpu.VMEM((MOE_PASS_TILES, MOE_TILE, half), BF16),
                        pltpu.VMEM((MOE_PASS_TILES, MOE_TILE, half), BF16),
                        pltpu.VMEM((MOE_PASS_TILES, MOE_TILE, D_MODEL), F32),
                        pltpu.VMEM((2, MOE_TILE, half), jnp.uint32),
                        pltpu.VMEM((D_MODEL, tf), BF16), pltpu.VMEM((D_MODEL, tf), BF16),
                        pltpu.VMEM((tf, D_MODEL), BF16),
                        pltpu.SemaphoreType.DMA((2,)), pltpu.SemaphoreType.DMA(())],
    )
    return pl.pallas_call(
        _experts_kernel,
        grid_spec=grid_spec,
        out_shape=jax.ShapeDtypeStruct((xs.shape[0], D_MODEL), F32),
        name="moe_experts",
        compiler_params=_cparams("arbitrary", "arbitrary"),
    )(pass_expert, pass_first, pass_tiles, pass_zero, xs, w_gate, w_up, w_down)


def _combine_kernel(pos_ref, ys_hbm, gate_ref, h_ref, g2_ref, lng_ref, lnb_ref, o_ref, buf_ref, sem):
    tm = pos_ref.shape[1]
    _row_copies(lambda t, k: ys_hbm.at[pl.ds(pos_ref[k, t], 1)],
                lambda t, k: buf_ref.at[k, pl.ds(t, 1)], tm, sem)
    gate = gate_ref[...]
    f = gate[:, 0:1] * buf_ref[0] + gate[:, 1:2] * buf_ref[1]
    o_ref[...] = _layer_norm(DEEPNORM_ALPHA * h_ref[...] + g2_ref[...] * f, lng_ref[...], lnb_ref[...])


def _combine_call(pos, ys, gate, h1, modr, layer, lng, lnb, *, batch, is_ctx):
    rows = h1.shape[0]
    nblk, _, tm = pos.shape
    nb = (rows // batch) // tm if not is_ctx else 1
    row_fn = (lambda i: batch) if is_ctx else (lambda i: i // nb)
    rowblk = lambda w: pl.BlockSpec((tm, w), lambda i: (i, 0))
    full = lambda shape: pl.BlockSpec(shape, lambda i: (0,) * len(shape))
    return pl.pallas_call(
        _combine_kernel,
        grid=(nblk,),
        in_specs=[pl.BlockSpec((None, 2, tm), lambda i: (i, 0, 0), memory_space=pltpu.SMEM),
                  pl.BlockSpec(memory_space=pl.ANY), rowblk(LANES), rowblk(D_MODEL),
                  _mod_spec(layer, 5, row_fn), full((1, D_MODEL)), full((1, D_MODEL))],
        out_specs=rowblk(D_MODEL),
        out_shape=jax.ShapeDtypeStruct((rows, D_MODEL), F32),
        scratch_shapes=[pltpu.VMEM((2, tm, D_MODEL), F32), pltpu.SemaphoreType.DMA(())],
        name="moe_combine",
        compiler_params=_cparams("arbitrary"),
    )(pos, ys, gate, h1, modr, lng, lnb)


def _moe_call(f_packed, h1, modr, layer, wr_t, b_r, tri, w_gate, w_up, w_down, moe_layer, lng, lnb,
              *, batch, is_ctx):
    rows = h1.shape[0]
    n_tiles = (2 * rows) // MOE_TILE + N_EXPERTS
    n_passes = n_tiles // MOE_PASS_TILES + N_EXPERTS
    route, gate, cnt = _router_call(h1, modr, layer, wr_t, b_r, tri, batch=batch, is_ctx=is_ctx)
    pos, src, n_used, *passes = _route_plan(route, cnt[:, :, 0], n_tiles, n_passes)
    xs = _dispatch_call(n_used, src, f_packed)
    ys = _experts_call(*passes, xs, w_gate, w_up, w_down, moe_layer)
    return _combine_call(pos, ys, gate, h1, modr, layer, lng, lnb, batch=batch, is_ctx=is_ctx)


def _rope_tables(seq):
    pos = jnp.arange(seq, dtype=jnp.int32)
    row = (pos // GRID_W).astype(F32)
    col = (pos % GRID_W).astype(F32)
    n_freq = HEAD_DIM // 4
    inv_freq = ROPE_THETA ** (-jnp.arange(n_freq, dtype=F32) / n_freq)
    ang = jnp.concatenate([row[:, None] * inv_freq, col[:, None] * inv_freq], axis=-1)
    cos, sin = jnp.cos(ang), jnp.sin(ang)
    reps = LANES // (HEAD_DIM // 2)
    sign = jnp.tile(jnp.concatenate([-jnp.ones(HEAD_DIM // 2, F32), jnp.ones(HEAD_DIM // 2, F32)]),
                    LANES // HEAD_DIM)
    return jnp.tile(cos, (1, reps)), jnp.tile(sin, (1, reps)) * sign


def kernel(x, c, ctx, c_ctx, ada_w, ada_b, w_in, w_out, na_rpb, ga_q_norm, ga_k_norm,
           ssm_lambda_re, ssm_lambda_im, ssm_log_step, ssm_b_re, ssm_b_im, ssm_c_re, ssm_c_im,
           ssm_d, ssm_w_glu, sw_sink, ln1_g, ln1_b, ln2_g, ln2_b,
           ffn_w_gate, ffn_w_up, ffn_w_down,
           moe_w_router, moe_b_router, moe_w_gate, moe_w_up, moe_w_down):
    batch, seq, _ = x.shape
    n_ctx = ctx.shape[1]
    assert batch < MOD_ROWS and seq % 512 == 0 and n_ctx % 256 == 0

    c_all = jnp.zeros((MOD_ROWS, D_MODEL), F32).at[:batch].set(c).at[batch].set(c_ctx)
    modr = _mods_call(c_all, ada_w, ada_b)

    cos_t, sin_t = _rope_tables(seq)
    seg = jnp.kron(jnp.eye(256 // HEAD_DIM, dtype=F32), jnp.ones((HEAD_DIM, HEAD_DIM), F32)).astype(BF16)
    tri = (jnp.arange(MOE_TM)[:, None] < jnp.arange(MOE_TM)[None, :]).astype(BF16)
    sw_mask = _window_mask_table(seq)

    h = x.reshape(batch * seq, D_MODEL)
    hc = ctx.reshape(batch * n_ctx, D_MODEL)
    names = ("qa", "ka", "va", "qb", "kb", "vb", "u", "qd", "kd", "vd")

    for layer in range(DEPTH):
        need_ctx = layer < DEPTH - 1
        w_in_l = w_in[layer].astype(BF16)
        w_out_l = w_out[layer].astype(BF16)
        gq = jnp.tile(ga_q_norm[layer].astype(F32), 256 // HEAD_DIM)[None, :]
        gk = jnp.tile(ga_k_norm[layer].astype(F32), 128 // HEAD_DIM)[None, :]
        sink = sw_sink[layer].astype(F32)
        ln1 = (ln1_g[layer][None, :], ln1_b[layer][None, :])
        ln2 = (ln2_g[layer][None, :], ln2_b[layer][None, :])

        lat = dict(zip(names, _in_proj_call(h, modr, layer, w_in_l, cos_t, sin_t, seg, gq, gk,
                                            batch=batch, is_ctx=False)))
        cx = dict(zip(names, _in_proj_call(hc, modr, layer, w_in_l, cos_t, sin_t, seg, gq, gk,
                                           batch=batch, is_ctx=True)))

        na_bias, na_mask = _na_tables(na_rpb[layer], seq)
        out_a = _na_call(lat["qa"], lat["ka"], lat["va"], cx["ka"], cx["va"], na_bias, na_mask, batch=batch)
        out_b = _global_call(lat["qb"], lat["kb"], lat["vb"], cx["kb"], cx["vb"], batch=batch)
        out_d = _window_call(sink, lat["qd"], lat["kd"], lat["vd"], cx["kd"], cx["vd"], sw_mask, batch=batch)

        dirs = [_s5_operands(ssm_lambda_re[layer, d], ssm_lambda_im[layer, d], ssm_log_step[layer, d],
                             ssm_b_re[layer, d], ssm_b_im[layer, d], ssm_c_re[layer, d], ssm_c_im[layer, d])
                for d in range(2)]
        fwd = _s5_pass(cx["u"], lat["u"], None, *dirs[0], None, None, batch=batch, reverse=False)
        ctx_c, out_c = _s5_pass(cx["u"], lat["u"], fwd, *dirs[1], ssm_d[layer].astype(F32)[None, :],
                                ssm_w_glu[layer].astype(BF16), batch=batch, reverse=True)

        routed = layer % 2 == 1
        h1, f_in = _out_proj_call(out_a, out_b, out_c, out_d, w_out_l, h, modr, layer, *ln1,
                                  batch=batch, is_ctx=False, packed=routed)
        streams = [(h1, f_in, False)]
        if need_ctx:
            ctx_a, ctx_b, ctx_d = _ctx_attn_call(sink, cx, batch=batch)
            hc1, fc_in = _out_proj_call(ctx_a, ctx_b, ctx_c, ctx_d, w_out_l, hc, modr, layer, *ln1,
                                        batch=batch, is_ctx=True, packed=routed)
            streams.append((hc1, fc_in, True))

        i = layer // 2
        outs = []
        if not routed:
            wg, wu, wd = (ffn_w_gate[i].astype(BF16), ffn_w_up[i].astype(BF16), ffn_w_down[i].astype(BF16))
            for s1, sf, is_ctx in streams:
                outs.append(_ffn_call(sf, wg, wu, wd, s1, modr, layer, *ln2, batch=batch, is_ctx=is_ctx))
        else:
            wr_t = moe_w_router[i].astype(F32).T
            b_r = moe_b_router[i].astype(F32)[:, None]
            for s1, sf, is_ctx in streams:
                outs.append(_moe_call(sf, s1, modr, layer, wr_t, b_r, tri, moe_w_gate, moe_w_up, moe_w_down, i,
                                      *ln2, batch=batch, is_ctx=is_ctx))
        h = outs[0]
        if need_ctx:
            hc = outs[1]
    return h.reshape(batch, seq, D_MODEL)
```

```python
import functools

import jax
import jax.numpy as jnp
from jax import lax
from jax.experimental import pallas as pl
from jax.experimental.pallas import tpu as pltpu

F32 = jnp.float32
BF16 = jnp.bfloat16

D_MODEL = 1024
DEPTH = 4
GRID_W = 64
HEAD_DIM = 64
NA_ROWS = 8
NA_COLS = 16
SSM_GROUPS = 16
SSM_GROUP_CH = 16
SSM_STATE = 64
SSM_WIDTH = SSM_GROUPS * SSM_GROUP_CH
SSM_FLAT = SSM_GROUPS * SSM_STATE
SW_WINDOW = 128
N_EXPERTS = 8
ADA_CHUNKS = 6
ROPE_THETA = 10000.0
LN_EPS = 1e-6
RMS_EPS = 1e-6
NEG_INF = -1e30
DEEPNORM_ALPHA = (2 * DEPTH) ** 0.25
ATT_SCALE = HEAD_DIM ** -0.5

C_QA, C_KA, C_VA, C_QB, C_KB, C_VB, C_U, C_QD, C_KD, C_VD, IN_WIDTH = (
    0, 256, 512, 768, 1024, 1152, 1280, 1536, 1792, 1920, 2048)

LANES = 128
SUBLANES = 8
MOD_ROWS = 16
VMEM_LIMIT = 56 * 1024 * 1024


def _cparams(*sem):
    return pltpu.CompilerParams(dimension_semantics=sem, vmem_limit_bytes=VMEM_LIMIT)


def _dot(a, b):
    return jnp.dot(a, b, preferred_element_type=F32)


def _dot_nt(a, b):
    return lax.dot_general(a, b, (((1,), (1,)), ((), ())), preferred_element_type=F32)


def _dot_tn(a, b):
    return lax.dot_general(a, b, (((0,), (0,)), ((), ())), preferred_element_type=F32)


def _split_bf16(x):
    hi = x.astype(BF16)
    lo = (x - hi.astype(F32)).astype(BF16)
    return hi, lo


def _layer_norm(z, g, b):
    zc = z - jnp.mean(z, axis=-1, keepdims=True)
    y = zc * lax.rsqrt(jnp.mean(zc * zc, axis=-1, keepdims=True) + LN_EPS)
    return y * g + b


def _mods_kernel(c_ref, w_ref, b_ref, o_ref):
    c = c_ref[...]
    act = (c * jax.nn.sigmoid(c)).astype(BF16)
    o_ref[...] = _dot(act, w_ref[...].astype(BF16)) + b_ref[...]


def _mods_call(c_all, ada_w, ada_b):
    tn = 1536
    n = ADA_CHUNKS * D_MODEL
    out = pl.pallas_call(
        _mods_kernel,
        grid=(DEPTH, n // tn),
        in_specs=[
            pl.BlockSpec((MOD_ROWS, D_MODEL), lambda l, j: (0, 0)),
            pl.BlockSpec((None, D_MODEL, tn), lambda l, j: (l, 0, j)),
            pl.BlockSpec((None, 1, tn), lambda l, j: (l, 0, j)),
        ],
        out_specs=pl.BlockSpec((None, MOD_ROWS, tn), lambda l, j: (l, 0, j)),
        out_shape=jax.ShapeDtypeStruct((DEPTH, MOD_ROWS, n), F32),
        name="mods",
        compiler_params=_cparams("arbitrary", "arbitrary"),
    )(c_all, ada_w, ada_b.reshape(DEPTH, 1, n))
    return out.reshape(DEPTH * MOD_ROWS * ADA_CHUNKS, 1, D_MODEL)


def _mod_spec(layer, chunk, row_fn):
    def index(i, *_):
        return ((layer * MOD_ROWS + row_fn(i)) * ADA_CHUNKS + chunk, 0, 0)
    return pl.BlockSpec((None, 1, D_MODEL), index)


def _seg_rms(x, seg, g):
    hi, lo = _split_bf16(x * x)
    ss = _dot(hi, seg) + _dot(lo, seg)
    return x * lax.rsqrt(ss * (1.0 / HEAD_DIM) + RMS_EPS) * g


def _rope(x, cos, sin_signed, first_half):
    outs = []
    for j in range(x.shape[1] // LANES):
        xs = x[:, j * LANES:(j + 1) * LANES]
        partner = jnp.where(first_half,
                            pltpu.roll(xs, LANES - HEAD_DIM // 2, 1),
                            pltpu.roll(xs, HEAD_DIM // 2, 1))
        outs.append(xs * cos + partner * sin_signed)
    return outs[0] if len(outs) == 1 else jnp.concatenate(outs, axis=1)


def _pad_heads(x, lane, ones_lane=False):
    low = lane < HEAD_DIM
    fill = jnp.where(lane == HEAD_DIM, 1.0, 0.0) if ones_lane else 0.0
    outs = []
    for j in range(x.shape[1] // LANES):
        xs = x[:, j * LANES:(j + 1) * LANES]
        outs.append(jnp.where(low, xs, fill))
        outs.append(jnp.where(low, pltpu.roll(xs, HEAD_DIM, 1), fill))
    return jnp.concatenate(outs, axis=1).astype(BF16)


def _in_proj_kernel(x_ref, sh_ref, sc_ref, w_ref, cos_ref, sin_ref, seg_ref, gq_ref, gk_ref,
                    qa_ref, ka_ref, va_ref, qb_ref, kb_ref, vb_ref, u_ref, qd_ref, kd_ref, vd_ref,
                    *, rope):
    a = (x_ref[...] * (1.0 + sc_ref[...]) + sh_ref[...]).astype(BF16)
    lane = lax.broadcasted_iota(jnp.int32, (a.shape[0], LANES), 1)

    def proj(c0, c1):
        return _dot(a, w_ref[:, c0:c1])

    qa_ref[...] = _pad_heads(proj(C_QA, C_KA) * ATT_SCALE, lane)
    ka_ref[...] = _pad_heads(proj(C_KA, C_VA), lane)
    va_ref[...] = _pad_heads(proj(C_VA, C_QB), lane, ones_lane=True)
    vb_ref[...] = _pad_heads(proj(C_VB, C_U), lane, ones_lane=True)
    u_ref[...] = proj(C_U, C_QD)
    vd_ref[...] = _pad_heads(proj(C_VD, IN_WIDTH), lane, ones_lane=True)

    seg = seg_ref[...]
    qb = _seg_rms(proj(C_QB, C_KB), seg, gq_ref[...])
    kb = _seg_rms(proj(C_KB, C_VB), seg[:LANES, :LANES], gk_ref[...])
    qd = proj(C_QD, C_KD)
    kd = proj(C_KD, C_VD)
    if rope:
        cos = cos_ref[...]
        sin = sin_ref[...]
        first_half = (lane % HEAD_DIM) < (HEAD_DIM // 2)
        qb = _rope(qb, cos, sin, first_half)
        kb = _rope(kb, cos, sin, first_half)
        qd = _rope(qd, cos, sin, first_half)
        kd = _rope(kd, cos, sin, first_half)
    qb_ref[...] = _pad_heads(qb * ATT_SCALE, lane)
    kb_ref[...] = _pad_heads(kb, lane)
    qd_ref[...] = _pad_heads(qd * ATT_SCALE, lane)
    kd_ref[...] = _pad_heads(kd, lane)


def _in_proj_call(x, modr, layer, w_in, cos_t, sin_t, seg, gq, gk, *, batch, is_ctx):
    rows = x.shape[0]
    per_batch = rows // batch
    tm = 256 if is_ctx else 512
    nb = per_batch // tm
    row_fn = (lambda i: batch) if is_ctx else (lambda i: i // nb)
    table_spec = pl.BlockSpec((tm, LANES), (lambda i: (0, 0)) if is_ctx else (lambda i: (i % nb, 0)))
    full = lambda shape: pl.BlockSpec(shape, lambda i: (0,) * len(shape))
    rowblk = lambda w: pl.BlockSpec((tm, w), lambda i: (i, 0))
    wide = lambda w, dt: jax.ShapeDtypeStruct((rows, w), dt)
    out_shapes = [wide(512, BF16), wide(512, BF16), wide(512, BF16),
                  wide(512, BF16), wide(256, BF16), wide(256, BF16),
                  jax.ShapeDtypeStruct((per_batch, batch * SSM_WIDTH), F32),
                  wide(512, BF16), wide(256, BF16), wide(256, BF16)]
    out_specs = [rowblk(512), rowblk(512), rowblk(512), rowblk(512), rowblk(256), rowblk(256),
                 pl.BlockSpec((tm, SSM_WIDTH), lambda i: (i % nb, i // nb)),
                 rowblk(512), rowblk(256), rowblk(256)]
    return pl.pallas_call(
        functools.partial(_in_proj_kernel, rope=not is_ctx),
        grid=(rows // tm,),
        in_specs=[rowblk(D_MODEL), _mod_spec(layer, 0, row_fn), _mod_spec(layer, 1, row_fn),
                  full((D_MODEL, IN_WIDTH)), table_spec, table_spec, full((256, 256)),
                  full((1, 256)), full((1, 128))],
        out_specs=out_specs,
        out_shape=out_shapes,
        name="in_proj_ctx" if is_ctx else "in_proj",
        compiler_params=_cparams("arbitrary"),
    )(x, modr, modr, w_in, cos_t, sin_t, seg, gq, gk)


KV_CHUNK = 256


def _tile(h):
    return slice(h * LANES, (h + 1) * LANES)


def _attend(q, parts, sink=None):
    def score(k_fn, mod_fn, c):
        s = _dot_nt(q, k_fn(c))
        return s if mod_fn is None else mod_fn(s, c)

    tiles = [(k_fn, v_fn, mod_fn, c) for k_fn, v_fn, n, mod_fn in parts for c in range(0, n, KV_CHUNK)]
    m = sink
    acc = None
    for k_fn, v_fn, mod_fn, c in tiles:
        s = score(k_fn, mod_fn, c)
        m_tile = jnp.max(s, axis=-1, keepdims=True)
        m_new = m_tile if m is None else jnp.maximum(m, m_tile)
        pv = _dot(jnp.exp(s - m_new).astype(BF16), v_fn(c))
        acc = pv if acc is None else acc * jnp.exp(m - m_new) + pv
        m = m_new
    den = acc[:, HEAD_DIM:HEAD_DIM + 1]
    if sink is not None:
        den = den + jnp.exp(sink - m)
    return acc * (1.0 / den)


def _pack_pair(a, b):
    lane = lax.broadcasted_iota(jnp.int32, a.shape, 1)
    return jnp.where(lane < HEAD_DIM, a, pltpu.roll(b, HEAD_DIM, 1)).astype(BF16)


def _ref_part(k_ref, v_ref, t, n_keys, start=0, mod_fn=None):
    return (lambda c: k_ref[pl.ds(start + c, KV_CHUNK), _tile(t)],
            lambda c: v_ref[pl.ds(start + c, KV_CHUNK), _tile(t)], n_keys, mod_fn)


def _gqa_group(q_ref, g, parts, sink_pair=None):
    tq = q_ref.shape[0]
    q2 = jnp.concatenate([q_ref[:, _tile(2 * g)], q_ref[:, _tile(2 * g + 1)]], axis=0)
    sink = None
    if sink_pair is not None:
        row = lax.broadcasted_iota(jnp.int32, (2 * tq, 1), 0)
        sink = jnp.where(row < tq, sink_pair[0], sink_pair[1])
    o2 = _attend(q2, parts, sink)
    return _pack_pair(o2[:tq], o2[tq:])


NA_QROWS = 4
NA_BAND = 12


def _na_band_start(j, rows):
    return jnp.clip(j * NA_QROWS - NA_ROWS // 2, 0, rows - NA_BAND)


def _na_kernel(q_ref, k_ref, v_ref, kc_ref, vc_ref, bias_ref, mask_ref, o_ref):
    j = pl.program_id(1)
    rows = k_ref.shape[0] // GRID_W
    start = pl.multiple_of(_na_band_start(j, rows) * GRID_W, GRID_W)
    outs = []
    for h in range(4):
        def local_scores(s, c, h=h):
            cols = slice(c, c + KV_CHUNK)
            return jnp.where(mask_ref[:, cols] > 0.5, s + bias_ref[h, :, cols], NEG_INF)

        parts = [_ref_part(k_ref, v_ref, h, NA_BAND * GRID_W, start, local_scores),
                 _ref_part(kc_ref, vc_ref, h, kc_ref.shape[0])]
        outs.append(_attend(q_ref[:, _tile(h)], parts))
    o_ref[:, _tile(0)] = _pack_pair(outs[0], outs[1])
    o_ref[:, _tile(1)] = _pack_pair(outs[2], outs[3])


def _na_call(qa, ka, va, ka_c, va_c, bias_tab, mask_tab, *, batch):
    seq = qa.shape[0] // batch
    n_ctx = ka_c.shape[0] // batch
    nj = seq // (NA_QROWS * GRID_W)
    tq = NA_QROWS * GRID_W
    nk = NA_BAND * GRID_W

    def pattern(j):
        return jnp.where(j == 0, 0, jnp.where(j == nj - 1, 2, 1))

    return pl.pallas_call(
        _na_kernel,
        grid=(batch, nj),
        in_specs=[
            pl.BlockSpec((tq, 512), lambda b, j: (b * nj + j, 0)),
            pl.BlockSpec((seq, 512), lambda b, j: (b, 0)),
            pl.BlockSpec((seq, 512), lambda b, j: (b, 0)),
            pl.BlockSpec((n_ctx, 512), lambda b, j: (b, 0)),
            pl.BlockSpec((n_ctx, 512), lambda b, j: (b, 0)),
            pl.BlockSpec((None, 4, tq, nk), lambda b, j: (pattern(j), 0, 0, 0)),
            pl.BlockSpec((None, tq, nk), lambda b, j: (pattern(j), 0, 0)),
        ],
        out_specs=pl.BlockSpec((tq, 256), lambda b, j: (b * nj + j, 0)),
        out_shape=jax.ShapeDtypeStruct((qa.shape[0], 256), BF16),
        name="na_attn",
        compiler_params=_cparams("arbitrary", "arbitrary"),
    )(qa, ka, va, ka_c, va_c, bias_tab, mask_tab)


def _na_tables(rpb, seq):
    rows = seq // GRID_W
    nj = rows // NA_QROWS
    c_idx = jnp.arange(GRID_W)
    col_start = jnp.clip(c_idx - NA_COLS // 2, 0, GRID_W - NA_COLS)
    col_valid = (c_idx[None, :] >= col_start[:, None]) & (c_idx[None, :] < col_start[:, None] + NA_COLS)
    col_rel = jnp.clip(c_idx[None, :] - c_idx[:, None], 1 - NA_COLS, NA_COLS - 1) + NA_COLS - 1
    blocks = jnp.array([0, 1, nj - 1])
    q_row = blocks[:, None] * NA_QROWS + jnp.arange(NA_QROWS)[None, :]
    k_row = _na_band_start(blocks, rows)[:, None] + jnp.arange(NA_BAND)[None, :]
    row_start = jnp.clip(q_row - NA_ROWS // 2, 0, rows - NA_ROWS)
    row_valid = ((k_row[:, None, :] >= row_start[:, :, None])
                 & (k_row[:, None, :] < row_start[:, :, None] + NA_ROWS))
    row_rel = jnp.clip(k_row[:, None, :] - q_row[:, :, None] + NA_ROWS - 1, 0, 2 * NA_ROWS - 2)
    row_sel = jax.nn.one_hot(row_rel, 2 * NA_ROWS - 1, dtype=F32)
    col_sel = jax.nn.one_hot(col_rel, 2 * NA_COLS - 1, dtype=F32)
    exact = lax.Precision.HIGHEST
    by_col = jnp.einsum("hab,qkb->haqk", rpb.astype(F32), col_sel, precision=exact)
    bias = jnp.einsum("pria,haqk->phrqik", row_sel, by_col, precision=exact)
    valid = row_valid[:, :, None, :, None] & col_valid[None, None, :, None, :]
    tq, nk = NA_QROWS * GRID_W, NA_BAND * GRID_W
    return bias.reshape(3, 4, tq, nk), valid.astype(F32).reshape(3, tq, nk)


def _global_kernel(q_ref, k_ref, v_ref, kc_ref, vc_ref, o_ref):
    for g in range(2):
        parts = [_ref_part(kc_ref, vc_ref, g, kc_ref.shape[0]), _ref_part(k_ref, v_ref, g, k_ref.shape[0])]
        o_ref[:, _tile(g)] = _gqa_group(q_ref, g, parts)


def _global_call(qb, kb, vb, kb_c, vb_c, *, batch):
    seq = qb.shape[0] // batch
    n_ctx = kb_c.shape[0] // batch
    tq = 256
    nq = seq // tq
    return pl.pallas_call(
        _global_kernel,
        grid=(batch, nq),
        in_specs=[
            pl.BlockSpec((tq, 512), lambda b, i: (b * nq + i, 0)),
            pl.BlockSpec((seq, 256), lambda b, i: (b, 0)),
            pl.BlockSpec((seq, 256), lambda b, i: (b, 0)),
            pl.BlockSpec((n_ctx, 256), lambda b, i: (b, 0)),
            pl.BlockSpec((n_ctx, 256), lambda b, i: (b, 0)),
        ],
        out_specs=pl.BlockSpec((tq, 256), lambda b, i: (b * nq + i, 0)),
        out_shape=jax.ShapeDtypeStruct((qb.shape[0], 256), BF16),
        name="global_attn",
        compiler_params=_cparams("arbitrary", "arbitrary"),
    )(qb, kb, vb, kb_c, vb_c)


SW_TQ = 256
SW_SPAN = SW_TQ + 2 * SW_WINDOW


def _window_start(i, seq):
    return jnp.clip(i * SW_TQ - SW_WINDOW, 0, seq - SW_SPAN)


def _window_kernel(sink_ref, q_ref, k_ref, v_ref, kc_ref, vc_ref, mask_ref, o_ref):
    i = pl.program_id(1)
    start = pl.multiple_of(_window_start(i, k_ref.shape[0]), LANES)

    def in_window(s, c):
        return jnp.where(mask_ref[:, c:c + KV_CHUNK] > 0.5, s, NEG_INF)

    for g in range(2):
        parts = [_ref_part(k_ref, v_ref, g, SW_SPAN, start, in_window),
                 _ref_part(kc_ref, vc_ref, g, kc_ref.shape[0])]
        o_ref[:, _tile(g)] = _gqa_group(q_ref, g, parts, sink_pair=(sink_ref[2 * g], sink_ref[2 * g + 1]))


def _window_mask_table(seq):
    nq = seq // SW_TQ
    blocks = jnp.array([0, 1, nq - 1])
    qpos = blocks[:, None] * SW_TQ + jnp.arange(SW_TQ)[None, :]
    kpos = _window_start(blocks, seq)[:, None] + jnp.arange(SW_SPAN)[None, :]
    valid = jnp.abs(kpos[:, None, :] - qpos[:, :, None]) <= SW_WINDOW
    return jnp.tile(valid.astype(F32), (1, 2, 1))


def _window_call(sink, qd, kd, vd, kd_c, vd_c, mask_tab, *, batch):
    seq = qd.shape[0] // batch
    n_ctx = kd_c.shape[0] // batch
    nq = seq // SW_TQ

    def pattern(i):
        return jnp.where(i == 0, 0, jnp.where(i == nq - 1, 2, 1))

    return pl.pallas_call(
        _window_kernel,
        grid=(batch, nq),
        in_specs=[
            pl.BlockSpec(memory_space=pltpu.SMEM),
            pl.BlockSpec((SW_TQ, 512), lambda b, i: (b * nq + i, 0)),
            pl.BlockSpec((seq, 256), lambda b, i: (b, 0)),
            pl.BlockSpec((seq, 256), lambda b, i: (b, 0)),
            pl.BlockSpec((n_ctx, 256), lambda b, i: (b, 0)),
            pl.BlockSpec((n_ctx, 256), lambda b, i: (b, 0)),
            pl.BlockSpec((None, 2 * SW_TQ, SW_SPAN), lambda b, i: (pattern(i), 0, 0)),
        ],
        out_specs=pl.BlockSpec((SW_TQ, 256), lambda b, i: (b * nq + i, 0)),
        out_shape=jax.ShapeDtypeStruct((qd.shape[0], 256), BF16),
        name="window_attn",
        compiler_params=_cparams("arbitrary", "arbitrary"),
    )(sink, qd, kd, vd, kd_c, vd_c, mask_tab)


def _ctx_attn_kernel(sink_ref, qa_ref, ka_ref, va_ref, qb_ref, kb_ref, vb_ref, qd_ref, kd_ref, vd_ref,
                     oa_ref, ob_ref, od_ref):
    n = ka_ref.shape[0]
    outs = [_attend(qa_ref[:, _tile(h)], [_ref_part(ka_ref, va_ref, h, n)]) for h in range(4)]
    oa_ref[:, _tile(0)] = _pack_pair(outs[0], outs[1])
    oa_ref[:, _tile(1)] = _pack_pair(outs[2], outs[3])
    for g in range(2):
        ob_ref[:, _tile(g)] = _gqa_group(qb_ref, g, [_ref_part(kb_ref, vb_ref, g, n)])
        od_ref[:, _tile(g)] = _gqa_group(qd_ref, g, [_ref_part(kd_ref, vd_ref, g, n)],
                                         sink_pair=(sink_ref[2 * g], sink_ref[2 * g + 1]))


def _ctx_attn_call(sink, cx, *, batch):
    n_ctx = cx["qa"].shape[0] // batch
    blk = lambda w: pl.BlockSpec((n_ctx, w), lambda b: (b, 0))
    names = ("qa", "ka", "va", "qb", "kb", "vb", "qd", "kd", "vd")
    widths = (512, 512, 512, 512, 256, 256, 512, 256, 256)
    shape = jax.ShapeDtypeStruct((cx["qa"].shape[0], 256), BF16)
    return pl.pallas_call(
        _ctx_attn_kernel,
        grid=(batch,),
        in_specs=[pl.BlockSpec(memory_space=pltpu.SMEM)] + [blk(w) for w in widths],
        out_specs=[blk(256)] * 3,
        out_shape=[shape] * 3,
        name="ctx_attn",
        compiler_params=_cparams("arbitrary"),
    )(sink, *[cx[n] for n in names])


def _s5_discretise(lam_re, lam_im, log_step, b_re, b_im):
    step = jnp.exp(log_step.astype(F32))[:, None]
    lam_re = lam_re.astype(F32)
    lam_im = lam_im.astype(F32)
    mag = jnp.exp(lam_re * step)
    ab_re = mag * jnp.cos(lam_im * step)
    ab_im = mag * jnp.sin(lam_im * step)
    den = lam_re * lam_re + lam_im * lam_im
    num_re = ab_re - 1.0
    f_re = ((num_re * lam_re + ab_im * lam_im) / den)[..., None]
    f_im = ((ab_im * lam_re - num_re * lam_im) / den)[..., None]
    b_re = b_re.astype(F32)
    b_im = b_im.astype(F32)
    return ab_re, ab_im, f_re * b_re - f_im * b_im, f_re * b_im + f_im * b_re


def _s5_operands(lam_re, lam_im, log_step, b_re, b_im, c_re, c_im):
    ab_re, ab_im, bb_re, bb_im = _s5_discretise(lam_re, lam_im, log_step, b_re, b_im)
    eye = jnp.eye(SSM_GROUPS, dtype=F32)
    bd_in = lambda m: jnp.einsum("gph,gk->ghkp", m, eye).reshape(SSM_WIDTH, SSM_FLAT)
    bd_out = lambda m: jnp.einsum("ghp,gk->kpgh", m.astype(F32), eye).reshape(SSM_FLAT, SSM_WIDTH)
    a = jnp.stack([ab_re.reshape(SSM_FLAT), ab_im.reshape(SSM_FLAT)])
    bbd = jnp.concatenate([bd_in(bb_re), bd_in(bb_im)], axis=1).astype(BF16)
    cbd = jnp.concatenate([bd_out(c_re), -bd_out(c_im)], axis=0).astype(BF16)
    return a, bbd, cbd


def _s5_kernel(*refs, reverse, final, batch, n_ctx_chunks):
    if final:
        (uc_ref, ul_ref, pc_ref, pl_ref, a_ref, bbd_ref, cbd_ref, d_ref, wglu_ref,
         oc_ref, ol_ref, bu_ref, h_ref) = refs
    else:
        uc_ref, ul_ref, a_ref, bbd_ref, cbd_ref, oc_ref, ol_ref, bu_ref, h_ref = refs
        pc_ref = pl_ref = d_ref = wglu_ref = None
    c = pl.program_id(0)
    steps = bu_ref.shape[0] // batch

    @pl.when(c == 0)
    def _():
        h_ref[...] = jnp.zeros_like(h_ref)

    def run(u_ref, prev_ref, o_ref):
        u = u_ref[...]
        bu_ref[...] = _dot(u.astype(BF16), bbd_ref[...])
        ar = jnp.broadcast_to(a_ref[0:1, :], (batch, SSM_FLAT))
        ai = jnp.broadcast_to(a_ref[1:2, :], (batch, SSM_FLAT))

        def step(i, h):
            t = steps - 1 - i if reverse else i
            row = pl.multiple_of(t * batch, batch)
            b = bu_ref[pl.ds(row, batch), :]
            hr, hi = h[:, :SSM_FLAT], h[:, SSM_FLAT:]
            nr = ar * hr - ai * hi + b[:, :SSM_FLAT]
            ni = ar * hi + ai * hr + b[:, SSM_FLAT:]
            hn = jnp.concatenate([nr, ni], axis=1)
            bu_ref[pl.ds(row, batch), :] = hn
            return hn

        h_ref[...] = lax.fori_loop(0, steps, step, h_ref[...], unroll=True)
        y = _dot(bu_ref[...].astype(BF16), cbd_ref[...])
        if not final:
            o_ref[...] = y
        else:
            y = jax.nn.gelu(y + prev_ref[...] + d_ref[...] * u)
            z = _dot(y.astype(BF16), wglu_ref[...])
            o_ref[...] = (y * jax.nn.sigmoid(z)).astype(BF16)

    @pl.when(c < n_ctx_chunks)
    def _():
        run(uc_ref, pc_ref, oc_ref)

    @pl.when(c >= n_ctx_chunks)
    def _():
        run(ul_ref, pl_ref, ol_ref)


def _s5_pass(u_ctx, u_lat, prev, a, bbd, cbd, d_skip, w_glu, *, batch, reverse):
    final = prev is not None
    t_chunk = 128
    rows = t_chunk * batch
    ncc = u_ctx.shape[0] // t_chunk
    nlc = u_lat.shape[0] // t_chunk
    flat = lambda x: x.reshape(x.shape[0] * batch, SSM_WIDTH)
    if reverse:
        ctx_idx = lambda c: (jnp.maximum(ncc - 1 - c, 0), 0)
        lat_idx = lambda c: (nlc - 1 - jnp.maximum(c - ncc, 0), 0)
    else:
        ctx_idx = lambda c: (jnp.minimum(c, ncc - 1), 0)
        lat_idx = lambda c: (jnp.maximum(c - ncc, 0), 0)
    full = lambda shape: pl.BlockSpec(shape, lambda c: (0,) * len(shape))
    u_specs = [pl.BlockSpec((rows, SSM_WIDTH), ctx_idx), pl.BlockSpec((rows, SSM_WIDTH), lat_idx)]
    par_specs = [full((2, SSM_FLAT)), full((SSM_WIDTH, 2 * SSM_FLAT)), full((2 * SSM_FLAT, SSM_WIDTH))]
    args = [flat(u_ctx), flat(u_lat)]
    in_specs = list(u_specs)
    if final:
        args += [flat(prev[0]), flat(prev[1])]
        in_specs += u_specs
    args += [a, bbd, cbd]
    in_specs += par_specs
    if final:
        args += [d_skip, w_glu]
        in_specs += [full((1, SSM_WIDTH)), full((SSM_WIDTH, SSM_WIDTH))]
    dt = BF16 if final else F32
    oc, ol = pl.pallas_call(
        functools.partial(_s5_kernel, reverse=reverse, final=final, batch=batch, n_ctx_chunks=ncc),
        grid=(ncc + nlc,),
        in_specs=in_specs,
        out_specs=u_specs,
        out_shape=[jax.ShapeDtypeStruct((u_ctx.shape[0] * batch, SSM_WIDTH), dt),
                   jax.ShapeDtypeStruct((u_lat.shape[0] * batch, SSM_WIDTH), dt)],
        scratch_shapes=[pltpu.VMEM((rows, 2 * SSM_FLAT), F32), pltpu.VMEM((batch, 2 * SSM_FLAT), F32)],
        name="s5_bwd_glu" if final else "s5_fwd",
        compiler_params=_cparams("arbitrary"),
    )(*args)
    return oc.reshape(u_ctx.shape), ol.reshape(u_lat.shape)


def _bf16_bits(x):
    return lax.bitcast_convert_type(x.astype(BF16).astype(F32), jnp.uint32)


def _pack_bf16_pair(a, b):
    return _bf16_bits(a) | lax.shift_right_logical(_bf16_bits(b), jnp.uint32(16))


def _unpack_bf16_pair(u):
    a = lax.bitcast_convert_type(u & jnp.uint32(0xFFFF0000), F32)
    b = lax.bitcast_convert_type(lax.shift_left(u, jnp.uint32(16)), F32)
    return a.astype(BF16), b.astype(BF16)


def _out_proj_kernel(oa_ref, ob_ref, oc_ref, od_ref, w_ref, x_ref, g1_ref, sh2_ref, sc2_ref,
                     lng_ref, lnb_ref, h_ref, f_ref, *, packed):
    y = (_dot(oa_ref[...], w_ref[0:256, :]) + _dot(ob_ref[...], w_ref[256:512, :])
         + _dot(oc_ref[...], w_ref[512:768, :]) + _dot(od_ref[...], w_ref[768:1024, :]))
    h1 = _layer_norm(DEEPNORM_ALPHA * x_ref[...] + g1_ref[...] * y, lng_ref[...], lnb_ref[...])
    h_ref[...] = h1
    f = h1 * (1.0 + sc2_ref[...]) + sh2_ref[...]
    if packed:
        f_ref[...] = _pack_bf16_pair(f[:, :D_MODEL // 2], f[:, D_MODEL // 2:])
    else:
        f_ref[...] = f.astype(BF16)


def _out_proj_call(oa, ob, oc_tm, od, w_out, x, modr, layer, lng, lnb, *, batch, is_ctx, packed):
    rows = x.shape[0]
    per_batch = rows // batch
    tm = 256 if is_ctx else 512
    nb = per_batch // tm
    row_fn = (lambda i: batch) if is_ctx else (lambda i: i // nb)
    rowblk = lambda w: pl.BlockSpec((tm, w), lambda i: (i, 0))
    full = lambda shape: pl.BlockSpec(shape, lambda i: (0,) * len(shape))
    f_width, f_dtype = (D_MODEL // 2, jnp.uint32) if packed else (D_MODEL, BF16)
    return pl.pallas_call(
        functools.partial(_out_proj_kernel, packed=packed),
        grid=(rows // tm,),
        in_specs=[rowblk(256), rowblk(256),
                  pl.BlockSpec((tm, SSM_WIDTH), lambda i: (i % nb, i // nb)),
                  rowblk(256), full((D_MODEL, D_MODEL)), rowblk(D_MODEL),
                  _mod_spec(layer, 2, row_fn), _mod_spec(layer, 3, row_fn), _mod_spec(layer, 4, row_fn),
                  full((1, D_MODEL)), full((1, D_MODEL))],
        out_specs=[rowblk(D_MODEL), rowblk(f_width)],
        out_shape=[jax.ShapeDtypeStruct((rows, D_MODEL), F32), jax.ShapeDtypeStruct((rows, f_width), f_dtype)],
        name="out_proj_ctx" if is_ctx else "out_proj",
        compiler_params=_cparams("arbitrary"),
    )(oa, ob, oc_tm, od, w_out, x, modr, modr, modr, lng, lnb)


def _ffn_kernel(f_ref, wg_ref, wu_ref, wd_ref, h_ref, g2_ref, lng_ref, lnb_ref, o_ref, acc_ref):
    j = pl.program_id(1)

    @pl.when(j == 0)
    def _():
        acc_ref[...] = jnp.zeros_like(acc_ref)

    f = f_ref[...]
    g = _dot(f, wg_ref[...])
    u = _dot(f, wu_ref[...])
    acc_ref[...] += _dot((g * jax.nn.sigmoid(g) * u).astype(BF16), wd_ref[...])

    @pl.when(j == pl.num_programs(1) - 1)
    def _():
        o_ref[...] = _layer_norm(DEEPNORM_ALPHA * h_ref[...] + g2_ref[...] * acc_ref[...],
                                 lng_ref[...], lnb_ref[...])


def _ffn_call(f_in, wg, wu, wd, h1, modr, layer, lng, lnb, *, batch, is_ctx):
    rows = h1.shape[0]
    per_batch = rows // batch
    tm = 256 if is_ctx else 512
    d_ff = wg.shape[1]
    tf = d_ff // 2
    nb = per_batch // tm
    row_fn = (lambda i: batch) if is_ctx else (lambda i: i // nb)
    rowblk = lambda w: pl.BlockSpec((tm, w), lambda i, j: (i, 0))
    full = lambda shape: pl.BlockSpec(shape, lambda i, j: (0,) * len(shape))
    return pl.pallas_call(
        _ffn_kernel,
        grid=(rows // tm, d_ff // tf),
        in_specs=[rowblk(D_MODEL),
                  pl.BlockSpec((D_MODEL, tf), lambda i, j: (0, j)),
                  pl.BlockSpec((D_MODEL, tf), lambda i, j: (0, j)),
                  pl.BlockSpec((tf, D_MODEL), lambda i, j: (j, 0)),
                  rowblk(D_MODEL), _mod_spec(layer, 5, row_fn), full((1, D_MODEL)), full((1, D_MODEL))],
        out_specs=rowblk(D_MODEL),
        out_shape=jax.ShapeDtypeStruct((rows, D_MODEL), F32),
        scratch_shapes=[pltpu.VMEM((tm, D_MODEL), F32)],
        name="ffn_ctx" if is_ctx else "ffn",
        compiler_params=_cparams("arbitrary", "arbitrary"),
    )(f_in, wg, wu, wd, h1, modr, lng, lnb)


MOE_TM = 1024
MOE_TILE = 512
MOE_PASS_TILES = 10
MOE_FF_CHUNKS = 7
ROUTE_ROWS = 8


def _router_kernel(h_ref, sh2_ref, sc2_ref, wr_ref, br_ref, tri_ref, route_ref, gate_ref, cnt_ref):
    tm = h_ref.shape[0]
    f = h_ref[...] * (1.0 + sc2_ref[...]) + sh2_ref[...]
    f_hi, f_lo = _split_bf16(f)
    w_hi, w_lo = _split_bf16(wr_ref[...])
    logits = _dot_nt(w_hi, f_hi) + _dot_nt(w_hi, f_lo) + _dot_nt(w_lo, f_hi) + br_ref[...]
    ie = lax.broadcasted_iota(jnp.int32, logits.shape, 0)
    m1 = jnp.max(logits, axis=0, keepdims=True)
    i1 = jnp.min(jnp.where(logits == m1, ie, N_EXPERTS), axis=0, keepdims=True)
    rest = jnp.where(ie == i1, -jnp.inf, logits)
    m2 = jnp.max(rest, axis=0, keepdims=True)
    i2 = jnp.min(jnp.where(rest == m2, ie, N_EXPERTS), axis=0, keepdims=True)
    e2 = jnp.exp(m2 - m1)
    den = 1.0 + e2
    sel = jnp.where((ie == i1) | (ie == i2), 1.0, 0.0)
    rank = _dot(sel.astype(BF16), tri_ref[...])
    r1 = jnp.sum(jnp.where(ie == i1, rank, 0.0), axis=0, keepdims=True).astype(jnp.int32)
    r2 = jnp.sum(jnp.where(ie == i2, rank, 0.0), axis=0, keepdims=True).astype(jnp.int32)
    row = lax.broadcasted_iota(jnp.int32, (ROUTE_ROWS, tm), 0)
    route_ref[...] = jnp.where(row == 0, i1, jnp.where(row == 1, i2, jnp.where(row == 2, r1,
                               jnp.where(row == 3, r2, 0))))
    lrow = lax.broadcasted_iota(jnp.int32, (LANES, tm), 0)
    gate_ref[...] = jnp.where(lrow == 0, 1.0 / den, jnp.where(lrow == 1, e2 / den, 0.0)).T
    cnt = jnp.sum(sel, axis=1, keepdims=True)
    cnt_ref[...] = jnp.broadcast_to(cnt, cnt_ref.shape).astype(jnp.int32)


def _router_call(h1, modr, layer, wr_t, b_r, tri, *, batch, is_ctx):
    rows = h1.shape[0]
    tm = MOE_TM
    nblk = rows // tm
    nb = (rows // batch) // tm if not is_ctx else 1
    row_fn = (lambda i: batch) if is_ctx else (lambda i: i // nb)
    full = lambda shape: pl.BlockSpec(shape, lambda i: (0,) * len(shape))
    return pl.pallas_call(
        _router_kernel,
        grid=(nblk,),
        in_specs=[pl.BlockSpec((tm, D_MODEL), lambda i: (i, 0)),
                  _mod_spec(layer, 3, row_fn), _mod_spec(layer, 4, row_fn),
                  full((N_EXPERTS, D_MODEL)), full((N_EXPERTS, 1)), full((tm, tm))],
        out_specs=[pl.BlockSpec((None, ROUTE_ROWS, tm), lambda i: (i, 0, 0)),
                   pl.BlockSpec((tm, LANES), lambda i: (i, 0)),
                   pl.BlockSpec((None, N_EXPERTS, LANES), lambda i: (i, 0, 0))],
        out_shape=[jax.ShapeDtypeStruct((nblk, ROUTE_ROWS, tm), jnp.int32),
                   jax.ShapeDtypeStruct((rows, LANES), F32),
                   jax.ShapeDtypeStruct((nblk, N_EXPERTS, LANES), jnp.int32)],
        name="router_ctx" if is_ctx else "router",
        compiler_params=_cparams("arbitrary"),
    )(h1, modr, modr, wr_t, b_r, tri)


def _route_plan(route, cnt, n_tiles, n_passes):
    total = jnp.sum(cnt, axis=0)
    tiles = lax.div(total + (MOE_TILE - 1), MOE_TILE)
    tile_start = jnp.cumsum(tiles) - tiles
    base = tile_start[None, :] * MOE_TILE + jnp.cumsum(cnt, axis=0) - cnt
    choice = jax.nn.one_hot(route[:, 0:2, :], N_EXPERTS, dtype=jnp.int32)
    pos = jnp.sum(choice * base[:, None, None, :], axis=-1) + route[:, 2:4, :]
    nblk, _, tm = pos.shape
    token = jnp.broadcast_to((jnp.arange(nblk)[:, None, None] * tm + jnp.arange(tm)[None, None, :]), pos.shape)
    src = jnp.zeros((n_tiles * MOE_TILE,), jnp.int32).at[pos.reshape(-1)].set(
        token.reshape(-1).astype(jnp.int32), unique_indices=True, mode="promise_in_bounds")
    src = src.reshape(n_tiles, 1, MOE_TILE)
    passes = lax.div(tiles + (MOE_PASS_TILES - 1), MOE_PASS_TILES)
    pass_end = jnp.cumsum(passes)
    p = jnp.arange(n_passes, dtype=jnp.int32)
    owner = jnp.sum((p[:, None] >= pass_end[None, :]).astype(jnp.int32), axis=1)
    last = jnp.sum((pass_end[-1] - 1 >= pass_end).astype(jnp.int32))
    expert = jnp.where(p < pass_end[-1], owner, last)
    within = (p - (pass_end - passes)[expert]) * MOE_PASS_TILES
    n = jnp.where(p < pass_end[-1], jnp.clip(tiles[expert] - within, 0, MOE_PASS_TILES), 0)
    first = tile_start[expert] + within
    used = jnp.sum(tiles)
    first = jnp.where(p == pass_end[-1], used, first)
    n_zero = jnp.where(p == pass_end[-1], n_tiles - used, 0)
    i32 = lambda a: a.astype(jnp.int32)
    return i32(pos), src, i32(used).reshape(1), i32(expert), i32(first), i32(n), i32(n_zero)


def _row_copies(src_of, dst_of, n_rows, sem):
    def copy(t, k):
        return pltpu.make_async_copy(src_of(t, k), dst_of(t, k), sem)

    def start(t, carry):
        copy(t, 0).start()
        copy(t, 1).start()
        return carry

    def wait(t, carry):
        copy(t, 0).wait()
        copy(t, 1).wait()
        return carry

    lax.fori_loop(0, n_rows, start, 0, unroll=8)
    lax.fori_loop(0, n_rows, wait, 0, unroll=8)


DISPATCH_TILES = 4


def _dispatch_kernel(used_ref, src_ref, x_hbm, xs_ref, sem):
    half = xs_ref.shape[0] // 2
    first_tile = pl.program_id(0) * DISPATCH_TILES

    @pl.when(first_tile < used_ref[0])
    def _():
        _row_copies(lambda r, k: x_hbm.at[pl.ds(src_ref[0, k * half + r], 1)],
                    lambda r, k: xs_ref.at[pl.ds(k * half + r, 1)], half, sem)

    @pl.when(first_tile >= used_ref[0])
    def _():
        xs_ref[...] = jnp.zeros_like(xs_ref)


def _dispatch_call(n_used, src, x_packed):
    n_rows = src.shape[0] * MOE_TILE
    step_rows = DISPATCH_TILES * MOE_TILE
    width = x_packed.shape[1]
    grid_spec = pltpu.PrefetchScalarGridSpec(
        num_scalar_prefetch=1,
        grid=(n_rows // step_rows,),
        in_specs=[pl.BlockSpec((None, 1, step_rows), lambda t, nu: (t, 0, 0), memory_space=pltpu.SMEM),
                  pl.BlockSpec(memory_space=pl.ANY)],
        out_specs=pl.BlockSpec((step_rows, width), lambda t, nu: (t, 0)),
        scratch_shapes=[pltpu.SemaphoreType.DMA(())],
    )
    return pl.pallas_call(
        _dispatch_kernel,
        grid_spec=grid_spec,
        out_shape=jax.ShapeDtypeStruct((n_rows, width), x_packed.dtype),
        name="moe_dispatch",
        compiler_params=_cparams("arbitrary"),
    )(n_used, src.reshape(n_rows // step_rows, 1, step_rows), x_packed)


def _experts_kernel(pe_ref, pf_ref, pn_ref, pz_ref, xs_hbm, wg_ref, wu_ref, wd_ref, ys_hbm,
                    xa_buf, xb_buf, y_buf, stage_buf, wg_buf, wu_buf, wd_buf, load_sem, store_sem):
    del pe_ref
    p, j = pl.program_id(0), pl.program_id(1)
    last_j = pl.num_programs(1) - 1
    n_tiles = pn_ref[p]
    first = pf_ref[p]
    half = D_MODEL // 2

    def tile_rows(k):
        return pl.ds(pl.multiple_of((first + k) * MOE_TILE, MOE_TILE), MOE_TILE)

    def load(k):
        slot = lax.rem(k, 2)
        return pltpu.make_async_copy(xs_hbm.at[tile_rows(k)], stage_buf.at[slot], load_sem.at[slot])

    def store(k, src):
        return pltpu.make_async_copy(src, ys_hbm.at[tile_rows(k)], store_sem)

    def for_tiles(n, body):
        lax.fori_loop(0, n, lambda k, c: (body(k), c)[1], 0)

    @pl.when((pz_ref[p] > 0) & (j == 0))
    def _():
        y_buf[0] = jnp.zeros(y_buf.shape[1:], F32)
        for_tiles(pz_ref[p], lambda k: store(k, y_buf.at[0]).start())
        for_tiles(pz_ref[p], lambda k: store(k, y_buf.at[0]).wait())

    @pl.when(n_tiles > 0)
    def _():
        wg_buf[...] = wg_ref[...].astype(BF16)
        wu_buf[...] = wu_ref[...].astype(BF16)
        wd_buf[...] = wd_ref[...].astype(BF16)

        def swiglu(k):
            xa, xb = xa_buf[k], xb_buf[k]
            g = _dot(xa, wg_buf[:half, :]) + _dot(xb, wg_buf[half:, :])
            u = _dot(xa, wu_buf[:half, :]) + _dot(xb, wu_buf[half:, :])
            return _dot((g * jax.nn.sigmoid(g) * u).astype(BF16), wd_buf[...])

        @pl.when(j == 0)
        def _():
            load(0).start()

            def first_chunk(k):
                load(k).wait()

                @pl.when(k + 1 < n_tiles)
                def _():
                    load(k + 1).start()

                xa_buf[k], xb_buf[k] = _unpack_bf16_pair(stage_buf[lax.rem(k, 2)])
                y_buf[k] = swiglu(k)
            for_tiles(n_tiles, first_chunk)

        @pl.when((j > 0) & (j < last_j))
        def _():
            def middle_chunk(k):
                y_buf[k] += swiglu(k)
            for_tiles(n_tiles, middle_chunk)

        @pl.when(j == last_j)
        def _():
            def last_chunk(k):
                y_buf[k] += swiglu(k)
                store(k, y_buf.at[k]).start()
            for_tiles(n_tiles, last_chunk)
            for_tiles(n_tiles, lambda k: store(k, y_buf.at[k]).wait())


def _experts_call(pass_expert, pass_first, pass_tiles, pass_zero, xs, w_gate, w_up, w_down, moe_layer):
    n_passes = pass_expert.shape[0]
    d_ff = w_gate.shape[3]
    tf = d_ff // MOE_FF_CHUNKS
    half = D_MODEL // 2
    chunk = lambda p, j, pn: jnp.where(pn[p] > 0, j, MOE_FF_CHUNKS - 1)
    cols = pl.BlockSpec((None, None, D_MODEL, tf),
                        lambda p, j, pe, pf, pn, pz: (moe_layer, pe[p], 0, chunk(p, j, pn)))
    grid_spec = pltpu.PrefetchScalarGridSpec(
        num_scalar_prefetch=4,
        grid=(n_passes, MOE_FF_CHUNKS),
        in_specs=[pl.BlockSpec(memory_space=pl.ANY), cols, cols,
                  pl.BlockSpec((None, None, tf, D_MODEL),
                               lambda p, j, pe, pf, pn, pz: (moe_layer, pe[p], chunk(p, j, pn), 0))],
        out_specs=pl.BlockSpec(memory_space=pl.ANY),
        scratch_shapes=[pltpu.VMEM((MOE_PASS_TILES, MOE_TILE, half), BF16),
                        pltpu.VMEM((MOE_PASS_TILES, MOE_TILE, half), BF16),
                        pltpu.VMEM((MOE_PASS_TILES, MOE_TILE, D_MODEL), F32),
                        pltpu.VMEM((2, MOE_TILE, half), jnp.uint32),
                        pltpu.VMEM((D_MODEL, tf), BF16), pltpu.VMEM((D_MODEL, tf), BF16),
                        pltpu.VMEM((tf, D_MODEL), BF16),
                        pltpu.SemaphoreType.DMA((2,)), pltpu.SemaphoreType.DMA(())],
    )
    return pl.pallas_call(
        _experts_kernel,
        grid_spec=grid_spec,
        out_shape=jax.ShapeDtypeStruct((xs.shape[0], D_MODEL), F32),
        name="moe_experts",
        compiler_params=_cparams("arbitrary", "arbitrary"),
    )(pass_expert, pass_first, pass_tiles, pass_zero, xs, w_gate, w_up, w_down)


def _combine_kernel(pos_ref, ys_hbm, gate_ref, h_ref, g2_ref, lng_ref, lnb_ref, o_ref, buf_ref, sem):
    tm = pos_ref.shape[1]
    _row_copies(lambda t, k: ys_hbm.at[pl.ds(pos_ref[k, t], 1)],
                lambda t, k: buf_ref.at[k, pl.ds(t, 1)], tm, sem)
    gate = gate_ref[...]
    f = gate[:, 0:1] * buf_ref[0] + gate[:, 1:2] * buf_ref[1]
    o_ref[...] = _layer_norm(DEEPNORM_ALPHA * h_ref[...] + g2_ref[...] * f, lng_ref[...], lnb_ref[...])


def _combine_call(pos, ys, gate, h1, modr, layer, lng, lnb, *, batch, is_ctx):
    rows = h1.shape[0]
    nblk, _, tm = pos.shape
    nb = (rows // batch) // tm if not is_ctx else 1
    row_fn = (lambda i: batch) if is_ctx else (lambda i: i // nb)
    rowblk = lambda w: pl.BlockSpec((tm, w), lambda i: (i, 0))
    full = lambda shape: pl.BlockSpec(shape, lambda i: (0,) * len(shape))
    return pl.pallas_call(
        _combine_kernel,
        grid=(nblk,),
        in_specs=[pl.BlockSpec((None, 2, tm), lambda i: (i, 0, 0), memory_space=pltpu.SMEM),
                  pl.BlockSpec(memory_space=pl.ANY), rowblk(LANES), rowblk(D_MODEL),
                  _mod_spec(layer, 5, row_fn), full((1, D_MODEL)), full((1, D_MODEL))],
        out_specs=rowblk(D_MODEL),
        out_shape=jax.ShapeDtypeStruct((rows, D_MODEL), F32),
        scratch_shapes=[pltpu.VMEM((2, tm, D_MODEL), F32), pltpu.SemaphoreType.DMA(())],
        name="moe_combine",
        compiler_params=_cparams("arbitrary"),
    )(pos, ys, gate, h1, modr, lng, lnb)


def _moe_call(f_packed, h1, modr, layer, wr_t, b_r, tri, w_gate, w_up, w_down, moe_layer, lng, lnb,
              *, batch, is_ctx):
    rows = h1.shape[0]
    n_tiles = (2 * rows) // MOE_TILE + N_EXPERTS
    assert n_tiles % DISPATCH_TILES == 0
    n_passes = n_tiles // MOE_PASS_TILES + N_EXPERTS
    route, gate, cnt = _router_call(h1, modr, layer, wr_t, b_r, tri, batch=batch, is_ctx=is_ctx)
    pos, src, n_used, *passes = _route_plan(route, cnt[:, :, 0], n_tiles, n_passes)
    xs = _dispatch_call(n_used, src, f_packed)
    ys = _experts_call(*passes, xs, w_gate, w_up, w_down, moe_layer)
    return _combine_call(pos, ys, gate, h1, modr, layer, lng, lnb, batch=batch, is_ctx=is_ctx)


def _rope_tables(seq):
    pos = jnp.arange(seq, dtype=jnp.int32)
    row = (pos // GRID_W).astype(F32)
    col = (pos % GRID_W).astype(F32)
    n_freq = HEAD_DIM // 4
    inv_freq = ROPE_THETA ** (-jnp.arange(n_freq, dtype=F32) / n_freq)
    ang = jnp.concatenate([row[:, None] * inv_freq, col[:, None] * inv_freq], axis=-1)
    cos, sin = jnp.cos(ang), jnp.sin(ang)
    reps = LANES // (HEAD_DIM // 2)
    sign = jnp.tile(jnp.concatenate([-jnp.ones(HEAD_DIM // 2, F32), jnp.ones(HEAD_DIM // 2, F32)]),
                    LANES // HEAD_DIM)
    return jnp.tile(cos, (1, reps)), jnp.tile(sin, (1, reps)) * sign


def kernel(x, c, ctx, c_ctx, ada_w, ada_b, w_in, w_out, na_rpb, ga_q_norm, ga_k_norm,
           ssm_lambda_re, ssm_lambda_im, ssm_log_step, ssm_b_re, ssm_b_im, ssm_c_re, ssm_c_im,
           ssm_d, ssm_w_glu, sw_sink, ln1_g, ln1_b, ln2_g, ln2_b,
           ffn_w_gate, ffn_w_up, ffn_w_down,
           moe_w_router, moe_b_router, moe_w_gate, moe_w_up, moe_w_down):
    batch, seq, _ = x.shape
    n_ctx = ctx.shape[1]
    assert batch < MOD_ROWS and seq % 512 == 0 and n_ctx % 256 == 0

    c_all = jnp.zeros((MOD_ROWS, D_MODEL), F32).at[:batch].set(c).at[batch].set(c_ctx)
    modr = _mods_call(c_all, ada_w, ada_b)

    cos_t, sin_t = _rope_tables(seq)
    seg = jnp.kron(jnp.eye(256 // HEAD_DIM, dtype=F32), jnp.ones((HEAD_DIM, HEAD_DIM), F32)).astype(BF16)
    tri = (jnp.arange(MOE_TM)[:, None] < jnp.arange(MOE_TM)[None, :]).astype(BF16)
    sw_mask = _window_mask_table(seq)

    h = x.reshape(batch * seq, D_MODEL)
    hc = ctx.reshape(batch * n_ctx, D_MODEL)
    names = ("qa", "ka", "va", "qb", "kb", "vb", "u", "qd", "kd", "vd")

    for layer in range(DEPTH):
        need_ctx = layer < DEPTH - 1
        w_in_l = w_in[layer].astype(BF16)
        w_out_l = w_out[layer].astype(BF16)
        gq = jnp.tile(ga_q_norm[layer].astype(F32), 256 // HEAD_DIM)[None, :]
        gk = jnp.tile(ga_k_norm[layer].astype(F32), 128 // HEAD_DIM)[None, :]
        sink = sw_sink[layer].astype(F32)
        ln1 = (ln1_g[layer][None, :], ln1_b[layer][None, :])
        ln2 = (ln2_g[layer][None, :], ln2_b[layer][None, :])

        lat = dict(zip(names, _in_proj_call(h, modr, layer, w_in_l, cos_t, sin_t, seg, gq, gk,
                                            batch=batch, is_ctx=False)))
        cx = dict(zip(names, _in_proj_call(hc, modr, layer, w_in_l, cos_t, sin_t, seg, gq, gk,
                                           batch=batch, is_ctx=True)))

        na_bias, na_mask = _na_tables(na_rpb[layer], seq)
        out_a = _na_call(lat["qa"], lat["ka"], lat["va"], cx["ka"], cx["va"], na_bias, na_mask, batch=batch)
        out_b = _global_call(lat["qb"], lat["kb"], lat["vb"], cx["kb"], cx["vb"], batch=batch)
        out_d = _window_call(sink, lat["qd"], lat["kd"], lat["vd"], cx["kd"], cx["vd"], sw_mask, batch=batch)

        dirs = [_s5_operands(ssm_lambda_re[layer, d], ssm_lambda_im[layer, d], ssm_log_step[layer, d],
                             ssm_b_re[layer, d], ssm_b_im[layer, d], ssm_c_re[layer, d], ssm_c_im[layer, d])
                for d in range(2)]
        fwd = _s5_pass(cx["u"], lat["u"], None, *dirs[0], None, None, batch=batch, reverse=False)
        ctx_c, out_c = _s5_pass(cx["u"], lat["u"], fwd, *dirs[1], ssm_d[layer].astype(F32)[None, :],
                                ssm_w_glu[layer].astype(BF16), batch=batch, reverse=True)

        routed = layer % 2 == 1
        h1, f_in = _out_proj_call(out_a, out_b, out_c, out_d, w_out_l, h, modr, layer, *ln1,
                                  batch=batch, is_ctx=False, packed=routed)
        streams = [(h1, f_in, False)]
        if need_ctx:
            ctx_a, ctx_b, ctx_d = _ctx_attn_call(sink, cx, batch=batch)
            hc1, fc_in = _out_proj_call(ctx_a, ctx_b, ctx_c, ctx_d, w_out_l, hc, modr, layer, *ln1,
                                        batch=batch, is_ctx=True, packed=routed)
            streams.append((hc1, fc_in, True))

        i = layer // 2
        outs = []
        if not routed:
            wg, wu, wd = (ffn_w_gate[i].astype(BF16), ffn_w_up[i].astype(BF16), ffn_w_down[i].astype(BF16))
            for s1, sf, is_ctx in streams:
                outs.append(_ffn_call(sf, wg, wu, wd, s1, modr, layer, *ln2, batch=batch, is_ctx=is_ctx))
        else:
            wr_t = moe_w_router[i].astype(F32).T
            b_r = moe_b_router[i].astype(F32)[:, None]
            for s1, sf, is_ctx in streams:
                outs.append(_moe_call(sf, s1, modr, layer, wr_t, b_r, tri, moe_w_gate, moe_w_up, moe_w_down, i,
                                      *ln2, batch=batch, is_ctx=is_ctx))
        h = outs[0]
        if need_ctx:
            hc = outs[1]
    return h.reshape(batch, seq, D_MODEL)
```

```python
import functools

import jax
import jax.numpy as jnp
from jax import lax
from jax.experimental import pallas as pl
from jax.experimental.pallas import tpu as pltpu

F32 = jnp.float32
BF16 = jnp.bfloat16

D_MODEL = 1024
DEPTH = 4
GRID_W = 64
HEAD_DIM = 64
NA_ROWS = 8
NA_COLS = 16
SSM_GROUPS = 16
SSM_GROUP_CH = 16
SSM_STATE = 64
SSM_WIDTH = SSM_GROUPS * SSM_GROUP_CH
SSM_FLAT = SSM_GROUPS * SSM_STATE
SW_WINDOW = 128
N_EXPERTS = 8
ADA_CHUNKS = 6
ROPE_THETA = 10000.0
LN_EPS = 1e-6
RMS_EPS = 1e-6
NEG_INF = -1e30
DEEPNORM_ALPHA = (2 * DEPTH) ** 0.25
ATT_SCALE = HEAD_DIM ** -0.5

C_QA, C_KA, C_VA, C_QB, C_KB, C_VB, C_U, C_QD, C_KD, C_VD, IN_WIDTH = (
    0, 256, 512, 768, 1024, 1152, 1280, 1536, 1792, 1920, 2048)

LANES = 128
SUBLANES = 8
MOD_ROWS = 16
VMEM_LIMIT = 56 * 1024 * 1024


def _cparams(*sem):
    return pltpu.CompilerParams(dimension_semantics=sem, vmem_limit_bytes=VMEM_LIMIT)


def _dot(a, b):
    return jnp.dot(a, b, preferred_element_type=F32)


def _dot_nt(a, b):
    return lax.dot_general(a, b, (((1,), (1,)), ((), ())), preferred_element_type=F32)


def _dot_tn(a, b):
    return lax.dot_general(a, b, (((0,), (0,)), ((), ())), preferred_element_type=F32)


def _split_bf16(x):
    hi = x.astype(BF16)
    lo = (x - hi.astype(F32)).astype(BF16)
    return hi, lo


def _layer_norm(z, g, b):
    zc = z - jnp.mean(z, axis=-1, keepdims=True)
    y = zc * lax.rsqrt(jnp.mean(zc * zc, axis=-1, keepdims=True) + LN_EPS)
    return y * g + b


def _mods_kernel(c_ref, w_ref, b_ref, o_ref):
    c = c_ref[...]
    act = (c * jax.nn.sigmoid(c)).astype(BF16)
    o_ref[...] = _dot(act, w_ref[...].astype(BF16)) + b_ref[...]


def _mods_call(c_all, ada_w, ada_b):
    tn = 1536
    n = ADA_CHUNKS * D_MODEL
    out = pl.pallas_call(
        _mods_kernel,
        grid=(DEPTH, n // tn),
        in_specs=[
            pl.BlockSpec((MOD_ROWS, D_MODEL), lambda l, j: (0, 0)),
            pl.BlockSpec((None, D_MODEL, tn), lambda l, j: (l, 0, j)),
            pl.BlockSpec((None, 1, tn), lambda l, j: (l, 0, j)),
        ],
        out_specs=pl.BlockSpec((None, MOD_ROWS, tn), lambda l, j: (l, 0, j)),
        out_shape=jax.ShapeDtypeStruct((DEPTH, MOD_ROWS, n), F32),
        name="mods",
        compiler_params=_cparams("arbitrary", "arbitrary"),
    )(c_all, ada_w, ada_b.reshape(DEPTH, 1, n))
    return out.reshape(DEPTH * MOD_ROWS * ADA_CHUNKS, 1, D_MODEL)


def _mod_spec(layer, chunk, row_fn):
    def index(i, *_):
        return ((layer * MOD_ROWS + row_fn(i)) * ADA_CHUNKS + chunk, 0, 0)
    return pl.BlockSpec((None, 1, D_MODEL), index)


def _seg_rms(x, seg, g):
    hi, lo = _split_bf16(x * x)
    ss = _dot(hi, seg) + _dot(lo, seg)
    return x * lax.rsqrt(ss * (1.0 / HEAD_DIM) + RMS_EPS) * g


def _rope(x, cos, sin_signed, first_half):
    outs = []
    for j in range(x.shape[1] // LANES):
        xs = x[:, j * LANES:(j + 1) * LANES]
        partner = jnp.where(first_half,
                            pltpu.roll(xs, LANES - HEAD_DIM // 2, 1),
                            pltpu.roll(xs, HEAD_DIM // 2, 1))
        outs.append(xs * cos + partner * sin_signed)
    return outs[0] if len(outs) == 1 else jnp.concatenate(outs, axis=1)


def _pad_heads(x, lane, ones_lane=False):
    low = lane < HEAD_DIM
    fill = jnp.where(lane == HEAD_DIM, 1.0, 0.0) if ones_lane else 0.0
    outs = []
    for j in range(x.shape[1] // LANES):
        xs = x[:, j * LANES:(j + 1) * LANES]
        outs.append(jnp.where(low, xs, fill))
        outs.append(jnp.where(low, pltpu.roll(xs, HEAD_DIM, 1), fill))
    return jnp.concatenate(outs, axis=1).astype(BF16)


def _in_proj_kernel(x_ref, sh_ref, sc_ref, w_ref, cos_ref, sin_ref, seg_ref, gq_ref, gk_ref,
                    qa_ref, ka_ref, va_ref, qb_ref, kb_ref, vb_ref, u_ref, qd_ref, kd_ref, vd_ref,
                    *, rope):
    a = (x_ref[...] * (1.0 + sc_ref[...]) + sh_ref[...]).astype(BF16)
    lane = lax.broadcasted_iota(jnp.int32, (a.shape[0], LANES), 1)

    def proj(c0, c1):
        return _dot(a, w_ref[:, c0:c1])

    qa_ref[...] = _pad_heads(proj(C_QA, C_KA) * ATT_SCALE, lane)
    ka_ref[...] = _pad_heads(proj(C_KA, C_VA), lane)
    va_ref[...] = _pad_heads(proj(C_VA, C_QB), lane, ones_lane=True)
    vb_ref[...] = _pad_heads(proj(C_VB, C_U), lane, ones_lane=True)
    u_ref[...] = proj(C_U, C_QD)
    vd_ref[...] = _pad_heads(proj(C_VD, IN_WIDTH), lane, ones_lane=True)

    seg = seg_ref[...]
    qb = _seg_rms(proj(C_QB, C_KB), seg, gq_ref[...])
    kb = _seg_rms(proj(C_KB, C_VB), seg[:LANES, :LANES], gk_ref[...])
    qd = proj(C_QD, C_KD)
    kd = proj(C_KD, C_VD)
    if rope:
        cos = cos_ref[...]
        sin = sin_ref[...]
        first_half = (lane % HEAD_DIM) < (HEAD_DIM // 2)
        qb = _rope(qb, cos, sin, first_half)
        kb = _rope(kb, cos, sin, first_half)
        qd = _rope(qd, cos, sin, first_half)
        kd = _rope(kd, cos, sin, first_half)
    qb_ref[...] = _pad_heads(qb * ATT_SCALE, lane)
    kb_ref[...] = _pad_heads(kb, lane)
    qd_ref[...] = _pad_heads(qd * ATT_SCALE, lane)
    kd_ref[...] = _pad_heads(kd, lane)


def _in_proj_call(x, modr, layer, w_in, cos_t, sin_t, seg, gq, gk, *, batch, is_ctx):
    rows = x.shape[0]
    per_batch = rows // batch
    tm = 256 if is_ctx else 512
    nb = per_batch // tm
    row_fn = (lambda i: batch) if is_ctx else (lambda i: i // nb)
    table_spec = pl.BlockSpec((tm, LANES), (lambda i: (0, 0)) if is_ctx else (lambda i: (i % nb, 0)))
    full = lambda shape: pl.BlockSpec(shape, lambda i: (0,) * len(shape))
    rowblk = lambda w: pl.BlockSpec((tm, w), lambda i: (i, 0))
    wide = lambda w, dt: jax.ShapeDtypeStruct((rows, w), dt)
    out_shapes = [wide(512, BF16), wide(512, BF16), wide(512, BF16),
                  wide(512, BF16), wide(256, BF16), wide(256, BF16),
                  jax.ShapeDtypeStruct((per_batch, batch * SSM_WIDTH), F32),
                  wide(512, BF16), wide(256, BF16), wide(256, BF16)]
    out_specs = [rowblk(512), rowblk(512), rowblk(512), rowblk(512), rowblk(256), rowblk(256),
                 pl.BlockSpec((tm, SSM_WIDTH), lambda i: (i % nb, i // nb)),
                 rowblk(512), rowblk(256), rowblk(256)]
    return pl.pallas_call(
        functools.partial(_in_proj_kernel, rope=not is_ctx),
        grid=(rows // tm,),
        in_specs=[rowblk(D_MODEL), _mod_spec(layer, 0, row_fn), _mod_spec(layer, 1, row_fn),
                  full((D_MODEL, IN_WIDTH)), table_spec, table_spec, full((256, 256)),
                  full((1, 256)), full((1, 128))],
        out_specs=out_specs,
        out_shape=out_shapes,
        name="in_proj_ctx" if is_ctx else "in_proj",
        compiler_params=_cparams("arbitrary"),
    )(x, modr, modr, w_in, cos_t, sin_t, seg, gq, gk)


KV_CHUNK = 256


def _tile(h):
    return slice(h * LANES, (h + 1) * LANES)


def _attend(q, parts, sink=None):
    def score(k_fn, mod_fn, c):
        s = _dot_nt(q, k_fn(c))
        return s if mod_fn is None else mod_fn(s, c)

    tiles = [(k_fn, v_fn, mod_fn, c) for k_fn, v_fn, n, mod_fn in parts for c in range(0, n, KV_CHUNK)]
    m = sink
    acc = None
    for k_fn, v_fn, mod_fn, c in tiles:
        s = score(k_fn, mod_fn, c)
        m_tile = jnp.max(s, axis=-1, keepdims=True)
        m_new = m_tile if m is None else jnp.maximum(m, m_tile)
        pv = _dot(jnp.exp(s - m_new).astype(BF16), v_fn(c))
        acc = pv if acc is None else acc * jnp.exp(m - m_new) + pv
        m = m_new
    den = acc[:, HEAD_DIM:HEAD_DIM + 1]
    if sink is not None:
        den = den + jnp.exp(sink - m)
    return acc * (1.0 / den)


def _pack_pair(a, b):
    lane = lax.broadcasted_iota(jnp.int32, a.shape, 1)
    return jnp.where(lane < HEAD_DIM, a, pltpu.roll(b, HEAD_DIM, 1)).astype(BF16)


def _ref_part(k_ref, v_ref, t, n_keys, start=0, mod_fn=None):
    return (lambda c: k_ref[pl.ds(start + c, KV_CHUNK), _tile(t)],
            lambda c: v_ref[pl.ds(start + c, KV_CHUNK), _tile(t)], n_keys, mod_fn)


def _gqa_group(q_ref, g, parts, sink_pair=None):
    tq = q_ref.shape[0]
    q2 = jnp.concatenate([q_ref[:, _tile(2 * g)], q_ref[:, _tile(2 * g + 1)]], axis=0)
    sink = None
    if sink_pair is not None:
        row = lax.broadcasted_iota(jnp.int32, (2 * tq, 1), 0)
        sink = jnp.where(row < tq, sink_pair[0], sink_pair[1])
    o2 = _attend(q2, parts, sink)
    return _pack_pair(o2[:tq], o2[tq:])


NA_QROWS = 4
NA_BAND = 12


def _na_band_start(j, rows):
    return jnp.clip(j * NA_QROWS - NA_ROWS // 2, 0, rows - NA_BAND)


def _na_kernel(q_ref, k_ref, v_ref, kc_ref, vc_ref, bias_ref, mask_ref, o_ref):
    j = pl.program_id(1)
    rows = k_ref.shape[0] // GRID_W
    start = pl.multiple_of(_na_band_start(j, rows) * GRID_W, GRID_W)
    outs = []
    for h in range(4):
        def local_scores(s, c, h=h):
            cols = slice(c, c + KV_CHUNK)
            return jnp.where(mask_ref[:, cols] > 0.5, s + bias_ref[h, :, cols], NEG_INF)

        parts = [_ref_part(k_ref, v_ref, h, NA_BAND * GRID_W, start, local_scores),
                 _ref_part(kc_ref, vc_ref, h, kc_ref.shape[0])]
        outs.append(_attend(q_ref[:, _tile(h)], parts))
    o_ref[:, _tile(0)] = _pack_pair(outs[0], outs[1])
    o_ref[:, _tile(1)] = _pack_pair(outs[2], outs[3])


def _na_call(qa, ka, va, ka_c, va_c, bias_tab, mask_tab, *, batch):
    seq = qa.shape[0] // batch
    n_ctx = ka_c.shape[0] // batch
    nj = seq // (NA_QROWS * GRID_W)
    tq = NA_QROWS * GRID_W
    nk = NA_BAND * GRID_W

    def pattern(j):
        return jnp.where(j == 0, 0, jnp.where(j == nj - 1, 2, 1))

    return pl.pallas_call(
        _na_kernel,
        grid=(batch, nj),
        in_specs=[
            pl.BlockSpec((tq, 512), lambda b, j: (b * nj + j, 0)),
            pl.BlockSpec((seq, 512), lambda b, j: (b, 0)),
            pl.BlockSpec((seq, 512), lambda b, j: (b, 0)),
            pl.BlockSpec((n_ctx, 512), lambda b, j: (b, 0)),
            pl.BlockSpec((n_ctx, 512), lambda b, j: (b, 0)),
            pl.BlockSpec((None, 4, tq, nk), lambda b, j: (pattern(j), 0, 0, 0)),
            pl.BlockSpec((None, tq, nk), lambda b, j: (pattern(j), 0, 0)),
        ],
        out_specs=pl.BlockSpec((tq, 256), lambda b, j: (b * nj + j, 0)),
        out_shape=jax.ShapeDtypeStruct((qa.shape[0], 256), BF16),
        name="na_attn",
        compiler_params=_cparams("arbitrary", "arbitrary"),
    )(qa, ka, va, ka_c, va_c, bias_tab, mask_tab)


def _na_tables(rpb, seq):
    rows = seq // GRID_W
    nj = rows // NA_QROWS
    c_idx = jnp.arange(GRID_W)
    col_start = jnp.clip(c_idx - NA_COLS // 2, 0, GRID_W - NA_COLS)
    col_valid = (c_idx[None, :] >= col_start[:, None]) & (c_idx[None, :] < col_start[:, None] + NA_COLS)
    col_rel = jnp.clip(c_idx[None, :] - c_idx[:, None], 1 - NA_COLS, NA_COLS - 1) + NA_COLS - 1
    blocks = jnp.array([0, 1, nj - 1])
    q_row = blocks[:, None] * NA_QROWS + jnp.arange(NA_QROWS)[None, :]
    k_row = _na_band_start(blocks, rows)[:, None] + jnp.arange(NA_BAND)[None, :]
    row_start = jnp.clip(q_row - NA_ROWS // 2, 0, rows - NA_ROWS)
    row_valid = ((k_row[:, None, :] >= row_start[:, :, None])
                 & (k_row[:, None, :] < row_start[:, :, None] + NA_ROWS))
    row_rel = jnp.clip(k_row[:, None, :] - q_row[:, :, None] + NA_ROWS - 1, 0, 2 * NA_ROWS - 2)
    row_sel = jax.nn.one_hot(row_rel, 2 * NA_ROWS - 1, dtype=F32)
    col_sel = jax.nn.one_hot(col_rel, 2 * NA_COLS - 1, dtype=F32)
    exact = lax.Precision.HIGHEST
    by_col = jnp.einsum("hab,qkb->haqk", rpb.astype(F32), col_sel, precision=exact)
    bias = jnp.einsum("pria,haqk->phrqik", row_sel, by_col, precision=exact)
    valid = row_valid[:, :, None, :, None] & col_valid[None, None, :, None, :]
    tq, nk = NA_QROWS * GRID_W, NA_BAND * GRID_W
    return bias.reshape(3, 4, tq, nk), valid.astype(F32).reshape(3, tq, nk)


def _global_kernel(q_ref, k_ref, v_ref, kc_ref, vc_ref, o_ref):
    for g in range(2):
        parts = [_ref_part(kc_ref, vc_ref, g, kc_ref.shape[0]), _ref_part(k_ref, v_ref, g, k_ref.shape[0])]
        o_ref[:, _tile(g)] = _gqa_group(q_ref, g, parts)


def _global_call(qb, kb, vb, kb_c, vb_c, *, batch):
    seq = qb.shape[0] // batch
    n_ctx = kb_c.shape[0] // batch
    tq = 256
    nq = seq // tq
    return pl.pallas_call(
        _global_kernel,
        grid=(batch, nq),
        in_specs=[
            pl.BlockSpec((tq, 512), lambda b, i: (b * nq + i, 0)),
            pl.BlockSpec((seq, 256), lambda b, i: (b, 0)),
            pl.BlockSpec((seq, 256), lambda b, i: (b, 0)),
            pl.BlockSpec((n_ctx, 256), lambda b, i: (b, 0)),
            pl.BlockSpec((n_ctx, 256), lambda b, i: (b, 0)),
        ],
        out_specs=pl.BlockSpec((tq, 256), lambda b, i: (b * nq + i, 0)),
        out_shape=jax.ShapeDtypeStruct((qb.shape[0], 256), BF16),
        name="global_attn",
        compiler_params=_cparams("arbitrary", "arbitrary"),
    )(qb, kb, vb, kb_c, vb_c)


SW_TQ = 256
SW_SPAN = SW_TQ + 2 * SW_WINDOW


def _window_start(i, seq):
    return jnp.clip(i * SW_TQ - SW_WINDOW, 0, seq - SW_SPAN)


def _window_kernel(sink_ref, q_ref, k_ref, v_ref, kc_ref, vc_ref, mask_ref, o_ref):
    i = pl.program_id(1)
    start = pl.multiple_of(_window_start(i, k_ref.shape[0]), LANES)

    def in_window(s, c):
        return jnp.where(mask_ref[:, c:c + KV_CHUNK] > 0.5, s, NEG_INF)

    for g in range(2):
        parts = [_ref_part(k_ref, v_ref, g, SW_SPAN, start, in_window),
                 _ref_part(kc_ref, vc_ref, g, kc_ref.shape[0])]
        o_ref[:, _tile(g)] = _gqa_group(q_ref, g, parts, sink_pair=(sink_ref[2 * g], sink_ref[2 * g + 1]))


def _window_mask_table(seq):
    nq = seq // SW_TQ
    blocks = jnp.array([0, 1, nq - 1])
    qpos = blocks[:, None] * SW_TQ + jnp.arange(SW_TQ)[None, :]
    kpos = _window_start(blocks, seq)[:, None] + jnp.arange(SW_SPAN)[None, :]
    valid = jnp.abs(kpos[:, None, :] - qpos[:, :, None]) <= SW_WINDOW
    return jnp.tile(valid.astype(F32), (1, 2, 1))


def _window_call(sink, qd, kd, vd, kd_c, vd_c, mask_tab, *, batch):
    seq = qd.shape[0] // batch
    n_ctx = kd_c.shape[0] // batch
    nq = seq // SW_TQ

    def pattern(i):
        return jnp.where(i == 0, 0, jnp.where(i == nq - 1, 2, 1))

    return pl.pallas_call(
        _window_kernel,
        grid=(batch, nq),
        in_specs=[
            pl.BlockSpec(memory_space=pltpu.SMEM),
            pl.BlockSpec((SW_TQ, 512), lambda b, i: (b * nq + i, 0)),
            pl.BlockSpec((seq, 256), lambda b, i: (b, 0)),
            pl.BlockSpec((seq, 256), lambda b, i: (b, 0)),
            pl.BlockSpec((n_ctx, 256), lambda b, i: (b, 0)),
            pl.BlockSpec((n_ctx, 256), lambda b, i: (b, 0)),
            pl.BlockSpec((None, 2 * SW_TQ, SW_SPAN), lambda b, i: (pattern(i), 0, 0)),
        ],
        out_specs=pl.BlockSpec((SW_TQ, 256), lambda b, i: (b * nq + i, 0)),
        out_shape=jax.ShapeDtypeStruct((qd.shape[0], 256), BF16),
        name="window_attn",
        compiler_params=_cparams("arbitrary", "arbitrary"),
    )(sink, qd, kd, vd, kd_c, vd_c, mask_tab)


def _ctx_attn_kernel(sink_ref, qa_ref, ka_ref, va_ref, qb_ref, kb_ref, vb_ref, qd_ref, kd_ref, vd_ref,
                     oa_ref, ob_ref, od_ref):
    n = ka_ref.shape[0]
    outs = [_attend(qa_ref[:, _tile(h)], [_ref_part(ka_ref, va_ref, h, n)]) for h in range(4)]
    oa_ref[:, _tile(0)] = _pack_pair(outs[0], outs[1])
    oa_ref[:, _tile(1)] = _pack_pair(outs[2], outs[3])
    for g in range(2):
        ob_ref[:, _tile(g)] = _gqa_group(qb_ref, g, [_ref_part(kb_ref, vb_ref, g, n)])
        od_ref[:, _tile(g)] = _gqa_group(qd_ref, g, [_ref_part(kd_ref, vd_ref, g, n)],
                                         sink_pair=(sink_ref[2 * g], sink_ref[2 * g + 1]))


def _ctx_attn_call(sink, cx, *, batch):
    n_ctx = cx["qa"].shape[0] // batch
    blk = lambda w: pl.BlockSpec((n_ctx, w), lambda b: (b, 0))
    names = ("qa", "ka", "va", "qb", "kb", "vb", "qd", "kd", "vd")
    widths = (512, 512, 512, 512, 256, 256, 512, 256, 256)
    shape = jax.ShapeDtypeStruct((cx["qa"].shape[0], 256), BF16)
    return pl.pallas_call(
        _ctx_attn_kernel,
        grid=(batch,),
        in_specs=[pl.BlockSpec(memory_space=pltpu.SMEM)] + [blk(w) for w in widths],
        out_specs=[blk(256)] * 3,
        out_shape=[shape] * 3,
        name="ctx_attn",
        compiler_params=_cparams("arbitrary"),
    )(sink, *[cx[n] for n in names])


def _s5_discretise(lam_re, lam_im, log_step, b_re, b_im):
    step = jnp.exp(log_step.astype(F32))[:, None]
    lam_re = lam_re.astype(F32)
    lam_im = lam_im.astype(F32)
    mag = jnp.exp(lam_re * step)
    ab_re = mag * jnp.cos(lam_im * step)
    ab_im = mag * jnp.sin(lam_im * step)
    den = lam_re * lam_re + lam_im * lam_im
    num_re = ab_re - 1.0
    f_re = ((num_re * lam_re + ab_im * lam_im) / den)[..., None]
    f_im = ((ab_im * lam_re - num_re * lam_im) / den)[..., None]
    b_re = b_re.astype(F32)
    b_im = b_im.astype(F32)
    return ab_re, ab_im, f_re * b_re - f_im * b_im, f_re * b_im + f_im * b_re


def _s5_operands(lam_re, lam_im, log_step, b_re, b_im, c_re, c_im):
    ab_re, ab_im, bb_re, bb_im = _s5_discretise(lam_re, lam_im, log_step, b_re, b_im)
    eye = jnp.eye(SSM_GROUPS, dtype=F32)
    bd_in = lambda m: jnp.einsum("gph,gk->ghkp", m, eye).reshape(SSM_WIDTH, SSM_FLAT)
    bd_out = lambda m: jnp.einsum("ghp,gk->kpgh", m.astype(F32), eye).reshape(SSM_FLAT, SSM_WIDTH)
    a = jnp.stack([ab_re.reshape(SSM_FLAT), ab_im.reshape(SSM_FLAT)])
    bbd = jnp.concatenate([bd_in(bb_re), bd_in(bb_im)], axis=1).astype(BF16)
    cbd = jnp.concatenate([bd_out(c_re), -bd_out(c_im)], axis=0).astype(BF16)
    return a, bbd, cbd


def _s5_kernel(*refs, reverse, final, batch, n_ctx_chunks):
    if final:
        (uc_ref, ul_ref, pc_ref, pl_ref, a_ref, bbd_ref, cbd_ref, d_ref, wglu_ref,
         oc_ref, ol_ref, bu_ref, h_ref) = refs
    else:
        uc_ref, ul_ref, a_ref, bbd_ref, cbd_ref, oc_ref, ol_ref, bu_ref, h_ref = refs
        pc_ref = pl_ref = d_ref = wglu_ref = None
    c = pl.program_id(0)
    steps = bu_ref.shape[0] // batch

    @pl.when(c == 0)
    def _():
        h_ref[...] = jnp.zeros_like(h_ref)

    def run(u_ref, prev_ref, o_ref):
        u = u_ref[...]
        bu_ref[...] = _dot(u.astype(BF16), bbd_ref[...])
        ar = jnp.broadcast_to(a_ref[0:1, :], (batch, SSM_FLAT))
        ai = jnp.broadcast_to(a_ref[1:2, :], (batch, SSM_FLAT))

        def step(i, h):
            t = steps - 1 - i if reverse else i
            row = pl.multiple_of(t * batch, batch)
            b = bu_ref[pl.ds(row, batch), :]
            hr, hi = h[:, :SSM_FLAT], h[:, SSM_FLAT:]
            nr = ar * hr - ai * hi + b[:, :SSM_FLAT]
            ni = ar * hi + ai * hr + b[:, SSM_FLAT:]
            hn = jnp.concatenate([nr, ni], axis=1)
            bu_ref[pl.ds(row, batch), :] = hn
            return hn

        h_ref[...] = lax.fori_loop(0, steps, step, h_ref[...], unroll=True)
        y = _dot(bu_ref[...].astype(BF16), cbd_ref[...])
        if not final:
            o_ref[...] = y
        else:
            y = jax.nn.gelu(y + prev_ref[...] + d_ref[...] * u)
            z = _dot(y.astype(BF16), wglu_ref[...])
            o_ref[...] = (y * jax.nn.sigmoid(z)).astype(BF16)

    @pl.when(c < n_ctx_chunks)
    def _():
        run(uc_ref, pc_ref, oc_ref)

    @pl.when(c >= n_ctx_chunks)
    def _():
        run(ul_ref, pl_ref, ol_ref)


def _s5_pass(u_ctx, u_lat, prev, a, bbd, cbd, d_skip, w_glu, *, batch, reverse):
    final = prev is not None
    t_chunk = 128
    rows = t_chunk * batch
    ncc = u_ctx.shape[0] // rows
    nlc = u_lat.shape[0] // rows
    if reverse:
        ctx_idx = lambda c: (jnp.maximum(ncc - 1 - c, 0), 0)
        lat_idx = lambda c: (nlc - 1 - jnp.maximum(c - ncc, 0), 0)
    else:
        ctx_idx = lambda c: (jnp.minimum(c, ncc - 1), 0)
        lat_idx = lambda c: (jnp.maximum(c - ncc, 0), 0)
    full = lambda shape: pl.BlockSpec(shape, lambda c: (0,) * len(shape))
    u_specs = [pl.BlockSpec((rows, SSM_WIDTH), ctx_idx), pl.BlockSpec((rows, SSM_WIDTH), lat_idx)]
    par_specs = [full((2, SSM_FLAT)), full((SSM_WIDTH, 2 * SSM_FLAT)), full((2 * SSM_FLAT, SSM_WIDTH))]
    args = [u_ctx, u_lat]
    in_specs = list(u_specs)
    if final:
        args += list(prev)
        in_specs += u_specs
    args += [a, bbd, cbd]
    in_specs += par_specs
    if final:
        args += [d_skip, w_glu]
        in_specs += [full((1, SSM_WIDTH)), full((SSM_WIDTH, SSM_WIDTH))]
    dt = BF16 if final else F32
    return pl.pallas_call(
        functools.partial(_s5_kernel, reverse=reverse, final=final, batch=batch, n_ctx_chunks=ncc),
        grid=(ncc + nlc,),
        in_specs=in_specs,
        out_specs=u_specs,
        out_shape=[jax.ShapeDtypeStruct(u_ctx.shape, dt), jax.ShapeDtypeStruct(u_lat.shape, dt)],
        scratch_shapes=[pltpu.VMEM((rows, 2 * SSM_FLAT), F32), pltpu.VMEM((batch, 2 * SSM_FLAT), F32)],
        name="s5_bwd_glu" if final else "s5_fwd",
        compiler_params=_cparams("arbitrary"),
    )(*args)


def _bf16_bits(x):
    return lax.bitcast_convert_type(x.astype(BF16).astype(F32), jnp.uint32)


def _pack_bf16_pair(a, b):
    return _bf16_bits(a) | lax.shift_right_logical(_bf16_bits(b), jnp.uint32(16))


def _unpack_bf16_pair(u):
    a = lax.bitcast_convert_type(u & jnp.uint32(0xFFFF0000), F32)
    b = lax.bitcast_convert_type(lax.shift_left(u, jnp.uint32(16)), F32)
    return a.astype(BF16), b.astype(BF16)


def _out_proj_kernel(oa_ref, ob_ref, oc_ref, od_ref, w_ref, x_ref, g1_ref, sh2_ref, sc2_ref,
                     lng_ref, lnb_ref, h_ref, f_ref, *, packed):
    y = (_dot(oa_ref[...], w_ref[0:256, :]) + _dot(ob_ref[...], w_ref[256:512, :])
         + _dot(oc_ref[...], w_ref[512:768, :]) + _dot(od_ref[...], w_ref[768:1024, :]))
    h1 = _layer_norm(DEEPNORM_ALPHA * x_ref[...] + g1_ref[...] * y, lng_ref[...], lnb_ref[...])
    h_ref[...] = h1
    f = h1 * (1.0 + sc2_ref[...]) + sh2_ref[...]
    if packed:
        f_ref[...] = _pack_bf16_pair(f[:, :D_MODEL // 2], f[:, D_MODEL // 2:])
    else:
        f_ref[...] = f.astype(BF16)


def _out_proj_call(oa, ob, oc_tm, od, w_out, x, modr, layer, lng, lnb, *, batch, is_ctx, packed):
    rows = x.shape[0]
    per_batch = rows // batch
    tm = 256 if is_ctx else 512
    nb = per_batch // tm
    row_fn = (lambda i: batch) if is_ctx else (lambda i: i // nb)
    rowblk = lambda w: pl.BlockSpec((tm, w), lambda i: (i, 0))
    full = lambda shape: pl.BlockSpec(shape, lambda i: (0,) * len(shape))
    f_width, f_dtype = (D_MODEL // 2, jnp.uint32) if packed else (D_MODEL, BF16)
    return pl.pallas_call(
        functools.partial(_out_proj_kernel, packed=packed),
        grid=(rows // tm,),
        in_specs=[rowblk(256), rowblk(256),
                  pl.BlockSpec((tm, SSM_WIDTH), lambda i: (i % nb, i // nb)),
                  rowblk(256), full((D_MODEL, D_MODEL)), rowblk(D_MODEL),
                  _mod_spec(layer, 2, row_fn), _mod_spec(layer, 3, row_fn), _mod_spec(layer, 4, row_fn),
                  full((1, D_MODEL)), full((1, D_MODEL))],
        out_specs=[rowblk(D_MODEL), rowblk(f_width)],
        out_shape=[jax.ShapeDtypeStruct((rows, D_MODEL), F32), jax.ShapeDtypeStruct((rows, f_width), f_dtype)],
        name="out_proj_ctx" if is_ctx else "out_proj",
        compiler_params=_cparams("arbitrary"),
    )(oa, ob, oc_tm, od, w_out, x, modr, modr, modr, lng, lnb)


def _ffn_kernel(f_ref, wg_ref, wu_ref, wd_ref, h_ref, g2_ref, lng_ref, lnb_ref, o_ref, acc_ref):
    j = pl.program_id(1)

    @pl.when(j == 0)
    def _():
        acc_ref[...] = jnp.zeros_like(acc_ref)

    f = f_ref[...]
    g = _dot(f, wg_ref[...])
    u = _dot(f, wu_ref[...])
    acc_ref[...] += _dot((g * jax.nn.sigmoid(g) * u).astype(BF16), wd_ref[...])

    @pl.when(j == pl.num_programs(1) - 1)
    def _():
        o_ref[...] = _layer_norm(DEEPNORM_ALPHA * h_ref[...] + g2_ref[...] * acc_ref[...],
                                 lng_ref[...], lnb_ref[...])


def _ffn_call(f_in, wg, wu, wd, h1, modr, layer, lng, lnb, *, batch, is_ctx):
    rows = h1.shape[0]
    per_batch = rows // batch
    tm = 512
    d_ff = wg.shape[1]
    tf = d_ff // 2
    nb = per_batch // tm
    row_fn = (lambda i: batch) if is_ctx else (lambda i: i // nb)
    rowblk = lambda w: pl.BlockSpec((tm, w), lambda i, j: (i, 0))
    full = lambda shape: pl.BlockSpec(shape, lambda i, j: (0,) * len(shape))
    return pl.pallas_call(
        _ffn_kernel,
        grid=(rows // tm, d_ff // tf),
        in_specs=[rowblk(D_MODEL),
                  pl.BlockSpec((D_MODEL, tf), lambda i, j: (0, j)),
                  pl.BlockSpec((D_MODEL, tf), lambda i, j: (0, j)),
                  pl.BlockSpec((tf, D_MODEL), lambda i, j: (j, 0)),
                  rowblk(D_MODEL), _mod_spec(layer, 5, row_fn), full((1, D_MODEL)), full((1, D_MODEL))],
        out_specs=rowblk(D_MODEL),
        out_shape=jax.ShapeDtypeStruct((rows, D_MODEL), F32),
        scratch_shapes=[pltpu.VMEM((tm, D_MODEL), F32)],
        name="ffn_ctx" if is_ctx else "ffn",
        compiler_params=_cparams("arbitrary", "arbitrary"),
    )(f_in, wg, wu, wd, h1, modr, lng, lnb)


MOE_TM = 1024
MOE_TILE = 512
MOE_PASS_TILES = 10
MOE_FF_CHUNKS = 7
ROUTE_ROWS = 8


def _router_kernel(h_ref, sh2_ref, sc2_ref, wr_ref, br_ref, tri_ref, route_ref, gate_ref, cnt_ref):
    tm = h_ref.shape[0]
    f = h_ref[...] * (1.0 + sc2_ref[...]) + sh2_ref[...]
    f_hi, f_lo = _split_bf16(f)
    w_hi, w_lo = _split_bf16(wr_ref[...])
    logits = _dot_nt(w_hi, f_hi) + _dot_nt(w_hi, f_lo) + _dot_nt(w_lo, f_hi) + br_ref[...]
    ie = lax.broadcasted_iota(jnp.int32, logits.shape, 0)
    m1 = jnp.max(logits, axis=0, keepdims=True)
    i1 = jnp.min(jnp.where(logits == m1, ie, N_EXPERTS), axis=0, keepdims=True)
    rest = jnp.where(ie == i1, -jnp.inf, logits)
    m2 = jnp.max(rest, axis=0, keepdims=True)
    i2 = jnp.min(jnp.where(rest == m2, ie, N_EXPERTS), axis=0, keepdims=True)
    e2 = jnp.exp(m2 - m1)
    den = 1.0 + e2
    sel = jnp.where((ie == i1) | (ie == i2), 1.0, 0.0)
    rank = _dot(sel.astype(BF16), tri_ref[...])
    r1 = jnp.sum(jnp.where(ie == i1, rank, 0.0), axis=0, keepdims=True).astype(jnp.int32)
    r2 = jnp.sum(jnp.where(ie == i2, rank, 0.0), axis=0, keepdims=True).astype(jnp.int32)
    row = lax.broadcasted_iota(jnp.int32, (ROUTE_ROWS, tm), 0)
    route_ref[...] = jnp.where(row == 0, i1, jnp.where(row == 1, i2, jnp.where(row == 2, r1,
                               jnp.where(row == 3, r2, 0))))
    lrow = lax.broadcasted_iota(jnp.int32, (LANES, tm), 0)
    gate_ref[...] = jnp.where(lrow == 0, 1.0 / den, jnp.where(lrow == 1, e2 / den, 0.0)).T
    cnt = jnp.sum(sel, axis=1, keepdims=True)
    cnt_ref[...] = jnp.broadcast_to(cnt, cnt_ref.shape).astype(jnp.int32)


def _router_call(h1, modr, layer, wr_t, b_r, tri, *, batch, is_ctx):
    rows = h1.shape[0]
    tm = MOE_TM
    nblk = rows // tm
    nb = (rows // batch) // tm if not is_ctx else 1
    row_fn = (lambda i: batch) if is_ctx else (lambda i: i // nb)
    full = lambda shape: pl.BlockSpec(shape, lambda i: (0,) * len(shape))
    return pl.pallas_call(
        _router_kernel,
        grid=(nblk,),
        in_specs=[pl.BlockSpec((tm, D_MODEL), lambda i: (i, 0)),
                  _mod_spec(layer, 3, row_fn), _mod_spec(layer, 4, row_fn),
                  full((N_EXPERTS, D_MODEL)), full((N_EXPERTS, 1)), full((tm, tm))],
        out_specs=[pl.BlockSpec((None, ROUTE_ROWS, tm), lambda i: (i, 0, 0)),
                   pl.BlockSpec((tm, LANES), lambda i: (i, 0)),
                   pl.BlockSpec((None, N_EXPERTS, LANES), lambda i: (i, 0, 0))],
        out_shape=[jax.ShapeDtypeStruct((nblk, ROUTE_ROWS, tm), jnp.int32),
                   jax.ShapeDtypeStruct((rows, LANES), F32),
                   jax.ShapeDtypeStruct((nblk, N_EXPERTS, LANES), jnp.int32)],
        name="router_ctx" if is_ctx else "router",
        compiler_params=_cparams("arbitrary"),
    )(h1, modr, modr, wr_t, b_r, tri)


def _route_plan(route, cnt, n_tiles, n_passes):
    total = jnp.sum(cnt, axis=0)
    tiles = lax.div(total + (MOE_TILE - 1), MOE_TILE)
    tile_start = jnp.cumsum(tiles) - tiles
    base = tile_start[None, :] * MOE_TILE + jnp.cumsum(cnt, axis=0) - cnt
    choice = jax.nn.one_hot(route[:, 0:2, :], N_EXPERTS, dtype=jnp.int32)
    pos = jnp.sum(choice * base[:, None, None, :], axis=-1) + route[:, 2:4, :]
    nblk, _, tm = pos.shape
    token = jnp.broadcast_to((jnp.arange(nblk)[:, None, None] * tm + jnp.arange(tm)[None, None, :]), pos.shape)
    src = jnp.zeros((n_tiles * MOE_TILE,), jnp.int32).at[pos.reshape(-1)].set(
        token.reshape(-1).astype(jnp.int32), unique_indices=True, mode="promise_in_bounds")
    src = src.reshape(n_tiles, 1, MOE_TILE)
    passes = lax.div(tiles + (MOE_PASS_TILES - 1), MOE_PASS_TILES)
    pass_end = jnp.cumsum(passes)
    p = jnp.arange(n_passes, dtype=jnp.int32)
    owner = jnp.sum((p[:, None] >= pass_end[None, :]).astype(jnp.int32), axis=1)
    last = jnp.sum((pass_end[-1] - 1 >= pass_end).astype(jnp.int32))
    expert = jnp.where(p < pass_end[-1], owner, last)
    within = (p - (pass_end - passes)[expert]) * MOE_PASS_TILES
    n = jnp.where(p < pass_end[-1], jnp.clip(tiles[expert] - within, 0, MOE_PASS_TILES), 0)
    first = tile_start[expert] + within
    used = jnp.sum(tiles)
    first = jnp.where(p == pass_end[-1], used, first)
    n_zero = jnp.where(p == pass_end[-1], n_tiles - used, 0)
    i32 = lambda a: a.astype(jnp.int32)
    return i32(pos), src, i32(used).reshape(1), i32(expert), i32(first), i32(n), i32(n_zero)


def _row_copies(src_of, dst_of, n_rows, sem):
    def copy(t, k):
        return pltpu.make_async_copy(src_of(t, k), dst_of(t, k), sem)

    def start(t, carry):
        copy(t, 0).start(priority=0)
        copy(t, 1).start(priority=1)
        return carry

    def wait(t, carry):
        copy(t, 0).wait()
        copy(t, 1).wait()
        return carry

    lax.fori_loop(0, n_rows, start, 0, unroll=8)
    lax.fori_loop(0, n_rows, wait, 0, unroll=8)


DISPATCH_TILES = 4


def _dispatch_kernel(used_ref, src_ref, x_hbm, xs_ref, sem):
    half = xs_ref.shape[0] // 2
    first_tile = pl.program_id(0) * DISPATCH_TILES

    @pl.when(first_tile < used_ref[0])
    def _():
        _row_copies(lambda r, k: x_hbm.at[pl.ds(src_ref[0, k * half + r], 1)],
                    lambda r, k: xs_ref.at[pl.ds(k * half + r, 1)], half, sem)

    @pl.when(first_tile >= used_ref[0])
    def _():
        xs_ref[...] = jnp.zeros_like(xs_ref)


def _dispatch_call(n_used, src, x_packed):
    n_rows = src.shape[0] * MOE_TILE
    step_rows = DISPATCH_TILES * MOE_TILE
    width = x_packed.shape[1]
    grid_spec = pltpu.PrefetchScalarGridSpec(
        num_scalar_prefetch=1,
        grid=(n_rows // step_rows,),
        in_specs=[pl.BlockSpec((None, 1, step_rows), lambda t, nu: (t, 0, 0), memory_space=pltpu.SMEM),
                  pl.BlockSpec(memory_space=pl.ANY)],
        out_specs=pl.BlockSpec((step_rows, width), lambda t, nu: (t, 0)),
        scratch_shapes=[pltpu.SemaphoreType.DMA(())],
    )
    return pl.pallas_call(
        _dispatch_kernel,
        grid_spec=grid_spec,
        out_shape=jax.ShapeDtypeStruct((n_rows, width), x_packed.dtype),
        name="moe_dispatch",
        compiler_params=_cparams("arbitrary"),
    )(n_used, src.reshape(n_rows // step_rows, 1, step_rows), x_packed)


def _experts_kernel(pe_ref, pf_ref, pn_ref, pz_ref, xs_hbm, wg_ref, wu_ref, wd_ref, ys_hbm,
                    xa_buf, xb_buf, y_buf, stage_buf, wg_buf, wu_buf, wd_buf, load_sem, store_sem):
    del pe_ref
    p, j = pl.program_id(0), pl.program_id(1)
    last_j = pl.num_programs(1) - 1
    n_tiles = pn_ref[p]
    first = pf_ref[p]
    half = D_MODEL // 2

    def tile_rows(k):
        return pl.ds(pl.multiple_of((first + k) * MOE_TILE, MOE_TILE), MOE_TILE)

    def load(k):
        slot = lax.rem(k, 2)
        return pltpu.make_async_copy(xs_hbm.at[tile_rows(k)], stage_buf.at[slot], load_sem.at[slot])

    def store(k, src):
        return pltpu.make_async_copy(src, ys_hbm.at[tile_rows(k)], store_sem)

    def for_tiles(n, body):
        lax.fori_loop(0, n, lambda k, c: (body(k), c)[1], 0)

    @pl.when((pz_ref[p] > 0) & (j == 0))
    def _():
        y_buf[0] = jnp.zeros(y_buf.shape[1:], F32)
        for_tiles(pz_ref[p], lambda k: store(k, y_buf.at[0]).start())
        for_tiles(pz_ref[p], lambda k: store(k, y_buf.at[0]).wait())

    @pl.when(n_tiles > 0)
    def _():
        wg_buf[...] = wg_ref[...].astype(BF16)
        wu_buf[...] = wu_ref[...].astype(BF16)
        wd_buf[...] = wd_ref[...].astype(BF16)

        def swiglu(k):
            xa, xb = xa_buf[k], xb_buf[k]
            g = _dot(xa, wg_buf[:half, :]) + _dot(xb, wg_buf[half:, :])
            u = _dot(xa, wu_buf[:half, :]) + _dot(xb, wu_buf[half:, :])
            return _dot((g * jax.nn.sigmoid(g) * u).astype(BF16), wd_buf[...])

        @pl.when(j == 0)
        def _():
            load(0).start()

            def first_chunk(k):
                load(k).wait()

                @pl.when(k + 1 < n_tiles)
                def _():
                    load(k + 1).start()

                xa_buf[k], xb_buf[k] = _unpack_bf16_pair(stage_buf[lax.rem(k, 2)])
                y_buf[k] = swiglu(k)
            for_tiles(n_tiles, first_chunk)

        @pl.when((j > 0) & (j < last_j))
        def _():
            def middle_chunk(k):
                y_buf[k] += swiglu(k)
            for_tiles(n_tiles, middle_chunk)

        @pl.when(j == last_j)
        def _():
            def last_chunk(k):
                y_buf[k] += swiglu(k)
                store(k, y_buf.at[k]).start()
            for_tiles(n_tiles, last_chunk)
            for_tiles(n_tiles, lambda k: store(k, y_buf.at[k]).wait())


def _experts_call(pass_expert, pass_first, pass_tiles, pass_zero, xs, w_gate, w_up, w_down, moe_layer):
    n_passes = pass_expert.shape[0]
    d_ff = w_gate.shape[3]
    tf = d_ff // MOE_FF_CHUNKS
    half = D_MODEL // 2
    chunk = lambda p, j, pn: jnp.where(pn[p] > 0, j, MOE_FF_CHUNKS - 1)
    cols = pl.BlockSpec((None, None, D_MODEL, tf),
                        lambda p, j, pe, pf, pn, pz: (moe_layer, pe[p], 0, chunk(p, j, pn)))
    grid_spec = pltpu.PrefetchScalarGridSpec(
        num_scalar_prefetch=4,
        grid=(n_passes, MOE_FF_CHUNKS),
        in_specs=[pl.BlockSpec(memory_space=pl.ANY), cols, cols,
                  pl.BlockSpec((None, None, tf, D_MODEL),
                               lambda p, j, pe, pf, pn, pz: (moe_layer, pe[p], chunk(p, j, pn), 0))],
        out_specs=pl.BlockSpec(memory_space=pl.ANY),
        scratch_shapes=[pltpu.VMEM((MOE_PASS_TILES, MOE_TILE, half), BF16),
                        pltpu.VMEM((MOE_PASS_TILES, MOE_TILE, half), BF16),
                        pltpu.VMEM((MOE_PASS_TILES, MOE_TILE, D_MODEL), F32),
                        pltpu.VMEM((2, MOE_TILE, half), jnp.uint32),
                        pltpu.VMEM((D_MODEL, tf), BF16), pltpu.VMEM((D_MODEL, tf), BF16),
                        pltpu.VMEM((tf, D_MODEL), BF16),
                        pltpu.SemaphoreType.DMA((2,)), pltpu.SemaphoreType.DMA(())],
    )
    return pl.pallas_call(
        _experts_kernel,
        grid_spec=grid_spec,
        out_shape=jax.ShapeDtypeStruct((xs.shape[0], D_MODEL), F32),
        name="moe_experts",
        compiler_params=_cparams("arbitrary", "arbitrary"),
    )(pass_expert, pass_first, pass_tiles, pass_zero, xs, w_gate, w_up, w_down)


def _combine_kernel(pos_ref, ys_hbm, gate_ref, h_ref, g2_ref, lng_ref, lnb_ref, o_ref, buf_ref, sem):
    tm = pos_ref.shape[1]
    _row_copies(lambda t, k: ys_hbm.at[pl.ds(pos_ref[k, t], 1)],
                lambda t, k: buf_ref.at[k, pl.ds(t, 1)], tm, sem)
    gate = gate_ref[...]
    f = gate[:, 0:1] * buf_ref[0] + gate[:, 1:2] * buf_ref[1]
    o_ref[...] = _layer_norm(DEEPNORM_ALPHA * h_ref[...] + g2_ref[...] * f, lng_ref[...], lnb_ref[...])


def _combine_call(pos, ys, gate, h1, modr, layer, lng, lnb, *, batch, is_ctx):
    rows = h1.shape[0]
    nblk, _, tm = pos.shape
    nb = (rows // batch) // tm if not is_ctx else 1
    row_fn = (lambda i: batch) if is_ctx else (lambda i: i // nb)
    rowblk = lambda w: pl.BlockSpec((tm, w), lambda i: (i, 0))
    full = lambda shape: pl.BlockSpec(shape, lambda i: (0,) * len(shape))
    return pl.pallas_call(
        _combine_kernel,
        grid=(nblk,),
        in_specs=[pl.BlockSpec((None, 2, tm), lambda i: (i, 0, 0), memory_space=pltpu.SMEM),
                  pl.BlockSpec(memory_space=pl.ANY), rowblk(LANES), rowblk(D_MODEL),
                  _mod_spec(layer, 5, row_fn), full((1, D_MODEL)), full((1, D_MODEL))],
        out_specs=rowblk(D_MODEL),
        out_shape=jax.ShapeDtypeStruct((rows, D_MODEL), F32),
        scratch_shapes=[pltpu.VMEM((2, tm, D_MODEL), F32), pltpu.SemaphoreType.DMA(())],
        name="moe_combine",
        compiler_params=_cparams("arbitrary"),
    )(pos, ys, gate, h1, modr, lng, lnb)


def _moe_call(f_packed, h1, modr, layer, wr_t, b_r, tri, w_gate, w_up, w_down, moe_layer, lng, lnb,
              *, batch, is_ctx):
    rows = h1.shape[0]
    n_tiles = (2 * rows) // MOE_TILE + N_EXPERTS
    assert n_tiles % DISPATCH_TILES == 0
    n_passes = n_tiles // MOE_PASS_TILES + N_EXPERTS
    route, gate, cnt = _router_call(h1, modr, layer, wr_t, b_r, tri, batch=batch, is_ctx=is_ctx)
    pos, src, n_used, *passes = _route_plan(route, cnt[:, :, 0], n_tiles, n_passes)
    xs = _dispatch_call(n_used, src, f_packed)
    ys = _experts_call(*passes, xs, w_gate, w_up, w_down, moe_layer)
    return _combine_call(pos, ys, gate, h1, modr, layer, lng, lnb, batch=batch, is_ctx=is_ctx)


def _rope_tables(seq):
    pos = jnp.arange(seq, dtype=jnp.int32)
    row = (pos // GRID_W).astype(F32)
    col = (pos % GRID_W).astype(F32)
    n_freq = HEAD_DIM // 4
    inv_freq = ROPE_THETA ** (-jnp.arange(n_freq, dtype=F32) / n_freq)
    ang = jnp.concatenate([row[:, None] * inv_freq, col[:, None] * inv_freq], axis=-1)
    cos, sin = jnp.cos(ang), jnp.sin(ang)
    reps = LANES // (HEAD_DIM // 2)
    sign = jnp.tile(jnp.concatenate([-jnp.ones(HEAD_DIM // 2, F32), jnp.ones(HEAD_DIM // 2, F32)]),
                    LANES // HEAD_DIM)
    return jnp.tile(cos, (1, reps)), jnp.tile(sin, (1, reps)) * sign


def kernel(x, c, ctx, c_ctx, ada_w, ada_b, w_in, w_out, na_rpb, ga_q_norm, ga_k_norm,
           ssm_lambda_re, ssm_lambda_im, ssm_log_step, ssm_b_re, ssm_b_im, ssm_c_re, ssm_c_im,
           ssm_d, ssm_w_glu, sw_sink, ln1_g, ln1_b, ln2_g, ln2_b,
           ffn_w_gate, ffn_w_up, ffn_w_down,
           moe_w_router, moe_b_router, moe_w_gate, moe_w_up, moe_w_down):
    batch, seq, _ = x.shape
    n_ctx = ctx.shape[1]
    assert batch < MOD_ROWS and seq % 512 == 0 and n_ctx % 256 == 0

    c_all = jnp.zeros((MOD_ROWS, D_MODEL), F32).at[:batch].set(c).at[batch].set(c_ctx)
    modr = _mods_call(c_all, ada_w, ada_b)

    cos_t, sin_t = _rope_tables(seq)
    seg = jnp.kron(jnp.eye(256 // HEAD_DIM, dtype=F32), jnp.ones((HEAD_DIM, HEAD_DIM), F32)).astype(BF16)
    tri = (jnp.arange(MOE_TM)[:, None] < jnp.arange(MOE_TM)[None, :]).astype(BF16)
    sw_mask = _window_mask_table(seq)

    h = x.reshape(batch * seq, D_MODEL)
    hc = ctx.reshape(batch * n_ctx, D_MODEL)
    names = ("qa", "ka", "va", "qb", "kb", "vb", "u", "qd", "kd", "vd")

    for layer in range(DEPTH):
        need_ctx = layer < DEPTH - 1
        w_in_l = w_in[layer].astype(BF16)
        w_out_l = w_out[layer].astype(BF16)
        gq = jnp.tile(ga_q_norm[layer].astype(F32), 256 // HEAD_DIM)[None, :]
        gk = jnp.tile(ga_k_norm[layer].astype(F32), 128 // HEAD_DIM)[None, :]
        sink = sw_sink[layer].astype(F32)
        ln1 = (ln1_g[layer][None, :], ln1_b[layer][None, :])
        ln2 = (ln2_g[layer][None, :], ln2_b[layer][None, :])

        lat = dict(zip(names, _in_proj_call(h, modr, layer, w_in_l, cos_t, sin_t, seg, gq, gk,
                                            batch=batch, is_ctx=False)))
        cx = dict(zip(names, _in_proj_call(hc, modr, layer, w_in_l, cos_t, sin_t, seg, gq, gk,
                                           batch=batch, is_ctx=True)))

        na_bias, na_mask = _na_tables(na_rpb[layer], seq)
        out_a = _na_call(lat["qa"], lat["ka"], lat["va"], cx["ka"], cx["va"], na_bias, na_mask, batch=batch)
        out_b = _global_call(lat["qb"], lat["kb"], lat["vb"], cx["kb"], cx["vb"], batch=batch)
        out_d = _window_call(sink, lat["qd"], lat["kd"], lat["vd"], cx["kd"], cx["vd"], sw_mask, batch=batch)

        dirs = [_s5_operands(ssm_lambda_re[layer, d], ssm_lambda_im[layer, d], ssm_log_step[layer, d],
                             ssm_b_re[layer, d], ssm_b_im[layer, d], ssm_c_re[layer, d], ssm_c_im[layer, d])
                for d in range(2)]
        rows_tb = lambda a: a.reshape(a.shape[0] * batch, SSM_WIDTH)
        u_tb = (rows_tb(cx["u"]), rows_tb(lat["u"]))
        fwd = _s5_pass(*u_tb, None, *dirs[0], None, None, batch=batch, reverse=False)
        ctx_c, out_c = _s5_pass(*u_tb, fwd, *dirs[1], ssm_d[layer].astype(F32)[None, :],
                                ssm_w_glu[layer].astype(BF16), batch=batch, reverse=True)
        ctx_c, out_c = ctx_c.reshape(cx["u"].shape), out_c.reshape(lat["u"].shape)

        routed = layer % 2 == 1
        h1, f_in = _out_proj_call(out_a, out_b, out_c, out_d, w_out_l, h, modr, layer, *ln1,
                                  batch=batch, is_ctx=False, packed=routed)
        streams = [(h1, f_in, False)]
        if need_ctx:
            ctx_a, ctx_b, ctx_d = _ctx_attn_call(sink, cx, batch=batch)
            hc1, fc_in = _out_proj_call(ctx_a, ctx_b, ctx_c, ctx_d, w_out_l, hc, modr, layer, *ln1,
                                        batch=batch, is_ctx=True, packed=routed)
            streams.append((hc1, fc_in, True))

        i = layer // 2
        outs = []
        if not routed:
            wg, wu, wd = (ffn_w_gate[i].astype(BF16), ffn_w_up[i].astype(BF16), ffn_w_down[i].astype(BF16))
            for s1, sf, is_ctx in streams:
                outs.append(_ffn_call(sf, wg, wu, wd, s1, modr, layer, *ln2, batch=batch, is_ctx=is_ctx))
        else:
            wr_t = moe_w_router[i].astype(F32).T
            b_r = moe_b_router[i].astype(F32)[:, None]
            for s1, sf, is_ctx in streams:
                outs.append(_moe_call(sf, s1, modr, layer, wr_t, b_r, tri, moe_w_gate, moe_w_up, moe_w_down, i,
                                      *ln2, batch=batch, is_ctx=is_ctx))
        h = outs[0]
        if need_ctx:
            hc = outs[1]
    return h.reshape(batch, seq, D_MODEL)
```

```python
import functools

import jax
import jax.numpy as jnp
from jax import lax
from jax.experimental import pallas as pl
from jax.experimental.pallas import tpu as pltpu

F32 = jnp.float32
BF16 = jnp.bfloat16

D_MODEL = 1024
DEPTH = 4
GRID_W = 64
HEAD_DIM = 64
NA_ROWS = 8
NA_COLS = 16
SSM_GROUPS = 16
SSM_GROUP_CH = 16
SSM_STATE = 64
SSM_WIDTH = SSM_GROUPS * SSM_GROUP_CH
SSM_FLAT = SSM_GROUPS * SSM_STATE
SW_WINDOW = 128
N_EXPERTS = 8
ADA_CHUNKS = 6
ROPE_THETA = 10000.0
LN_EPS = 1e-6
RMS_EPS = 1e-6
NEG_INF = -1e30
DEEPNORM_ALPHA = (2 * DEPTH) ** 0.25
ATT_SCALE = HEAD_DIM ** -0.5

C_QA, C_KA, C_VA, C_QB, C_KB, C_VB, C_U, C_QD, C_KD, C_VD, IN_WIDTH = (
    0, 256, 512, 768, 1024, 1152, 1280, 1536, 1792, 1920, 2048)

LANES = 128
SUBLANES = 8
MOD_ROWS = 16
VMEM_LIMIT = 56 * 1024 * 1024


def _cparams(*sem):
    return pltpu.CompilerParams(dimension_semantics=sem, vmem_limit_bytes=VMEM_LIMIT)


def _dot(a, b):
    return jnp.dot(a, b, preferred_element_type=F32)


def _dot_nt(a, b):
    return lax.dot_general(a, b, (((1,), (1,)), ((), ())), preferred_element_type=F32)


def _dot_tn(a, b):
    return lax.dot_general(a, b, (((0,), (0,)), ((), ())), preferred_element_type=F32)


def _split_bf16(x):
    hi = x.astype(BF16)
    lo = (x - hi.astype(F32)).astype(BF16)
    return hi, lo


def _layer_norm(z, g, b):
    zc = z - jnp.mean(z, axis=-1, keepdims=True)
    y = zc * lax.rsqrt(jnp.mean(zc * zc, axis=-1, keepdims=True) + LN_EPS)
    return y * g + b


def _mods_kernel(c_ref, w_ref, b_ref, o_ref):
    c = c_ref[...]
    act = (c * jax.nn.sigmoid(c)).astype(BF16)
    o_ref[...] = _dot(act, w_ref[...].astype(BF16)) + b_ref[...]


def _mods_call(c_all, ada_w, ada_b):
    tn = 1536
    n = ADA_CHUNKS * D_MODEL
    out = pl.pallas_call(
        _mods_kernel,
        grid=(DEPTH, n // tn),
        in_specs=[
            pl.BlockSpec((MOD_ROWS, D_MODEL), lambda l, j: (0, 0)),
            pl.BlockSpec((None, D_MODEL, tn), lambda l, j: (l, 0, j)),
            pl.BlockSpec((None, 1, tn), lambda l, j: (l, 0, j)),
        ],
        out_specs=pl.BlockSpec((None, MOD_ROWS, tn), lambda l, j: (l, 0, j)),
        out_shape=jax.ShapeDtypeStruct((DEPTH, MOD_ROWS, n), F32),
        name="mods",
        compiler_params=_cparams("arbitrary", "arbitrary"),
    )(c_all, ada_w, ada_b.reshape(DEPTH, 1, n))
    return out.reshape(DEPTH * MOD_ROWS * ADA_CHUNKS, 1, D_MODEL)


def _mod_spec(layer, chunk, row_fn):
    def index(i, *_):
        return ((layer * MOD_ROWS + row_fn(i)) * ADA_CHUNKS + chunk, 0, 0)
    return pl.BlockSpec((None, 1, D_MODEL), index)


def _seg_rms(x, seg, g):
    hi, lo = _split_bf16(x * x)
    ss = _dot(hi, seg) + _dot(lo, seg)
    return x * lax.rsqrt(ss * (1.0 / HEAD_DIM) + RMS_EPS) * g


def _rope(x, cos, sin_signed, first_half):
    outs = []
    for j in range(x.shape[1] // LANES):
        xs = x[:, j * LANES:(j + 1) * LANES]
        partner = jnp.where(first_half,
                            pltpu.roll(xs, LANES - HEAD_DIM // 2, 1),
                            pltpu.roll(xs, HEAD_DIM // 2, 1))
        outs.append(xs * cos + partner * sin_signed)
    return outs[0] if len(outs) == 1 else jnp.concatenate(outs, axis=1)


def _pad_heads(x, lane, ones_lane=False):
    low = lane < HEAD_DIM
    fill = jnp.where(lane == HEAD_DIM, 1.0, 0.0) if ones_lane else 0.0
    outs = []
    for j in range(x.shape[1] // LANES):
        xs = x[:, j * LANES:(j + 1) * LANES]
        outs.append(jnp.where(low, xs, fill))
        outs.append(jnp.where(low, pltpu.roll(xs, HEAD_DIM, 1), fill))
    return jnp.concatenate(outs, axis=1).astype(BF16)


def _in_proj_kernel(x_ref, sh_ref, sc_ref, w_ref, cos_ref, sin_ref, seg_ref, gq_ref, gk_ref,
                    qa_ref, ka_ref, va_ref, qb_ref, kb_ref, vb_ref, u_ref, qd_ref, kd_ref, vd_ref,
                    *, rope):
    a = (x_ref[...] * (1.0 + sc_ref[...]) + sh_ref[...]).astype(BF16)
    lane = lax.broadcasted_iota(jnp.int32, (a.shape[0], LANES), 1)

    def proj(c0, c1):
        return _dot(a, w_ref[:, c0:c1])

    qa_ref[...] = _pad_heads(proj(C_QA, C_KA) * ATT_SCALE, lane)
    ka_ref[...] = _pad_heads(proj(C_KA, C_VA), lane)
    va_ref[...] = _pad_heads(proj(C_VA, C_QB), lane, ones_lane=True)
    vb_ref[...] = _pad_heads(proj(C_VB, C_U), lane, ones_lane=True)
    u_ref[...] = proj(C_U, C_QD)
    vd_ref[...] = _pad_heads(proj(C_VD, IN_WIDTH), lane, ones_lane=True)

    seg = seg_ref[...]
    qb = _seg_rms(proj(C_QB, C_KB), seg, gq_ref[...])
    kb = _seg_rms(proj(C_KB, C_VB), seg[:LANES, :LANES], gk_ref[...])
    qd = proj(C_QD, C_KD)
    kd = proj(C_KD, C_VD)
    if rope:
        cos = cos_ref[...]
        sin = sin_ref[...]
        first_half = (lane % HEAD_DIM) < (HEAD_DIM // 2)
        qb = _rope(qb, cos, sin, first_half)
        kb = _rope(kb, cos, sin, first_half)
        qd = _rope(qd, cos, sin, first_half)
        kd = _rope(kd, cos, sin, first_half)
    qb_ref[...] = _pad_heads(qb * ATT_SCALE, lane)
    kb_ref[...] = _pad_heads(kb, lane)
    qd_ref[...] = _pad_heads(qd * ATT_SCALE, lane)
    kd_ref[...] = _pad_heads(kd, lane)


def _in_proj_call(x, modr, layer, w_in, cos_t, sin_t, seg, gq, gk, *, batch, is_ctx, x_start=0, rows=None):
    rows = x.shape[0] if rows is None else rows
    per_batch = rows // batch
    tm = 256 if is_ctx else 512
    nb = per_batch // tm
    row_fn = (lambda i: batch) if is_ctx else (lambda i: i // nb)
    table_spec = pl.BlockSpec((tm, LANES), (lambda i: (0, 0)) if is_ctx else (lambda i: (i % nb, 0)))
    full = lambda shape: pl.BlockSpec(shape, lambda i: (0,) * len(shape))
    rowblk = lambda w: pl.BlockSpec((tm, w), lambda i: (i, 0))
    wide = lambda w, dt: jax.ShapeDtypeStruct((rows, w), dt)
    out_shapes = [wide(512, BF16), wide(512, BF16), wide(512, BF16),
                  wide(512, BF16), wide(256, BF16), wide(256, BF16),
                  jax.ShapeDtypeStruct((per_batch, batch * SSM_WIDTH), F32),
                  wide(512, BF16), wide(256, BF16), wide(256, BF16)]
    out_specs = [rowblk(512), rowblk(512), rowblk(512), rowblk(512), rowblk(256), rowblk(256),
                 pl.BlockSpec((tm, SSM_WIDTH), lambda i: (i % nb, i // nb)),
                 rowblk(512), rowblk(256), rowblk(256)]
    return pl.pallas_call(
        functools.partial(_in_proj_kernel, rope=not is_ctx),
        grid=(rows // tm,),
        in_specs=[pl.BlockSpec((tm, D_MODEL), lambda i: (x_start // tm + i, 0)),
                  _mod_spec(layer, 0, row_fn), _mod_spec(layer, 1, row_fn),
                  full((D_MODEL, IN_WIDTH)), table_spec, table_spec, full((256, 256)),
                  full((1, 256)), full((1, 128))],
        out_specs=out_specs,
        out_shape=out_shapes,
        name="in_proj_ctx" if is_ctx else "in_proj",
        compiler_params=_cparams("arbitrary"),
    )(x, modr, modr, w_in, cos_t, sin_t, seg, gq, gk)


KV_CHUNK = 256


def _tile(h):
    return slice(h * LANES, (h + 1) * LANES)


def _attend(q, parts, sink=None):
    def score(k_fn, mod_fn, c):
        s = _dot_nt(q, k_fn(c))
        return s if mod_fn is None else mod_fn(s, c)

    tiles = [(k_fn, v_fn, mod_fn, c) for k_fn, v_fn, n, mod_fn in parts for c in range(0, n, KV_CHUNK)]
    m = sink
    acc = None
    for k_fn, v_fn, mod_fn, c in tiles:
        s = score(k_fn, mod_fn, c)
        m_tile = jnp.max(s, axis=-1, keepdims=True)
        m_new = m_tile if m is None else jnp.maximum(m, m_tile)
        pv = _dot(jnp.exp(s - m_new).astype(BF16), v_fn(c))
        acc = pv if acc is None else acc * jnp.exp(m - m_new) + pv
        m = m_new
    den = acc[:, HEAD_DIM:HEAD_DIM + 1]
    if sink is not None:
        den = den + jnp.exp(sink - m)
    return acc * (1.0 / den)


def _pack_pair(a, b):
    lane = lax.broadcasted_iota(jnp.int32, a.shape, 1)
    return jnp.where(lane < HEAD_DIM, a, pltpu.roll(b, HEAD_DIM, 1)).astype(BF16)


def _ref_part(k_ref, v_ref, t, n_keys, start=0, mod_fn=None):
    return (lambda c: k_ref[pl.ds(start + c, KV_CHUNK), _tile(t)],
            lambda c: v_ref[pl.ds(start + c, KV_CHUNK), _tile(t)], n_keys, mod_fn)


def _gqa_group(q_ref, g, parts, sink_pair=None):
    tq = q_ref.shape[0]
    q2 = jnp.concatenate([q_ref[:, _tile(2 * g)], q_ref[:, _tile(2 * g + 1)]], axis=0)
    sink = None
    if sink_pair is not None:
        row = lax.broadcasted_iota(jnp.int32, (2 * tq, 1), 0)
        sink = jnp.where(row < tq, sink_pair[0], sink_pair[1])
    o2 = _attend(q2, parts, sink)
    return _pack_pair(o2[:tq], o2[tq:])


NA_QROWS = 4
NA_BAND = 12


def _na_band_start(j, rows):
    return jnp.clip(j * NA_QROWS - NA_ROWS // 2, 0, rows - NA_BAND)


def _na_kernel(q_ref, k_ref, v_ref, kc_ref, vc_ref, bias_ref, mask_ref, o_ref):
    j = pl.program_id(1)
    rows = k_ref.shape[0] // GRID_W
    start = pl.multiple_of(_na_band_start(j, rows) * GRID_W, GRID_W)
    outs = []
    for h in range(4):
        def local_scores(s, c, h=h):
            cols = slice(c, c + KV_CHUNK)
            return jnp.where(mask_ref[:, cols] > 0.5, s + bias_ref[h, :, cols], NEG_INF)

        parts = [_ref_part(k_ref, v_ref, h, NA_BAND * GRID_W, start, local_scores),
                 _ref_part(kc_ref, vc_ref, h, kc_ref.shape[0])]
        outs.append(_attend(q_ref[:, _tile(h)], parts))
    o_ref[:, _tile(0)] = _pack_pair(outs[0], outs[1])
    o_ref[:, _tile(1)] = _pack_pair(outs[2], outs[3])


def _na_call(qa, ka, va, ka_c, va_c, bias_tab, mask_tab, *, batch):
    seq = qa.shape[0] // batch
    n_ctx = ka_c.shape[0] // batch
    nj = seq // (NA_QROWS * GRID_W)
    tq = NA_QROWS * GRID_W
    nk = NA_BAND * GRID_W

    def pattern(j):
        return jnp.where(j == 0, 0, jnp.where(j == nj - 1, 2, 1))

    return pl.pallas_call(
        _na_kernel,
        grid=(batch, nj),
        in_specs=[
            pl.BlockSpec((tq, 512), lambda b, j: (b * nj + j, 0)),
            pl.BlockSpec((seq, 512), lambda b, j: (b, 0)),
            pl.BlockSpec((seq, 512), lambda b, j: (b, 0)),
            pl.BlockSpec((n_ctx, 512), lambda b, j: (b, 0)),
            pl.BlockSpec((n_ctx, 512), lambda b, j: (b, 0)),
            pl.BlockSpec((None, 4, tq, nk), lambda b, j: (pattern(j), 0, 0, 0)),
            pl.BlockSpec((None, tq, nk), lambda b, j: (pattern(j), 0, 0)),
        ],
        out_specs=pl.BlockSpec((tq, 256), lambda b, j: (b * nj + j, 0)),
        out_shape=jax.ShapeDtypeStruct((qa.shape[0], 256), BF16),
        name="na_attn",
        compiler_params=_cparams("arbitrary", "arbitrary"),
    )(qa, ka, va, ka_c, va_c, bias_tab, mask_tab)


def _na_tables(rpb, seq):
    rows = seq // GRID_W
    nj = rows // NA_QROWS
    c_idx = jnp.arange(GRID_W)
    col_start = jnp.clip(c_idx - NA_COLS // 2, 0, GRID_W - NA_COLS)
    col_valid = (c_idx[None, :] >= col_start[:, None]) & (c_idx[None, :] < col_start[:, None] + NA_COLS)
    col_rel = jnp.clip(c_idx[None, :] - c_idx[:, None], 1 - NA_COLS, NA_COLS - 1) + NA_COLS - 1
    blocks = jnp.array([0, 1, nj - 1])
    q_row = blocks[:, None] * NA_QROWS + jnp.arange(NA_QROWS)[None, :]
    k_row = _na_band_start(blocks, rows)[:, None] + jnp.arange(NA_BAND)[None, :]
    row_start = jnp.clip(q_row - NA_ROWS // 2, 0, rows - NA_ROWS)
    row_valid = ((k_row[:, None, :] >= row_start[:, :, None])
                 & (k_row[:, None, :] < row_start[:, :, None] + NA_ROWS))
    row_rel = jnp.clip(k_row[:, None, :] - q_row[:, :, None] + NA_ROWS - 1, 0, 2 * NA_ROWS - 2)
    row_sel = jax.nn.one_hot(row_rel, 2 * NA_ROWS - 1, dtype=F32)
    col_sel = jax.nn.one_hot(col_rel, 2 * NA_COLS - 1, dtype=F32)
    exact = lax.Precision.HIGHEST
    by_col = jnp.einsum("hab,qkb->haqk", rpb.astype(F32), col_sel, precision=exact)
    bias = jnp.einsum("pria,haqk->phrqik", row_sel, by_col, precision=exact)
    valid = row_valid[:, :, None, :, None] & col_valid[None, None, :, None, :]
    tq, nk = NA_QROWS * GRID_W, NA_BAND * GRID_W
    return bias.reshape(3, 4, tq, nk), valid.astype(F32).reshape(3, tq, nk)


def _global_kernel(q_ref, k_ref, v_ref, kc_ref, vc_ref, o_ref):
    for g in range(2):
        parts = [_ref_part(kc_ref, vc_ref, g, kc_ref.shape[0]), _ref_part(k_ref, v_ref, g, k_ref.shape[0])]
        o_ref[:, _tile(g)] = _gqa_group(q_ref, g, parts)


def _global_call(qb, kb, vb, kb_c, vb_c, *, batch):
    seq = qb.shape[0] // batch
    n_ctx = kb_c.shape[0] // batch
    tq = 256
    nq = seq // tq
    return pl.pallas_call(
        _global_kernel,
        grid=(batch, nq),
        in_specs=[
            pl.BlockSpec((tq, 512), lambda b, i: (b * nq + i, 0)),
            pl.BlockSpec((seq, 256), lambda b, i: (b, 0)),
            pl.BlockSpec((seq, 256), lambda b, i: (b, 0)),
            pl.BlockSpec((n_ctx, 256), lambda b, i: (b, 0)),
            pl.BlockSpec((n_ctx, 256), lambda b, i: (b, 0)),
        ],
        out_specs=pl.BlockSpec((tq, 256), lambda b, i: (b * nq + i, 0)),
        out_shape=jax.ShapeDtypeStruct((qb.shape[0], 256), BF16),
        name="global_attn",
        compiler_params=_cparams("arbitrary", "arbitrary"),
    )(qb, kb, vb, kb_c, vb_c)


SW_TQ = 256
SW_SPAN = SW_TQ + 2 * SW_WINDOW


def _window_start(i, seq):
    return jnp.clip(i * SW_TQ - SW_WINDOW, 0, seq - SW_SPAN)


def _window_kernel(sink_ref, q_ref, k_ref, v_ref, kc_ref, vc_ref, mask_ref, o_ref):
    i = pl.program_id(1)
    start = pl.multiple_of(_window_start(i, k_ref.shape[0]), LANES)

    def in_window(s, c):
        return jnp.where(mask_ref[:, c:c + KV_CHUNK] > 0.5, s, NEG_INF)

    for g in range(2):
        parts = [_ref_part(k_ref, v_ref, g, SW_SPAN, start, in_window),
                 _ref_part(kc_ref, vc_ref, g, kc_ref.shape[0])]
        o_ref[:, _tile(g)] = _gqa_group(q_ref, g, parts, sink_pair=(sink_ref[2 * g], sink_ref[2 * g + 1]))


def _window_mask_table(seq):
    nq = seq // SW_TQ
    blocks = jnp.array([0, 1, nq - 1])
    qpos = blocks[:, None] * SW_TQ + jnp.arange(SW_TQ)[None, :]
    kpos = _window_start(blocks, seq)[:, None] + jnp.arange(SW_SPAN)[None, :]
    valid = jnp.abs(kpos[:, None, :] - qpos[:, :, None]) <= SW_WINDOW
    return jnp.tile(valid.astype(F32), (1, 2, 1))


def _window_call(sink, qd, kd, vd, kd_c, vd_c, mask_tab, *, batch):
    seq = qd.shape[0] // batch
    n_ctx = kd_c.shape[0] // batch
    nq = seq // SW_TQ

    def pattern(i):
        return jnp.where(i == 0, 0, jnp.where(i == nq - 1, 2, 1))

    return pl.pallas_call(
        _window_kernel,
        grid=(batch, nq),
        in_specs=[
            pl.BlockSpec(memory_space=pltpu.SMEM),
            pl.BlockSpec((SW_TQ, 512), lambda b, i: (b * nq + i, 0)),
            pl.BlockSpec((seq, 256), lambda b, i: (b, 0)),
            pl.BlockSpec((seq, 256), lambda b, i: (b, 0)),
            pl.BlockSpec((n_ctx, 256), lambda b, i: (b, 0)),
            pl.BlockSpec((n_ctx, 256), lambda b, i: (b, 0)),
            pl.BlockSpec((None, 2 * SW_TQ, SW_SPAN), lambda b, i: (pattern(i), 0, 0)),
        ],
        out_specs=pl.BlockSpec((SW_TQ, 256), lambda b, i: (b * nq + i, 0)),
        out_shape=jax.ShapeDtypeStruct((qd.shape[0], 256), BF16),
        name="window_attn",
        compiler_params=_cparams("arbitrary", "arbitrary"),
    )(sink, qd, kd, vd, kd_c, vd_c, mask_tab)


def _ctx_attn_kernel(sink_ref, qa_ref, ka_ref, va_ref, qb_ref, kb_ref, vb_ref, qd_ref, kd_ref, vd_ref,
                     oa_ref, ob_ref, od_ref):
    n = ka_ref.shape[0]
    outs = [_attend(qa_ref[:, _tile(h)], [_ref_part(ka_ref, va_ref, h, n)]) for h in range(4)]
    oa_ref[:, _tile(0)] = _pack_pair(outs[0], outs[1])
    oa_ref[:, _tile(1)] = _pack_pair(outs[2], outs[3])
    for g in range(2):
        ob_ref[:, _tile(g)] = _gqa_group(qb_ref, g, [_ref_part(kb_ref, vb_ref, g, n)])
        od_ref[:, _tile(g)] = _gqa_group(qd_ref, g, [_ref_part(kd_ref, vd_ref, g, n)],
                                         sink_pair=(sink_ref[2 * g], sink_ref[2 * g + 1]))


def _ctx_attn_call(sink, cx, *, batch):
    n_ctx = cx["qa"].shape[0] // batch
    blk = lambda w: pl.BlockSpec((n_ctx, w), lambda b: (b, 0))
    names = ("qa", "ka", "va", "qb", "kb", "vb", "qd", "kd", "vd")
    widths = (512, 512, 512, 512, 256, 256, 512, 256, 256)
    shape = jax.ShapeDtypeStruct((cx["qa"].shape[0], 256), BF16)
    return pl.pallas_call(
        _ctx_attn_kernel,
        grid=(batch,),
        in_specs=[pl.BlockSpec(memory_space=pltpu.SMEM)] + [blk(w) for w in widths],
        out_specs=[blk(256)] * 3,
        out_shape=[shape] * 3,
        name="ctx_attn",
        compiler_params=_cparams("arbitrary"),
    )(sink, *[cx[n] for n in names])


def _s5_discretise(lam_re, lam_im, log_step, b_re, b_im):
    step = jnp.exp(log_step.astype(F32))[:, None]
    lam_re = lam_re.astype(F32)
    lam_im = lam_im.astype(F32)
    mag = jnp.exp(lam_re * step)
    ab_re = mag * jnp.cos(lam_im * step)
    ab_im = mag * jnp.sin(lam_im * step)
    den = lam_re * lam_re + lam_im * lam_im
    num_re = ab_re - 1.0
    f_re = ((num_re * lam_re + ab_im * lam_im) / den)[..., None]
    f_im = ((ab_im * lam_re - num_re * lam_im) / den)[..., None]
    b_re = b_re.astype(F32)
    b_im = b_im.astype(F32)
    return ab_re, ab_im, f_re * b_re - f_im * b_im, f_re * b_im + f_im * b_re


def _s5_operands(lam_re, lam_im, log_step, b_re, b_im, c_re, c_im):
    ab_re, ab_im, bb_re, bb_im = _s5_discretise(lam_re, lam_im, log_step, b_re, b_im)
    eye = jnp.eye(SSM_GROUPS, dtype=F32)
    bd_in = lambda m: jnp.einsum("gph,gk->ghkp", m, eye).reshape(SSM_WIDTH, SSM_FLAT)
    bd_out = lambda m: jnp.einsum("ghp,gk->kpgh", m.astype(F32), eye).reshape(SSM_FLAT, SSM_WIDTH)
    a = jnp.stack([ab_re.reshape(SSM_FLAT), ab_im.reshape(SSM_FLAT)])
    bbd = jnp.concatenate([bd_in(bb_re), bd_in(bb_im)], axis=1).astype(BF16)
    cbd = jnp.concatenate([bd_out(c_re), -bd_out(c_im)], axis=0).astype(BF16)
    return a, bbd, cbd


def _s5_kernel(*refs, reverse, final, batch, n_ctx_chunks):
    if final:
        (uc_ref, ul_ref, pc_ref, pl_ref, a_ref, bbd_ref, cbd_ref, d_ref, wglu_ref,
         oc_ref, ol_ref, bu_ref, h_ref) = refs
    else:
        uc_ref, ul_ref, a_ref, bbd_ref, cbd_ref, oc_ref, ol_ref, bu_ref, h_ref = refs
        pc_ref = pl_ref = d_ref = wglu_ref = None
    c = pl.program_id(0)
    steps = bu_ref.shape[0] // batch

    @pl.when(c == 0)
    def _():
        h_ref[...] = jnp.zeros_like(h_ref)

    def run(u_ref, prev_ref, o_ref):
        u = u_ref[...]
        bu_ref[...] = _dot(u.astype(BF16), bbd_ref[...])
        ar = jnp.broadcast_to(a_ref[0:1, :], (batch, SSM_FLAT))
        ai = jnp.broadcast_to(a_ref[1:2, :], (batch, SSM_FLAT))

        def step(i, h):
            t = steps - 1 - i if reverse else i
            row = pl.multiple_of(t * batch, batch)
            b = bu_ref[pl.ds(row, batch), :]
            hr, hi = h[:, :SSM_FLAT], h[:, SSM_FLAT:]
            nr = ar * hr - ai * hi + b[:, :SSM_FLAT]
            ni = ar * hi + ai * hr + b[:, SSM_FLAT:]
            hn = jnp.concatenate([nr, ni], axis=1)
            bu_ref[pl.ds(row, batch), :] = hn
            return hn

        h_ref[...] = lax.fori_loop(0, steps, step, h_ref[...], unroll=True)
        y = _dot(bu_ref[...].astype(BF16), cbd_ref[...])
        if not final:
            o_ref[...] = y
        else:
            y = jax.nn.gelu(y + prev_ref[...] + d_ref[...] * u)
            z = _dot(y.astype(BF16), wglu_ref[...])
            o_ref[...] = (y * jax.nn.sigmoid(z)).astype(BF16)

    @pl.when(c < n_ctx_chunks)
    def _():
        run(uc_ref, pc_ref, oc_ref)

    @pl.when(c >= n_ctx_chunks)
    def _():
        run(ul_ref, pl_ref, ol_ref)


def _s5_pass(u_ctx, u_lat, prev, a, bbd, cbd, d_skip, w_glu, *, batch, reverse):
    final = prev is not None
    t_chunk = 128
    rows = t_chunk * batch
    ncc = u_ctx.shape[0] // rows
    nlc = u_lat.shape[0] // rows
    if reverse:
        ctx_idx = lambda c: (jnp.maximum(ncc - 1 - c, 0), 0)
        lat_idx = lambda c: (nlc - 1 - jnp.maximum(c - ncc, 0), 0)
    else:
        ctx_idx = lambda c: (jnp.minimum(c, ncc - 1), 0)
        lat_idx = lambda c: (jnp.maximum(c - ncc, 0), 0)
    full = lambda shape: pl.BlockSpec(shape, lambda c: (0,) * len(shape))
    u_specs = [pl.BlockSpec((rows, SSM_WIDTH), ctx_idx), pl.BlockSpec((rows, SSM_WIDTH), lat_idx)]
    par_specs = [full((2, SSM_FLAT)), full((SSM_WIDTH, 2 * SSM_FLAT)), full((2 * SSM_FLAT, SSM_WIDTH))]
    args = [u_ctx, u_lat]
    in_specs = list(u_specs)
    if final:
        args += list(prev)
        in_specs += u_specs
    args += [a, bbd, cbd]
    in_specs += par_specs
    if final:
        args += [d_skip, w_glu]
        in_specs += [full((1, SSM_WIDTH)), full((SSM_WIDTH, SSM_WIDTH))]
    dt = BF16 if final else F32
    return pl.pallas_call(
        functools.partial(_s5_kernel, reverse=reverse, final=final, batch=batch, n_ctx_chunks=ncc),
        grid=(ncc + nlc,),
        in_specs=in_specs,
        out_specs=u_specs,
        out_shape=[jax.ShapeDtypeStruct(u_ctx.shape, dt), jax.ShapeDtypeStruct(u_lat.shape, dt)],
        scratch_shapes=[pltpu.VMEM((rows, 2 * SSM_FLAT), F32), pltpu.VMEM((batch, 2 * SSM_FLAT), F32)],
        name="s5_bwd_glu" if final else "s5_fwd",
        compiler_params=_cparams("arbitrary"),
    )(*args)


def _bf16_bits(x):
    return lax.bitcast_convert_type(x.astype(BF16).astype(F32), jnp.uint32)


def _pack_bf16_pair(a, b):
    return _bf16_bits(a) | lax.shift_right_logical(_bf16_bits(b), jnp.uint32(16))


def _unpack_bf16_pair(u):
    a = lax.bitcast_convert_type(u & jnp.uint32(0xFFFF0000), F32)
    b = lax.bitcast_convert_type(lax.shift_left(u, jnp.uint32(16)), F32)
    return a.astype(BF16), b.astype(BF16)


def _out_proj_kernel(oa_ref, ob_ref, oc_ref, od_ref, w_ref, x_ref, g1_ref, sh2_ref, sc2_ref,
                     lng_ref, lnb_ref, h_ref, f_ref, *, packed):
    y = (_dot(oa_ref[...], w_ref[0:256, :]) + _dot(ob_ref[...], w_ref[256:512, :])
         + _dot(oc_ref[...], w_ref[512:768, :]) + _dot(od_ref[...], w_ref[768:1024, :]))
    h1 = _layer_norm(DEEPNORM_ALPHA * x_ref[...] + g1_ref[...] * y, lng_ref[...], lnb_ref[...])
    h_ref[...] = h1
    f = h1 * (1.0 + sc2_ref[...]) + sh2_ref[...]
    if packed:
        f_ref[...] = _pack_bf16_pair(f[:, :D_MODEL // 2], f[:, D_MODEL // 2:])
    else:
        f_ref[...] = f.astype(BF16)


def _out_proj_inplace_kernel(oa_ref, ob_ref, oc_ref, od_ref, w_ref, x_ref, g1_ref, sh2_ref, sc2_ref,
                             lng_ref, lnb_ref, h_dest_ref, f_dest_ref, h_ref, f_ref, *, packed):
    del h_dest_ref, f_dest_ref
    _out_proj_kernel(oa_ref, ob_ref, oc_ref, od_ref, w_ref, x_ref, g1_ref, sh2_ref, sc2_ref,
                     lng_ref, lnb_ref, h_ref, f_ref, packed=packed)


def _out_proj_fill_kernel(*refs, packed, n_blocks):
    h_ref, f_ref = refs[-2:]

    @pl.when(pl.program_id(0) < n_blocks)
    def _():
        _out_proj_kernel(*refs, packed=packed)

    @pl.when(pl.program_id(0) >= n_blocks)
    def _():
        h_ref[...] = jnp.zeros_like(h_ref)
        f_ref[...] = jnp.zeros_like(f_ref)


def _out_proj_call(oa, ob, oc_tm, od, w_out, x, modr, layer, lng, lnb, *, batch, is_ctx, packed,
                   x_start=0, rows=None, out_rows=None, out_start=0, dest=None):
    rows = x.shape[0] if rows is None else rows
    out_rows = rows if out_rows is None else out_rows
    per_batch = rows // batch
    tm = 256 if is_ctx else 512
    nb = per_batch // tm
    n_blocks = rows // tm
    n_fill = 0 if dest is not None else (out_rows - out_start - rows) // tm
    row_fn = (lambda i: batch) if is_ctx else (lambda i: jnp.minimum(i, n_blocks - 1) // nb)
    blk = lambda i: jnp.minimum(i, n_blocks - 1)
    rowblk = lambda w, start=0: pl.BlockSpec((tm, w), lambda i: (start // tm + blk(i), 0))
    outblk = lambda w: pl.BlockSpec((tm, w), lambda i: (out_start // tm + i, 0))
    full = lambda shape: pl.BlockSpec(shape, lambda i: (0,) * len(shape))
    f_width, f_dtype = (D_MODEL // 2, jnp.uint32) if packed else (D_MODEL, BF16)
    in_specs = [rowblk(256), rowblk(256),
                pl.BlockSpec((tm, SSM_WIDTH), lambda i: (blk(i) % nb, blk(i) // nb)),
                rowblk(256), full((D_MODEL, D_MODEL)), rowblk(D_MODEL, x_start),
                _mod_spec(layer, 2, row_fn), _mod_spec(layer, 3, row_fn), _mod_spec(layer, 4, row_fn),
                full((1, D_MODEL)), full((1, D_MODEL))]
    args = [oa, ob, oc_tm, od, w_out, x, modr, modr, modr, lng, lnb]
    aliases = {}
    body = functools.partial(_out_proj_kernel, packed=packed)
    if dest is not None:
        aliases = {len(args): 0, len(args) + 1: 1}
        in_specs += [pl.BlockSpec(memory_space=pl.ANY)] * 2
        args += list(dest)
        body = functools.partial(_out_proj_inplace_kernel, packed=packed)
    elif n_fill:
        body = functools.partial(_out_proj_fill_kernel, packed=packed, n_blocks=n_blocks)
    return pl.pallas_call(
        body,
        grid=(n_blocks + n_fill,),
        in_specs=in_specs,
        out_specs=[outblk(D_MODEL), outblk(f_width)],
        out_shape=[jax.ShapeDtypeStruct((out_rows, D_MODEL), F32),
                   jax.ShapeDtypeStruct((out_rows, f_width), f_dtype)],
        input_output_aliases=aliases,
        name="out_proj_ctx" if is_ctx else "out_proj",
        compiler_params=_cparams("arbitrary"),
    )(*args)


def _ffn_kernel(f_ref, wg_ref, wu_ref, wd_ref, h_ref, g2_ref, lng_ref, lnb_ref, o_ref, acc_ref):
    j = pl.program_id(1)

    @pl.when(j == 0)
    def _():
        acc_ref[...] = jnp.zeros_like(acc_ref)

    f = f_ref[...]
    g = _dot(f, wg_ref[...])
    u = _dot(f, wu_ref[...])
    acc_ref[...] += _dot((g * jax.nn.sigmoid(g) * u).astype(BF16), wd_ref[...])

    @pl.when(j == pl.num_programs(1) - 1)
    def _():
        o_ref[...] = _layer_norm(DEEPNORM_ALPHA * h_ref[...] + g2_ref[...] * acc_ref[...],
                                 lng_ref[...], lnb_ref[...])


def _ffn_call(f_in, wg, wu, wd, h1, modr, layer, lng, lnb, *, batch, is_ctx):
    rows = h1.shape[0]
    per_batch = rows // batch
    tm = 512
    d_ff = wg.shape[1]
    tf = d_ff // 2
    nb = per_batch // tm
    row_fn = (lambda i: batch) if is_ctx else (lambda i: i // nb)
    rowblk = lambda w: pl.BlockSpec((tm, w), lambda i, j: (i, 0))
    full = lambda shape: pl.BlockSpec(shape, lambda i, j: (0,) * len(shape))
    return pl.pallas_call(
        _ffn_kernel,
        grid=(rows // tm, d_ff // tf),
        in_specs=[rowblk(D_MODEL),
                  pl.BlockSpec((D_MODEL, tf), lambda i, j: (0, j)),
                  pl.BlockSpec((D_MODEL, tf), lambda i, j: (0, j)),
                  pl.BlockSpec((tf, D_MODEL), lambda i, j: (j, 0)),
                  rowblk(D_MODEL), _mod_spec(layer, 5, row_fn), full((1, D_MODEL)), full((1, D_MODEL))],
        out_specs=rowblk(D_MODEL),
        out_shape=jax.ShapeDtypeStruct((rows, D_MODEL), F32),
        scratch_shapes=[pltpu.VMEM((tm, D_MODEL), F32)],
        name="ffn_ctx" if is_ctx else "ffn",
        compiler_params=_cparams("arbitrary", "arbitrary"),
    )(f_in, wg, wu, wd, h1, modr, lng, lnb)


MOE_TM = 1024
MOE_TILE = 512
MOE_PASS_TILES = 10
MOE_FF_CHUNKS = 7
ROUTE_ROWS = 8


def _router_kernel(h_ref, sh2_ref, sc2_ref, wr_ref, br_ref, tri_ref, route_ref, gate_ref, cnt_ref):
    tm = h_ref.shape[0]
    f = h_ref[...] * (1.0 + sc2_ref[...]) + sh2_ref[...]
    f_hi, f_lo = _split_bf16(f)
    w_hi, w_lo = _split_bf16(wr_ref[...])
    logits = _dot_nt(w_hi, f_hi) + _dot_nt(w_hi, f_lo) + _dot_nt(w_lo, f_hi) + br_ref[...]
    ie = lax.broadcasted_iota(jnp.int32, logits.shape, 0)
    m1 = jnp.max(logits, axis=0, keepdims=True)
    i1 = jnp.min(jnp.where(logits == m1, ie, N_EXPERTS), axis=0, keepdims=True)
    rest = jnp.where(ie == i1, -jnp.inf, logits)
    m2 = jnp.max(rest, axis=0, keepdims=True)
    i2 = jnp.min(jnp.where(rest == m2, ie, N_EXPERTS), axis=0, keepdims=True)
    e2 = jnp.exp(m2 - m1)
    den = 1.0 + e2
    sel = jnp.where((ie == i1) | (ie == i2), 1.0, 0.0)
    rank = _dot(sel.astype(BF16), tri_ref[...])
    r1 = jnp.sum(jnp.where(ie == i1, rank, 0.0), axis=0, keepdims=True).astype(jnp.int32)
    r2 = jnp.sum(jnp.where(ie == i2, rank, 0.0), axis=0, keepdims=True).astype(jnp.int32)
    row = lax.broadcasted_iota(jnp.int32, (ROUTE_ROWS, tm), 0)
    route_ref[...] = jnp.where(row == 0, i1, jnp.where(row == 1, i2, jnp.where(row == 2, r1,
                               jnp.where(row == 3, r2, 0))))
    lrow = lax.broadcasted_iota(jnp.int32, (LANES, tm), 0)
    gate_ref[...] = jnp.where(lrow == 0, 1.0 / den, jnp.where(lrow == 1, e2 / den, 0.0)).T
    cnt = jnp.sum(sel, axis=1, keepdims=True)
    cnt_ref[...] = jnp.broadcast_to(cnt, cnt_ref.shape).astype(jnp.int32)


def _router_call(h1, modr, layer, wr_t, b_r, tri, *, batch, is_ctx):
    rows = h1.shape[0]
    tm = MOE_TM
    nblk = rows // tm
    nb = (rows // batch) // tm if not is_ctx else 1
    row_fn = (lambda i: batch) if is_ctx else (lambda i: i // nb)
    full = lambda shape: pl.BlockSpec(shape, lambda i: (0,) * len(shape))
    return pl.pallas_call(
        _router_kernel,
        grid=(nblk,),
        in_specs=[pl.BlockSpec((tm, D_MODEL), lambda i: (i, 0)),
                  _mod_spec(layer, 3, row_fn), _mod_spec(layer, 4, row_fn),
                  full((N_EXPERTS, D_MODEL)), full((N_EXPERTS, 1)), full((tm, tm))],
        out_specs=[pl.BlockSpec((None, ROUTE_ROWS, tm), lambda i: (i, 0, 0)),
                   pl.BlockSpec((tm, LANES), lambda i: (i, 0)),
                   pl.BlockSpec((None, N_EXPERTS, LANES), lambda i: (i, 0, 0))],
        out_shape=[jax.ShapeDtypeStruct((nblk, ROUTE_ROWS, tm), jnp.int32),
                   jax.ShapeDtypeStruct((rows, LANES), F32),
                   jax.ShapeDtypeStruct((nblk, N_EXPERTS, LANES), jnp.int32)],
        name="router_ctx" if is_ctx else "router",
        compiler_params=_cparams("arbitrary"),
    )(h1, modr, modr, wr_t, b_r, tri)


def _route_plan(route, cnt, n_tiles, n_passes):
    total = jnp.sum(cnt, axis=0)
    tiles = lax.div(total + (MOE_TILE - 1), MOE_TILE)
    tile_start = jnp.cumsum(tiles) - tiles
    base = tile_start[None, :] * MOE_TILE + jnp.cumsum(cnt, axis=0) - cnt
    choice = jax.nn.one_hot(route[:, 0:2, :], N_EXPERTS, dtype=jnp.int32)
    pos = jnp.sum(choice * base[:, None, None, :], axis=-1) + route[:, 2:4, :]
    nblk, _, tm = pos.shape
    token = jnp.broadcast_to((jnp.arange(nblk)[:, None, None] * tm + jnp.arange(tm)[None, None, :]), pos.shape)
    src = jnp.zeros((n_tiles * MOE_TILE,), jnp.int32).at[pos.reshape(-1)].set(
        token.reshape(-1).astype(jnp.int32), unique_indices=True, mode="promise_in_bounds")
    src = src.reshape(n_tiles, 1, MOE_TILE)
    passes = lax.div(tiles + (MOE_PASS_TILES - 1), MOE_PASS_TILES)
    pass_end = jnp.cumsum(passes)
    p = jnp.arange(n_passes, dtype=jnp.int32)
    owner = jnp.sum((p[:, None] >= pass_end[None, :]).astype(jnp.int32), axis=1)
    last = jnp.sum((pass_end[-1] - 1 >= pass_end).astype(jnp.int32))
    expert = jnp.where(p < pass_end[-1], owner, last)
    within = (p - (pass_end - passes)[expert]) * MOE_PASS_TILES
    n = jnp.where(p < pass_end[-1], jnp.clip(tiles[expert] - within, 0, MOE_PASS_TILES), 0)
    first = tile_start[expert] + within
    used = jnp.sum(tiles)
    first = jnp.where(p == pass_end[-1], used, first)
    n_zero = jnp.where(p == pass_end[-1], n_tiles - used, 0)
    i32 = lambda a: a.astype(jnp.int32)
    return i32(pos), src, i32(used).reshape(1), i32(expert), i32(first), i32(n), i32(n_zero)


def _row_copies(src_of, dst_of, n_rows, sem):
    def copy(t, k):
        return pltpu.make_async_copy(src_of(t, k), dst_of(t, k), sem)

    def start(t, carry):
        copy(t, 0).start(priority=0)
        copy(t, 1).start(priority=1)
        return carry

    def wait(t, carry):
        copy(t, 0).wait()
        copy(t, 1).wait()
        return carry

    lax.fori_loop(0, n_rows, start, 0, unroll=8)
    lax.fori_loop(0, n_rows, wait, 0, unroll=8)


DISPATCH_TILES = 4


def _dispatch_kernel(used_ref, src_ref, x_hbm, xs_ref, sem):
    half = xs_ref.shape[0] // 2
    first_tile = pl.program_id(0) * DISPATCH_TILES

    @pl.when(first_tile < used_ref[0])
    def _():
        _row_copies(lambda r, k: x_hbm.at[pl.ds(src_ref[0, k * half + r], 1)],
                    lambda r, k: xs_ref.at[pl.ds(k * half + r, 1)], half, sem)

    @pl.when(first_tile >= used_ref[0])
    def _():
        xs_ref[...] = jnp.zeros_like(xs_ref)


def _dispatch_call(n_used, src, x_packed):
    n_rows = src.shape[0] * MOE_TILE
    step_rows = DISPATCH_TILES * MOE_TILE
    width = x_packed.shape[1]
    grid_spec = pltpu.PrefetchScalarGridSpec(
        num_scalar_prefetch=1,
        grid=(n_rows // step_rows,),
        in_specs=[pl.BlockSpec((None, 1, step_rows), lambda t, nu: (t, 0, 0), memory_space=pltpu.SMEM),
                  pl.BlockSpec(memory_space=pl.ANY)],
        out_specs=pl.BlockSpec((step_rows, width), lambda t, nu: (t, 0)),
        scratch_shapes=[pltpu.SemaphoreType.DMA(())],
    )
    return pl.pallas_call(
        _dispatch_kernel,
        grid_spec=grid_spec,
        out_shape=jax.ShapeDtypeStruct((n_rows, width), x_packed.dtype),
        name="moe_dispatch",
        compiler_params=_cparams("arbitrary"),
    )(n_used, src.reshape(n_rows // step_rows, 1, step_rows), x_packed)


def _experts_kernel(pe_ref, pf_ref, pn_ref, pz_ref, xs_hbm, wg_ref, wu_ref, wd_ref, ys_hbm,
                    xa_buf, xb_buf, y_buf, stage_buf, wg_buf, wu_buf, wd_buf, load_sem, store_sem):
    del pe_ref
    p, j = pl.program_id(0), pl.program_id(1)
    last_j = pl.num_programs(1) - 1
    n_tiles = pn_ref[p]
    first = pf_ref[p]
    half = D_MODEL // 2

    def tile_rows(k):
        return pl.ds(pl.multiple_of((first + k) * MOE_TILE, MOE_TILE), MOE_TILE)

    def load(k):
        slot = lax.rem(k, 2)
        return pltpu.make_async_copy(xs_hbm.at[tile_rows(k)], stage_buf.at[slot], load_sem.at[slot])

    def store(k, src):
        return pltpu.make_async_copy(src, ys_hbm.at[tile_rows(k)], store_sem)

    def for_tiles(n, body):
        lax.fori_loop(0, n, lambda k, c: (body(k), c)[1], 0)

    @pl.when((pz_ref[p] > 0) & (j == 0))
    def _():
        y_buf[0] = jnp.zeros(y_buf.shape[1:], F32)
        for_tiles(pz_ref[p], lambda k: store(k, y_buf.at[0]).start())
        for_tiles(pz_ref[p], lambda k: store(k, y_buf.at[0]).wait())

    @pl.when(n_tiles > 0)
    def _():
        wg_buf[...] = wg_ref[...].astype(BF16)
        wu_buf[...] = wu_ref[...].astype(BF16)
        wd_buf[...] = wd_ref[...].astype(BF16)

        def swiglu(k):
            xa, xb = xa_buf[k], xb_buf[k]
            g = _dot(xa, wg_buf[:half, :]) + _dot(xb, wg_buf[half:, :])
            u = _dot(xa, wu_buf[:half, :]) + _dot(xb, wu_buf[half:, :])
            return _dot((g * jax.nn.sigmoid(g) * u).astype(BF16), wd_buf[...])

        @pl.when(j == 0)
        def _():
            load(0).start()

            def first_chunk(k):
                load(k).wait()

                @pl.when(k + 1 < n_tiles)
                def _():
                    load(k + 1).start()

                xa_buf[k], xb_buf[k] = _unpack_bf16_pair(stage_buf[lax.rem(k, 2)])
                y_buf[k] = swiglu(k)
            for_tiles(n_tiles, first_chunk)

        @pl.when((j > 0) & (j < last_j))
        def _():
            def middle_chunk(k):
                y_buf[k] += swiglu(k)
            for_tiles(n_tiles, middle_chunk)

        @pl.when(j == last_j)
        def _():
            def last_chunk(k):
                y_buf[k] += swiglu(k)
                store(k, y_buf.at[k]).start()
            for_tiles(n_tiles, last_chunk)
            for_tiles(n_tiles, lambda k: store(k, y_buf.at[k]).wait())


def _experts_call(pass_expert, pass_first, pass_tiles, pass_zero, xs, w_gate, w_up, w_down, moe_layer):
    n_passes = pass_expert.shape[0]
    d_ff = w_gate.shape[3]
    tf = d_ff // MOE_FF_CHUNKS
    half = D_MODEL // 2
    chunk = lambda p, j, pn: jnp.where(pn[p] > 0, j, MOE_FF_CHUNKS - 1)
    cols = pl.BlockSpec((None, None, D_MODEL, tf),
                        lambda p, j, pe, pf, pn, pz: (moe_layer, pe[p], 0, chunk(p, j, pn)))
    grid_spec = pltpu.PrefetchScalarGridSpec(
        num_scalar_prefetch=4,
        grid=(n_passes, MOE_FF_CHUNKS),
        in_specs=[pl.BlockSpec(memory_space=pl.ANY), cols, cols,
                  pl.BlockSpec((None, None, tf, D_MODEL),
                               lambda p, j, pe, pf, pn, pz: (moe_layer, pe[p], chunk(p, j, pn), 0))],
        out_specs=pl.BlockSpec(memory_space=pl.ANY),
        scratch_shapes=[pltpu.VMEM((MOE_PASS_TILES, MOE_TILE, half), BF16),
                        pltpu.VMEM((MOE_PASS_TILES, MOE_TILE, half), BF16),
                        pltpu.VMEM((MOE_PASS_TILES, MOE_TILE, D_MODEL), F32),
                        pltpu.VMEM((2, MOE_TILE, half), jnp.uint32),
                        pltpu.VMEM((D_MODEL, tf), BF16), pltpu.VMEM((D_MODEL, tf), BF16),
                        pltpu.VMEM((tf, D_MODEL), BF16),
                        pltpu.SemaphoreType.DMA((2,)), pltpu.SemaphoreType.DMA(())],
    )
    return pl.pallas_call(
        _experts_kernel,
        grid_spec=grid_spec,
        out_shape=jax.ShapeDtypeStruct((xs.shape[0], D_MODEL), F32),
        name="moe_experts",
        compiler_params=_cparams("arbitrary", "arbitrary"),
    )(pass_expert, pass_first, pass_tiles, pass_zero, xs, w_gate, w_up, w_down)


def _combine_kernel(pos_ref, ys_hbm, gate_ref, h_ref, g2_ref, lng_ref, lnb_ref, o_ref, buf_ref, sem):
    tm = pos_ref.shape[1]
    _row_copies(lambda t, k: ys_hbm.at[pl.ds(pos_ref[k, t], 1)],
                lambda t, k: buf_ref.at[k, pl.ds(t, 1)], tm, sem)
    gate = gate_ref[...]
    f = gate[:, 0:1] * buf_ref[0] + gate[:, 1:2] * buf_ref[1]
    o_ref[...] = _layer_norm(DEEPNORM_ALPHA * h_ref[...] + g2_ref[...] * f, lng_ref[...], lnb_ref[...])


def _combine_call(pos, ys, gate, h1, modr, layer, lng, lnb, *, batch, is_ctx):
    rows = h1.shape[0]
    nblk, _, tm = pos.shape
    nb = (rows // batch) // tm if not is_ctx else 1
    row_fn = (lambda i: batch) if is_ctx else (lambda i: i // nb)
    rowblk = lambda w: pl.BlockSpec((tm, w), lambda i: (i, 0))
    full = lambda shape: pl.BlockSpec(shape, lambda i: (0,) * len(shape))
    return pl.pallas_call(
        _combine_kernel,
        grid=(nblk,),
        in_specs=[pl.BlockSpec((None, 2, tm), lambda i: (i, 0, 0), memory_space=pltpu.SMEM),
                  pl.BlockSpec(memory_space=pl.ANY), rowblk(LANES), rowblk(D_MODEL),
                  _mod_spec(layer, 5, row_fn), full((1, D_MODEL)), full((1, D_MODEL))],
        out_specs=rowblk(D_MODEL),
        out_shape=jax.ShapeDtypeStruct((rows, D_MODEL), F32),
        scratch_shapes=[pltpu.VMEM((2, tm, D_MODEL), F32), pltpu.SemaphoreType.DMA(())],
        name="moe_combine",
        compiler_params=_cparams("arbitrary"),
    )(pos, ys, gate, h1, modr, lng, lnb)


def _moe_call(f_packed, h1, modr, layer, wr_t, b_r, tri, w_gate, w_up, w_down, moe_layer, lng, lnb,
              *, batch, is_ctx):
    rows = h1.shape[0]
    n_tiles = (2 * rows) // MOE_TILE + N_EXPERTS
    assert n_tiles % DISPATCH_TILES == 0
    n_passes = n_tiles // MOE_PASS_TILES + N_EXPERTS
    route, gate, cnt = _router_call(h1, modr, layer, wr_t, b_r, tri, batch=batch, is_ctx=is_ctx)
    pos, src, n_used, *passes = _route_plan(route, cnt[:, :, 0], n_tiles, n_passes)
    xs = _dispatch_call(n_used, src, f_packed)
    ys = _experts_call(*passes, xs, w_gate, w_up, w_down, moe_layer)
    return _combine_call(pos, ys, gate, h1, modr, layer, lng, lnb, batch=batch, is_ctx=is_ctx)


def _rope_tables(seq):
    pos = jnp.arange(seq, dtype=jnp.int32)
    row = (pos // GRID_W).astype(F32)
    col = (pos % GRID_W).astype(F32)
    n_freq = HEAD_DIM // 4
    inv_freq = ROPE_THETA ** (-jnp.arange(n_freq, dtype=F32) / n_freq)
    ang = jnp.concatenate([row[:, None] * inv_freq, col[:, None] * inv_freq], axis=-1)
    cos, sin = jnp.cos(ang), jnp.sin(ang)
    reps = LANES // (HEAD_DIM // 2)
    sign = jnp.tile(jnp.concatenate([-jnp.ones(HEAD_DIM // 2, F32), jnp.ones(HEAD_DIM // 2, F32)]),
                    LANES // HEAD_DIM)
    return jnp.tile(cos, (1, reps)), jnp.tile(sin, (1, reps)) * sign


def kernel(x, c, ctx, c_ctx, ada_w, ada_b, w_in, w_out, na_rpb, ga_q_norm, ga_k_norm,
           ssm_lambda_re, ssm_lambda_im, ssm_log_step, ssm_b_re, ssm_b_im, ssm_c_re, ssm_c_im,
           ssm_d, ssm_w_glu, sw_sink, ln1_g, ln1_b, ln2_g, ln2_b,
           ffn_w_gate, ffn_w_up, ffn_w_down,
           moe_w_router, moe_b_router, moe_w_gate, moe_w_up, moe_w_down):
    batch, seq, _ = x.shape
    n_ctx = ctx.shape[1]
    assert batch < MOD_ROWS and seq % 512 == 0 and n_ctx % 256 == 0

    c_all = jnp.zeros((MOD_ROWS, D_MODEL), F32).at[:batch].set(c).at[batch].set(c_ctx)
    modr = _mods_call(c_all, ada_w, ada_b)

    cos_t, sin_t = _rope_tables(seq)
    seg = jnp.kron(jnp.eye(256 // HEAD_DIM, dtype=F32), jnp.ones((HEAD_DIM, HEAD_DIM), F32)).astype(BF16)
    tri = (jnp.arange(MOE_TM)[:, None] < jnp.arange(MOE_TM)[None, :]).astype(BF16)
    sw_mask = _window_mask_table(seq)

    n_lat, n_c = batch * seq, batch * n_ctx
    h = (x.reshape(n_lat, D_MODEL), 0, n_lat)
    hc = (ctx.reshape(n_c, D_MODEL), 0, n_c)
    names = ("qa", "ka", "va", "qb", "kb", "vb", "u", "qd", "kd", "vd")

    for layer in range(DEPTH):
        need_ctx = layer < DEPTH - 1
        w_in_l = w_in[layer].astype(BF16)
        w_out_l = w_out[layer].astype(BF16)
        gq = jnp.tile(ga_q_norm[layer].astype(F32), 256 // HEAD_DIM)[None, :]
        gk = jnp.tile(ga_k_norm[layer].astype(F32), 128 // HEAD_DIM)[None, :]
        sink = sw_sink[layer].astype(F32)
        ln1 = (ln1_g[layer][None, :], ln1_b[layer][None, :])
        ln2 = (ln2_g[layer][None, :], ln2_b[layer][None, :])

        lat = dict(zip(names, _in_proj_call(h[0], modr, layer, w_in_l, cos_t, sin_t, seg, gq, gk,
                                            batch=batch, is_ctx=False, x_start=h[1], rows=h[2])))
        cx = dict(zip(names, _in_proj_call(hc[0], modr, layer, w_in_l, cos_t, sin_t, seg, gq, gk,
                                           batch=batch, is_ctx=True, x_start=hc[1], rows=hc[2])))

        na_bias, na_mask = _na_tables(na_rpb[layer], seq)
        out_a = _na_call(lat["qa"], lat["ka"], lat["va"], cx["ka"], cx["va"], na_bias, na_mask, batch=batch)
        out_b = _global_call(lat["qb"], lat["kb"], lat["vb"], cx["kb"], cx["vb"], batch=batch)
        out_d = _window_call(sink, lat["qd"], lat["kd"], lat["vd"], cx["kd"], cx["vd"], sw_mask, batch=batch)

        dirs = [_s5_operands(ssm_lambda_re[layer, d], ssm_lambda_im[layer, d], ssm_log_step[layer, d],
                             ssm_b_re[layer, d], ssm_b_im[layer, d], ssm_c_re[layer, d], ssm_c_im[layer, d])
                for d in range(2)]
        rows_tb = lambda a: a.reshape(a.shape[0] * batch, SSM_WIDTH)
        u_tb = (rows_tb(cx["u"]), rows_tb(lat["u"]))
        fwd = _s5_pass(*u_tb, None, *dirs[0], None, None, batch=batch, reverse=False)
        ctx_c, out_c = _s5_pass(*u_tb, fwd, *dirs[1], ssm_d[layer].astype(F32)[None, :],
                                ssm_w_glu[layer].astype(BF16), batch=batch, reverse=True)
        ctx_c, out_c = ctx_c.reshape(cx["u"].shape), out_c.reshape(lat["u"].shape)

        routed = layer % 2 == 1
        joint = routed and need_ctx
        h1, f_in = _out_proj_call(out_a, out_b, out_c, out_d, w_out_l, h[0], modr, layer, *ln1,
                                  batch=batch, is_ctx=False, packed=routed, x_start=h[1], rows=h[2],
                                  out_rows=n_lat + n_c if joint else None)
        streams = [(h1, f_in, False)]
        if need_ctx:
            ctx_a, ctx_b, ctx_d = _ctx_attn_call(sink, cx, batch=batch)
            hc1, fc_in = _out_proj_call(ctx_a, ctx_b, ctx_c, ctx_d, w_out_l, hc[0], modr, layer, *ln1,
                                        batch=batch, is_ctx=True, packed=routed, x_start=hc[1], rows=hc[2],
                                        **(dict(out_rows=n_lat + n_c, out_start=n_lat, dest=(h1, f_in))
                                           if joint else {}))
            streams = [(hc1, fc_in, False)] if joint else streams + [(hc1, fc_in, True)]

        i = layer // 2
        outs = []
        if not routed:
            wg, wu, wd = (ffn_w_gate[i].astype(BF16), ffn_w_up[i].astype(BF16), ffn_w_down[i].astype(BF16))
            for s1, sf, is_ctx in streams:
                outs.append(_ffn_call(sf, wg, wu, wd, s1, modr, layer, *ln2, batch=batch, is_ctx=is_ctx))
        else:
            wr_t = moe_w_router[i].astype(F32).T
            b_r = moe_b_router[i].astype(F32)[:, None]
            for s1, sf, is_ctx in streams:
                outs.append(_moe_call(sf, s1, modr, layer, wr_t, b_r, tri, moe_w_gate, moe_w_up, moe_w_down, i,
                                      *ln2, batch=batch, is_ctx=is_ctx))
        if joint:
            h, hc = (outs[0], 0, n_lat), (outs[0], n_lat, n_c)
        else:
            h = (outs[0], 0, n_lat)
            if need_ctx:
                hc = (outs[1], 0, n_c)
    return h[0].reshape(batch, seq, D_MODEL)
```

```python
import functools

import jax
import jax.numpy as jnp
from jax import lax
from jax.experimental import pallas as pl
from jax.experimental.pallas import tpu as pltpu

F32 = jnp.float32
BF16 = jnp.bfloat16

D_MODEL = 1024
DEPTH = 4
GRID_W = 64
HEAD_DIM = 64
NA_ROWS = 8
NA_COLS = 16
SSM_GROUPS = 16
SSM_GROUP_CH = 16
SSM_STATE = 64
SSM_WIDTH = SSM_GROUPS * SSM_GROUP_CH
SSM_FLAT = SSM_GROUPS * SSM_STATE
SW_WINDOW = 128
N_EXPERTS = 8
ADA_CHUNKS = 6
ROPE_THETA = 10000.0
LN_EPS = 1e-6
RMS_EPS = 1e-6
NEG_INF = -1e30
DEEPNORM_ALPHA = (2 * DEPTH) ** 0.25
ATT_SCALE = HEAD_DIM ** -0.5

C_QA, C_KA, C_VA, C_QB, C_KB, C_VB, C_U, C_QD, C_KD, C_VD, IN_WIDTH = (
    0, 256, 512, 768, 1024, 1152, 1280, 1536, 1792, 1920, 2048)

LANES = 128
SUBLANES = 8
MOD_ROWS = 16
VMEM_LIMIT = 56 * 1024 * 1024


def _cparams(*sem):
    return pltpu.CompilerParams(dimension_semantics=sem, vmem_limit_bytes=VMEM_LIMIT)


def _dot(a, b):
    return jnp.dot(a, b, preferred_element_type=F32)


def _dot_nt(a, b):
    return lax.dot_general(a, b, (((1,), (1,)), ((), ())), preferred_element_type=F32)


def _dot_tn(a, b):
    return lax.dot_general(a, b, (((0,), (0,)), ((), ())), preferred_element_type=F32)


def _split_bf16(x):
    hi = x.astype(BF16)
    lo = (x - hi.astype(F32)).astype(BF16)
    return hi, lo


def _layer_norm(z, g, b):
    zc = z - jnp.mean(z, axis=-1, keepdims=True)
    y = zc * lax.rsqrt(jnp.mean(zc * zc, axis=-1, keepdims=True) + LN_EPS)
    return y * g + b


def _mods_kernel(c_ref, w_ref, b_ref, o_ref):
    c = c_ref[...]
    act = (c * jax.nn.sigmoid(c)).astype(BF16)
    o_ref[...] = _dot(act, w_ref[...].astype(BF16)) + b_ref[...]


def _mods_call(c_all, ada_w, ada_b):
    tn = 1536
    n = ADA_CHUNKS * D_MODEL
    out = pl.pallas_call(
        _mods_kernel,
        grid=(DEPTH, n // tn),
        in_specs=[
            pl.BlockSpec((MOD_ROWS, D_MODEL), lambda l, j: (0, 0)),
            pl.BlockSpec((None, D_MODEL, tn), lambda l, j: (l, 0, j)),
            pl.BlockSpec((None, 1, tn), lambda l, j: (l, 0, j)),
        ],
        out_specs=pl.BlockSpec((None, MOD_ROWS, tn), lambda l, j: (l, 0, j)),
        out_shape=jax.ShapeDtypeStruct((DEPTH, MOD_ROWS, n), F32),
        name="mods",
        compiler_params=_cparams("arbitrary", "arbitrary"),
    )(c_all, ada_w, ada_b.reshape(DEPTH, 1, n))
    return out.reshape(DEPTH * MOD_ROWS * ADA_CHUNKS, 1, D_MODEL)


def _mod_spec(layer, chunk, row_fn):
    def index(i, *_):
        return ((layer * MOD_ROWS + row_fn(i)) * ADA_CHUNKS + chunk, 0, 0)
    return pl.BlockSpec((None, 1, D_MODEL), index)


def _seg_rms(x, seg, g):
    hi, lo = _split_bf16(x * x)
    ss = _dot(hi, seg) + _dot(lo, seg)
    return x * lax.rsqrt(ss * (1.0 / HEAD_DIM) + RMS_EPS) * g


def _rope(x, cos, sin_signed, first_half):
    outs = []
    for j in range(x.shape[1] // LANES):
        xs = x[:, j * LANES:(j + 1) * LANES]
        partner = jnp.where(first_half,
                            pltpu.roll(xs, LANES - HEAD_DIM // 2, 1),
                            pltpu.roll(xs, HEAD_DIM // 2, 1))
        outs.append(xs * cos + partner * sin_signed)
    return outs[0] if len(outs) == 1 else jnp.concatenate(outs, axis=1)


def _pad_heads(x, lane, ones_lane=False):
    low = lane < HEAD_DIM
    fill = jnp.where(lane == HEAD_DIM, 1.0, 0.0) if ones_lane else 0.0
    outs = []
    for j in range(x.shape[1] // LANES):
        xs = x[:, j * LANES:(j + 1) * LANES]
        outs.append(jnp.where(low, xs, fill))
        outs.append(jnp.where(low, pltpu.roll(xs, HEAD_DIM, 1), fill))
    return jnp.concatenate(outs, axis=1).astype(BF16)


def _in_proj_kernel(x_ref, sh_ref, sc_ref, w_ref, cos_ref, sin_ref, seg_ref, gq_ref, gk_ref,
                    qa_ref, ka_ref, va_ref, qb_ref, kb_ref, vb_ref, u_ref, qd_ref, kd_ref, vd_ref,
                    *, rope):
    a = (x_ref[...] * (1.0 + sc_ref[...]) + sh_ref[...]).astype(BF16)
    lane = lax.broadcasted_iota(jnp.int32, (a.shape[0], LANES), 1)

    def proj(c0, c1):
        return _dot(a, w_ref[:, c0:c1])

    qa_ref[...] = _pad_heads(proj(C_QA, C_KA) * ATT_SCALE, lane)
    ka_ref[...] = _pad_heads(proj(C_KA, C_VA), lane)
    va_ref[...] = _pad_heads(proj(C_VA, C_QB), lane, ones_lane=True)
    vb_ref[...] = _pad_heads(proj(C_VB, C_U), lane, ones_lane=True)
    u_ref[...] = proj(C_U, C_QD)
    vd_ref[...] = _pad_heads(proj(C_VD, IN_WIDTH), lane, ones_lane=True)

    seg = seg_ref[...]
    qb = _seg_rms(proj(C_QB, C_KB), seg, gq_ref[...])
    kb = _seg_rms(proj(C_KB, C_VB), seg[:LANES, :LANES], gk_ref[...])
    qd = proj(C_QD, C_KD)
    kd = proj(C_KD, C_VD)
    if rope:
        cos = cos_ref[...]
        sin = sin_ref[...]
        first_half = (lane % HEAD_DIM) < (HEAD_DIM // 2)
        qb = _rope(qb, cos, sin, first_half)
        kb = _rope(kb, cos, sin, first_half)
        qd = _rope(qd, cos, sin, first_half)
        kd = _rope(kd, cos, sin, first_half)
    qb_ref[...] = _pad_heads(qb * ATT_SCALE, lane)
    kb_ref[...] = _pad_heads(kb, lane)
    qd_ref[...] = _pad_heads(qd * ATT_SCALE, lane)
    kd_ref[...] = _pad_heads(kd, lane)


def _in_proj_call(x, modr, layer, w_in, cos_t, sin_t, seg, gq, gk, *, batch, is_ctx, x_start=0, rows=None):
    rows = x.shape[0] if rows is None else rows
    per_batch = rows // batch
    tm = 256 if is_ctx else 1024
    nb = per_batch // tm
    row_fn = (lambda i: batch) if is_ctx else (lambda i: i // nb)
    table_spec = pl.BlockSpec((tm, LANES), (lambda i: (0, 0)) if is_ctx else (lambda i: (i % nb, 0)))
    full = lambda shape: pl.BlockSpec(shape, lambda i: (0,) * len(shape))
    rowblk = lambda w: pl.BlockSpec((tm, w), lambda i: (i, 0))
    wide = lambda w, dt: jax.ShapeDtypeStruct((rows, w), dt)
    out_shapes = [wide(512, BF16), wide(512, BF16), wide(512, BF16),
                  wide(512, BF16), wide(256, BF16), wide(256, BF16),
                  jax.ShapeDtypeStruct((per_batch, batch * SSM_WIDTH), F32),
                  wide(512, BF16), wide(256, BF16), wide(256, BF16)]
    out_specs = [rowblk(512), rowblk(512), rowblk(512), rowblk(512), rowblk(256), rowblk(256),
                 pl.BlockSpec((tm, SSM_WIDTH), lambda i: (i % nb, i // nb)),
                 rowblk(512), rowblk(256), rowblk(256)]
    return pl.pallas_call(
        functools.partial(_in_proj_kernel, rope=not is_ctx),
        grid=(rows // tm,),
        in_specs=[pl.BlockSpec((tm, D_MODEL), lambda i: (x_start // tm + i, 0)),
                  _mod_spec(layer, 0, row_fn), _mod_spec(layer, 1, row_fn),
                  full((D_MODEL, IN_WIDTH)), table_spec, table_spec, full((256, 256)),
                  full((1, 256)), full((1, 128))],
        out_specs=out_specs,
        out_shape=out_shapes,
        name="in_proj_ctx" if is_ctx else "in_proj",
        compiler_params=_cparams("arbitrary"),
    )(x, modr, modr, w_in, cos_t, sin_t, seg, gq, gk)


KV_CHUNK = 256


def _tile(h):
    return slice(h * LANES, (h + 1) * LANES)


def _attend(q, parts, sink=None):
    def score(k_fn, mod_fn, c):
        s = _dot_nt(q, k_fn(c))
        return s if mod_fn is None else mod_fn(s, c)

    tiles = [(k_fn, v_fn, mod_fn, c) for k_fn, v_fn, n, mod_fn in parts for c in range(0, n, KV_CHUNK)]
    m = sink
    acc = None
    for k_fn, v_fn, mod_fn, c in tiles:
        s = score(k_fn, mod_fn, c)
        m_tile = jnp.max(s, axis=-1, keepdims=True)
        m_new = m_tile if m is None else jnp.maximum(m, m_tile)
        pv = _dot(jnp.exp(s - m_new).astype(BF16), v_fn(c))
        acc = pv if acc is None else acc * jnp.exp(m - m_new) + pv
        m = m_new
    den = acc[:, HEAD_DIM:HEAD_DIM + 1]
    if sink is not None:
        den = den + jnp.exp(sink - m)
    return acc * (1.0 / den)


def _pack_pair(a, b):
    lane = lax.broadcasted_iota(jnp.int32, a.shape, 1)
    return jnp.where(lane < HEAD_DIM, a, pltpu.roll(b, HEAD_DIM, 1)).astype(BF16)


def _ref_part(k_ref, v_ref, t, n_keys, start=0, mod_fn=None):
    return (lambda c: k_ref[pl.ds(start + c, KV_CHUNK), _tile(t)],
            lambda c: v_ref[pl.ds(start + c, KV_CHUNK), _tile(t)], n_keys, mod_fn)


def _gqa_group(q_ref, g, parts, sink_pair=None):
    tq = q_ref.shape[0]
    q2 = jnp.concatenate([q_ref[:, _tile(2 * g)], q_ref[:, _tile(2 * g + 1)]], axis=0)
    sink = None
    if sink_pair is not None:
        row = lax.broadcasted_iota(jnp.int32, (2 * tq, 1), 0)
        sink = jnp.where(row < tq, sink_pair[0], sink_pair[1])
    o2 = _attend(q2, parts, sink)
    return _pack_pair(o2[:tq], o2[tq:])


NA_QROWS = 4
NA_BAND = 12


def _na_band_start(j, rows):
    return jnp.clip(j * NA_QROWS - NA_ROWS // 2, 0, rows - NA_BAND)


def _na_kernel(q_ref, k_ref, v_ref, kc_ref, vc_ref, bias_ref, mask_ref, o_ref):
    j = pl.program_id(1)
    rows = k_ref.shape[0] // GRID_W
    start = pl.multiple_of(_na_band_start(j, rows) * GRID_W, GRID_W)
    outs = []
    for h in range(4):
        def local_scores(s, c, h=h):
            cols = slice(c, c + KV_CHUNK)
            return jnp.where(mask_ref[:, cols] > 0.5, s + bias_ref[h, :, cols], NEG_INF)

        parts = [_ref_part(k_ref, v_ref, h, NA_BAND * GRID_W, start, local_scores),
                 _ref_part(kc_ref, vc_ref, h, kc_ref.shape[0])]
        outs.append(_attend(q_ref[:, _tile(h)], parts))
    o_ref[:, _tile(0)] = _pack_pair(outs[0], outs[1])
    o_ref[:, _tile(1)] = _pack_pair(outs[2], outs[3])


def _na_call(qa, ka, va, ka_c, va_c, bias_tab, mask_tab, *, batch):
    seq = qa.shape[0] // batch
    n_ctx = ka_c.shape[0] // batch
    nj = seq // (NA_QROWS * GRID_W)
    tq = NA_QROWS * GRID_W
    nk = NA_BAND * GRID_W

    def pattern(j):
        return jnp.where(j == 0, 0, jnp.where(j == nj - 1, 2, 1))

    return pl.pallas_call(
        _na_kernel,
        grid=(batch, nj),
        in_specs=[
            pl.BlockSpec((tq, 512), lambda b, j: (b * nj + j, 0)),
            pl.BlockSpec((seq, 512), lambda b, j: (b, 0)),
            pl.BlockSpec((seq, 512), lambda b, j: (b, 0)),
            pl.BlockSpec((n_ctx, 512), lambda b, j: (b, 0)),
            pl.BlockSpec((n_ctx, 512), lambda b, j: (b, 0)),
            pl.BlockSpec((None, 4, tq, nk), lambda b, j: (pattern(j), 0, 0, 0)),
            pl.BlockSpec((None, tq, nk), lambda b, j: (pattern(j), 0, 0)),
        ],
        out_specs=pl.BlockSpec((tq, 256), lambda b, j: (b * nj + j, 0)),
        out_shape=jax.ShapeDtypeStruct((qa.shape[0], 256), BF16),
        name="na_attn",
        compiler_params=_cparams("arbitrary", "arbitrary"),
    )(qa, ka, va, ka_c, va_c, bias_tab, mask_tab)


def _na_tables(rpb, seq):
    rows = seq // GRID_W
    nj = rows // NA_QROWS
    c_idx = jnp.arange(GRID_W)
    col_start = jnp.clip(c_idx - NA_COLS // 2, 0, GRID_W - NA_COLS)
    col_valid = (c_idx[None, :] >= col_start[:, None]) & (c_idx[None, :] < col_start[:, None] + NA_COLS)
    col_rel = jnp.clip(c_idx[None, :] - c_idx[:, None], 1 - NA_COLS, NA_COLS - 1) + NA_COLS - 1
    blocks = jnp.array([0, 1, nj - 1])
    q_row = blocks[:, None] * NA_QROWS + jnp.arange(NA_QROWS)[None, :]
    k_row = _na_band_start(blocks, rows)[:, None] + jnp.arange(NA_BAND)[None, :]
    row_start = jnp.clip(q_row - NA_ROWS // 2, 0, rows - NA_ROWS)
    row_valid = ((k_row[:, None, :] >= row_start[:, :, None])
                 & (k_row[:, None, :] < row_start[:, :, None] + NA_ROWS))
    row_rel = jnp.clip(k_row[:, None, :] - q_row[:, :, None] + NA_ROWS - 1, 0, 2 * NA_ROWS - 2)
    row_sel = jax.nn.one_hot(row_rel, 2 * NA_ROWS - 1, dtype=F32)
    col_sel = jax.nn.one_hot(col_rel, 2 * NA_COLS - 1, dtype=F32)
    exact = lax.Precision.HIGHEST
    by_col = jnp.einsum("hab,qkb->haqk", rpb.astype(F32), col_sel, precision=exact)
    bias = jnp.einsum("pria,haqk->phrqik", row_sel, by_col, precision=exact)
    valid = row_valid[:, :, None, :, None] & col_valid[None, None, :, None, :]
    tq, nk = NA_QROWS * GRID_W, NA_BAND * GRID_W
    return bias.reshape(3, 4, tq, nk), valid.astype(F32).reshape(3, tq, nk)


def _global_kernel(q_ref, k_ref, v_ref, kc_ref, vc_ref, o_ref):
    for g in range(2):
        parts = [_ref_part(kc_ref, vc_ref, g, kc_ref.shape[0]), _ref_part(k_ref, v_ref, g, k_ref.shape[0])]
        o_ref[:, _tile(g)] = _gqa_group(q_ref, g, parts)


def _global_call(qb, kb, vb, kb_c, vb_c, *, batch):
    seq = qb.shape[0] // batch
    n_ctx = kb_c.shape[0] // batch
    tq = 256
    nq = seq // tq
    return pl.pallas_call(
        _global_kernel,
        grid=(batch, nq),
        in_specs=[
            pl.BlockSpec((tq, 512), lambda b, i: (b * nq + i, 0)),
            pl.BlockSpec((seq, 256), lambda b, i: (b, 0)),
            pl.BlockSpec((seq, 256), lambda b, i: (b, 0)),
            pl.BlockSpec((n_ctx, 256), lambda b, i: (b, 0)),
            pl.BlockSpec((n_ctx, 256), lambda b, i: (b, 0)),
        ],
        out_specs=pl.BlockSpec((tq, 256), lambda b, i: (b * nq + i, 0)),
        out_shape=jax.ShapeDtypeStruct((qb.shape[0], 256), BF16),
        name="global_attn",
        compiler_params=_cparams("arbitrary", "arbitrary"),
    )(qb, kb, vb, kb_c, vb_c)


SW_TQ = 256
SW_SPAN = SW_TQ + 2 * SW_WINDOW


def _window_start(i, seq):
    return jnp.clip(i * SW_TQ - SW_WINDOW, 0, seq - SW_SPAN)


def _window_kernel(sink_ref, q_ref, k_ref, v_ref, kc_ref, vc_ref, mask_ref, o_ref):
    i = pl.program_id(1)
    start = pl.multiple_of(_window_start(i, k_ref.shape[0]), LANES)

    def in_window(s, c):
        return jnp.where(mask_ref[:, c:c + KV_CHUNK] > 0.5, s, NEG_INF)

    for g in range(2):
        parts = [_ref_part(k_ref, v_ref, g, SW_SPAN, start, in_window),
                 _ref_part(kc_ref, vc_ref, g, kc_ref.shape[0])]
        o_ref[:, _tile(g)] = _gqa_group(q_ref, g, parts, sink_pair=(sink_ref[2 * g], sink_ref[2 * g + 1]))


def _window_mask_table(seq):
    nq = seq // SW_TQ
    blocks = jnp.array([0, 1, nq - 1])
    qpos = blocks[:, None] * SW_TQ + jnp.arange(SW_TQ)[None, :]
    kpos = _window_start(blocks, seq)[:, None] + jnp.arange(SW_SPAN)[None, :]
    valid = jnp.abs(kpos[:, None, :] - qpos[:, :, None]) <= SW_WINDOW
    return jnp.tile(valid.astype(F32), (1, 2, 1))


def _window_call(sink, qd, kd, vd, kd_c, vd_c, mask_tab, *, batch):
    seq = qd.shape[0] // batch
    n_ctx = kd_c.shape[0] // batch
    nq = seq // SW_TQ

    def pattern(i):
        return jnp.where(i == 0, 0, jnp.where(i == nq - 1, 2, 1))

    return pl.pallas_call(
        _window_kernel,
        grid=(batch, nq),
        in_specs=[
            pl.BlockSpec(memory_space=pltpu.SMEM),
            pl.BlockSpec((SW_TQ, 512), lambda b, i: (b * nq + i, 0)),
            pl.BlockSpec((seq, 256), lambda b, i: (b, 0)),
            pl.BlockSpec((seq, 256), lambda b, i: (b, 0)),
            pl.BlockSpec((n_ctx, 256), lambda b, i: (b, 0)),
            pl.BlockSpec((n_ctx, 256), lambda b, i: (b, 0)),
            pl.BlockSpec((None, 2 * SW_TQ, SW_SPAN), lambda b, i: (pattern(i), 0, 0)),
        ],
        out_specs=pl.BlockSpec((SW_TQ, 256), lambda b, i: (b * nq + i, 0)),
        out_shape=jax.ShapeDtypeStruct((qd.shape[0], 256), BF16),
        name="window_attn",
        compiler_params=_cparams("arbitrary", "arbitrary"),
    )(sink, qd, kd, vd, kd_c, vd_c, mask_tab)


def _ctx_attn_kernel(sink_ref, qa_ref, ka_ref, va_ref, qb_ref, kb_ref, vb_ref, qd_ref, kd_ref, vd_ref,
                     oa_ref, ob_ref, od_ref):
    n = ka_ref.shape[0]
    outs = [_attend(qa_ref[:, _tile(h)], [_ref_part(ka_ref, va_ref, h, n)]) for h in range(4)]
    oa_ref[:, _tile(0)] = _pack_pair(outs[0], outs[1])
    oa_ref[:, _tile(1)] = _pack_pair(outs[2], outs[3])
    for g in range(2):
        ob_ref[:, _tile(g)] = _gqa_group(qb_ref, g, [_ref_part(kb_ref, vb_ref, g, n)])
        od_ref[:, _tile(g)] = _gqa_group(qd_ref, g, [_ref_part(kd_ref, vd_ref, g, n)],
                                         sink_pair=(sink_ref[2 * g], sink_ref[2 * g + 1]))


def _ctx_attn_call(sink, cx, *, batch):
    n_ctx = cx["qa"].shape[0] // batch
    blk = lambda w: pl.BlockSpec((n_ctx, w), lambda b: (b, 0))
    names = ("qa", "ka", "va", "qb", "kb", "vb", "qd", "kd", "vd")
    widths = (512, 512, 512, 512, 256, 256, 512, 256, 256)
    shape = jax.ShapeDtypeStruct((cx["qa"].shape[0], 256), BF16)
    return pl.pallas_call(
        _ctx_attn_kernel,
        grid=(batch,),
        in_specs=[pl.BlockSpec(memory_space=pltpu.SMEM)] + [blk(w) for w in widths],
        out_specs=[blk(256)] * 3,
        out_shape=[shape] * 3,
        name="ctx_attn",
        compiler_params=_cparams("arbitrary"),
    )(sink, *[cx[n] for n in names])


def _s5_discretise(lam_re, lam_im, log_step, b_re, b_im):
    step = jnp.exp(log_step.astype(F32))[:, None]
    lam_re = lam_re.astype(F32)
    lam_im = lam_im.astype(F32)
    mag = jnp.exp(lam_re * step)
    ab_re = mag * jnp.cos(lam_im * step)
    ab_im = mag * jnp.sin(lam_im * step)
    den = lam_re * lam_re + lam_im * lam_im
    num_re = ab_re - 1.0
    f_re = ((num_re * lam_re + ab_im * lam_im) / den)[..., None]
    f_im = ((ab_im * lam_re - num_re * lam_im) / den)[..., None]
    b_re = b_re.astype(F32)
    b_im = b_im.astype(F32)
    return ab_re, ab_im, f_re * b_re - f_im * b_im, f_re * b_im + f_im * b_re


def _s5_operands(lam_re, lam_im, log_step, b_re, b_im, c_re, c_im):
    ab_re, ab_im, bb_re, bb_im = _s5_discretise(lam_re, lam_im, log_step, b_re, b_im)
    eye = jnp.eye(SSM_GROUPS, dtype=F32)
    bd_in = lambda m: jnp.einsum("gph,gk->ghkp", m, eye).reshape(SSM_WIDTH, SSM_FLAT)
    bd_out = lambda m: jnp.einsum("ghp,gk->kpgh", m.astype(F32), eye).reshape(SSM_FLAT, SSM_WIDTH)
    a = jnp.stack([ab_re.reshape(SSM_FLAT), ab_im.reshape(SSM_FLAT)])
    bbd = jnp.concatenate([bd_in(bb_re), bd_in(bb_im)], axis=1).astype(BF16)
    cbd = jnp.concatenate([bd_out(c_re), -bd_out(c_im)], axis=0).astype(BF16)
    return a, bbd, cbd


def _s5_kernel(*refs, reverse, final, batch, n_ctx_chunks):
    if final:
        (uc_ref, ul_ref, pc_ref, pl_ref, a_ref, bbd_ref, cbd_ref, d_ref, wglu_ref,
         oc_ref, ol_ref, bu_ref, h_ref) = refs
    else:
        uc_ref, ul_ref, a_ref, bbd_ref, cbd_ref, oc_ref, ol_ref, bu_ref, h_ref = refs
        pc_ref = pl_ref = d_ref = wglu_ref = None
    c = pl.program_id(0)
    steps = bu_ref.shape[0] // batch

    @pl.when(c == 0)
    def _():
        h_ref[...] = jnp.zeros_like(h_ref)

    def run(u_ref, prev_ref, o_ref):
        u = u_ref[...]
        bu_ref[...] = _dot(u.astype(BF16), bbd_ref[...])
        ar = jnp.broadcast_to(a_ref[0:1, :], (batch, SSM_FLAT))
        ai = jnp.broadcast_to(a_ref[1:2, :], (batch, SSM_FLAT))

        def step(i, h):
            t = steps - 1 - i if reverse else i
            row = pl.multiple_of(t * batch, batch)
            b = bu_ref[pl.ds(row, batch), :]
            hr, hi = h[:, :SSM_FLAT], h[:, SSM_FLAT:]
            nr = ar * hr - ai * hi + b[:, :SSM_FLAT]
            ni = ar * hi + ai * hr + b[:, SSM_FLAT:]
            hn = jnp.concatenate([nr, ni], axis=1)
            bu_ref[pl.ds(row, batch), :] = hn
            return hn

        h_ref[...] = lax.fori_loop(0, steps, step, h_ref[...], unroll=True)
        y = _dot(bu_ref[...].astype(BF16), cbd_ref[...])
        if not final:
            o_ref[...] = y
        else:
            y = jax.nn.gelu(y + prev_ref[...] + d_ref[...] * u)
            z = _dot(y.astype(BF16), wglu_ref[...])
            o_ref[...] = (y * jax.nn.sigmoid(z)).astype(BF16)

    @pl.when(c < n_ctx_chunks)
    def _():
        run(uc_ref, pc_ref, oc_ref)

    @pl.when(c >= n_ctx_chunks)
    def _():
        run(ul_ref, pl_ref, ol_ref)


def _s5_pass(u_ctx, u_lat, prev, a, bbd, cbd, d_skip, w_glu, *, batch, reverse):
    final = prev is not None
    t_chunk = 128
    rows = t_chunk * batch
    ncc = u_ctx.shape[0] // rows
    nlc = u_lat.shape[0] // rows
    if reverse:
        ctx_idx = lambda c: (jnp.maximum(ncc - 1 - c, 0), 0)
        lat_idx = lambda c: (nlc - 1 - jnp.maximum(c - ncc, 0), 0)
    else:
        ctx_idx = lambda c: (jnp.minimum(c, ncc - 1), 0)
        lat_idx = lambda c: (jnp.maximum(c - ncc, 0), 0)
    full = lambda shape: pl.BlockSpec(shape, lambda c: (0,) * len(shape))
    u_specs = [pl.BlockSpec((rows, SSM_WIDTH), ctx_idx), pl.BlockSpec((rows, SSM_WIDTH), lat_idx)]
    par_specs = [full((2, SSM_FLAT)), full((SSM_WIDTH, 2 * SSM_FLAT)), full((2 * SSM_FLAT, SSM_WIDTH))]
    args = [u_ctx, u_lat]
    in_specs = list(u_specs)
    if final:
        args += list(prev)
        in_specs += u_specs
    args += [a, bbd, cbd]
    in_specs += par_specs
    if final:
        args += [d_skip, w_glu]
        in_specs += [full((1, SSM_WIDTH)), full((SSM_WIDTH, SSM_WIDTH))]
    dt = BF16 if final else F32
    return pl.pallas_call(
        functools.partial(_s5_kernel, reverse=reverse, final=final, batch=batch, n_ctx_chunks=ncc),
        grid=(ncc + nlc,),
        in_specs=in_specs,
        out_specs=u_specs,
        out_shape=[jax.ShapeDtypeStruct(u_ctx.shape, dt), jax.ShapeDtypeStruct(u_lat.shape, dt)],
        scratch_shapes=[pltpu.VMEM((rows, 2 * SSM_FLAT), F32), pltpu.VMEM((batch, 2 * SSM_FLAT), F32)],
        name="s5_bwd_glu" if final else "s5_fwd",
        compiler_params=_cparams("arbitrary"),
    )(*args)


def _bf16_bits(x):
    return lax.bitcast_convert_type(x.astype(BF16).astype(F32), jnp.uint32)


def _pack_bf16_pair(a, b):
    return _bf16_bits(a) | lax.shift_right_logical(_bf16_bits(b), jnp.uint32(16))


def _unpack_bf16_pair(u):
    a = lax.bitcast_convert_type(u & jnp.uint32(0xFFFF0000), F32)
    b = lax.bitcast_convert_type(lax.shift_left(u, jnp.uint32(16)), F32)
    return a.astype(BF16), b.astype(BF16)


def _out_proj_kernel(oa_ref, ob_ref, oc_ref, od_ref, w_ref, x_ref, g1_ref, sh2_ref, sc2_ref,
                     lng_ref, lnb_ref, h_ref, f_ref, *, packed):
    y = (_dot(oa_ref[...], w_ref[0:256, :]) + _dot(ob_ref[...], w_ref[256:512, :])
         + _dot(oc_ref[...], w_ref[512:768, :]) + _dot(od_ref[...], w_ref[768:1024, :]))
    h1 = _layer_norm(DEEPNORM_ALPHA * x_ref[...] + g1_ref[...] * y, lng_ref[...], lnb_ref[...])
    h_ref[...] = h1
    f = h1 * (1.0 + sc2_ref[...]) + sh2_ref[...]
    if packed:
        f_ref[...] = _pack_bf16_pair(f[:, :D_MODEL // 2], f[:, D_MODEL // 2:])
    else:
        f_ref[...] = f.astype(BF16)


def _out_proj_inplace_kernel(oa_ref, ob_ref, oc_ref, od_ref, w_ref, x_ref, g1_ref, sh2_ref, sc2_ref,
                             lng_ref, lnb_ref, h_dest_ref, f_dest_ref, h_ref, f_ref, *, packed):
    del h_dest_ref, f_dest_ref
    _out_proj_kernel(oa_ref, ob_ref, oc_ref, od_ref, w_ref, x_ref, g1_ref, sh2_ref, sc2_ref,
                     lng_ref, lnb_ref, h_ref, f_ref, packed=packed)


def _out_proj_fill_kernel(*refs, packed, n_blocks):
    h_ref, f_ref = refs[-2:]

    @pl.when(pl.program_id(0) < n_blocks)
    def _():
        _out_proj_kernel(*refs, packed=packed)

    @pl.when(pl.program_id(0) >= n_blocks)
    def _():
        h_ref[...] = jnp.zeros_like(h_ref)
        f_ref[...] = jnp.zeros_like(f_ref)


def _out_proj_call(oa, ob, oc_tm, od, w_out, x, modr, layer, lng, lnb, *, batch, is_ctx, packed,
                   x_start=0, rows=None, out_rows=None, out_start=0, dest=None):
    rows = x.shape[0] if rows is None else rows
    out_rows = rows if out_rows is None else out_rows
    per_batch = rows // batch
    tm = 256 if is_ctx else 1024
    nb = per_batch // tm
    n_blocks = rows // tm
    n_fill = 0 if dest is not None else (out_rows - out_start - rows) // tm
    row_fn = (lambda i: batch) if is_ctx else (lambda i: jnp.minimum(i, n_blocks - 1) // nb)
    blk = lambda i: jnp.minimum(i, n_blocks - 1)
    rowblk = lambda w, start=0: pl.BlockSpec((tm, w), lambda i: (start // tm + blk(i), 0))
    outblk = lambda w: pl.BlockSpec((tm, w), lambda i: (out_start // tm + i, 0))
    full = lambda shape: pl.BlockSpec(shape, lambda i: (0,) * len(shape))
    f_width, f_dtype = (D_MODEL // 2, jnp.uint32) if packed else (D_MODEL, BF16)
    in_specs = [rowblk(256), rowblk(256),
                pl.BlockSpec((tm, SSM_WIDTH), lambda i: (blk(i) % nb, blk(i) // nb)),
                rowblk(256), full((D_MODEL, D_MODEL)), rowblk(D_MODEL, x_start),
                _mod_spec(layer, 2, row_fn), _mod_spec(layer, 3, row_fn), _mod_spec(layer, 4, row_fn),
                full((1, D_MODEL)), full((1, D_MODEL))]
    args = [oa, ob, oc_tm, od, w_out, x, modr, modr, modr, lng, lnb]
    aliases = {}
    body = functools.partial(_out_proj_kernel, packed=packed)
    if dest is not None:
        aliases = {len(args): 0, len(args) + 1: 1}
        in_specs += [pl.BlockSpec(memory_space=pl.ANY)] * 2
        args += list(dest)
        body = functools.partial(_out_proj_inplace_kernel, packed=packed)
    elif n_fill:
        body = functools.partial(_out_proj_fill_kernel, packed=packed, n_blocks=n_blocks)
    return pl.pallas_call(
        body,
        grid=(n_blocks + n_fill,),
        in_specs=in_specs,
        out_specs=[outblk(D_MODEL), outblk(f_width)],
        out_shape=[jax.ShapeDtypeStruct((out_rows, D_MODEL), F32),
                   jax.ShapeDtypeStruct((out_rows, f_width), f_dtype)],
        input_output_aliases=aliases,
        name="out_proj_ctx" if is_ctx else "out_proj",
        compiler_params=_cparams("arbitrary"),
    )(*args)


def _ffn_kernel(f_ref, wg_ref, wu_ref, wd_ref, h_ref, g2_ref, lng_ref, lnb_ref, o_ref, acc_ref):
    j = pl.program_id(1)

    @pl.when(j == 0)
    def _():
        acc_ref[...] = jnp.zeros_like(acc_ref)

    f = f_ref[...]
    g = _dot(f, wg_ref[...])
    u = _dot(f, wu_ref[...])
    acc_ref[...] += _dot((g * jax.nn.sigmoid(g) * u).astype(BF16), wd_ref[...])

    @pl.when(j == pl.num_programs(1) - 1)
    def _():
        o_ref[...] = _layer_norm(DEEPNORM_ALPHA * h_ref[...] + g2_ref[...] * acc_ref[...],
                                 lng_ref[...], lnb_ref[...])


def _ffn_call(f_in, wg, wu, wd, h1, modr, layer, lng, lnb, *, batch, is_ctx):
    rows = h1.shape[0]
    per_batch = rows // batch
    tm = 512
    d_ff = wg.shape[1]
    tf = d_ff // 2
    nb = per_batch // tm
    row_fn = (lambda i: batch) if is_ctx else (lambda i: i // nb)
    rowblk = lambda w: pl.BlockSpec((tm, w), lambda i, j: (i, 0))
    full = lambda shape: pl.BlockSpec(shape, lambda i, j: (0,) * len(shape))
    return pl.pallas_call(
        _ffn_kernel,
        grid=(rows // tm, d_ff // tf),
        in_specs=[rowblk(D_MODEL),
                  pl.BlockSpec((D_MODEL, tf), lambda i, j: (0, j)),
                  pl.BlockSpec((D_MODEL, tf), lambda i, j: (0, j)),
                  pl.BlockSpec((tf, D_MODEL), lambda i, j: (j, 0)),
                  rowblk(D_MODEL), _mod_spec(layer, 5, row_fn), full((1, D_MODEL)), full((1, D_MODEL))],
        out_specs=rowblk(D_MODEL),
        out_shape=jax.ShapeDtypeStruct((rows, D_MODEL), F32),
        scratch_shapes=[pltpu.VMEM((tm, D_MODEL), F32)],
        name="ffn_ctx" if is_ctx else "ffn",
        compiler_params=_cparams("arbitrary", "arbitrary"),
    )(f_in, wg, wu, wd, h1, modr, lng, lnb)


MOE_TM = 1024
MOE_TILE = 512
MOE_PASS_TILES = 10
MOE_FF_CHUNKS = 7
ROUTE_ROWS = 8


def _router_kernel(h_ref, sh2_ref, sc2_ref, wr_ref, br_ref, tri_ref, route_ref, gate_ref, cnt_ref):
    tm = h_ref.shape[0]
    f = h_ref[...] * (1.0 + sc2_ref[...]) + sh2_ref[...]
    f_hi, f_lo = _split_bf16(f)
    w_hi, w_lo = _split_bf16(wr_ref[...])
    logits = _dot_nt(w_hi, f_hi) + _dot_nt(w_hi, f_lo) + _dot_nt(w_lo, f_hi) + br_ref[...]
    ie = lax.broadcasted_iota(jnp.int32, logits.shape, 0)
    m1 = jnp.max(logits, axis=0, keepdims=True)
    i1 = jnp.min(jnp.where(logits == m1, ie, N_EXPERTS), axis=0, keepdims=True)
    rest = jnp.where(ie == i1, -jnp.inf, logits)
    m2 = jnp.max(rest, axis=0, keepdims=True)
    i2 = jnp.min(jnp.where(rest == m2, ie, N_EXPERTS), axis=0, keepdims=True)
    e2 = jnp.exp(m2 - m1)
    den = 1.0 + e2
    sel = jnp.where((ie == i1) | (ie == i2), 1.0, 0.0)
    rank = _dot(sel.astype(BF16), tri_ref[...])
    r1 = jnp.sum(jnp.where(ie == i1, rank, 0.0), axis=0, keepdims=True).astype(jnp.int32)
    r2 = jnp.sum(jnp.where(ie == i2, rank, 0.0), axis=0, keepdims=True).astype(jnp.int32)
    row = lax.broadcasted_iota(jnp.int32, (ROUTE_ROWS, tm), 0)
    route_ref[...] = jnp.where(row == 0, i1, jnp.where(row == 1, i2, jnp.where(row == 2, r1,
                               jnp.where(row == 3, r2, 0))))
    lrow = lax.broadcasted_iota(jnp.int32, (LANES, tm), 0)
    gate_ref[...] = jnp.where(lrow == 0, 1.0 / den, jnp.where(lrow == 1, e2 / den, 0.0)).T
    cnt = jnp.sum(sel, axis=1, keepdims=True)
    cnt_ref[...] = jnp.broadcast_to(cnt, cnt_ref.shape).astype(jnp.int32)


def _router_call(h1, modr, layer, wr_t, b_r, tri, *, batch, is_ctx):
    rows = h1.shape[0]
    tm = MOE_TM
    nblk = rows // tm
    nb = (rows // batch) // tm if not is_ctx else 1
    row_fn = (lambda i: batch) if is_ctx else (lambda i: i // nb)
    full = lambda shape: pl.BlockSpec(shape, lambda i: (0,) * len(shape))
    return pl.pallas_call(
        _router_kernel,
        grid=(nblk,),
        in_specs=[pl.BlockSpec((tm, D_MODEL), lambda i: (i, 0)),
                  _mod_spec(layer, 3, row_fn), _mod_spec(layer, 4, row_fn),
                  full((N_EXPERTS, D_MODEL)), full((N_EXPERTS, 1)), full((tm, tm))],
        out_specs=[pl.BlockSpec((None, ROUTE_ROWS, tm), lambda i: (i, 0, 0)),
                   pl.BlockSpec((tm, LANES), lambda i: (i, 0)),
                   pl.BlockSpec((None, N_EXPERTS, LANES), lambda i: (i, 0, 0))],
        out_shape=[jax.ShapeDtypeStruct((nblk, ROUTE_ROWS, tm), jnp.int32),
                   jax.ShapeDtypeStruct((rows, LANES), F32),
                   jax.ShapeDtypeStruct((nblk, N_EXPERTS, LANES), jnp.int32)],
        name="router_ctx" if is_ctx else "router",
        compiler_params=_cparams("arbitrary"),
    )(h1, modr, modr, wr_t, b_r, tri)


def _route_plan(route, cnt, n_tiles, n_passes):
    total = jnp.sum(cnt, axis=0)
    tiles = lax.div(total + (MOE_TILE - 1), MOE_TILE)
    tile_start = jnp.cumsum(tiles) - tiles
    base = tile_start[None, :] * MOE_TILE + jnp.cumsum(cnt, axis=0) - cnt
    choice = jax.nn.one_hot(route[:, 0:2, :], N_EXPERTS, dtype=jnp.int32)
    pos = jnp.sum(choice * base[:, None, None, :], axis=-1) + route[:, 2:4, :]
    nblk, _, tm = pos.shape
    token = jnp.broadcast_to((jnp.arange(nblk)[:, None, None] * tm + jnp.arange(tm)[None, None, :]), pos.shape)
    src = jnp.zeros((n_tiles * MOE_TILE,), jnp.int32).at[pos.reshape(-1)].set(
        token.reshape(-1).astype(jnp.int32), unique_indices=True, mode="promise_in_bounds")
    src = src.reshape(n_tiles, 1, MOE_TILE)
    passes = lax.div(tiles + (MOE_PASS_TILES - 1), MOE_PASS_TILES)
    pass_end = jnp.cumsum(passes)
    p = jnp.arange(n_passes, dtype=jnp.int32)
    owner = jnp.sum((p[:, None] >= pass_end[None, :]).astype(jnp.int32), axis=1)
    last = jnp.sum((pass_end[-1] - 1 >= pass_end).astype(jnp.int32))
    expert = jnp.where(p < pass_end[-1], owner, last)
    within = (p - (pass_end - passes)[expert]) * MOE_PASS_TILES
    n = jnp.where(p < pass_end[-1], jnp.clip(tiles[expert] - within, 0, MOE_PASS_TILES), 0)
    first = tile_start[expert] + within
    used = jnp.sum(tiles)
    first = jnp.where(p == pass_end[-1], used, first)
    n_zero = jnp.where(p == pass_end[-1], n_tiles - used, 0)
    i32 = lambda a: a.astype(jnp.int32)
    return i32(pos), src, i32(used).reshape(1), i32(expert), i32(first), i32(n), i32(n_zero)


def _row_copies(src_of, dst_of, n_rows, sem):
    def copy(t, k):
        return pltpu.make_async_copy(src_of(t, k), dst_of(t, k), sem)

    def start(t, carry):
        copy(t, 0).start(priority=0)
        copy(t, 1).start(priority=1)
        return carry

    def wait(t, carry):
        copy(t, 0).wait()
        copy(t, 1).wait()
        return carry

    lax.fori_loop(0, n_rows, start, 0, unroll=8)
    lax.fori_loop(0, n_rows, wait, 0, unroll=8)


DISPATCH_TILES = 8


def _dispatch_kernel(used_ref, src_ref, x_hbm, xs_ref, sem):
    half = xs_ref.shape[0] // 2
    first_tile = pl.program_id(0) * DISPATCH_TILES

    @pl.when(first_tile < used_ref[0])
    def _():
        _row_copies(lambda r, k: x_hbm.at[pl.ds(src_ref[0, k * half + r], 1)],
                    lambda r, k: xs_ref.at[pl.ds(k * half + r, 1)], half, sem)

    @pl.when(first_tile >= used_ref[0])
    def _():
        xs_ref[...] = jnp.zeros_like(xs_ref)


def _dispatch_call(n_used, src, x_packed):
    n_rows = src.shape[0] * MOE_TILE
    step_rows = DISPATCH_TILES * MOE_TILE
    width = x_packed.shape[1]
    grid_spec = pltpu.PrefetchScalarGridSpec(
        num_scalar_prefetch=1,
        grid=(n_rows // step_rows,),
        in_specs=[pl.BlockSpec((None, 1, step_rows), lambda t, nu: (t, 0, 0), memory_space=pltpu.SMEM),
                  pl.BlockSpec(memory_space=pl.ANY)],
        out_specs=pl.BlockSpec((step_rows, width), lambda t, nu: (t, 0)),
        scratch_shapes=[pltpu.SemaphoreType.DMA(())],
    )
    return pl.pallas_call(
        _dispatch_kernel,
        grid_spec=grid_spec,
        out_shape=jax.ShapeDtypeStruct((n_rows, width), x_packed.dtype),
        name="moe_dispatch",
        compiler_params=_cparams("arbitrary"),
    )(n_used, src.reshape(n_rows // step_rows, 1, step_rows), x_packed)


def _experts_kernel(pe_ref, pf_ref, pn_ref, pz_ref, xs_hbm, wg_ref, wu_ref, wd_ref, ys_hbm,
                    xa_buf, xb_buf, y_buf, stage_buf, wg_buf, wu_buf, wd_buf, load_sem, store_sem):
    del pe_ref
    p, j = pl.program_id(0), pl.program_id(1)
    last_j = pl.num_programs(1) - 1
    n_tiles = pn_ref[p]
    first = pf_ref[p]
    half = D_MODEL // 2

    def tile_rows(k):
        return pl.ds(pl.multiple_of((first + k) * MOE_TILE, MOE_TILE), MOE_TILE)

    def load(k):
        slot = lax.rem(k, 2)
        return pltpu.make_async_copy(xs_hbm.at[tile_rows(k)], stage_buf.at[slot], load_sem.at[slot])

    def store(k, src):
        return pltpu.make_async_copy(src, ys_hbm.at[tile_rows(k)], store_sem)

    def for_tiles(n, body):
        lax.fori_loop(0, n, lambda k, c: (body(k), c)[1], 0)

    @pl.when((pz_ref[p] > 0) & (j == 0))
    def _():
        y_buf[0] = jnp.zeros(y_buf.shape[1:], F32)
        for_tiles(pz_ref[p], lambda k: store(k, y_buf.at[0]).start())
        for_tiles(pz_ref[p], lambda k: store(k, y_buf.at[0]).wait())

    @pl.when(n_tiles > 0)
    def _():
        wg_buf[...] = wg_ref[...].astype(BF16)
        wu_buf[...] = wu_ref[...].astype(BF16)
        wd_buf[...] = wd_ref[...].astype(BF16)

        def swiglu(k):
            xa, xb = xa_buf[k], xb_buf[k]
            g = _dot(xa, wg_buf[:half, :]) + _dot(xb, wg_buf[half:, :])
            u = _dot(xa, wu_buf[:half, :]) + _dot(xb, wu_buf[half:, :])
            return _dot((g * jax.nn.sigmoid(g) * u).astype(BF16), wd_buf[...])

        @pl.when(j == 0)
        def _():
            load(0).start()

            def first_chunk(k):
                load(k).wait()

                @pl.when(k + 1 < n_tiles)
                def _():
                    load(k + 1).start()

                xa_buf[k], xb_buf[k] = _unpack_bf16_pair(stage_buf[lax.rem(k, 2)])
                y_buf[k] = swiglu(k)
            for_tiles(n_tiles, first_chunk)

        @pl.when((j > 0) & (j < last_j))
        def _():
            def middle_chunk(k):
                y_buf[k] += swiglu(k)
            for_tiles(n_tiles, middle_chunk)

        @pl.when(j == last_j)
        def _():
            def last_chunk(k):
                y_buf[k] += swiglu(k)
                store(k, y_buf.at[k]).start()
            for_tiles(n_tiles, last_chunk)
            for_tiles(n_tiles, lambda k: store(k, y_buf.at[k]).wait())


def _experts_call(pass_expert, pass_first, pass_tiles, pass_zero, xs, w_gate, w_up, w_down, moe_layer):
    n_passes = pass_expert.shape[0]
    d_ff = w_gate.shape[3]
    tf = d_ff // MOE_FF_CHUNKS
    half = D_MODEL // 2
    chunk = lambda p, j, pn: jnp.where(pn[p] > 0, j, MOE_FF_CHUNKS - 1)
    cols = pl.BlockSpec((None, None, D_MODEL, tf),
                        lambda p, j, pe, pf, pn, pz: (moe_layer, pe[p], 0, chunk(p, j, pn)))
    grid_spec = pltpu.PrefetchScalarGridSpec(
        num_scalar_prefetch=4,
        grid=(n_passes, MOE_FF_CHUNKS),
        in_specs=[pl.BlockSpec(memory_space=pl.ANY), cols, cols,
                  pl.BlockSpec((None, None, tf, D_MODEL),
                               lambda p, j, pe, pf, pn, pz: (moe_layer, pe[p], chunk(p, j, pn), 0))],
        out_specs=pl.BlockSpec(memory_space=pl.ANY),
        scratch_shapes=[pltpu.VMEM((MOE_PASS_TILES, MOE_TILE, half), BF16),
                        pltpu.VMEM((MOE_PASS_TILES, MOE_TILE, half), BF16),
                        pltpu.VMEM((MOE_PASS_TILES, MOE_TILE, D_MODEL), F32),
                        pltpu.VMEM((2, MOE_TILE, half), jnp.uint32),
                        pltpu.VMEM((D_MODEL, tf), BF16), pltpu.VMEM((D_MODEL, tf), BF16),
                        pltpu.VMEM((tf, D_MODEL), BF16),
                        pltpu.SemaphoreType.DMA((2,)), pltpu.SemaphoreType.DMA(())],
    )
    return pl.pallas_call(
        _experts_kernel,
        grid_spec=grid_spec,
        out_shape=jax.ShapeDtypeStruct((xs.shape[0], D_MODEL), F32),
        name="moe_experts",
        compiler_params=_cparams("arbitrary", "arbitrary"),
    )(pass_expert, pass_first, pass_tiles, pass_zero, xs, w_gate, w_up, w_down)


def _combine_kernel(pos_ref, ys_hbm, gate_ref, h_ref, g2_ref, lng_ref, lnb_ref, o_ref, buf_ref, sem):
    tm = pos_ref.shape[1]
    _row_copies(lambda t, k: ys_hbm.at[pl.ds(pos_ref[k, t], 1)],
                lambda t, k: buf_ref.at[k, pl.ds(t, 1)], tm, sem)
    gate = gate_ref[...]
    f = gate[:, 0:1] * buf_ref[0] + gate[:, 1:2] * buf_ref[1]
    o_ref[...] = _layer_norm(DEEPNORM_ALPHA * h_ref[...] + g2_ref[...] * f, lng_ref[...], lnb_ref[...])


def _combine_call(pos, ys, gate, h1, modr, layer, lng, lnb, *, batch, is_ctx):
    rows = h1.shape[0]
    nblk, _, tm = pos.shape
    nb = (rows // batch) // tm if not is_ctx else 1
    row_fn = (lambda i: batch) if is_ctx else (lambda i: i // nb)
    rowblk = lambda w: pl.BlockSpec((tm, w), lambda i: (i, 0))
    full = lambda shape: pl.BlockSpec(shape, lambda i: (0,) * len(shape))
    return pl.pallas_call(
        _combine_kernel,
        grid=(nblk,),
        in_specs=[pl.BlockSpec((None, 2, tm), lambda i: (i, 0, 0), memory_space=pltpu.SMEM),
                  pl.BlockSpec(memory_space=pl.ANY), rowblk(LANES), rowblk(D_MODEL),
                  _mod_spec(layer, 5, row_fn), full((1, D_MODEL)), full((1, D_MODEL))],
        out_specs=rowblk(D_MODEL),
        out_shape=jax.ShapeDtypeStruct((rows, D_MODEL), F32),
        scratch_shapes=[pltpu.VMEM((2, tm, D_MODEL), F32), pltpu.SemaphoreType.DMA(())],
        name="moe_combine",
        compiler_params=_cparams("arbitrary"),
    )(pos, ys, gate, h1, modr, lng, lnb)


def _moe_call(f_packed, h1, modr, layer, wr_t, b_r, tri, w_gate, w_up, w_down, moe_layer, lng, lnb,
              *, batch, is_ctx):
    rows = h1.shape[0]
    n_tiles = (2 * rows) // MOE_TILE + N_EXPERTS
    assert n_tiles % DISPATCH_TILES == 0
    n_passes = n_tiles // MOE_PASS_TILES + N_EXPERTS
    route, gate, cnt = _router_call(h1, modr, layer, wr_t, b_r, tri, batch=batch, is_ctx=is_ctx)
    pos, src, n_used, *passes = _route_plan(route, cnt[:, :, 0], n_tiles, n_passes)
    xs = _dispatch_call(n_used, src, f_packed)
    ys = _experts_call(*passes, xs, w_gate, w_up, w_down, moe_layer)
    return _combine_call(pos, ys, gate, h1, modr, layer, lng, lnb, batch=batch, is_ctx=is_ctx)


def _rope_tables(seq):
    pos = jnp.arange(seq, dtype=jnp.int32)
    row = (pos // GRID_W).astype(F32)
    col = (pos % GRID_W).astype(F32)
    n_freq = HEAD_DIM // 4
    inv_freq = ROPE_THETA ** (-jnp.arange(n_freq, dtype=F32) / n_freq)
    ang = jnp.concatenate([row[:, None] * inv_freq, col[:, None] * inv_freq], axis=-1)
    cos, sin = jnp.cos(ang), jnp.sin(ang)
    reps = LANES // (HEAD_DIM // 2)
    sign = jnp.tile(jnp.concatenate([-jnp.ones(HEAD_DIM // 2, F32), jnp.ones(HEAD_DIM // 2, F32)]),
                    LANES // HEAD_DIM)
    return jnp.tile(cos, (1, reps)), jnp.tile(sin, (1, reps)) * sign


def kernel(x, c, ctx, c_ctx, ada_w, ada_b, w_in, w_out, na_rpb, ga_q_norm, ga_k_norm,
           ssm_lambda_re, ssm_lambda_im, ssm_log_step, ssm_b_re, ssm_b_im, ssm_c_re, ssm_c_im,
           ssm_d, ssm_w_glu, sw_sink, ln1_g, ln1_b, ln2_g, ln2_b,
           ffn_w_gate, ffn_w_up, ffn_w_down,
           moe_w_router, moe_b_router, moe_w_gate, moe_w_up, moe_w_down):
    batch, seq, _ = x.shape
    n_ctx = ctx.shape[1]
    assert batch < MOD_ROWS and seq % 512 == 0 and n_ctx % 256 == 0

    c_all = jnp.zeros((MOD_ROWS, D_MODEL), F32).at[:batch].set(c).at[batch].set(c_ctx)
    modr = _mods_call(c_all, ada_w, ada_b)

    cos_t, sin_t = _rope_tables(seq)
    seg = jnp.kron(jnp.eye(256 // HEAD_DIM, dtype=F32), jnp.ones((HEAD_DIM, HEAD_DIM), F32)).astype(BF16)
    tri = (jnp.arange(MOE_TM)[:, None] < jnp.arange(MOE_TM)[None, :]).astype(BF16)
    sw_mask = _window_mask_table(seq)

    n_lat, n_c = batch * seq, batch * n_ctx
    h = (x.reshape(n_lat, D_MODEL), 0, n_lat)
    hc = (ctx.reshape(n_c, D_MODEL), 0, n_c)
    names = ("qa", "ka", "va", "qb", "kb", "vb", "u", "qd", "kd", "vd")

    for layer in range(DEPTH):
        need_ctx = layer < DEPTH - 1
        w_in_l = w_in[layer].astype(BF16)
        w_out_l = w_out[layer].astype(BF16)
        gq = jnp.tile(ga_q_norm[layer].astype(F32), 256 // HEAD_DIM)[None, :]
        gk = jnp.tile(ga_k_norm[layer].astype(F32), 128 // HEAD_DIM)[None, :]
        sink = sw_sink[layer].astype(F32)
        ln1 = (ln1_g[layer][None, :], ln1_b[layer][None, :])
        ln2 = (ln2_g[layer][None, :], ln2_b[layer][None, :])

        lat = dict(zip(names, _in_proj_call(h[0], modr, layer, w_in_l, cos_t, sin_t, seg, gq, gk,
                                            batch=batch, is_ctx=False, x_start=h[1], rows=h[2])))
        cx = dict(zip(names, _in_proj_call(hc[0], modr, layer, w_in_l, cos_t, sin_t, seg, gq, gk,
                                           batch=batch, is_ctx=True, x_start=hc[1], rows=hc[2])))

        na_bias, na_mask = _na_tables(na_rpb[layer], seq)
        out_a = _na_call(lat["qa"], lat["ka"], lat["va"], cx["ka"], cx["va"], na_bias, na_mask, batch=batch)
        out_b = _global_call(lat["qb"], lat["kb"], lat["vb"], cx["kb"], cx["vb"], batch=batch)
        out_d = _window_call(sink, lat["qd"], lat["kd"], lat["vd"], cx["kd"], cx["vd"], sw_mask, batch=batch)

        dirs = [_s5_operands(ssm_lambda_re[layer, d], ssm_lambda_im[layer, d], ssm_log_step[layer, d],
                             ssm_b_re[layer, d], ssm_b_im[layer, d], ssm_c_re[layer, d], ssm_c_im[layer, d])
                for d in range(2)]
        rows_tb = lambda a: a.reshape(a.shape[0] * batch, SSM_WIDTH)
        u_tb = (rows_tb(cx["u"]), rows_tb(lat["u"]))
        fwd = _s5_pass(*u_tb, None, *dirs[0], None, None, batch=batch, reverse=False)
        ctx_c, out_c = _s5_pass(*u_tb, fwd, *dirs[1], ssm_d[layer].astype(F32)[None, :],
                                ssm_w_glu[layer].astype(BF16), batch=batch, reverse=True)
        ctx_c, out_c = ctx_c.reshape(cx["u"].shape), out_c.reshape(lat["u"].shape)

        routed = layer % 2 == 1
        joint = routed and need_ctx
        h1, f_in = _out_proj_call(out_a, out_b, out_c, out_d, w_out_l, h[0], modr, layer, *ln1,
                                  batch=batch, is_ctx=False, packed=routed, x_start=h[1], rows=h[2],
                                  out_rows=n_lat + n_c if joint else None)
        streams = [(h1, f_in, False)]
        if need_ctx:
            ctx_a, ctx_b, ctx_d = _ctx_attn_call(sink, cx, batch=batch)
            hc1, fc_in = _out_proj_call(ctx_a, ctx_b, ctx_c, ctx_d, w_out_l, hc[0], modr, layer, *ln1,
                                        batch=batch, is_ctx=True, packed=routed, x_start=hc[1], rows=hc[2],
                                        **(dict(out_rows=n_lat + n_c, out_start=n_lat, dest=(h1, f_in))
                                           if joint else {}))
            streams = [(hc1, fc_in, False)] if joint else streams + [(hc1, fc_in, True)]

        i = layer // 2
        outs = []
        if not routed:
            wg, wu, wd = (ffn_w_gate[i].astype(BF16), ffn_w_up[i].astype(BF16), ffn_w_down[i].astype(BF16))
            for s1, sf, is_ctx in streams:
                outs.append(_ffn_call(sf, wg, wu, wd, s1, modr, layer, *ln2, batch=batch, is_ctx=is_ctx))
        else:
            wr_t = moe_w_router[i].astype(F32).T
            b_r = moe_b_router[i].astype(F32)[:, None]
            for s1, sf, is_ctx in streams:
                outs.append(_moe_call(sf, s1, modr, layer, wr_t, b_r, tri, moe_w_gate, moe_w_up, moe_w_down, i,
                                      *ln2, batch=batch, is_ctx=is_ctx))
        if joint:
            h, hc = (outs[0], 0, n_lat), (outs[0], n_lat, n_c)
        else:
            h = (outs[0], 0, n_lat)
            if need_ctx:
                hc = (outs[1], 0, n_c)
    return h[0].reshape(batch, seq, D_MODEL)
```
